```python
import math
import jax, jax.numpy as jnp
from jax import lax
import numpy as np

D_MODEL = 1024
BATCH = 2
SEQ = 8192
DEPTH = 2
DEC_BATCH = 2
DEC_SEQ = 16384
PAST_LEN = 128

GRID_W = 64
HEAD_DIM = 64
ATTN_HEADS = 8
ATTN_KV_HEADS = 2
ATTN_WIDTH = ATTN_HEADS * HEAD_DIM
KV_WIDTH = ATTN_KV_HEADS * HEAD_DIM
Q_BLOCK = 128
ROPE_THETA = 10000.0
ROPE_FREQS = HEAD_DIM // 4
SSM_GROUP = 16
SSM_GROUPS = 24
SSM_WIDTH = SSM_GROUP * SSM_GROUPS
SSM_STATE = 64
HYENA_WIDTH = 384
HYENA_ORDER = 2
FILTER_BANDS = 16
FILTER_EMB = 2 * FILTER_BANDS + 1
FILTER_HIDDEN = 64
HYENA_SLOW_DECAY = 3.07
HYENA_FAST_DECAY = 15.35
MEM_TOKENS = 256
MEM_HEADS = 4
MEM_WIDTH = MEM_HEADS * HEAD_DIM
MIX_WIDTH = ATTN_WIDTH + SSM_WIDTH + HYENA_WIDTH + MEM_WIDTH
IN_WIDTH = 2 * ATTN_WIDTH + 2 * KV_WIDTH + 2 * SSM_WIDTH + (HYENA_ORDER + 2) * HYENA_WIDTH + MEM_WIDTH
EPS = 1e-6

kernel_name = 'hymba_style_hybrid_encoder'


def rms_norm(x, g):
    xf = x.astype(jnp.float32)
    y = xf * lax.rsqrt(jnp.mean(xf * xf, axis=-1, keepdims=True) + EPS)
    return (y * g.astype(jnp.float32)).astype(x.dtype)


def axial_rope_tables(length):
    rows = length // GRID_W
    row = jnp.broadcast_to(jnp.arange(rows, dtype=jnp.float32)[:, None], (rows, GRID_W)).reshape(length)
    col = jnp.broadcast_to(jnp.arange(GRID_W, dtype=jnp.float32)[None, :], (rows, GRID_W)).reshape(length)
    inv_freq = ROPE_THETA ** (-jnp.arange(ROPE_FREQS, dtype=jnp.float32) / ROPE_FREQS)
    ang = jnp.stack([row[:, None] * inv_freq, col[:, None] * inv_freq], axis=1)
    ang = jnp.broadcast_to(ang[:, :, None, :], (length, 2, 2, ROPE_FREQS)).reshape(length, HEAD_DIM)
    return jnp.cos(ang), jnp.sin(ang)


def apply_axial_rope(x, cos, sin):
    xs = x.reshape(x.shape[:-1] + (2, 2, ROPE_FREQS))
    rot = jnp.stack([-xs[..., 1, :], xs[..., 0, :]], axis=-2).reshape(x.shape)
    return x * cos[:, None, :] + rot * sin[:, None, :]


def blocked_gqa(q, k, v):
    bsz, length = q.shape[0], q.shape[1]
    n_blocks = length // Q_BLOCK
    groups = ATTN_HEADS // ATTN_KV_HEADS
    qb = q.reshape(bsz, n_blocks, Q_BLOCK, ATTN_KV_HEADS, groups, HEAD_DIM).transpose(1, 0, 2, 3, 4, 5)
    scale = HEAD_DIM ** -0.5

    def one_block(q_blk):
        s = jnp.einsum('bqkgd,bskd->bkgqs', q_blk, k, preferred_element_type=jnp.float32) * scale
        p = jax.nn.softmax(s, axis=-1).astype(v.dtype)
        return jnp.einsum('bkgqs,bskd->bqkgd', p, v)

    o = lax.map(one_block, qb)
    return o.transpose(1, 0, 2, 3, 4, 5).reshape(bsz, length, ATTN_WIDTH)


def memory_attention(q, k, v):
    s = jnp.einsum('blhd,bmhd->bhlm', q, k, preferred_element_type=jnp.float32) * HEAD_DIM ** -0.5
    p = jax.nn.softmax(s, axis=-1).astype(v.dtype)
    return jnp.einsum('bhlm,bmhd->blhd', p, v).reshape(q.shape[0], q.shape[1], MEM_WIDTH)


def _linear_recurrence(e1, e2):
    a1, b1 = e1
    a2, b2 = e2
    return a1 * a2, a2 * b1 + b2


def s5_bidirectional(u, a_re, a_im, log_step, b_re, b_im, c_re, c_im, d, w_glu):
    bsz, length, _ = u.shape
    uf = u.astype(jnp.float32)
    ug = uf.reshape(bsz, length, SSM_GROUPS, SSM_GROUP).astype(jnp.complex64)
    lam = lax.complex(a_re.astype(jnp.float32), a_im.astype(jnp.float32))
    step = jnp.exp(log_step.astype(jnp.float32))[..., None]
    a_bar = jnp.exp(lam * step)
    b_bar = ((a_bar - 1.0) / lam)[..., None] * lax.complex(b_re.astype(jnp.float32), b_im.astype(jnp.float32))
    c = lax.complex(c_re.astype(jnp.float32), c_im.astype(jnp.float32))

    def run_direction(i, reverse):
        bu = jnp.einsum('gpc,blgc->blgp', b_bar[i], ug)
        a = jnp.broadcast_to(a_bar[i], bu.shape)
        _, h = lax.associative_scan(_linear_recurrence, (a, bu), reverse=reverse, axis=1)
        return jnp.einsum('gcp,blgp->blgc', c[i], h).real

    y = (run_direction(0, False) + run_direction(1, True)).reshape(bsz, length, SSM_WIDTH) + d.astype(jnp.float32) * uf
    g = jax.nn.gelu(y)
    return (g * jax.nn.sigmoid(g @ w_glu.astype(jnp.float32))).astype(u.dtype)


def hyena_filters(length, w1, b1, w2, b2, w3, log_decay):
    f32 = jnp.float32
    pos = jnp.arange(length, dtype=f32)
    t = (pos / length)[:, None]
    wpos = (2.0 * math.pi / length) * pos[:, None]
    bands = jnp.linspace(1e-4, FILTER_BANDS - 1, FILTER_BANDS, dtype=f32)
    z = jnp.concatenate([t, jnp.cos(bands * wpos), -jnp.sin(bands * wpos)], axis=-1)
    hid = jnp.sin(z @ w1.astype(f32) + b1.astype(f32))
    hid = jnp.sin(hid @ w2.astype(f32) + b2.astype(f32))
    filt = (hid @ w3.astype(f32)).reshape(length, HYENA_ORDER, 2, HYENA_WIDTH)
    filt = filt * jnp.exp(-t[:, :, None, None] * jnp.exp(log_decay.astype(f32)))
    kern = jnp.concatenate([filt[:, :, 0],
                            jnp.zeros((1, HYENA_ORDER, HYENA_WIDTH), f32),
                            filt[:0:-1, :, 1]], axis=0)
    return jnp.fft.rfft(kern, axis=0)


def long_conv(u, kf, bias):
    length = u.shape[1]
    y = jnp.fft.irfft(jnp.fft.rfft(u, n=2 * length, axis=1) * kf, n=2 * length, axis=1)[:, :length]
    return y + u * bias


def short_conv(x, w, b):
    xp = jnp.pad(x, ((0, 0), (1, 1), (0, 0)))
    return xp[:, :-2] * w[0] + xp[:, 1:-1] * w[1] + xp[:, 2:] * w[2] + b


def mixer_layer(x, mem, cos, sin, hyena_kf, pre_g, post_g, w_in, q_g, k_g, mem_g, w_mem_kv,
                a_re, a_im, log_step, b_re, b_im, c_re, c_im, ssm_d, w_glu,
                hy_short_w, hy_short_b, hy_bias, branch_g, w_out):
    bsz, length, _ = x.shape
    h = rms_norm(x, pre_g)
    proj = h @ w_in
    sizes = [ATTN_WIDTH, KV_WIDTH, KV_WIDTH, ATTN_WIDTH, SSM_WIDTH, SSM_WIDTH,
             3 * HYENA_WIDTH, HYENA_WIDTH, MEM_WIDTH]
    points = np.cumsum(sizes)[:-1].tolist()
    q, k, v, attn_gate, ssm_u, ssm_gate, hy_in, hy_gate, mem_q = jnp.split(proj, points, axis=-1)

    q = rms_norm(q.reshape(bsz, length, ATTN_HEADS, HEAD_DIM), q_g)
    k = rms_norm(k.reshape(bsz, length, ATTN_KV_HEADS, HEAD_DIM), k_g)
    q = apply_axial_rope(q.astype(jnp.float32), cos, sin).astype(x.dtype)
    k = apply_axial_rope(k.astype(jnp.float32), cos, sin).astype(x.dtype)
    attn = blocked_gqa(q, k, v.reshape(bsz, length, ATTN_KV_HEADS, HEAD_DIM)) * jax.nn.silu(attn_gate)

    ssm = s5_bidirectional(ssm_u, a_re, a_im, log_step, b_re, b_im, c_re, c_im, ssm_d, w_glu) * jax.nn.silu(ssm_gate)

    hx = short_conv(hy_in.astype(jnp.float32), hy_short_w.astype(jnp.float32), hy_short_b.astype(jnp.float32))
    hv, hx1, hx2 = jnp.split(hx, 3, axis=-1)
    z = hv
    for n, gate_n in enumerate((hx1, hx2)):
        z = gate_n * long_conv(z, hyena_kf[:, n], hy_bias[n].astype(jnp.float32))
    hyena = z.astype(x.dtype) * jax.nn.silu(hy_gate)

    mem_kv = rms_norm(mem, mem_g) @ w_mem_kv
    mk, mv = jnp.split(mem_kv, 2, axis=-1)
    cross = memory_attention(mem_q.reshape(bsz, length, MEM_HEADS, HEAD_DIM),
                             mk.reshape(bsz, -1, MEM_HEADS, HEAD_DIM),
                             mv.reshape(bsz, -1, MEM_HEADS, HEAD_DIM))

    g_attn, g_ssm, g_hy, g_mem = jnp.split(branch_g, [ATTN_WIDTH, ATTN_WIDTH + SSM_WIDTH,
                                                       ATTN_WIDTH + SSM_WIDTH + HYENA_WIDTH])
    mixed = jnp.concatenate([rms_norm(attn, g_attn), rms_norm(ssm, g_ssm),
                             rms_norm(hyena, g_hy), rms_norm(cross, g_mem)], axis=-1) @ w_out
    return x + rms_norm(mixed, post_g)


def setup_inputs(seed: int = 0) -> dict:
    key = jax.random.key(seed)
    ks = jax.random.split(key, 32)
    f32 = jnp.float32

    def nrm(k, shape, scale):
        return scale * jax.random.normal(k, shape, f32)

    ssm_a_im = jnp.broadcast_to(math.pi * jnp.arange(SSM_STATE, dtype=f32), (DEPTH, 2, SSM_GROUPS, SSM_STATE))
    decay0 = jnp.log(jnp.linspace(HYENA_SLOW_DECAY, HYENA_FAST_DECAY, HYENA_WIDTH, dtype=f32))
    return {
        'x_prompt': nrm(ks[0], (BATCH, SEQ, D_MODEL), 1.0),
        'x_sample': nrm(ks[1], (DEC_BATCH, DEC_SEQ, D_MODEL), 1.0),
        'mem_prompt': nrm(ks[2], (BATCH, MEM_TOKENS, D_MODEL), 1.0),
        'mem_sample': nrm(ks[3], (DEC_BATCH, MEM_TOKENS, D_MODEL), 1.0),
        'pre_norm': 1.0 + nrm(ks[4], (DEPTH, D_MODEL), 0.02),
        'post_norm': 1.0 + nrm(ks[5], (DEPTH, D_MODEL), 0.02),
        'w_in': nrm(ks[6], (DEPTH, D_MODEL, IN_WIDTH), D_MODEL ** -0.5),
        'q_norm': 1.0 + nrm(ks[7], (DEPTH, HEAD_DIM), 0.02),
        'k_norm': 1.0 + nrm(ks[8], (DEPTH, HEAD_DIM), 0.02),
        'mem_norm': 1.0 + nrm(ks[9], (DEPTH, D_MODEL), 0.02),
        'w_mem_kv': nrm(ks[10], (DEPTH, D_MODEL, 2 * MEM_WIDTH), D_MODEL ** -0.5),
        'ssm_a_re': -0.5 + nrm(ks[11], (DEPTH, 2, SSM_GROUPS, SSM_STATE), 0.01),
        'ssm_a_im': ssm_a_im + nrm(ks[12], (DEPTH, 2, SSM_GROUPS, SSM_STATE), 0.01),
        'ssm_log_step': jax.random.uniform(ks[13], (DEPTH, 2, SSM_GROUPS), f32,
                                           minval=math.log(1e-3), maxval=math.log(1e-1)),
        'ssm_b_re': nrm(ks[14], (DEPTH, 2, SSM_GROUPS, SSM_STATE, SSM_GROUP), (2 * SSM_GROUP) ** -0.5),
        'ssm_b_im': nrm(ks[15], (DEPTH, 2, SSM_GROUPS, SSM_STATE, SSM_GROUP), (2 * SSM_GROUP) ** -0.5),
        'ssm_c_re': nrm(ks[16], (DEPTH, 2, SSM_GROUPS, SSM_GROUP, SSM_STATE), (2 * SSM_STATE) ** -0.5),
        'ssm_c_im': nrm(ks[17], (DEPTH, 2, SSM_GROUPS, SSM_GROUP, SSM_STATE), (2 * SSM_STATE) ** -0.5),
        'ssm_d': nrm(ks[18], (DEPTH, SSM_WIDTH), 0.5),
        'ssm_w_glu': nrm(ks[19], (DEPTH, SSM_WIDTH, SSM_WIDTH), SSM_WIDTH ** -0.5),
        'hyena_short_w': nrm(ks[20], (DEPTH, 3, 3 * HYENA_WIDTH), 3 ** -0.5),
        'hyena_short_b': nrm(ks[21], (DEPTH, 3 * HYENA_WIDTH), 0.02),
        'hyena_ffn_w1': nrm(ks[22], (DEPTH, FILTER_EMB, FILTER_HIDDEN), FILTER_EMB ** -0.5),
        'hyena_ffn_b1': nrm(ks[23], (DEPTH, FILTER_HIDDEN), 0.1),
        'hyena_ffn_w2': nrm(ks[24], (DEPTH, FILTER_HIDDEN, FILTER_HIDDEN), FILTER_HIDDEN ** -0.5),
        'hyena_ffn_b2': nrm(ks[25], (DEPTH, FILTER_HIDDEN), 0.1),
        'hyena_ffn_w3': nrm(ks[26], (DEPTH, FILTER_HIDDEN, HYENA_ORDER * 2 * HYENA_WIDTH), FILTER_HIDDEN ** -0.5),
        'hyena_log_decay': decay0 + nrm(ks[27], (DEPTH, HYENA_ORDER, 2, HYENA_WIDTH), 0.05),
        'hyena_bias': nrm(ks[28], (DEPTH, HYENA_ORDER, HYENA_WIDTH), 1.0),
        'branch_norm': 1.0 + nrm(ks[29], (DEPTH, MIX_WIDTH), 0.02),
        'w_out': nrm(ks[30], (DEPTH, MIX_WIDTH, D_MODEL), MIX_WIDTH ** -0.5),
    }


def reference(x_prompt, x_sample, mem_prompt, mem_sample, pre_norm, post_norm, w_in, q_norm, k_norm,
              mem_norm, w_mem_kv, ssm_a_re, ssm_a_im, ssm_log_step, ssm_b_re, ssm_b_im, ssm_c_re, ssm_c_im,
              ssm_d, ssm_w_glu, hyena_short_w, hyena_short_b, hyena_ffn_w1, hyena_ffn_b1, hyena_ffn_w2,
              hyena_ffn_b2, hyena_ffn_w3, hyena_log_decay, hyena_bias, branch_norm, w_out):
    def run_group(x, mem):
        length = x.shape[1]
        cos, sin = axial_rope_tables(length)
        for layer in range(DEPTH):
            kf = hyena_filters(length, hyena_ffn_w1[layer], hyena_ffn_b1[layer], hyena_ffn_w2[layer],
                               hyena_ffn_b2[layer], hyena_ffn_w3[layer], hyena_log_decay[layer])
            x = mixer_layer(x, mem, cos, sin, kf, pre_norm[layer], post_norm[layer], w_in[layer],
                            q_norm[layer], k_norm[layer], mem_norm[layer], w_mem_kv[layer],
                            ssm_a_re[layer], ssm_a_im[layer], ssm_log_step[layer], ssm_b_re[layer],
                            ssm_b_im[layer], ssm_c_re[layer], ssm_c_im[layer], ssm_d[layer], ssm_w_glu[layer],
                            hyena_short_w[layer], hyena_short_b[layer], hyena_bias[layer],
                            branch_norm[layer], w_out[layer])
        return x

    y_prompt = run_group(x_prompt, mem_prompt)
    y_sample = run_group(x_sample, mem_sample)
    return (y_prompt, y_sample)
```

```python
import functools
import math

import jax
import jax.numpy as jnp
import numpy as np
from jax import lax
from jax.experimental import pallas as pl
from jax.experimental.pallas import tpu as pltpu

F32 = jnp.float32
BF16 = jnp.bfloat16

D_MODEL = 1024
DEPTH = 2
GRID_W = 64
HEAD_DIM = 64
ATTN_HEADS = 8
ATTN_KV_HEADS = 2
ATTN_GROUP = ATTN_HEADS // ATTN_KV_HEADS
ATTN_WIDTH = ATTN_HEADS * HEAD_DIM
KV_WIDTH = ATTN_KV_HEADS * HEAD_DIM
ROPE_THETA = 10000.0
ROPE_FREQS = HEAD_DIM // 4
SSM_GROUP = 16
SSM_GROUPS = 24
SSM_WIDTH = SSM_GROUP * SSM_GROUPS
SSM_STATE = 64
HYENA_WIDTH = 384
HYENA_ORDER = 2
FILTER_BANDS = 16
FILTER_HIDDEN = 64
MEM_TOKENS = 256
MEM_HEADS = 4
MEM_WIDTH = MEM_HEADS * HEAD_DIM
MIX_WIDTH = ATTN_WIDTH + SSM_WIDTH + HYENA_WIDTH + MEM_WIDTH
EPS = 1e-6

TOK_WIDTH = 2 * ATTN_WIDTH + 2 * KV_WIDTH + MEM_WIDTH
CH_WIDTH = 2 * SSM_WIDTH + (HYENA_ORDER + 2) * HYENA_WIDTH

LANES = 128
SUBLANES = 8
VMEM_LIMIT = 56 * 1024 * 1024
TOKEN_TILE = 512
ATTN_Q_TILE = 256
ATTN_K_TILE = 512
SSM_CHUNK = LANES
DFT_INNER = 256
HYENA_CH_BLOCK = 8
FILTER_LANE_TILE = 2048


def _cparams(sem):
    return pltpu.CompilerParams(dimension_semantics=sem, vmem_limit_bytes=VMEM_LIMIT)


def _silu(x):
    return x * (1.0 / (1.0 + jnp.exp(-x)))


def _sigmoid(x):
    return 1.0 / (1.0 + jnp.exp(-x))


def _nt_dot(a, b):
    return lax.dot_general(a, b, (((1,), (1,)), ((), ())), preferred_element_type=F32)


def _rope_128(xn, cos, s_lo, s_hi):
    outs = []
    for c in range(xn.shape[1] // LANES):
        xc = xn[:, LANES * c:LANES * (c + 1)]
        outs.append(xc * cos + pltpu.roll(xc, LANES - ROPE_FREQS, 1) * s_lo + pltpu.roll(xc, ROPE_FREQS, 1) * s_hi)
    return outs[0] if len(outs) == 1 else jnp.concatenate(outs, axis=1)


def _inproj_kernel(x_ref, pre_g_ref, wtok_ref, wch_ref, cos_ref, slo_ref, shi_ref, qg_ref, kg_ref, ones_ref,
                   q_ref, k_ref, v_ref, ga_ref, mq_ref, su_ref, sg_ref, hi_ref, hg_ref):
    x = x_ref[...]
    h = x * lax.rsqrt(jnp.mean(x * x, axis=-1, keepdims=True) + EPS) * pre_g_ref[...]
    hb = h.astype(BF16)
    tok = jnp.dot(hb, wtok_ref[...], preferred_element_type=F32)
    q = tok[:, 0:ATTN_WIDTH]
    k = tok[:, ATTN_WIDTH:ATTN_WIDTH + KV_WIDTH]
    v = tok[:, ATTN_WIDTH + KV_WIDTH:ATTN_WIDTH + 2 * KV_WIDTH]
    ga = tok[:, ATTN_WIDTH + 2 * KV_WIDTH:2 * ATTN_WIDTH + 2 * KV_WIDTH]
    mq = tok[:, 2 * ATTN_WIDTH + 2 * KV_WIDTH:]
    ones = ones_ref[...]
    q_ms = jnp.dot((q * q).astype(BF16), ones, preferred_element_type=F32) * (1.0 / HEAD_DIM)
    k_ms = jnp.dot((k * k).astype(BF16), ones[:KV_WIDTH, :KV_WIDTH], preferred_element_type=F32) * (1.0 / HEAD_DIM)
    qn = q * lax.rsqrt(q_ms + EPS) * qg_ref[...]
    kn = k * lax.rsqrt(k_ms + EPS) * kg_ref[...]
    cos, s_lo, s_hi = cos_ref[...], slo_ref[...], shi_ref[...]
    qr = _rope_128(qn, cos, s_lo, s_hi) * (HEAD_DIM ** -0.5)
    kr = _rope_128(kn, cos, s_lo, s_hi)
    q_ref[...] = qr.astype(BF16)
    for j in range(ATTN_KV_HEADS):
        k_ref[j] = kr[:, HEAD_DIM * j:HEAD_DIM * (j + 1)].astype(BF16)
        v_ref[j] = v[:, HEAD_DIM * j:HEAD_DIM * (j + 1)].astype(BF16)
    ga_ref[...] = _silu(ga)
    mq_ref[...] = (mq * (HEAD_DIM ** -0.5)).astype(BF16)
    ch = _nt_dot(wch_ref[...], hb)
    su_ref[...] = ch[0:SSM_WIDTH]
    sg_ref[...] = _silu(ch[SSM_WIDTH:2 * SSM_WIDTH])
    hi_ref[...] = ch[2 * SSM_WIDTH:2 * SSM_WIDTH + 3 * HYENA_WIDTH]
    hg_ref[...] = _silu(ch[2 * SSM_WIDTH + 3 * HYENA_WIDTH:])


def _inproj(x2, pre_g, w_tok, w_ch_t, cos, s_lo, s_hi, qg, kg, ones, length):
    n = x2.shape[0]
    tm = TOKEN_TILE
    nt = n // tm
    tiles_per_seq = length // tm
    tok_spec = lambda w: pl.BlockSpec((tm, w), lambda i: (i, 0))
    ch_spec = lambda w: pl.BlockSpec((w, tm), lambda i: (0, i))
    const = lambda shape: pl.BlockSpec(shape, lambda i: (0,) * len(shape))
    pos_spec = pl.BlockSpec((tm, LANES), lambda i: (i % tiles_per_seq, 0))
    return pl.pallas_call(
        _inproj_kernel,
        grid=(nt,),
        in_specs=[tok_spec(D_MODEL), const((1, D_MODEL)), const((D_MODEL, TOK_WIDTH)), const((CH_WIDTH, D_MODEL)),
                  pos_spec, pos_spec, pos_spec, const((1, ATTN_WIDTH)), const((1, KV_WIDTH)),
                  const((ATTN_WIDTH, ATTN_WIDTH))],
        out_specs=[tok_spec(ATTN_WIDTH),
                   pl.BlockSpec((ATTN_KV_HEADS, tm, HEAD_DIM), lambda i: (0, i, 0)),
                   pl.BlockSpec((ATTN_KV_HEADS, tm, HEAD_DIM), lambda i: (0, i, 0)),
                   tok_spec(ATTN_WIDTH), tok_spec(MEM_WIDTH),
                   ch_spec(SSM_WIDTH), ch_spec(SSM_WIDTH), ch_spec(3 * HYENA_WIDTH), ch_spec(HYENA_WIDTH)],
        out_shape=[jax.ShapeDtypeStruct((n, ATTN_WIDTH), BF16),
                   jax.ShapeDtypeStruct((ATTN_KV_HEADS, n, HEAD_DIM), BF16),
                   jax.ShapeDtypeStruct((ATTN_KV_HEADS, n, HEAD_DIM), BF16),
                   jax.ShapeDtypeStruct((n, ATTN_WIDTH), F32),
                   jax.ShapeDtypeStruct((n, MEM_WIDTH), BF16),
                   jax.ShapeDtypeStruct((SSM_WIDTH, n), F32),
                   jax.ShapeDtypeStruct((SSM_WIDTH, n), F32),
                   jax.ShapeDtypeStruct((3 * HYENA_WIDTH, n), F32),
                   jax.ShapeDtypeStruct((HYENA_WIDTH, n), F32)],
        compiler_params=_cparams(("parallel",)),
        name="inproj",
    )(x2, pre_g, w_tok, w_ch_t, cos, s_lo, s_hi, qg, kg, ones)


def _flash_kernel(q_ref, k_ref, v_ref, o_ref, q4_s, m_s, l_s, acc_s, *, tq, n_kv):
    j = pl.program_id(3)

    @pl.when(j == 0)
    def _():
        for h in range(ATTN_GROUP):
            q4_s[h * tq:(h + 1) * tq, :] = q_ref[:, HEAD_DIM * h:HEAD_DIM * (h + 1)]
        m_s[...] = jnp.full(m_s.shape, -jnp.inf, F32)
        l_s[...] = jnp.zeros(l_s.shape, F32)
        acc_s[...] = jnp.zeros(acc_s.shape, F32)

    s = _nt_dot(q4_s[...], k_ref[0])
    m_prev = m_s[...]
    m_new = jnp.maximum(m_prev, jnp.max(s, axis=1, keepdims=True))
    alpha = jnp.exp(m_prev - m_new)
    p = jnp.exp(s - m_new)
    l_s[...] = alpha * l_s[...] + jnp.sum(p, axis=1, keepdims=True)
    acc_s[...] = alpha * acc_s[...] + jnp.dot(p.astype(BF16), v_ref[0], preferred_element_type=F32)
    m_s[...] = m_new

    @pl.when(j == n_kv - 1)
    def _():
        o = acc_s[...] * (1.0 / l_s[...])
        o_ref[...] = jnp.concatenate([o[h * tq:(h + 1) * tq] for h in range(ATTN_GROUP)], axis=1)


def _flash_attention(q, k, v, batch, length):
    n = q.shape[0]
    tq, tk = ATTN_Q_TILE, ATTN_K_TILE
    nq, nk = length // tq, length // tk
    gw = ATTN_GROUP * HEAD_DIM
    return pl.pallas_call(
        functools.partial(_flash_kernel, tq=tq, n_kv=nk),
        grid=(batch, ATTN_KV_HEADS, nq, nk),
        in_specs=[pl.BlockSpec((tq, gw), lambda b, h, i, j: (b * nq + i, h)),
                  pl.BlockSpec((1, tk, HEAD_DIM), lambda b, h, i, j: (h, b * nk + j, 0)),
                  pl.BlockSpec((1, tk, HEAD_DIM), lambda b, h, i, j: (h, b * nk + j, 0))],
        out_specs=pl.BlockSpec((tq, gw), lambda b, h, i, j: (b * nq + i, h)),
        out_shape=jax.ShapeDtypeStruct((n, ATTN_WIDTH), F32),
        scratch_shapes=[pltpu.VMEM((ATTN_GROUP * tq, HEAD_DIM), BF16),
                        pltpu.VMEM((ATTN_GROUP * tq, 1), F32),
                        pltpu.VMEM((ATTN_GROUP * tq, 1), F32),
                        pltpu.VMEM((ATTN_GROUP * tq, HEAD_DIM), F32)],
        compiler_params=_cparams(("parallel", "parallel", "parallel", "arbitrary")),
        name="flash_attn",
    )(q, k, v)


def _memkv_kernel(mem_ref, g_ref, w_ref, kv_ref):
    m = mem_ref[...]
    mn = m * lax.rsqrt(jnp.mean(m * m, axis=-1, keepdims=True) + EPS) * g_ref[...]
    kv_ref[...] = jnp.dot(mn.astype(BF16), w_ref[...], preferred_element_type=F32)


def _memkv(mem2, mem_g, w_kv):
    rows = mem2.shape[0]
    return pl.pallas_call(
        _memkv_kernel,
        grid=(rows // MEM_TOKENS,),
        in_specs=[pl.BlockSpec((MEM_TOKENS, D_MODEL), lambda i: (i, 0)),
                  pl.BlockSpec((1, D_MODEL), lambda i: (0, 0)),
                  pl.BlockSpec((D_MODEL, 2 * MEM_WIDTH), lambda i: (0, 0))],
        out_specs=pl.BlockSpec((MEM_TOKENS, 2 * MEM_WIDTH), lambda i: (i, 0)),
        out_shape=jax.ShapeDtypeStruct((rows, 2 * MEM_WIDTH), F32),
        compiler_params=_cparams(("parallel",)),
        name="mem_kv",
    )(mem2, mem_g, w_kv)


def _ssm_kernel(u_ref, w_ref, q_ref, a1_ref, a2_ref, y_ref, s_s, h_s, *, batch, n_chunks):
    t = SSM_CHUNK
    gt = SSM_GROUP * t
    rows = batch * n_chunks
    half = 2 * SSM_STATE
    u = jnp.concatenate([u_ref[0, c].reshape(rows, t) for c in range(SSM_GROUP)], axis=1).astype(BF16)
    r = jnp.dot(u, w_ref[0], preferred_element_type=F32)
    s_s[...] = r[:, gt:]
    a1f, a2f = a1_ref[0, :, :half], a2_ref[0, :, :half]
    a1b, a2b = a1_ref[0, :, half:], a2_ref[0, :, half:]

    sub = SUBLANES
    n_blocks = n_chunks // sub

    def step(kb, carry):
        new = []
        for b in range(batch):
            hf, hb = carry[2 * b], carry[2 * b + 1]
            base_f = pl.multiple_of(b * n_chunks + kb * sub, sub)
            base_b = pl.multiple_of(b * n_chunks + (n_blocks - 1 - kb) * sub, sub)
            sf = s_s[pl.ds(base_f, sub), 0:half]
            sb = s_s[pl.ds(base_b, sub), half:2 * half]
            hf_rows, hb_rows = [], [None] * sub
            for i in range(sub):
                hf_rows.append(hf)
                hf = a1f * hf + a2f * pltpu.roll(hf, SSM_STATE, 1) + sf[i:i + 1]
            for i in range(sub - 1, -1, -1):
                hb_rows[i] = hb
                hb = a1b * hb + a2b * pltpu.roll(hb, SSM_STATE, 1) + sb[i:i + 1]
            h_s[pl.ds(base_f, sub), 0:half] = jnp.concatenate(hf_rows, axis=0)
            h_s[pl.ds(base_b, sub), half:2 * half] = jnp.concatenate(hb_rows, axis=0)
            new += [hf, hb]
        return tuple(new)

    zero = jnp.zeros((1, half), F32)
    lax.fori_loop(0, n_blocks, step, (zero,) * (2 * batch))
    y = r[:, :gt] + jnp.dot(h_s[...].astype(BF16), q_ref[0], preferred_element_type=F32)
    for c in range(SSM_GROUP):
        y_ref[0, c] = y[:, c * t:(c + 1) * t].reshape(batch, n_chunks, t)


def _ssm_scan(u_t, w_gp, q_mat, a1, a2, batch, length):
    t = SSM_CHUNK
    nk = length // t
    gt = SSM_GROUP * t
    u5 = u_t.reshape(SSM_GROUPS, SSM_GROUP, batch, nk, t)
    blk = (1, SSM_GROUP, batch, nk, t)
    y5 = pl.pallas_call(
        functools.partial(_ssm_kernel, batch=batch, n_chunks=nk),
        grid=(SSM_GROUPS,),
        in_specs=[pl.BlockSpec(blk, lambda g: (g, 0, 0, 0, 0)),
                  pl.BlockSpec((1, gt, gt + 4 * SSM_STATE), lambda g: (g, 0, 0)),
                  pl.BlockSpec((1, 4 * SSM_STATE, gt), lambda g: (g, 0, 0)),
                  pl.BlockSpec((1, 1, 4 * SSM_STATE), lambda g: (g, 0, 0)),
                  pl.BlockSpec((1, 1, 4 * SSM_STATE), lambda g: (g, 0, 0))],
        out_specs=pl.BlockSpec(blk, lambda g: (g, 0, 0, 0, 0)),
        out_shape=jax.ShapeDtypeStruct(u5.shape, F32),
        scratch_shapes=[pltpu.VMEM((batch * nk, 4 * SSM_STATE), F32),
                        pltpu.VMEM((batch * nk, 4 * SSM_STATE), F32)],
        compiler_params=_cparams(("parallel",)),
        name="ssm_scan",
    )(u5, w_gp, q_mat, a1, a2)
    return y5.reshape(SSM_WIDTH, batch * length)


def _ssm_tables(a_re, a_im, log_step, b_re, b_im, c_re, c_im):
    t = SSM_CHUNK
    hi = lax.Precision.HIGHEST
    lam = lax.complex(a_re.astype(F32), a_im.astype(F32))
    step = jnp.exp(log_step.astype(F32))[..., None]
    ls = lam * step
    a_bar = jnp.exp(ls)
    b_bar = ((a_bar - 1.0) / lam)[..., None] * lax.complex(b_re.astype(F32), b_im.astype(F32))
    c = lax.complex(c_re.astype(F32), c_im.astype(F32))
    tau = jnp.arange(t + 1, dtype=F32)
    pw = jnp.exp(ls[..., None] * tau)
    kern = jnp.einsum('dgcp,dgpt,dgpe->dgtce', c, pw[..., :t], b_bar, precision=hi).real
    kf, kb = kern[0], kern[1]
    kcat = jnp.concatenate([kb[:, :0:-1], kf[:, :1] + kb[:, :1], kf[:, 1:]], axis=1)
    idx = jnp.arange(t)[None, :] - jnp.arange(t)[:, None] + (t - 1)
    g5 = kcat[:, idx]
    g_mat = g5.transpose(0, 4, 1, 3, 2).reshape(SSM_GROUPS, SSM_GROUP * t, SSM_GROUP * t)
    pw_f, pw_b = pw[0], pw[1]
    pf = pw_f[:, :, t - 1 - jnp.arange(t)][..., None] * b_bar[0][:, :, None, :]
    pb = pw_b[:, :, :t][..., None] * b_bar[1][:, :, None, :]
    to_rows = lambda z: z.transpose(0, 3, 2, 1).reshape(SSM_GROUPS, SSM_GROUP * t, SSM_STATE)
    p_mat = jnp.concatenate([to_rows(pf.real), to_rows(pf.imag), to_rows(pb.real), to_rows(pb.imag)], axis=-1)
    qf = c[0].transpose(0, 2, 1)[..., None] * pw_f[:, :, 1:][:, :, None, :]
    qb = c[1].transpose(0, 2, 1)[..., None] * pw_b[:, :, t - jnp.arange(t)][:, :, None, :]
    to_cols = lambda z: z.reshape(SSM_GROUPS, SSM_STATE, SSM_GROUP * t)
    q_mat = jnp.concatenate([to_cols(qf.real), -to_cols(qf.imag), to_cols(qb.real), -to_cols(qb.imag)], axis=1)
    at = pw[..., t]
    a1 = jnp.concatenate([at[0].real, at[0].real, at[1].real, at[1].real], axis=-1)[:, None, :]
    a2 = jnp.concatenate([-at[0].imag, at[0].imag, -at[1].imag, at[1].imag], axis=-1)[:, None, :]
    w_gp = jnp.concatenate([g_mat, p_mat], axis=-1).astype(BF16)
    return w_gp, q_mat.astype(BF16), a1, a2


def _dft_tables(length):
    j = DFT_INNER
    n = 2 * length
    rn = n // j
    rh = rn // 2
    k1 = jnp.arange(rn, dtype=jnp.int32)
    ang1 = (2.0 * math.pi / rn) * ((k1[:, None] * k1[None, :]) % rn).astype(F32)
    c1, s1 = jnp.cos(ang1), jnp.sin(ang1)
    f1_full = jnp.concatenate([c1, -s1], axis=0)
    f1_top = f1_full[:, :rh]
    f1_inv = jnp.concatenate([c1[:rh], -s1[:rh]], axis=1)
    jj = jnp.arange(j, dtype=jnp.int32)
    angt = (2.0 * math.pi / n) * (k1[:, None] * jj[None, :]).astype(F32)
    tw_re, tw_im = jnp.cos(angt), -jnp.sin(angt)
    ang2 = (2.0 * math.pi / j) * ((jj[:, None] * jj[None, :]) % j).astype(F32)
    f2_cat = jnp.concatenate([jnp.cos(ang2), -jnp.sin(ang2)], axis=1)
    return dict(f1_full=f1_full.astype(BF16), f1_top=f1_top.astype(BF16), f1_inv=f1_inv.astype(BF16),
                tw_re=tw_re, tw_im=tw_im, f2_cat=f2_cat.astype(BF16), rn=rn, rh=rh)


def _dft_forward(a, tw_re, tw_im, f2_cat, rn):
    j = DFT_INNER
    a_re, a_im = a[:rn], a[rn:]
    ap = jnp.concatenate([a_re * tw_re - a_im * tw_im, a_re * tw_im + a_im * tw_re], axis=0).astype(BF16)
    m = jnp.dot(ap, f2_cat, preferred_element_type=F32)
    return m[:rn, :j] - m[rn:, j:], m[:rn, j:] + m[rn:, :j]


def _dft_inverse(z_re, z_im, tw_re, tw_im, f2_cat, f1_inv, rn):
    j = DFT_INNER
    z = jnp.concatenate([z_re, z_im], axis=0).astype(BF16)
    m = jnp.dot(z, f2_cat, preferred_element_type=F32)
    b_re = m[:rn, :j] + m[rn:, j:]
    b_im = m[rn:, :j] - m[:rn, j:]
    bp = jnp.concatenate([b_re * tw_re + b_im * tw_im, b_im * tw_re - b_re * tw_im], axis=0).astype(BF16)
    return jnp.dot(f1_inv, bp, preferred_element_type=F32)


def _filter_hidden_kernel(bands_ref, w1t_ref, w1c_ref, w1s_ref, b1_ref, w2t_ref, b2_ref, hid_ref, *, length, tl):
    hi = lax.Precision.HIGHEST
    base = pl.program_id(0) * tl
    idx = (lax.broadcasted_iota(jnp.int32, (1, tl), 1) + base).astype(F32)
    for d in range(2):
        pos = idx if d == 0 else float(length) - idx
        tt = pos / float(length)
        wpos = (2.0 * math.pi / length) * pos
        arg = bands_ref[...] * wpos
        h1 = (w1t_ref[...] * tt
              + jnp.dot(w1c_ref[...], jnp.cos(arg), preferred_element_type=F32, precision=hi)
              - jnp.dot(w1s_ref[...], jnp.sin(arg), preferred_element_type=F32, precision=hi))
        h1 = jnp.sin(h1 + b1_ref[...])
        h2 = jnp.dot(w2t_ref[...], h1, preferred_element_type=F32, precision=hi)
        hid_ref[d] = jnp.sin(h2 + b2_ref[...])


def _filter_taps_kernel(hid_ref, w3t_ref, decay_ref, filt_ref, *, length, tl):
    hi = lax.Precision.HIGHEST
    d = pl.program_id(0) % 2
    base = pl.program_id(1) * tl
    idx = (lax.broadcasted_iota(jnp.int32, (1, tl), 1) + base).astype(F32)
    pos = jnp.where(d == 0, idx, float(length) - idx)
    tt = pos / float(length)
    f = jnp.dot(w3t_ref[...], hid_ref[0], preferred_element_type=F32, precision=hi)
    f = f * jnp.exp(-tt * decay_ref[...])
    filt_ref[...] = jnp.where(jnp.logical_and(d == 1, idx == 0.0), 0.0, f)


def _filter_dft_kernel(filt_ref, f1_ref, twre_ref, twim_ref, f2_ref, kf_ref, *, rn, cb):
    for c in range(cb):
        r = jnp.concatenate([filt_ref[0, 0, c], filt_ref[0, 1, c]], axis=0).astype(BF16)
        a = jnp.dot(f1_ref[...], r, preferred_element_type=F32)
        x_re, x_im = _dft_forward(a, twre_ref[...], twim_ref[...], f2_ref[...], rn)
        kf_ref[0, c, 0] = x_re
        kf_ref[0, c, 1] = x_im


def _hyena_filters(length, tabs, w1, b1, w2, b2, w3, log_decay):
    tl = min(FILTER_LANE_TILE, length)
    nl = length // tl
    rn, rh, j = tabs["rn"], tabs["rh"], DFT_INNER
    bands = jnp.linspace(1e-4, FILTER_BANDS - 1, FILTER_BANDS, dtype=F32)[:, None]
    w1f = w1.astype(F32)
    const = lambda shape: pl.BlockSpec(shape, lambda *_: (0,) * len(shape))
    hid = pl.pallas_call(
        functools.partial(_filter_hidden_kernel, length=length, tl=tl),
        grid=(nl,),
        in_specs=[const((FILTER_BANDS, 1)), const((FILTER_HIDDEN, 1)), const((FILTER_HIDDEN, FILTER_BANDS)),
                  const((FILTER_HIDDEN, FILTER_BANDS)), const((FILTER_HIDDEN, 1)),
                  const((FILTER_HIDDEN, FILTER_HIDDEN)), const((FILTER_HIDDEN, 1))],
        out_specs=pl.BlockSpec((2, FILTER_HIDDEN, tl), lambda i: (0, 0, i)),
        out_shape=jax.ShapeDtypeStruct((2, FILTER_HIDDEN, length), F32),
        compiler_params=_cparams(("parallel",)),
        name="hyena_filter_hidden",
    )(bands, w1f[0:1].T, w1f[1:1 + FILTER_BANDS].T, w1f[1 + FILTER_BANDS:].T, b1.astype(F32)[:, None],
      w2.astype(F32).T, b2.astype(F32)[:, None])
    n_od = HYENA_ORDER * 2
    w3t = w3.astype(F32).T.reshape(n_od, HYENA_WIDTH, FILTER_HIDDEN)
    decay = jnp.exp(log_decay.astype(F32)).reshape(n_od, HYENA_WIDTH, 1)
    filt = pl.pallas_call(
        functools.partial(_filter_taps_kernel, length=length, tl=tl),
        grid=(n_od, nl),
        in_specs=[pl.BlockSpec((1, FILTER_HIDDEN, tl), lambda od, i: (od % 2, 0, i)),
                  pl.BlockSpec((None, HYENA_WIDTH, FILTER_HIDDEN), lambda od, i: (od, 0, 0)),
                  pl.BlockSpec((None, HYENA_WIDTH, 1), lambda od, i: (od, 0, 0))],
        out_specs=pl.BlockSpec((None, HYENA_WIDTH, tl), lambda od, i: (od, 0, i)),
        out_shape=jax.ShapeDtypeStruct((n_od, HYENA_WIDTH, length), F32),
        compiler_params=_cparams(("parallel", "parallel")),
        name="hyena_filter_taps",
    )(hid, w3t, decay)
    filt6 = filt.reshape(HYENA_ORDER, 2, HYENA_WIDTH, rh, j)
    cb = HYENA_CH_BLOCK
    return pl.pallas_call(
        functools.partial(_filter_dft_kernel, rn=rn, cb=cb),
        grid=(HYENA_ORDER, HYENA_WIDTH // cb),
        in_specs=[pl.BlockSpec((1, 2, cb, rh, j), lambda o, c: (o, 0, c, 0, 0)),
                  const((2 * rn, rn)), const((rn, j)), const((rn, j)), const((j, 2 * j))],
        out_specs=pl.BlockSpec((1, cb, 2, rn, j), lambda o, c: (o, c, 0, 0, 0)),
        out_shape=jax.ShapeDtypeStruct((HYENA_ORDER, HYENA_WIDTH, 2, rn, j), F32),
        compiler_params=_cparams(("parallel", "parallel")),
        name="hyena_filter_dft",
    )(filt6, tabs["f1_full"], tabs["tw_re"], tabs["tw_im"], tabs["f2_cat"])


def _hyena_kernel(scw_ref, scb_ref, hb_ref, x_ref, hg_ref, kf_ref, f1_ref, f1i_ref, twre_ref, twim_ref, f2_ref,
                  o_ref, *, batch, rn, rh, cb):
    j = DFT_INNER
    n_inv = 1.0 / (rn * j)
    lane = lax.broadcasted_iota(jnp.int32, (rh, j), 1)
    row = lax.broadcasted_iota(jnp.int32, (rh, j), 0)
    first = jnp.logical_and(lane == 0, row == 0)
    last = jnp.logical_and(lane == j - 1, row == rh - 1)
    tw_re, tw_im = twre_ref[...], twim_ref[...]

    def prev_t(x):
        p = pltpu.roll(x, 1, 1)
        return jnp.where(first, 0.0, jnp.where(lane == 0, pltpu.roll(p, 1, 0), p))

    def next_t(x):
        p = pltpu.roll(x, j - 1, 1)
        return jnp.where(last, 0.0, jnp.where(lane == j - 1, pltpu.roll(p, rh - 1, 0), p))

    def long_conv(xs, k_re, k_im):
        outs = []
        for x in xs:
            a = jnp.dot(f1_ref[...], x.astype(BF16), preferred_element_type=F32)
            x_re, x_im = _dft_forward(a, tw_re, tw_im, f2_ref[...], rn)
            y = _dft_inverse(x_re * k_re - x_im * k_im, x_re * k_im + x_im * k_re,
                             tw_re, tw_im, f2_ref[...], f1i_ref[...], rn)
            outs.append(y * n_inv)
        return outs

    cbase = pl.program_id(0) * cb

    def channel(ci, carry):
        ch = cbase + ci
        segs = []
        for sgm in range(3):
            chs = sgm * HYENA_WIDTH + ch
            w0, w1, w2 = scw_ref[chs], scw_ref[3 * HYENA_WIDTH + chs], scw_ref[6 * HYENA_WIDTH + chs]
            bias = scb_ref[chs]
            seg = []
            for b in range(batch):
                x = x_ref[sgm, ci, b]
                seg.append(prev_t(x) * w0 + x * w1 + next_t(x) * w2 + bias)
            segs.append(seg)
        z = segs[0]
        for o in range(HYENA_ORDER):
            conv = long_conv(z, kf_ref[o, ci, 0], kf_ref[o, ci, 1])
            hbias = hb_ref[o * HYENA_WIDTH + ch]
            z = [segs[o + 1][b] * (conv[b] + z[b] * hbias) for b in range(batch)]
        for b in range(batch):
            o_ref[ci, b] = z[b] * hg_ref[ci, b]
        return carry

    lax.fori_loop(0, cb, channel, 0)


def _hyena(hi_t, hg_t, kf, tabs, short_w, short_b, hy_bias, batch, length):
    rn, rh, j = tabs["rn"], tabs["rh"], DFT_INNER
    cb = HYENA_CH_BLOCK
    x5 = hi_t.reshape(3, HYENA_WIDTH, batch, rh, j)
    g4 = hg_t.reshape(HYENA_WIDTH, batch, rh, j)
    smem = pl.BlockSpec(memory_space=pltpu.SMEM)
    const = lambda shape: pl.BlockSpec(shape, lambda c: (0,) * len(shape))
    out = pl.pallas_call(
        functools.partial(_hyena_kernel, batch=batch, rn=rn, rh=rh, cb=cb),
        grid=(HYENA_WIDTH // cb,),
        in_specs=[smem, smem, smem,
                  pl.BlockSpec((3, cb, batch, rh, j), lambda c: (0, c, 0, 0, 0)),
                  pl.BlockSpec((cb, batch, rh, j), lambda c: (c, 0, 0, 0)),
                  pl.BlockSpec((HYENA_ORDER, cb, 2, rn, j), lambda c: (0, c, 0, 0, 0)),
                  const((2 * rn, rh)), const((rh, 2 * rn)), const((rn, j)), const((rn, j)), const((j, 2 * j))],
        out_specs=pl.BlockSpec((cb, batch, rh, j), lambda c: (c, 0, 0, 0)),
        out_shape=jax.ShapeDtypeStruct(g4.shape, F32),
        compiler_params=_cparams(("parallel",)),
        name="hyena_conv",
    )(short_w.astype(F32).reshape(-1), short_b.astype(F32), hy_bias.astype(F32).reshape(-1),
      x5, g4, kf, tabs["f1_top"], tabs["f1_inv"], tabs["tw_re"], tabs["tw_im"], tabs["f2_cat"])
    return out.reshape(HYENA_WIDTH, batch * length)


def _rms_rows(x, g):
    return x * lax.rsqrt(jnp.mean(x * x, axis=-1, keepdims=True) + EPS) * g


def _rms_cols(x, g):
    return x * lax.rsqrt(jnp.mean(x * x, axis=0, keepdims=True) + EPS) * g


def _post_kernel(ao_ref, ga_ref, mq_ref, mk_ref, mv_ref, y_ref, u_ref, sg_ref, hy_ref, x_ref,
                 d_ref, wglu_ref, g_attn_ref, g_ssm_ref, g_hy_ref, g_mem_ref, post_g_ref, wo_ref, out_ref):
    attn_n = _rms_rows(ao_ref[...] * ga_ref[...], g_attn_ref[...])
    s = jnp.dot(mq_ref[...], mk_ref[0], preferred_element_type=F32)
    ps = []
    for h in range(MEM_HEADS):
        sh = s[:, MEM_TOKENS * h:MEM_TOKENS * (h + 1)]
        e = jnp.exp(sh - jnp.max(sh, axis=-1, keepdims=True))
        ps.append(e * (1.0 / jnp.sum(e, axis=-1, keepdims=True)))
    p = jnp.concatenate(ps, axis=1).astype(BF16)
    cross_n = _rms_rows(jnp.dot(p, mv_ref[0], preferred_element_type=F32), g_mem_ref[...])
    y = y_ref[...] + d_ref[...] * u_ref[...]
    g = y * (0.5 * (1.0 + jnp.tanh(math.sqrt(2.0 / math.pi) * (y + 0.044715 * (y * y * y)))))
    gz = jnp.dot(wglu_ref[...], g.astype(BF16), preferred_element_type=F32)
    ssm_n = _rms_cols(g * _sigmoid(gz) * sg_ref[...], g_ssm_ref[...])
    hy_n = _rms_cols(hy_ref[...], g_hy_ref[...])
    o1, o2, o3 = ATTN_WIDTH, ATTN_WIDTH + SSM_WIDTH, ATTN_WIDTH + SSM_WIDTH + HYENA_WIDTH
    mixed = (jnp.dot(attn_n.astype(BF16), wo_ref[0:o1], preferred_element_type=F32)
             + jnp.dot(ssm_n.T.astype(BF16), wo_ref[o1:o2], preferred_element_type=F32)
             + jnp.dot(hy_n.T.astype(BF16), wo_ref[o2:o3], preferred_element_type=F32)
             + jnp.dot(cross_n.astype(BF16), wo_ref[o3:], preferred_element_type=F32))
    out_ref[...] = x_ref[...] + _rms_rows(mixed, post_g_ref[...])


def _post(ao, ga, mq, mk_bd, mv_bd, y_t, u_t, sg_t, hy_t, x2, d, wglu_t, g_attn, g_ssm, g_hy, g_mem, post_g, wo,
          length):
    n = x2.shape[0]
    tm = TOKEN_TILE
    tiles_per_seq = length // tm
    tok_spec = lambda w: pl.BlockSpec((tm, w), lambda i: (i, 0))
    ch_spec = lambda w: pl.BlockSpec((w, tm), lambda i: (0, i))
    const = lambda shape: pl.BlockSpec(shape, lambda i: (0,) * len(shape))
    hm = MEM_HEADS * MEM_TOKENS
    return pl.pallas_call(
        _post_kernel,
        grid=(n // tm,),
        in_specs=[tok_spec(ATTN_WIDTH), tok_spec(ATTN_WIDTH), tok_spec(MEM_WIDTH),
                  pl.BlockSpec((1, MEM_WIDTH, hm), lambda i: (i // tiles_per_seq, 0, 0)),
                  pl.BlockSpec((1, hm, MEM_WIDTH), lambda i: (i // tiles_per_seq, 0, 0)),
                  ch_spec(SSM_WIDTH), ch_spec(SSM_WIDTH), ch_spec(SSM_WIDTH), ch_spec(HYENA_WIDTH),
                  tok_spec(D_MODEL),
                  const((SSM_WIDTH, 1)), const((SSM_WIDTH, SSM_WIDTH)),
                  const((1, ATTN_WIDTH)), const((SSM_WIDTH, 1)), const((HYENA_WIDTH, 1)), const((1, MEM_WIDTH)),
                  const((1, D_MODEL)), const((MIX_WIDTH, D_MODEL))],
        out_specs=tok_spec(D_MODEL),
        out_shape=jax.ShapeDtypeStruct((n, D_MODEL), F32),
        compiler_params=_cparams(("parallel",)),
        name="post",
    )(ao, ga, mq, mk_bd, mv_bd, y_t, u_t, sg_t, hy_t, x2, d, wglu_t, g_attn, g_ssm, g_hy, g_mem, post_g, wo)


def _rope_tables(length):
    rows = length // GRID_W
    row = jnp.broadcast_to(jnp.arange(rows, dtype=F32)[:, None], (rows, GRID_W)).reshape(length)
    col = jnp.broadcast_to(jnp.arange(GRID_W, dtype=F32)[None, :], (rows, GRID_W)).reshape(length)
    inv_freq = ROPE_THETA ** (-jnp.arange(ROPE_FREQS, dtype=F32) / ROPE_FREQS)
    ang = jnp.stack([row[:, None] * inv_freq, col[:, None] * inv_freq], axis=1)
    ang = jnp.broadcast_to(ang[:, :, None, :], (length, 2, 2, ROPE_FREQS)).reshape(length, HEAD_DIM)
    ang = jnp.concatenate([ang, ang], axis=1)
    cos, sin = jnp.cos(ang), jnp.sin(ang)
    low = (jnp.arange(LANES) % (2 * ROPE_FREQS)) < ROPE_FREQS
    return cos, jnp.where(low, -sin, 0.0), jnp.where(low, 0.0, sin)


def _block_diag_heads(mk, mv, batch):
    mk4 = mk.reshape(batch, MEM_TOKENS, MEM_HEADS, HEAD_DIM)
    mv4 = mv.reshape(batch, MEM_TOKENS, MEM_HEADS, HEAD_DIM)
    eye = jnp.eye(MEM_HEADS, dtype=mk.dtype)
    k_bd = jnp.einsum('bmhd,hg->bhdgm', mk4, eye).reshape(batch, MEM_WIDTH, MEM_HEADS * MEM_TOKENS)
    v_bd = jnp.einsum('bmhd,hg->bhmgd', mv4, eye).reshape(batch, MEM_HEADS * MEM_TOKENS, MEM_WIDTH)
    return k_bd.astype(BF16), v_bd.astype(BF16)


def _layer_weights(layer, p):
    w_in = p["w_in"][layer]
    a, kv = ATTN_WIDTH, KV_WIDTH
    tok_end = 2 * a + 2 * kv
    ch_end = tok_end + CH_WIDTH
    w_tok = jnp.concatenate([w_in[:, :tok_end], w_in[:, ch_end:]], axis=1).astype(BF16)
    w_ch_t = w_in[:, tok_end:ch_end].T.astype(BF16)
    bg = p["branch_norm"][layer].astype(F32)
    o1, o2, o3 = a, a + SSM_WIDTH, a + SSM_WIDTH + HYENA_WIDTH
    head_id = jnp.arange(a) // HEAD_DIM
    return dict(
        w_tok=w_tok, w_ch_t=w_ch_t,
        pre_g=p["pre_norm"][layer].astype(F32)[None, :], post_g=p["post_norm"][layer].astype(F32)[None, :],
        qg=jnp.tile(p["q_norm"][layer].astype(F32), ATTN_HEADS)[None, :],
        kg=jnp.tile(p["k_norm"][layer].astype(F32), ATTN_KV_HEADS)[None, :],
        ones=(head_id[:, None] == head_id[None, :]).astype(BF16),
        mem_g=p["mem_norm"][layer].astype(F32)[None, :], w_mem_kv=p["w_mem_kv"][layer].astype(BF16),
        ssm=_ssm_tables(p["ssm_a_re"][layer], p["ssm_a_im"][layer], p["ssm_log_step"][layer], p["ssm_b_re"][layer],
                        p["ssm_b_im"][layer], p["ssm_c_re"][layer], p["ssm_c_im"][layer]),
        d=p["ssm_d"][layer].astype(F32)[:, None], wglu_t=p["ssm_w_glu"][layer].T.astype(BF16),
        g_attn=bg[None, :o1], g_ssm=bg[o1:o2, None], g_hy=bg[o2:o3, None], g_mem=bg[None, o3:],
        wo=p["w_out"][layer].astype(BF16),
    )


def _mixer_layer(x2, mem2, lw, kf, tabs, rope, p, layer, batch, length):
    cos, s_lo, s_hi = rope
    q, k, v, ga, mq, su_t, sg_t, hi_t, hg_t = _inproj(x2, lw["pre_g"], lw["w_tok"], lw["w_ch_t"], cos, s_lo, s_hi,
                                                      lw["qg"], lw["kg"], lw["ones"], length)
    ao = _flash_attention(q, k, v, batch, length)
    mem_kv = _memkv(mem2, lw["mem_g"], lw["w_mem_kv"])
    mk_bd, mv_bd = _block_diag_heads(mem_kv[:, :MEM_WIDTH], mem_kv[:, MEM_WIDTH:], batch)
    y_t = _ssm_scan(su_t, *lw["ssm"], batch, length)
    hy_t = _hyena(hi_t, hg_t, kf, tabs, p["hyena_short_w"][layer], p["hyena_short_b"][layer],
                  p["hyena_bias"][layer], batch, length)
    return _post(ao, ga, mq, mk_bd, mv_bd, y_t, su_t, sg_t, hy_t, x2, lw["d"], lw["wglu_t"],
                 lw["g_attn"], lw["g_ssm"], lw["g_hy"], lw["g_mem"], lw["post_g"], lw["wo"], length)


def _run_group(x, mem, weights, p):
    batch, length, _ = x.shape
    rope = _rope_tables(length)
    tabs = _dft_tables(length)
    x2 = x.reshape(batch * length, D_MODEL)
    mem2 = mem.reshape(batch * MEM_TOKENS, D_MODEL)
    for layer in range(DEPTH):
        kf = _hyena_filters(length, tabs, p["hyena_ffn_w1"][layer], p["hyena_ffn_b1"][layer],
                            p["hyena_ffn_w2"][layer], p["hyena_ffn_b2"][layer], p["hyena_ffn_w3"][layer],
                            p["hyena_log_decay"][layer])
        x2 = _mixer_layer(x2, mem2, weights[layer], kf, tabs, rope, p, layer, batch, length)
    return x2.reshape(batch, length, D_MODEL)


def kernel(x_prompt, x_sample, mem_prompt, mem_sample, pre_norm, post_norm, w_in, q_norm, k_norm, mem_norm, w_mem_kv, ssm_a_re, ssm_a_im, ssm_log_step, ssm_b_re, ssm_b_im, ssm_c_re, ssm_c_im, ssm_d, ssm_w_glu, hyena_short_w, hyena_short_b, hyena_ffn_w1, hyena_ffn_b1, hyena_ffn_w2, hyena_ffn_b2, hyena_ffn_w3, hyena_log_decay, hyena_bias, branch_norm, w_out):
    p = dict(pre_norm=pre_norm, post_norm=post_norm, w_in=w_in, q_norm=q_norm, k_norm=k_norm, mem_norm=mem_norm,
             w_mem_kv=w_mem_kv, ssm_a_re=ssm_a_re, ssm_a_im=ssm_a_im, ssm_log_step=ssm_log_step, ssm_b_re=ssm_b_re,
             ssm_b_im=ssm_b_im, ssm_c_re=ssm_c_re, ssm_c_im=ssm_c_im, ssm_d=ssm_d, ssm_w_glu=ssm_w_glu,
             hyena_short_w=hyena_short_w, hyena_short_b=hyena_short_b, hyena_ffn_w1=hyena_ffn_w1,
             hyena_ffn_b1=hyena_ffn_b1, hyena_ffn_w2=hyena_ffn_w2, hyena_ffn_b2=hyena_ffn_b2,
             hyena_ffn_w3=hyena_ffn_w3, hyena_log_decay=hyena_log_decay, hyena_bias=hyena_bias,
             branch_norm=branch_norm, w_out=w_out)
    weights = [_layer_weights(layer, p) for layer in range(DEPTH)]
    return (_run_group(x_prompt, mem_prompt, weights, p), _run_group(x_sample, mem_sample, weights, p))
```

```python
import functools
import math

import jax
import jax.numpy as jnp
import numpy as np
from jax import lax
from jax.experimental import pallas as pl
from jax.experimental.pallas import tpu as pltpu

F32 = jnp.float32
BF16 = jnp.bfloat16

D_MODEL = 1024
DEPTH = 2
GRID_W = 64
HEAD_DIM = 64
ATTN_HEADS = 8
ATTN_KV_HEADS = 2
ATTN_GROUP = ATTN_HEADS // ATTN_KV_HEADS
ATTN_WIDTH = ATTN_HEADS * HEAD_DIM
KV_WIDTH = ATTN_KV_HEADS * HEAD_DIM
ROPE_THETA = 10000.0
ROPE_FREQS = HEAD_DIM // 4
SSM_GROUP = 16
SSM_GROUPS = 24
SSM_WIDTH = SSM_GROUP * SSM_GROUPS
SSM_STATE = 64
HYENA_WIDTH = 384
HYENA_ORDER = 2
FILTER_BANDS = 16
FILTER_HIDDEN = 64
MEM_TOKENS = 256
MEM_HEADS = 4
MEM_WIDTH = MEM_HEADS * HEAD_DIM
MIX_WIDTH = ATTN_WIDTH + SSM_WIDTH + HYENA_WIDTH + MEM_WIDTH
EPS = 1e-6

TOK_WIDTH = KV_WIDTH + MEM_WIDTH
CH_WIDTH = 2 * ATTN_WIDTH + KV_WIDTH + 2 * SSM_WIDTH + (HYENA_ORDER + 2) * HYENA_WIDTH
V_ROWS = HEAD_DIM + 16
Q_SCALE = HEAD_DIM ** -0.5 * math.log2(math.e)

LANES = 128
SUBLANES = 8
VMEM_LIMIT = 56 * 1024 * 1024
TOKEN_TILE = 512
ATTN_Q_TILE = 256
ATTN_K_TILE = 512
SSM_CHUNK = LANES
DFT_INNER = 256
HYENA_CH_BLOCK = 8
FILTER_LANE_TILE = 2048


def _cparams(sem):
    return pltpu.CompilerParams(dimension_semantics=sem, vmem_limit_bytes=VMEM_LIMIT)


def _silu(x):
    return x * (1.0 / (1.0 + jnp.exp(-x)))


def _sigmoid(x):
    return 1.0 / (1.0 + jnp.exp(-x))


def _nt_dot(a, b):
    return lax.dot_general(a, b, (((1,), (1,)), ((), ())), preferred_element_type=F32)


def _rope_128(xn, cos, s_lo, s_hi):
    outs = []
    for c in range(xn.shape[1] // LANES):
        xc = xn[:, LANES * c:LANES * (c + 1)]
        outs.append(xc * cos + pltpu.roll(xc, LANES - ROPE_FREQS, 1) * s_lo + pltpu.roll(xc, ROPE_FREQS, 1) * s_hi)
    return outs[0] if len(outs) == 1 else jnp.concatenate(outs, axis=1)


def _inproj_kernel(x_ref, pre_g_ref, wtok_ref, wch_ref, cos_ref, slo_ref, shi_ref, cost_ref, sint_ref,
                   qg_ref, kg_ref, ones_ref,
                   qt_ref, k_ref, vt_ref, ga_ref, mq_ref, su_ref, sg_ref, hi_ref, hg_ref):
    tm = x_ref.shape[0]
    x = x_ref[...]
    h = x * lax.rsqrt(jnp.mean(x * x, axis=-1, keepdims=True) + EPS) * pre_g_ref[...]
    hb = h.astype(BF16)
    tok = jnp.dot(hb, wtok_ref[...], preferred_element_type=F32)
    k = tok[:, 0:KV_WIDTH]
    mq = tok[:, KV_WIDTH:]
    k_ms = jnp.dot((k * k).astype(BF16), ones_ref[...], preferred_element_type=F32) * (1.0 / HEAD_DIM)
    kn = k * lax.rsqrt(k_ms + EPS) * kg_ref[...]
    kr = _rope_128(kn, cos_ref[...], slo_ref[...], shi_ref[...])
    for j in range(ATTN_KV_HEADS):
        k_ref[j] = kr[:, HEAD_DIM * j:HEAD_DIM * (j + 1)].astype(BF16)
    mq_ref[...] = (mq * (HEAD_DIM ** -0.5)).astype(BF16)
    ch = _nt_dot(wch_ref[...], hb)
    q3 = ch[0:ATTN_WIDTH].reshape(ATTN_HEADS, HEAD_DIM, tm)
    qn = q3 * lax.rsqrt(jnp.mean(q3 * q3, axis=1, keepdims=True) + EPS) * qg_ref[...][None]
    f = ROPE_FREQS
    rot = jnp.concatenate([qn[:, f:2 * f], qn[:, 0:f], qn[:, 3 * f:4 * f], qn[:, 2 * f:3 * f]], axis=1)
    qr = (qn * cost_ref[...][None] + rot * sint_ref[...][None]) * Q_SCALE
    qt_ref[...] = qr.reshape(ATTN_WIDTH, tm).astype(BF16)
    o = ATTN_WIDTH
    ones_row = (lax.broadcasted_iota(jnp.int32, (V_ROWS - HEAD_DIM, tm), 0) == 0).astype(BF16)
    for j in range(ATTN_KV_HEADS):
        vt_ref[j, 0:HEAD_DIM] = ch[o + HEAD_DIM * j:o + HEAD_DIM * (j + 1)].astype(BF16)
        vt_ref[j, HEAD_DIM:V_ROWS] = ones_row
    o += KV_WIDTH
    ga_ref[...] = _silu(ch[o:o + ATTN_WIDTH])
    o += ATTN_WIDTH
    su_ref[...] = ch[o:o + SSM_WIDTH]
    sg_ref[...] = _silu(ch[o + SSM_WIDTH:o + 2 * SSM_WIDTH])
    o += 2 * SSM_WIDTH
    hi_ref[...] = ch[o:o + 3 * HYENA_WIDTH]
    hg_ref[...] = _silu(ch[o + 3 * HYENA_WIDTH:])


def _inproj(x2, pre_g, w_tok, w_ch_t, rope, qg, kg, ones, length):
    cos, s_lo, s_hi, cos_t, sin_t = rope
    n = x2.shape[0]
    tm = TOKEN_TILE
    nt = n // tm
    tiles_per_seq = length // tm
    tok_spec = lambda w: pl.BlockSpec((tm, w), lambda i: (i, 0))
    ch_spec = lambda w: pl.BlockSpec((w, tm), lambda i: (0, i))
    const = lambda shape: pl.BlockSpec(shape, lambda i: (0,) * len(shape))
    pos_spec = pl.BlockSpec((tm, LANES), lambda i: (i % tiles_per_seq, 0))
    pos_t_spec = pl.BlockSpec((HEAD_DIM, tm), lambda i: (0, i % tiles_per_seq))
    return pl.pallas_call(
        _inproj_kernel,
        grid=(nt,),
        in_specs=[tok_spec(D_MODEL), const((1, D_MODEL)), const((D_MODEL, TOK_WIDTH)), const((CH_WIDTH, D_MODEL)),
                  pos_spec, pos_spec, pos_spec, pos_t_spec, pos_t_spec,
                  const((HEAD_DIM, 1)), const((1, KV_WIDTH)), const((KV_WIDTH, KV_WIDTH))],
        out_specs=[ch_spec(ATTN_WIDTH),
                   pl.BlockSpec((ATTN_KV_HEADS, tm, HEAD_DIM), lambda i: (0, i, 0)),
                   pl.BlockSpec((ATTN_KV_HEADS, V_ROWS, tm), lambda i: (0, 0, i)),
                   ch_spec(ATTN_WIDTH), tok_spec(MEM_WIDTH),
                   ch_spec(SSM_WIDTH), ch_spec(SSM_WIDTH), ch_spec(3 * HYENA_WIDTH), ch_spec(HYENA_WIDTH)],
        out_shape=[jax.ShapeDtypeStruct((ATTN_WIDTH, n), BF16),
                   jax.ShapeDtypeStruct((ATTN_KV_HEADS, n, HEAD_DIM), BF16),
                   jax.ShapeDtypeStruct((ATTN_KV_HEADS, V_ROWS, n), BF16),
                   jax.ShapeDtypeStruct((ATTN_WIDTH, n), F32),
                   jax.ShapeDtypeStruct((n, MEM_WIDTH), BF16),
                   jax.ShapeDtypeStruct((SSM_WIDTH, n), F32),
                   jax.ShapeDtypeStruct((SSM_WIDTH, n), F32),
                   jax.ShapeDtypeStruct((3 * HYENA_WIDTH, n), F32),
                   jax.ShapeDtypeStruct((HYENA_WIDTH, n), F32)],
        compiler_params=_cparams(("parallel",)),
        name="inproj",
    )(x2, pre_g, w_tok, w_ch_t, cos, s_lo, s_hi, cos_t, sin_t, qg, kg, ones)


def _flash_kernel(qt_ref, k_ref, vt_ref, o_ref, acc_s, *, tq, tk, n_kv):
    w = ATTN_GROUP * tq
    q4t = jnp.concatenate([qt_ref[HEAD_DIM * i:HEAD_DIM * (i + 1), :] for i in range(ATTN_GROUP)], axis=1)
    acc_s[...] = jnp.zeros(acc_s.shape, F32)

    def body(c, m):
        start = pl.multiple_of(c * tk, tk)
        s = jnp.dot(k_ref[0, pl.ds(start, tk), :], q4t, preferred_element_type=F32)
        m_new = jnp.maximum(m, jnp.max(s, axis=0, keepdims=True))
        p = jnp.exp2(s - m_new).astype(BF16)
        pv = jnp.dot(vt_ref[0, :, pl.ds(start, tk)], p, preferred_element_type=F32)
        acc_s[...] = jnp.exp2(m - m_new) * acc_s[...] + pv
        return m_new

    lax.fori_loop(0, n_kv, body, jnp.full((1, w), -jnp.inf, F32), unroll=4)
    o = acc_s[0:HEAD_DIM] * (1.0 / acc_s[HEAD_DIM:HEAD_DIM + 1])
    for i in range(ATTN_GROUP):
        o_ref[HEAD_DIM * i:HEAD_DIM * (i + 1), :] = o[:, i * tq:(i + 1) * tq]


def _flash_attention(q_t, k, v_t, batch, length):
    n = q_t.shape[1]
    tq, tk = ATTN_Q_TILE, ATTN_K_TILE
    nq, nk = length // tq, length // tk
    gw = ATTN_GROUP * HEAD_DIM
    return pl.pallas_call(
        functools.partial(_flash_kernel, tq=tq, tk=tk, n_kv=nk),
        grid=(batch, ATTN_KV_HEADS, nq),
        in_specs=[pl.BlockSpec((gw, tq), lambda b, h, i: (h, b * nq + i)),
                  pl.BlockSpec((1, length, HEAD_DIM), lambda b, h, i: (h, b, 0)),
                  pl.BlockSpec((1, V_ROWS, length), lambda b, h, i: (h, 0, b))],
        out_specs=pl.BlockSpec((gw, tq), lambda b, h, i: (h, b * nq + i)),
        out_shape=jax.ShapeDtypeStruct((ATTN_WIDTH, n), F32),
        scratch_shapes=[pltpu.VMEM((V_ROWS, ATTN_GROUP * tq), F32)],
        compiler_params=_cparams(("parallel", "parallel", "parallel")),
        name="flash_attn",
    )(q_t, k, v_t)


def _memkv_kernel(mem_ref, g_ref, w_ref, kv_ref):
    m = mem_ref[...]
    mn = m * lax.rsqrt(jnp.mean(m * m, axis=-1, keepdims=True) + EPS) * g_ref[...]
    kv_ref[...] = jnp.dot(mn.astype(BF16), w_ref[...], preferred_element_type=F32)


def _memkv(mem2, mem_g, w_kv):
    rows = mem2.shape[0]
    return pl.pallas_call(
        _memkv_kernel,
        grid=(rows // MEM_TOKENS,),
        in_specs=[pl.BlockSpec((MEM_TOKENS, D_MODEL), lambda i: (i, 0)),
                  pl.BlockSpec((1, D_MODEL), lambda i: (0, 0)),
                  pl.BlockSpec((D_MODEL, 2 * MEM_WIDTH), lambda i: (0, 0))],
        out_specs=pl.BlockSpec((MEM_TOKENS, 2 * MEM_WIDTH), lambda i: (i, 0)),
        out_shape=jax.ShapeDtypeStruct((rows, 2 * MEM_WIDTH), F32),
        compiler_params=_cparams(("parallel",)),
        name="mem_kv",
    )(mem2, mem_g, w_kv)


def _ssm_kernel(u_ref, w_ref, q_ref, a1_ref, a2_ref, y_ref, s_s, h_s, *, batch, n_chunks):
    t = SSM_CHUNK
    gt = SSM_GROUP * t
    rows = batch * n_chunks
    half = 2 * SSM_STATE
    u = jnp.concatenate([u_ref[0, c].reshape(rows, t) for c in range(SSM_GROUP)], axis=1).astype(BF16)
    r = jnp.dot(u, w_ref[0], preferred_element_type=F32)
    s_s[...] = r[:, gt:]
    a1f, a2f = a1_ref[0, :, :half], a2_ref[0, :, :half]
    a1b, a2b = a1_ref[0, :, half:], a2_ref[0, :, half:]

    sub = SUBLANES
    n_blocks = n_chunks // sub

    def step(kb, carry):
        new = []
        for b in range(batch):
            hf, hb = carry[2 * b], carry[2 * b + 1]
            base_f = pl.multiple_of(b * n_chunks + kb * sub, sub)
            base_b = pl.multiple_of(b * n_chunks + (n_blocks - 1 - kb) * sub, sub)
            sf = s_s[pl.ds(base_f, sub), 0:half]
            sb = s_s[pl.ds(base_b, sub), half:2 * half]
            hf_rows, hb_rows = [], [None] * sub
            for i in range(sub):
                hf_rows.append(hf)
                hf = a1f * hf + a2f * pltpu.roll(hf, SSM_STATE, 1) + sf[i:i + 1]
            for i in range(sub - 1, -1, -1):
                hb_rows[i] = hb
                hb = a1b * hb + a2b * pltpu.roll(hb, SSM_STATE, 1) + sb[i:i + 1]
            h_s[pl.ds(base_f, sub), 0:half] = jnp.concatenate(hf_rows, axis=0)
            h_s[pl.ds(base_b, sub), half:2 * half] = jnp.concatenate(hb_rows, axis=0)
            new += [hf, hb]
        return tuple(new)

    zero = jnp.zeros((1, half), F32)
    lax.fori_loop(0, n_blocks, step, (zero,) * (2 * batch))
    y = r[:, :gt] + jnp.dot(h_s[...].astype(BF16), q_ref[0], preferred_element_type=F32)
    for c in range(SSM_GROUP):
        y_ref[0, c] = y[:, c * t:(c + 1) * t].reshape(batch, n_chunks, t)


def _ssm_scan(u_t, w_gp, q_mat, a1, a2, batch, length):
    t = SSM_CHUNK
    nk = length // t
    gt = SSM_GROUP * t
    u5 = u_t.reshape(SSM_GROUPS, SSM_GROUP, batch, nk, t)
    blk = (1, SSM_GROUP, batch, nk, t)
    y5 = pl.pallas_call(
        functools.partial(_ssm_kernel, batch=batch, n_chunks=nk),
        grid=(SSM_GROUPS,),
        in_specs=[pl.BlockSpec(blk, lambda g: (g, 0, 0, 0, 0)),
                  pl.BlockSpec((1, gt, gt + 4 * SSM_STATE), lambda g: (g, 0, 0)),
                  pl.BlockSpec((1, 4 * SSM_STATE, gt), lambda g: (g, 0, 0)),
                  pl.BlockSpec((1, 1, 4 * SSM_STATE), lambda g: (g, 0, 0)),
                  pl.BlockSpec((1, 1, 4 * SSM_STATE), lambda g: (g, 0, 0))],
        out_specs=pl.BlockSpec(blk, lambda g: (g, 0, 0, 0, 0)),
        out_shape=jax.ShapeDtypeStruct(u5.shape, F32),
        scratch_shapes=[pltpu.VMEM((batch * nk, 4 * SSM_STATE), F32),
                        pltpu.VMEM((batch * nk, 4 * SSM_STATE), F32)],
        compiler_params=_cparams(("parallel",)),
        name="ssm_scan",
    )(u5, w_gp, q_mat, a1, a2)
    return y5.reshape(SSM_WIDTH, batch * length)


def _ssm_tables(a_re, a_im, log_step, b_re, b_im, c_re, c_im):
    t = SSM_CHUNK
    hi = lax.Precision.HIGHEST
    lam = lax.complex(a_re.astype(F32), a_im.astype(F32))
    step = jnp.exp(log_step.astype(F32))[..., None]
    ls = lam * step
    a_bar = jnp.exp(ls)
    b_bar = ((a_bar - 1.0) / lam)[..., None] * lax.complex(b_re.astype(F32), b_im.astype(F32))
    c = lax.complex(c_re.astype(F32), c_im.astype(F32))
    tau = jnp.arange(t + 1, dtype=F32)
    pw = jnp.exp(ls[..., None] * tau)
    kern = jnp.einsum('dgcp,dgpt,dgpe->dgtce', c, pw[..., :t], b_bar, precision=hi).real
    kf, kb = kern[0], kern[1]
    kcat = jnp.concatenate([kb[:, :0:-1], kf[:, :1] + kb[:, :1], kf[:, 1:]], axis=1)
    idx = jnp.arange(t)[None, :] - jnp.arange(t)[:, None] + (t - 1)
    g5 = kcat[:, idx]
    g_mat = g5.transpose(0, 4, 1, 3, 2).reshape(SSM_GROUPS, SSM_GROUP * t, SSM_GROUP * t)
    pw_f, pw_b = pw[0], pw[1]
    pf = pw_f[:, :, t - 1 - jnp.arange(t)][..., None] * b_bar[0][:, :, None, :]
    pb = pw_b[:, :, :t][..., None] * b_bar[1][:, :, None, :]
    to_rows = lambda z: z.transpose(0, 3, 2, 1).reshape(SSM_GROUPS, SSM_GROUP * t, SSM_STATE)
    p_mat = jnp.concatenate([to_rows(pf.real), to_rows(pf.imag), to_rows(pb.real), to_rows(pb.imag)], axis=-1)
    qf = c[0].transpose(0, 2, 1)[..., None] * pw_f[:, :, 1:][:, :, None, :]
    qb = c[1].transpose(0, 2, 1)[..., None] * pw_b[:, :, t - jnp.arange(t)][:, :, None, :]
    to_cols = lambda z: z.reshape(SSM_GROUPS, SSM_STATE, SSM_GROUP * t)
    q_mat = jnp.concatenate([to_cols(qf.real), -to_cols(qf.imag), to_cols(qb.real), -to_cols(qb.imag)], axis=1)
    at = pw[..., t]
    a1 = jnp.concatenate([at[0].real, at[0].real, at[1].real, at[1].real], axis=-1)[:, None, :]
    a2 = jnp.concatenate([-at[0].imag, at[0].imag, -at[1].imag, at[1].imag], axis=-1)[:, None, :]
    w_gp = jnp.concatenate([g_mat, p_mat], axis=-1).astype(BF16)
    return w_gp, q_mat.astype(BF16), a1, a2


def _dft_tables(length):
    j = DFT_INNER
    n = 2 * length
    rn = n // j
    rh = rn // 2
    k1 = jnp.arange(rn, dtype=jnp.int32)
    ang1 = (2.0 * math.pi / rn) * ((k1[:, None] * k1[None, :]) % rn).astype(F32)
    c1, s1 = jnp.cos(ang1), jnp.sin(ang1)
    f1_full = jnp.concatenate([c1, -s1], axis=0)
    f1_top = f1_full[:, :rh]
    f1_inv = jnp.concatenate([c1[:rh], -s1[:rh]], axis=1)
    jj = jnp.arange(j, dtype=jnp.int32)
    angt = (2.0 * math.pi / n) * (k1[:, None] * jj[None, :]).astype(F32)
    tw_re, tw_im = jnp.cos(angt), -jnp.sin(angt)
    ang2 = (2.0 * math.pi / j) * ((jj[:, None] * jj[None, :]) % j).astype(F32)
    f2_cat = jnp.concatenate([jnp.cos(ang2), -jnp.sin(ang2)], axis=1)
    return dict(f1_full=f1_full.astype(BF16), f1_top=f1_top.astype(BF16), f1_inv=f1_inv.astype(BF16),
                tw_re=tw_re, tw_im=tw_im, f2_cat=f2_cat.astype(BF16), rn=rn, rh=rh)


def _dft_forward(a, tw_re, tw_im, f2_cat, rn):
    j = DFT_INNER
    a_re, a_im = a[:rn], a[rn:]
    ap = jnp.concatenate([a_re * tw_re - a_im * tw_im, a_re * tw_im + a_im * tw_re], axis=0).astype(BF16)
    m = jnp.dot(ap, f2_cat, preferred_element_type=F32)
    return m[:rn, :j] - m[rn:, j:], m[:rn, j:] + m[rn:, :j]


def _dft_inverse(z_re, z_im, tw_re, tw_im, f2_cat, f1_inv, rn):
    j = DFT_INNER
    z = jnp.concatenate([z_re, z_im], axis=0).astype(BF16)
    m = jnp.dot(z, f2_cat, preferred_element_type=F32)
    b_re = m[:rn, :j] + m[rn:, j:]
    b_im = m[rn:, :j] - m[:rn, j:]
    bp = jnp.concatenate([b_re * tw_re + b_im * tw_im, b_im * tw_re - b_re * tw_im], axis=0).astype(BF16)
    return jnp.dot(f1_inv, bp, preferred_element_type=F32)


def _filter_hidden_kernel(bands_ref, w1t_ref, w1c_ref, w1s_ref, b1_ref, w2t_ref, b2_ref, hid_ref, *, length, tl):
    hi = lax.Precision.HIGHEST
    base = pl.program_id(0) * tl
    idx = (lax.broadcasted_iota(jnp.int32, (1, tl), 1) + base).astype(F32)
    for d in range(2):
        pos = idx if d == 0 else float(length) - idx
        tt = pos / float(length)
        wpos = (2.0 * math.pi / length) * pos
        arg = bands_ref[...] * wpos
        h1 = (w1t_ref[...] * tt
              + jnp.dot(w1c_ref[...], jnp.cos(arg), preferred_element_type=F32, precision=hi)
              - jnp.dot(w1s_ref[...], jnp.sin(arg), preferred_element_type=F32, precision=hi))
        h1 = jnp.sin(h1 + b1_ref[...])
        h2 = jnp.dot(w2t_ref[...], h1, preferred_element_type=F32, precision=hi)
        hid_ref[d] = jnp.sin(h2 + b2_ref[...])


def _filter_taps_kernel(hid_ref, w3t_ref, decay_ref, filt_ref, *, length, tl):
    hi = lax.Precision.HIGHEST
    d = pl.program_id(0) % 2
    base = pl.program_id(1) * tl
    idx = (lax.broadcasted_iota(jnp.int32, (1, tl), 1) + base).astype(F32)
    pos = jnp.where(d == 0, idx, float(length) - idx)
    tt = pos / float(length)
    f = jnp.dot(w3t_ref[...], hid_ref[0], preferred_element_type=F32, precision=hi)
    f = f * jnp.exp(-tt * decay_ref[...])
    filt_ref[...] = jnp.where(jnp.logical_and(d == 1, idx == 0.0), 0.0, f)


def _filter_dft_kernel(filt_ref, f1_ref, twre_ref, twim_ref, f2_ref, kf_ref, *, rn, cb):
    for c in range(cb):
        r = jnp.concatenate([filt_ref[0, 0, c], filt_ref[0, 1, c]], axis=0).astype(BF16)
        a = jnp.dot(f1_ref[...], r, preferred_element_type=F32)
        x_re, x_im = _dft_forward(a, twre_ref[...], twim_ref[...], f2_ref[...], rn)
        kf_ref[0, c, 0] = x_re
        kf_ref[0, c, 1] = x_im


def _hyena_filters(length, tabs, w1, b1, w2, b2, w3, log_decay):
    tl = min(FILTER_LANE_TILE, length)
    nl = length // tl
    rn, rh, j = tabs["rn"], tabs["rh"], DFT_INNER
    bands = jnp.linspace(1e-4, FILTER_BANDS - 1, FILTER_BANDS, dtype=F32)[:, None]
    w1f = w1.astype(F32)
    const = lambda shape: pl.BlockSpec(shape, lambda *_: (0,) * len(shape))
    hid = pl.pallas_call(
        functools.partial(_filter_hidden_kernel, length=length, tl=tl),
        grid=(nl,),
        in_specs=[const((FILTER_BANDS, 1)), const((FILTER_HIDDEN, 1)), const((FILTER_HIDDEN, FILTER_BANDS)),
                  const((FILTER_HIDDEN, FILTER_BANDS)), const((FILTER_HIDDEN, 1)),
                  const((FILTER_HIDDEN, FILTER_HIDDEN)), const((FILTER_HIDDEN, 1))],
        out_specs=pl.BlockSpec((2, FILTER_HIDDEN, tl), lambda i: (0, 0, i)),
        out_shape=jax.ShapeDtypeStruct((2, FILTER_HIDDEN, length), F32),
        compiler_params=_cparams(("parallel",)),
        name="hyena_filter_hidden",
    )(bands, w1f[0:1].T, w1f[1:1 + FILTER_BANDS].T, w1f[1 + FILTER_BANDS:].T, b1.astype(F32)[:, None],
      w2.astype(F32).T, b2.astype(F32)[:, None])
    n_od = HYENA_ORDER * 2
    w3t = w3.astype(F32).T.reshape(n_od, HYENA_WIDTH, FILTER_HIDDEN)
    decay = jnp.exp(log_decay.astype(F32)).reshape(n_od, HYENA_WIDTH, 1)
    filt = pl.pallas_call(
        functools.partial(_filter_taps_kernel, length=length, tl=tl),
        grid=(n_od, nl),
        in_specs=[pl.BlockSpec((1, FILTER_HIDDEN, tl), lambda od, i: (od % 2, 0, i)),
                  pl.BlockSpec((None, HYENA_WIDTH, FILTER_HIDDEN), lambda od, i: (od, 0, 0)),
                  pl.BlockSpec((None, HYENA_WIDTH, 1), lambda od, i: (od, 0, 0))],
        out_specs=pl.BlockSpec((None, HYENA_WIDTH, tl), lambda od, i: (od, 0, i)),
        out_shape=jax.ShapeDtypeStruct((n_od, HYENA_WIDTH, length), F32),
        compiler_params=_cparams(("parallel", "parallel")),
        name="hyena_filter_taps",
    )(hid, w3t, decay)
    filt6 = filt.reshape(HYENA_ORDER, 2, HYENA_WIDTH, rh, j)
    cb = HYENA_CH_BLOCK
    return pl.pallas_call(
        functools.partial(_filter_dft_kernel, rn=rn, cb=cb),
        grid=(HYENA_ORDER, HYENA_WIDTH // cb),
        in_specs=[pl.BlockSpec((1, 2, cb, rh, j), lambda o, c: (o, 0, c, 0, 0)),
                  const((2 * rn, rn)), const((rn, j)), const((rn, j)), const((j, 2 * j))],
        out_specs=pl.BlockSpec((1, cb, 2, rn, j), lambda o, c: (o, c, 0, 0, 0)),
        out_shape=jax.ShapeDtypeStruct((HYENA_ORDER, HYENA_WIDTH, 2, rn, j), F32),
        compiler_params=_cparams(("parallel", "parallel")),
        name="hyena_filter_dft",
    )(filt6, tabs["f1_full"], tabs["tw_re"], tabs["tw_im"], tabs["f2_cat"])


def _hyena_kernel(scw_ref, scb_ref, hb_ref, x_ref, hg_ref, kf_ref, f1_ref, f1i_ref, twre_ref, twim_ref, f2_ref,
                  o_ref, *, batch, rn, rh, cb):
    j = DFT_INNER
    n_inv = 1.0 / (rn * j)
    lane = lax.broadcasted_iota(jnp.int32, (rh, j), 1)
    row = lax.broadcasted_iota(jnp.int32, (rh, j), 0)
    first = jnp.logical_and(lane == 0, row == 0)
    last = jnp.logical_and(lane == j - 1, row == rh - 1)
    tw_re, tw_im = twre_ref[...], twim_ref[...]

    def prev_t(x):
        p = pltpu.roll(x, 1, 1)
        return jnp.where(first, 0.0, jnp.where(lane == 0, pltpu.roll(p, 1, 0), p))

    def next_t(x):
        p = pltpu.roll(x, j - 1, 1)
        return jnp.where(last, 0.0, jnp.where(lane == j - 1, pltpu.roll(p, rh - 1, 0), p))

    def long_conv(xs, k_re, k_im):
        outs = []
        for x in xs:
            a = jnp.dot(f1_ref[...], x.astype(BF16), preferred_element_type=F32)
            x_re, x_im = _dft_forward(a, tw_re, tw_im, f2_ref[...], rn)
            y = _dft_inverse(x_re * k_re - x_im * k_im, x_re * k_im + x_im * k_re,
                             tw_re, tw_im, f2_ref[...], f1i_ref[...], rn)
            outs.append(y * n_inv)
        return outs

    cbase = pl.program_id(0) * cb

    def channel(ci, carry):
        ch = cbase + ci
        segs = []
        for sgm in range(3):
            chs = sgm * HYENA_WIDTH + ch
            w0, w1, w2 = scw_ref[chs], scw_ref[3 * HYENA_WIDTH + chs], scw_ref[6 * HYENA_WIDTH + chs]
            bias = scb_ref[chs]
            seg = []
            for b in range(batch):
                x = x_ref[sgm, ci, b]
                seg.append(prev_t(x) * w0 + x * w1 + next_t(x) * w2 + bias)
            segs.append(seg)
        z = segs[0]
        for o in range(HYENA_ORDER):
            conv = long_conv(z, kf_ref[o, ci, 0], kf_ref[o, ci, 1])
            hbias = hb_ref[o * HYENA_WIDTH + ch]
            z = [segs[o + 1][b] * (conv[b] + z[b] * hbias) for b in range(batch)]
        for b in range(batch):
            o_ref[ci, b] = z[b] * hg_ref[ci, b]
        return carry

    lax.fori_loop(0, cb, channel, 0)


def _hyena(hi_t, hg_t, kf, tabs, short_w, short_b, hy_bias, batch, length):
    rn, rh, j = tabs["rn"], tabs["rh"], DFT_INNER
    cb = HYENA_CH_BLOCK
    x5 = hi_t.reshape(3, HYENA_WIDTH, batch, rh, j)
    g4 = hg_t.reshape(HYENA_WIDTH, batch, rh, j)
    smem = pl.BlockSpec(memory_space=pltpu.SMEM)
    const = lambda shape: pl.BlockSpec(shape, lambda c: (0,) * len(shape))
    out = pl.pallas_call(
        functools.partial(_hyena_kernel, batch=batch, rn=rn, rh=rh, cb=cb),
        grid=(HYENA_WIDTH // cb,),
        in_specs=[smem, smem, smem,
                  pl.BlockSpec((3, cb, batch, rh, j), lambda c: (0, c, 0, 0, 0)),
                  pl.BlockSpec((cb, batch, rh, j), lambda c: (c, 0, 0, 0)),
                  pl.BlockSpec((HYENA_ORDER, cb, 2, rn, j), lambda c: (0, c, 0, 0, 0)),
                  const((2 * rn, rh)), const((rh, 2 * rn)), const((rn, j)), const((rn, j)), const((j, 2 * j))],
        out_specs=pl.BlockSpec((cb, batch, rh, j), lambda c: (c, 0, 0, 0)),
        out_shape=jax.ShapeDtypeStruct(g4.shape, F32),
        compiler_params=_cparams(("parallel",)),
        name="hyena_conv",
    )(short_w.astype(F32).reshape(-1), short_b.astype(F32), hy_bias.astype(F32).reshape(-1),
      x5, g4, kf, tabs["f1_top"], tabs["f1_inv"], tabs["tw_re"], tabs["tw_im"], tabs["f2_cat"])
    return out.reshape(HYENA_WIDTH, batch * length)


def _rms_rows(x, g):
    return x * lax.rsqrt(jnp.mean(x * x, axis=-1, keepdims=True) + EPS) * g


def _rms_cols(x, g):
    return x * lax.rsqrt(jnp.mean(x * x, axis=0, keepdims=True) + EPS) * g


def _post_kernel(ao_ref, ga_ref, mq_ref, mk_ref, mv_ref, y_ref, u_ref, sg_ref, hy_ref, x_ref,
                 d_ref, wglu_ref, g_attn_ref, g_ssm_ref, g_hy_ref, g_mem_ref, post_g_ref, wo_ref, out_ref):
    attn_n = _rms_cols(ao_ref[...] * ga_ref[...], g_attn_ref[...])
    s = jnp.dot(mq_ref[...], mk_ref[0], preferred_element_type=F32)
    ps = []
    for h in range(MEM_HEADS):
        sh = s[:, MEM_TOKENS * h:MEM_TOKENS * (h + 1)]
        e = jnp.exp(sh - jnp.max(sh, axis=-1, keepdims=True))
        ps.append(e * (1.0 / jnp.sum(e, axis=-1, keepdims=True)))
    p = jnp.concatenate(ps, axis=1).astype(BF16)
    cross_n = _rms_rows(jnp.dot(p, mv_ref[0], preferred_element_type=F32), g_mem_ref[...])
    y = y_ref[...] + d_ref[...] * u_ref[...]
    g = y * (0.5 * (1.0 + jnp.tanh(math.sqrt(2.0 / math.pi) * (y + 0.044715 * (y * y * y)))))
    gz = jnp.dot(wglu_ref[...], g.astype(BF16), preferred_element_type=F32)
    ssm_n = _rms_cols(g * _sigmoid(gz) * sg_ref[...], g_ssm_ref[...])
    hy_n = _rms_cols(hy_ref[...], g_hy_ref[...])
    o1, o2, o3 = ATTN_WIDTH, ATTN_WIDTH + SSM_WIDTH, ATTN_WIDTH + SSM_WIDTH + HYENA_WIDTH
    mixed = (jnp.dot(attn_n.T.astype(BF16), wo_ref[0:o1], preferred_element_type=F32)
             + jnp.dot(ssm_n.T.astype(BF16), wo_ref[o1:o2], preferred_element_type=F32)
             + jnp.dot(hy_n.T.astype(BF16), wo_ref[o2:o3], preferred_element_type=F32)
             + jnp.dot(cross_n.astype(BF16), wo_ref[o3:], preferred_element_type=F32))
    out_ref[...] = x_ref[...] + _rms_rows(mixed, post_g_ref[...])


def _post(ao, ga, mq, mk_bd, mv_bd, y_t, u_t, sg_t, hy_t, x2, d, wglu_t, g_attn, g_ssm, g_hy, g_mem, post_g, wo,
          length):
    n = x2.shape[0]
    tm = TOKEN_TILE
    tiles_per_seq = length // tm
    tok_spec = lambda w: pl.BlockSpec((tm, w), lambda i: (i, 0))
    ch_spec = lambda w: pl.BlockSpec((w, tm), lambda i: (0, i))
    const = lambda shape: pl.BlockSpec(shape, lambda i: (0,) * len(shape))
    hm = MEM_HEADS * MEM_TOKENS
    return pl.pallas_call(
        _post_kernel,
        grid=(n // tm,),
        in_specs=[ch_spec(ATTN_WIDTH), ch_spec(ATTN_WIDTH), tok_spec(MEM_WIDTH),
                  pl.BlockSpec((1, MEM_WIDTH, hm), lambda i: (i // tiles_per_seq, 0, 0)),
                  pl.BlockSpec((1, hm, MEM_WIDTH), lambda i: (i // tiles_per_seq, 0, 0)),
                  ch_spec(SSM_WIDTH), ch_spec(SSM_WIDTH), ch_spec(SSM_WIDTH), ch_spec(HYENA_WIDTH),
                  tok_spec(D_MODEL),
                  const((SSM_WIDTH, 1)), const((SSM_WIDTH, SSM_WIDTH)),
                  const((ATTN_WIDTH, 1)), const((SSM_WIDTH, 1)), const((HYENA_WIDTH, 1)), const((1, MEM_WIDTH)),
                  const((1, D_MODEL)), const((MIX_WIDTH, D_MODEL))],
        out_specs=tok_spec(D_MODEL),
        out_shape=jax.ShapeDtypeStruct((n, D_MODEL), F32),
        compiler_params=_cparams(("parallel",)),
        name="post",
    )(ao, ga, mq, mk_bd, mv_bd, y_t, u_t, sg_t, hy_t, x2, d, wglu_t, g_attn, g_ssm, g_hy, g_mem, post_g, wo)


def _rope_tables(length):
    rows = length // GRID_W
    row = jnp.broadcast_to(jnp.arange(rows, dtype=F32)[:, None], (rows, GRID_W)).reshape(length)
    col = jnp.broadcast_to(jnp.arange(GRID_W, dtype=F32)[None, :], (rows, GRID_W)).reshape(length)
    inv_freq = ROPE_THETA ** (-jnp.arange(ROPE_FREQS, dtype=F32) / ROPE_FREQS)
    ang = jnp.stack([row[:, None] * inv_freq, col[:, None] * inv_freq], axis=1)
    ang = jnp.broadcast_to(ang[:, :, None, :], (length, 2, 2, ROPE_FREQS)).reshape(length, HEAD_DIM)
    cos1, sin1 = jnp.cos(ang), jnp.sin(ang)
    low1 = (jnp.arange(HEAD_DIM) % (2 * ROPE_FREQS)) < ROPE_FREQS
    cos_t, sin_t = cos1.T, jnp.where(low1, -sin1, sin1).T
    cos = jnp.concatenate([cos1, cos1], axis=1)
    sin = jnp.concatenate([sin1, sin1], axis=1)
    low = jnp.concatenate([low1, low1])
    return cos, jnp.where(low, -sin, 0.0), jnp.where(low, 0.0, sin), cos_t, sin_t


def _block_diag_heads(mk, mv, batch):
    mk4 = mk.reshape(batch, MEM_TOKENS, MEM_HEADS, HEAD_DIM)
    mv4 = mv.reshape(batch, MEM_TOKENS, MEM_HEADS, HEAD_DIM)
    eye = jnp.eye(MEM_HEADS, dtype=mk.dtype)
    k_bd = jnp.einsum('bmhd,hg->bhdgm', mk4, eye).reshape(batch, MEM_WIDTH, MEM_HEADS * MEM_TOKENS)
    v_bd = jnp.einsum('bmhd,hg->bhmgd', mv4, eye).reshape(batch, MEM_HEADS * MEM_TOKENS, MEM_WIDTH)
    return k_bd.astype(BF16), v_bd.astype(BF16)


def _layer_weights(layer, p):
    w_in = p["w_in"][layer]
    a, kv = ATTN_WIDTH, KV_WIDTH
    k0, v0, g0, mq0 = a, a + kv, a + 2 * kv, w_in.shape[1] - MEM_WIDTH
    w_tok = jnp.concatenate([w_in[:, k0:v0], w_in[:, mq0:]], axis=1).astype(BF16)
    w_ch_t = jnp.concatenate([w_in[:, :k0], w_in[:, v0:mq0]], axis=1).T.astype(BF16)
    bg = p["branch_norm"][layer].astype(F32)
    o1, o2, o3 = a, a + SSM_WIDTH, a + SSM_WIDTH + HYENA_WIDTH
    head_id = jnp.arange(kv) // HEAD_DIM
    return dict(
        w_tok=w_tok, w_ch_t=w_ch_t,
        pre_g=p["pre_norm"][layer].astype(F32)[None, :], post_g=p["post_norm"][layer].astype(F32)[None, :],
        qg=p["q_norm"][layer].astype(F32)[:, None],
        kg=jnp.tile(p["k_norm"][layer].astype(F32), ATTN_KV_HEADS)[None, :],
        ones=(head_id[:, None] == head_id[None, :]).astype(BF16),
        mem_g=p["mem_norm"][layer].astype(F32)[None, :], w_mem_kv=p["w_mem_kv"][layer].astype(BF16),
        ssm=_ssm_tables(p["ssm_a_re"][layer], p["ssm_a_im"][layer], p["ssm_log_step"][layer], p["ssm_b_re"][layer],
                        p["ssm_b_im"][layer], p["ssm_c_re"][layer], p["ssm_c_im"][layer]),
        d=p["ssm_d"][layer].astype(F32)[:, None], wglu_t=p["ssm_w_glu"][layer].T.astype(BF16),
        g_attn=bg[:o1, None], g_ssm=bg[o1:o2, None], g_hy=bg[o2:o3, None], g_mem=bg[None, o3:],
        wo=p["w_out"][layer].astype(BF16),
    )


def _mixer_layer(x2, mem2, lw, kf, tabs, rope, p, layer, batch, length):
    q_t, k, v_t, ga, mq, su_t, sg_t, hi_t, hg_t = _inproj(x2, lw["pre_g"], lw["w_tok"], lw["w_ch_t"], rope,
                                                          lw["qg"], lw["kg"], lw["ones"], length)
    ao = _flash_attention(q_t, k, v_t, batch, length)
    mem_kv = _memkv(mem2, lw["mem_g"], lw["w_mem_kv"])
    mk_bd, mv_bd = _block_diag_heads(mem_kv[:, :MEM_WIDTH], mem_kv[:, MEM_WIDTH:], batch)
    y_t = _ssm_scan(su_t, *lw["ssm"], batch, length)
    hy_t = _hyena(hi_t, hg_t, kf, tabs, p["hyena_short_w"][layer], p["hyena_short_b"][layer],
                  p["hyena_bias"][layer], batch, length)
    return _post(ao, ga, mq, mk_bd, mv_bd, y_t, su_t, sg_t, hy_t, x2, lw["d"], lw["wglu_t"],
                 lw["g_attn"], lw["g_ssm"], lw["g_hy"], lw["g_mem"], lw["post_g"], lw["wo"], length)


def _run_group(x, mem, weights, p):
    batch, length, _ = x.shape
    rope = _rope_tables(length)
    tabs = _dft_tables(length)
    x2 = x.reshape(batch * length, D_MODEL)
    mem2 = mem.reshape(batch * MEM_TOKENS, D_MODEL)
    for layer in range(DEPTH):
        kf = _hyena_filters(length, tabs, p["hyena_ffn_w1"][layer], p["hyena_ffn_b1"][layer],
                            p["hyena_ffn_w2"][layer], p["hyena_ffn_b2"][layer], p["hyena_ffn_w3"][layer],
                            p["hyena_log_decay"][layer])
        x2 = _mixer_layer(x2, mem2, weights[layer], kf, tabs, rope, p, layer, batch, length)
    return x2.reshape(batch, length, D_MODEL)


def kernel(x_prompt, x_sample, mem_prompt, mem_sample, pre_norm, post_norm, w_in, q_norm, k_norm, mem_norm, w_mem_kv, ssm_a_re, ssm_a_im, ssm_log_step, ssm_b_re, ssm_b_im, ssm_c_re, ssm_c_im, ssm_d, ssm_w_glu, hyena_short_w, hyena_short_b, hyena_ffn_w1, hyena_ffn_b1, hyena_ffn_w2, hyena_ffn_b2, hyena_ffn_w3, hyena_log_decay, hyena_bias, branch_norm, w_out):
    p = dict(pre_norm=pre_norm, post_norm=post_norm, w_in=w_in, q_norm=q_norm, k_norm=k_norm, mem_norm=mem_norm,
             w_mem_kv=w_mem_kv, ssm_a_re=ssm_a_re, ssm_a_im=ssm_a_im, ssm_log_step=ssm_log_step, ssm_b_re=ssm_b_re,
             ssm_b_im=ssm_b_im, ssm_c_re=ssm_c_re, ssm_c_im=ssm_c_im, ssm_d=ssm_d, ssm_w_glu=ssm_w_glu,
             hyena_short_w=hyena_short_w, hyena_short_b=hyena_short_b, hyena_ffn_w1=hyena_ffn_w1,
             hyena_ffn_b1=hyena_ffn_b1, hyena_ffn_w2=hyena_ffn_w2, hyena_ffn_b2=hyena_ffn_b2,
             hyena_ffn_w3=hyena_ffn_w3, hyena_log_decay=hyena_log_decay, hyena_bias=hyena_bias,
             branch_norm=branch_norm, w_out=w_out)
    weights = [_layer_weights(layer, p) for layer in range(DEPTH)]
    return (_run_group(x_prompt, mem_prompt, weights, p), _run_group(x_sample, mem_sample, weights, p))
```

```python
import functools
import math

import jax
import jax.numpy as jnp
import numpy as np
from jax import lax
from jax.experimental import pallas as pl
from jax.experimental.pallas import tpu as pltpu

F32 = jnp.float32
BF16 = jnp.bfloat16

D_MODEL = 1024
DEPTH = 2
GRID_W = 64
HEAD_DIM = 64
ATTN_HEADS = 8
ATTN_KV_HEADS = 2
ATTN_GROUP = ATTN_HEADS // ATTN_KV_HEADS
ATTN_WIDTH = ATTN_HEADS * HEAD_DIM
KV_WIDTH = ATTN_KV_HEADS * HEAD_DIM
ROPE_THETA = 10000.0
ROPE_FREQS = HEAD_DIM // 4
SSM_GROUP = 16
SSM_GROUPS = 24
SSM_WIDTH = SSM_GROUP * SSM_GROUPS
SSM_STATE = 64
HYENA_WIDTH = 384
HYENA_ORDER = 2
FILTER_BANDS = 16
FILTER_HIDDEN = 64
MEM_TOKENS = 256
MEM_HEADS = 4
MEM_WIDTH = MEM_HEADS * HEAD_DIM
MIX_WIDTH = ATTN_WIDTH + SSM_WIDTH + HYENA_WIDTH + MEM_WIDTH
EPS = 1e-6

TOK_WIDTH = KV_WIDTH + MEM_WIDTH
CH_WIDTH = 2 * ATTN_WIDTH + KV_WIDTH + 2 * SSM_WIDTH + (HYENA_ORDER + 2) * HYENA_WIDTH
V_ROWS = HEAD_DIM + 16
Q_SCALE = HEAD_DIM ** -0.5 * math.log2(math.e)

LANES = 128
SUBLANES = 8
VMEM_LIMIT = 56 * 1024 * 1024
TOKEN_TILE = 512
ATTN_Q_TILE = 256
ATTN_K_TILE = 512
SSM_CHUNK = LANES
DFT_INNER = 256
HYENA_CH_BLOCK = 8
HYENA_CH_SUB = 4
FILTER_LANE_TILE = 2048


def _cparams(sem):
    return pltpu.CompilerParams(dimension_semantics=sem, vmem_limit_bytes=VMEM_LIMIT)


def _silu(x):
    return x * (1.0 / (1.0 + jnp.exp(-x)))


def _sigmoid(x):
    return 1.0 / (1.0 + jnp.exp(-x))


def _nt_dot(a, b):
    return lax.dot_general(a, b, (((1,), (1,)), ((), ())), preferred_element_type=F32)


def _rope_128(xn, cos, s_lo, s_hi):
    outs = []
    for c in range(xn.shape[1] // LANES):
        xc = xn[:, LANES * c:LANES * (c + 1)]
        outs.append(xc * cos + pltpu.roll(xc, LANES - ROPE_FREQS, 1) * s_lo + pltpu.roll(xc, ROPE_FREQS, 1) * s_hi)
    return outs[0] if len(outs) == 1 else jnp.concatenate(outs, axis=1)


def _inproj_kernel(x_ref, pre_g_ref, wtok_ref, wch_ref, cos_ref, slo_ref, shi_ref, cost_ref, sint_ref,
                   qg_ref, kg_ref, ones_ref,
                   qt_ref, k_ref, vt_ref, ga_ref, mq_ref, su_ref, sg_ref, hi_ref, hg_ref):
    tm = x_ref.shape[0]
    x = x_ref[...]
    h = x * lax.rsqrt(jnp.mean(x * x, axis=-1, keepdims=True) + EPS) * pre_g_ref[...]
    hb = h.astype(BF16)
    tok = jnp.dot(hb, wtok_ref[...], preferred_element_type=F32)
    k = tok[:, 0:KV_WIDTH]
    mq = tok[:, KV_WIDTH:]
    k_ms = jnp.dot((k * k).astype(BF16), ones_ref[...], preferred_element_type=F32) * (1.0 / HEAD_DIM)
    kn = k * lax.rsqrt(k_ms + EPS) * kg_ref[...]
    kr = _rope_128(kn, cos_ref[...], slo_ref[...], shi_ref[...])
    for j in range(ATTN_KV_HEADS):
        k_ref[j] = kr[:, HEAD_DIM * j:HEAD_DIM * (j + 1)].astype(BF16)
    mq_ref[...] = (mq * (HEAD_DIM ** -0.5)).astype(BF16)
    ch = _nt_dot(wch_ref[...], hb)
    q3 = ch[0:ATTN_WIDTH].reshape(ATTN_HEADS, HEAD_DIM, tm)
    qn = q3 * lax.rsqrt(jnp.mean(q3 * q3, axis=1, keepdims=True) + EPS) * qg_ref[...][None]
    f = ROPE_FREQS
    rot = jnp.concatenate([qn[:, f:2 * f], qn[:, 0:f], qn[:, 3 * f:4 * f], qn[:, 2 * f:3 * f]], axis=1)
    qr = (qn * cost_ref[...][None] + rot * sint_ref[...][None]) * Q_SCALE
    qt_ref[...] = qr.reshape(ATTN_WIDTH, tm).astype(BF16)
    o = ATTN_WIDTH
    ones_row = (lax.broadcasted_iota(jnp.int32, (V_ROWS - HEAD_DIM, tm), 0) == 0).astype(BF16)
    for j in range(ATTN_KV_HEADS):
        vt_ref[j, 0:HEAD_DIM] = ch[o + HEAD_DIM * j:o + HEAD_DIM * (j + 1)].astype(BF16)
        vt_ref[j, HEAD_DIM:V_ROWS] = ones_row
    o += KV_WIDTH
    ga_ref[...] = _silu(ch[o:o + ATTN_WIDTH])
    o += ATTN_WIDTH
    su_ref[...] = ch[o:o + SSM_WIDTH]
    sg_ref[...] = _silu(ch[o + SSM_WIDTH:o + 2 * SSM_WIDTH])
    o += 2 * SSM_WIDTH
    hi_ref[...] = ch[o:o + 3 * HYENA_WIDTH]
    hg_ref[...] = _silu(ch[o + 3 * HYENA_WIDTH:])


def _inproj(x2, pre_g, w_tok, w_ch_t, rope, qg, kg, ones, length):
    cos, s_lo, s_hi, cos_t, sin_t = rope
    n = x2.shape[0]
    tm = TOKEN_TILE
    nt = n // tm
    tiles_per_seq = length // tm
    tok_spec = lambda w: pl.BlockSpec((tm, w), lambda i: (i, 0))
    ch_spec = lambda w: pl.BlockSpec((w, tm), lambda i: (0, i))
    const = lambda shape: pl.BlockSpec(shape, lambda i: (0,) * len(shape))
    pos_spec = pl.BlockSpec((tm, LANES), lambda i: (i % tiles_per_seq, 0))
    pos_t_spec = pl.BlockSpec((HEAD_DIM, tm), lambda i: (0, i % tiles_per_seq))
    return pl.pallas_call(
        _inproj_kernel,
        grid=(nt,),
        in_specs=[tok_spec(D_MODEL), const((1, D_MODEL)), const((D_MODEL, TOK_WIDTH)), const((CH_WIDTH, D_MODEL)),
                  pos_spec, pos_spec, pos_spec, pos_t_spec, pos_t_spec,
                  const((HEAD_DIM, 1)), const((1, KV_WIDTH)), const((KV_WIDTH, KV_WIDTH))],
        out_specs=[ch_spec(ATTN_WIDTH),
                   pl.BlockSpec((ATTN_KV_HEADS, tm, HEAD_DIM), lambda i: (0, i, 0)),
                   pl.BlockSpec((ATTN_KV_HEADS, V_ROWS, tm), lambda i: (0, 0, i)),
                   ch_spec(ATTN_WIDTH), tok_spec(MEM_WIDTH),
                   ch_spec(SSM_WIDTH), ch_spec(SSM_WIDTH), ch_spec(3 * HYENA_WIDTH), ch_spec(HYENA_WIDTH)],
        out_shape=[jax.ShapeDtypeStruct((ATTN_WIDTH, n), BF16),
                   jax.ShapeDtypeStruct((ATTN_KV_HEADS, n, HEAD_DIM), BF16),
                   jax.ShapeDtypeStruct((ATTN_KV_HEADS, V_ROWS, n), BF16),
                   jax.ShapeDtypeStruct((ATTN_WIDTH, n), F32),
                   jax.ShapeDtypeStruct((n, MEM_WIDTH), BF16),
                   jax.ShapeDtypeStruct((SSM_WIDTH, n), F32),
                   jax.ShapeDtypeStruct((SSM_WIDTH, n), F32),
                   jax.ShapeDtypeStruct((3 * HYENA_WIDTH, n), F32),
                   jax.ShapeDtypeStruct((HYENA_WIDTH, n), F32)],
        compiler_params=_cparams(("parallel",)),
        name="inproj",
    )(x2, pre_g, w_tok, w_ch_t, cos, s_lo, s_hi, cos_t, sin_t, qg, kg, ones)


def _flash_kernel(qt_ref, k_ref, vt_ref, o_ref, acc_s, s_s, *, tq, tk, n_kv):
    w = ATTN_GROUP * tq
    q4t = jnp.concatenate([qt_ref[HEAD_DIM * i:HEAD_DIM * (i + 1), :] for i in range(ATTN_GROUP)], axis=1)
    acc_s[...] = jnp.zeros(acc_s.shape, F32)

    def scores(c, slot):
        start = pl.multiple_of(c * tk, tk)
        s_s[slot] = jnp.dot(k_ref[0, pl.ds(start, tk), :], q4t, preferred_element_type=F32)

    def consume(c, slot, m):
        start = pl.multiple_of(c * tk, tk)
        m_new = jnp.maximum(m, jnp.max(s_s[slot], axis=0, keepdims=True))
        p = jnp.exp2(s_s[slot] - m_new).astype(BF16)
        pv = jnp.dot(vt_ref[0, :, pl.ds(start, tk)], p, preferred_element_type=F32)
        acc_s[...] = jnp.exp2(m - m_new) * acc_s[...] + pv
        return m_new

    scores(0, 0)

    def body(cc, m):
        c = 2 * cc
        scores(c + 1, 1)
        m = consume(c, 0, m)
        scores(jnp.minimum(c + 2, n_kv - 1), 0)
        return consume(c + 1, 1, m)

    lax.fori_loop(0, n_kv // 2, body, jnp.full((1, w), -jnp.inf, F32), unroll=2)
    o = acc_s[0:HEAD_DIM] * (1.0 / acc_s[HEAD_DIM:HEAD_DIM + 1])
    for i in range(ATTN_GROUP):
        o_ref[HEAD_DIM * i:HEAD_DIM * (i + 1), :] = o[:, i * tq:(i + 1) * tq]


def _flash_attention(q_t, k, v_t, batch, length):
    n = q_t.shape[1]
    tq, tk = ATTN_Q_TILE, ATTN_K_TILE
    nq, nk = length // tq, length // tk
    gw = ATTN_GROUP * HEAD_DIM
    return pl.pallas_call(
        functools.partial(_flash_kernel, tq=tq, tk=tk, n_kv=nk),
        grid=(batch, ATTN_KV_HEADS, nq),
        in_specs=[pl.BlockSpec((gw, tq), lambda b, h, i: (h, b * nq + i)),
                  pl.BlockSpec((1, length, HEAD_DIM), lambda b, h, i: (h, b, 0)),
                  pl.BlockSpec((1, V_ROWS, length), lambda b, h, i: (h, 0, b))],
        out_specs=pl.BlockSpec((gw, tq), lambda b, h, i: (h, b * nq + i)),
        out_shape=jax.ShapeDtypeStruct((ATTN_WIDTH, n), F32),
        scratch_shapes=[pltpu.VMEM((V_ROWS, ATTN_GROUP * tq), F32),
                        pltpu.VMEM((2, tk, ATTN_GROUP * tq), F32)],
        compiler_params=_cparams(("parallel", "parallel", "parallel")),
        name="flash_attn",
    )(q_t, k, v_t)


def _memkv_kernel(mem_ref, g_ref, w_ref, kv_ref):
    m = mem_ref[...]
    mn = m * lax.rsqrt(jnp.mean(m * m, axis=-1, keepdims=True) + EPS) * g_ref[...]
    kv_ref[...] = jnp.dot(mn.astype(BF16), w_ref[...], preferred_element_type=F32)


def _memkv(mem2, mem_g, w_kv):
    rows = mem2.shape[0]
    return pl.pallas_call(
        _memkv_kernel,
        grid=(rows // MEM_TOKENS,),
        in_specs=[pl.BlockSpec((MEM_TOKENS, D_MODEL), lambda i: (i, 0)),
                  pl.BlockSpec((1, D_MODEL), lambda i: (0, 0)),
                  pl.BlockSpec((D_MODEL, 2 * MEM_WIDTH), lambda i: (0, 0))],
        out_specs=pl.BlockSpec((MEM_TOKENS, 2 * MEM_WIDTH), lambda i: (i, 0)),
        out_shape=jax.ShapeDtypeStruct((rows, 2 * MEM_WIDTH), F32),
        compiler_params=_cparams(("parallel",)),
        name="mem_kv",
    )(mem2, mem_g, w_kv)


def _ssm_kernel(u_ref, kf_ref, kb_ref, p_ref, q_ref, a1_ref, a2_ref, y_ref, g_s, s_s, h_s, *, batch, n_chunks):
    t = SSM_CHUNK
    rows = batch * n_chunks
    half = 2 * SSM_STATE
    causal = lax.broadcasted_iota(jnp.int32, (t, t), 1) >= lax.broadcasted_iota(jnp.int32, (t, t), 0)

    def build(cp, carry):
        r0 = pl.multiple_of(cp * t, t)
        kf_rows, kb_rows = kf_ref[0, cp], kb_ref[0, cp]
        for c in range(SSM_GROUP):
            lo = pltpu.roll(jnp.broadcast_to(kf_rows[c:c + 1], (t, t)), 0, 1, stride=1, stride_axis=0)
            up = pltpu.roll(jnp.broadcast_to(kb_rows[c:c + 1], (t, t)), 0, 1, stride=1, stride_axis=0)
            g_s[pl.ds(r0, t), c * t:(c + 1) * t] = jnp.where(causal, lo, up).astype(BF16)
        return carry

    lax.fori_loop(0, SSM_GROUP, build, 0)
    u = jnp.concatenate([u_ref[0, c].reshape(rows, t) for c in range(SSM_GROUP)], axis=1).astype(BF16)
    y_intra = jnp.dot(u, g_s[...], preferred_element_type=F32)
    s_s[...] = jnp.dot(u, p_ref[0], preferred_element_type=F32)
    a1f, a2f = a1_ref[0, :, :half], a2_ref[0, :, :half]
    a1b, a2b = a1_ref[0, :, half:], a2_ref[0, :, half:]

    sub = SUBLANES
    n_blocks = n_chunks // sub

    def step(kb, carry):
        new = []
        for b in range(batch):
            hf, hb = carry[2 * b], carry[2 * b + 1]
            base_f = pl.multiple_of(b * n_chunks + kb * sub, sub)
            base_b = pl.multiple_of(b * n_chunks + (n_blocks - 1 - kb) * sub, sub)
            sf = s_s[pl.ds(base_f, sub), 0:half]
            sb = s_s[pl.ds(base_b, sub), half:2 * half]
            hf_rows, hb_rows = [], [None] * sub
            for i in range(sub):
                hf_rows.append(hf)
                hf = a1f * hf + a2f * pltpu.roll(hf, SSM_STATE, 1) + sf[i:i + 1]
            for i in range(sub - 1, -1, -1):
                hb_rows[i] = hb
                hb = a1b * hb + a2b * pltpu.roll(hb, SSM_STATE, 1) + sb[i:i + 1]
            h_s[pl.ds(base_f, sub), 0:half] = jnp.concatenate(hf_rows, axis=0)
            h_s[pl.ds(base_b, sub), half:2 * half] = jnp.concatenate(hb_rows, axis=0)
            new += [hf, hb]
        return tuple(new)

    zero = jnp.zeros((1, half), F32)
    lax.fori_loop(0, n_blocks, step, (zero,) * (2 * batch))
    y = y_intra + jnp.dot(h_s[...].astype(BF16), q_ref[0], preferred_element_type=F32)
    for c in range(SSM_GROUP):
        y_ref[0, c] = y[:, c * t:(c + 1) * t].reshape(batch, n_chunks, t)


def _ssm_scan(u_t, kf, kb, p_mat, q_mat, a1, a2, batch, length):
    t = SSM_CHUNK
    nk = length // t
    gt = SSM_GROUP * t
    u5 = u_t.reshape(SSM_GROUPS, SSM_GROUP, batch, nk, t)
    blk = (1, SSM_GROUP, batch, nk, t)
    lag_spec = pl.BlockSpec((1, SSM_GROUP, SSM_GROUP, t), lambda g: (g, 0, 0, 0))
    y5 = pl.pallas_call(
        functools.partial(_ssm_kernel, batch=batch, n_chunks=nk),
        grid=(SSM_GROUPS,),
        in_specs=[pl.BlockSpec(blk, lambda g: (g, 0, 0, 0, 0)), lag_spec, lag_spec,
                  pl.BlockSpec((1, gt, 4 * SSM_STATE), lambda g: (g, 0, 0)),
                  pl.BlockSpec((1, 4 * SSM_STATE, gt), lambda g: (g, 0, 0)),
                  pl.BlockSpec((1, 1, 4 * SSM_STATE), lambda g: (g, 0, 0)),
                  pl.BlockSpec((1, 1, 4 * SSM_STATE), lambda g: (g, 0, 0))],
        out_specs=pl.BlockSpec(blk, lambda g: (g, 0, 0, 0, 0)),
        out_shape=jax.ShapeDtypeStruct(u5.shape, F32),
        scratch_shapes=[pltpu.VMEM((gt, gt), BF16),
                        pltpu.VMEM((batch * nk, 4 * SSM_STATE), F32),
                        pltpu.VMEM((batch * nk, 4 * SSM_STATE), F32)],
        compiler_params=_cparams(("parallel",)),
        name="ssm_scan",
    )(u5, kf, kb, p_mat, q_mat, a1, a2)
    return y5.reshape(SSM_WIDTH, batch * length)


def _ssm_tables(a_re, a_im, log_step, b_re, b_im, c_re, c_im):
    t = SSM_CHUNK
    hi = lax.Precision.HIGHEST
    lam = lax.complex(a_re.astype(F32), a_im.astype(F32))
    step = jnp.exp(log_step.astype(F32))[..., None]
    ls = lam * step
    a_bar = jnp.exp(ls)
    b_bar = ((a_bar - 1.0) / lam)[..., None] * lax.complex(b_re.astype(F32), b_im.astype(F32))
    c = lax.complex(c_re.astype(F32), c_im.astype(F32))
    tau = jnp.arange(t + 1, dtype=F32)
    pw = jnp.exp(ls[..., None] * tau)
    kern = jnp.einsum('dgcp,dgpt,dgpe->dgtce', c, pw[..., :t], b_bar, precision=hi).real
    kf = kern[0].at[:, 0].add(kern[1][:, 0]).transpose(0, 3, 2, 1)
    kb = jnp.roll(kern[1][:, ::-1], 1, axis=1).transpose(0, 3, 2, 1)
    pw_f, pw_b = pw[0], pw[1]
    pf = pw_f[:, :, t - 1 - jnp.arange(t)][..., None] * b_bar[0][:, :, None, :]
    pb = pw_b[:, :, :t][..., None] * b_bar[1][:, :, None, :]
    to_rows = lambda z: z.transpose(0, 3, 2, 1).reshape(SSM_GROUPS, SSM_GROUP * t, SSM_STATE)
    p_mat = jnp.concatenate([to_rows(pf.real), to_rows(pf.imag), to_rows(pb.real), to_rows(pb.imag)], axis=-1)
    qf = c[0].transpose(0, 2, 1)[..., None] * pw_f[:, :, 1:][:, :, None, :]
    qb = c[1].transpose(0, 2, 1)[..., None] * pw_b[:, :, t - jnp.arange(t)][:, :, None, :]
    to_cols = lambda z: z.reshape(SSM_GROUPS, SSM_STATE, SSM_GROUP * t)
    q_mat = jnp.concatenate([to_cols(qf.real), -to_cols(qf.imag), to_cols(qb.real), -to_cols(qb.imag)], axis=1)
    at = pw[..., t]
    a1 = jnp.concatenate([at[0].real, at[0].real, at[1].real, at[1].real], axis=-1)[:, None, :]
    a2 = jnp.concatenate([-at[0].imag, at[0].imag, -at[1].imag, at[1].imag], axis=-1)[:, None, :]
    return kf, kb, p_mat.astype(BF16), q_mat.astype(BF16), a1, a2


def _dft_tables(length):
    j = DFT_INNER
    n = 2 * length
    rn = n // j
    rh = rn // 2
    k1 = jnp.arange(rn, dtype=jnp.int32)
    ang1 = (2.0 * math.pi / rn) * ((k1[:, None] * k1[None, :]) % rn).astype(F32)
    c1, s1 = jnp.cos(ang1), jnp.sin(ang1)
    f1_full = jnp.concatenate([c1, -s1], axis=0)
    f1_top = f1_full[:, :rh]
    f1_inv = jnp.concatenate([c1[:rh], -s1[:rh]], axis=1)
    jj = jnp.arange(j, dtype=jnp.int32)
    angt = (2.0 * math.pi / n) * (k1[:, None] * jj[None, :]).astype(F32)
    tw_re, tw_im = jnp.cos(angt), -jnp.sin(angt)
    ang2 = (2.0 * math.pi / j) * ((jj[:, None] * jj[None, :]) % j).astype(F32)
    f2_cat = jnp.concatenate([jnp.cos(ang2), -jnp.sin(ang2)], axis=1)
    return dict(f1_full=f1_full.astype(BF16), f1_top=f1_top.astype(BF16), f1_inv=f1_inv.astype(BF16),
                tw_re=tw_re, tw_im=tw_im, f2_cat=f2_cat.astype(BF16), rn=rn, rh=rh)


def _dft_forward(a, tw_re, tw_im, f2_cat, rn):
    j = DFT_INNER
    a_re, a_im = a[:rn], a[rn:]
    ap = jnp.concatenate([a_re * tw_re - a_im * tw_im, a_re * tw_im + a_im * tw_re], axis=0).astype(BF16)
    m = jnp.dot(ap, f2_cat, preferred_element_type=F32)
    return m[:rn, :j] - m[rn:, j:], m[:rn, j:] + m[rn:, :j]


def _dft_inverse(z_re, z_im, tw_re, tw_im, f2_cat, f1_inv, rn):
    j = DFT_INNER
    z = jnp.concatenate([z_re, z_im], axis=0).astype(BF16)
    m = jnp.dot(z, f2_cat, preferred_element_type=F32)
    b_re = m[:rn, :j] + m[rn:, j:]
    b_im = m[rn:, :j] - m[:rn, j:]
    bp = jnp.concatenate([b_re * tw_re + b_im * tw_im, b_im * tw_re - b_re * tw_im], axis=0).astype(BF16)
    return jnp.dot(f1_inv, bp, preferred_element_type=F32)


def _filter_hidden_kernel(bands_ref, w1t_ref, w1c_ref, w1s_ref, b1_ref, w2t_ref, b2_ref, hid_ref, *, length, tl):
    hi = lax.Precision.HIGHEST
    base = pl.program_id(0) * tl
    idx = (lax.broadcasted_iota(jnp.int32, (1, tl), 1) + base).astype(F32)
    for d in range(2):
        pos = idx if d == 0 else float(length) - idx
        tt = pos / float(length)
        wpos = (2.0 * math.pi / length) * pos
        arg = bands_ref[...] * wpos
        h1 = (w1t_ref[...] * tt
              + jnp.dot(w1c_ref[...], jnp.cos(arg), preferred_element_type=F32, precision=hi)
              - jnp.dot(w1s_ref[...], jnp.sin(arg), preferred_element_type=F32, precision=hi))
        h1 = jnp.sin(h1 + b1_ref[...])
        h2 = jnp.dot(w2t_ref[...], h1, preferred_element_type=F32, precision=hi)
        hid_ref[d] = jnp.sin(h2 + b2_ref[...])


def _filter_taps_kernel(hid_ref, w3t_ref, decay_ref, filt_ref, *, length, tl):
    hi = lax.Precision.HIGHEST
    d = pl.program_id(0) % 2
    base = pl.program_id(1) * tl
    idx = (lax.broadcasted_iota(jnp.int32, (1, tl), 1) + base).astype(F32)
    pos = jnp.where(d == 0, idx, float(length) - idx)
    tt = pos / float(length)
    f = jnp.dot(w3t_ref[...], hid_ref[0], preferred_element_type=F32, precision=hi)
    f = f * jnp.exp(-tt * decay_ref[...])
    filt_ref[...] = jnp.where(jnp.logical_and(d == 1, idx == 0.0), 0.0, f)


def _filter_dft_kernel(filt_ref, f1_ref, twre_ref, twim_ref, f2_ref, kf_ref, *, rn, cb):
    for c in range(cb):
        r = jnp.concatenate([filt_ref[0, 0, c], filt_ref[0, 1, c]], axis=0).astype(BF16)
        a = jnp.dot(f1_ref[...], r, preferred_element_type=F32)
        x_re, x_im = _dft_forward(a, twre_ref[...], twim_ref[...], f2_ref[...], rn)
        kf_ref[0, c, 0] = x_re
        kf_ref[0, c, 1] = x_im


def _hyena_filters(length, tabs, w1, b1, w2, b2, w3, log_decay):
    tl = min(FILTER_LANE_TILE, length)
    nl = length // tl
    rn, rh, j = tabs["rn"], tabs["rh"], DFT_INNER
    bands = jnp.linspace(1e-4, FILTER_BANDS - 1, FILTER_BANDS, dtype=F32)[:, None]
    w1f = w1.astype(F32)
    const = lambda shape: pl.BlockSpec(shape, lambda *_: (0,) * len(shape))
    hid = pl.pallas_call(
        functools.partial(_filter_hidden_kernel, length=length, tl=tl),
        grid=(nl,),
        in_specs=[const((FILTER_BANDS, 1)), const((FILTER_HIDDEN, 1)), const((FILTER_HIDDEN, FILTER_BANDS)),
                  const((FILTER_HIDDEN, FILTER_BANDS)), const((FILTER_HIDDEN, 1)),
                  const((FILTER_HIDDEN, FILTER_HIDDEN)), const((FILTER_HIDDEN, 1))],
        out_specs=pl.BlockSpec((2, FILTER_HIDDEN, tl), lambda i: (0, 0, i)),
        out_shape=jax.ShapeDtypeStruct((2, FILTER_HIDDEN, length), F32),
        compiler_params=_cparams(("parallel",)),
        name="hyena_filter_hidden",
    )(bands, w1f[0:1].T, w1f[1:1 + FILTER_BANDS].T, w1f[1 + FILTER_BANDS:].T, b1.astype(F32)[:, None],
      w2.astype(F32).T, b2.astype(F32)[:, None])
    n_od = HYENA_ORDER * 2
    w3t = w3.astype(F32).T.reshape(n_od, HYENA_WIDTH, FILTER_HIDDEN)
    decay = jnp.exp(log_decay.astype(F32)).reshape(n_od, HYENA_WIDTH, 1)
    filt = pl.pallas_call(
        functools.partial(_filter_taps_kernel, length=length, tl=tl),
        grid=(n_od, nl),
        in_specs=[pl.BlockSpec((1, FILTER_HIDDEN, tl), lambda od, i: (od % 2, 0, i)),
                  pl.BlockSpec((None, HYENA_WIDTH, FILTER_HIDDEN), lambda od, i: (od, 0, 0)),
                  pl.BlockSpec((None, HYENA_WIDTH, 1), lambda od, i: (od, 0, 0))],
        out_specs=pl.BlockSpec((None, HYENA_WIDTH, tl), lambda od, i: (od, 0, i)),
        out_shape=jax.ShapeDtypeStruct((n_od, HYENA_WIDTH, length), F32),
        compiler_params=_cparams(("parallel", "parallel")),
        name="hyena_filter_taps",
    )(hid, w3t, decay)
    filt6 = filt.reshape(HYENA_ORDER, 2, HYENA_WIDTH, rh, j)
    cb = HYENA_CH_BLOCK
    return pl.pallas_call(
        functools.partial(_filter_dft_kernel, rn=rn, cb=cb),
        grid=(HYENA_ORDER, HYENA_WIDTH // cb),
        in_specs=[pl.BlockSpec((1, 2, cb, rh, j), lambda o, c: (o, 0, c, 0, 0)),
                  const((2 * rn, rn)), const((rn, j)), const((rn, j)), const((j, 2 * j))],
        out_specs=pl.BlockSpec((1, cb, 2, rn, j), lambda o, c: (o, c, 0, 0, 0)),
        out_shape=jax.ShapeDtypeStruct((HYENA_ORDER, HYENA_WIDTH, 2, rn, j), F32),
        compiler_params=_cparams(("parallel", "parallel")),
        name="hyena_filter_dft",
    )(filt6, tabs["f1_full"], tabs["tw_re"], tabs["tw_im"], tabs["f2_cat"])


def _hyena_kernel(scw_ref, scb_ref, hb_ref, x_ref, hg_ref, kf_ref, f1_ref, f1i_ref, twre_ref, twim_ref, f2_ref,
                  o_ref, *, batch, rn, rh, cb, cs):
    j = DFT_INNER
    n_inv = 1.0 / (rn * j)
    lane = lax.broadcasted_iota(jnp.int32, (rh, j), 1)
    row = lax.broadcasted_iota(jnp.int32, (rh, j), 0)
    first = jnp.logical_and(lane == 0, row == 0)
    last = jnp.logical_and(lane == j - 1, row == rh - 1)
    tw_re, tw_im = twre_ref[...], twim_ref[...]

    def prev_t(x):
        p = pltpu.roll(x, 1, 1)
        return jnp.where(first, 0.0, jnp.where(lane == 0, pltpu.roll(p, 1, 0), p))

    def next_t(x):
        p = pltpu.roll(x, j - 1, 1)
        return jnp.where(last, 0.0, jnp.where(lane == j - 1, pltpu.roll(p, rh - 1, 0), p))

    def split(m, i):
        return m[2 * rn * i:2 * rn * i + rn], m[2 * rn * i + rn:2 * rn * (i + 1)]

    def long_conv(xs, ks):
        rows = []
        for x in xs:
            a = jnp.dot(f1_ref[...], x.astype(BF16), preferred_element_type=F32)
            a_re, a_im = a[:rn], a[rn:]
            rows += [a_re * tw_re - a_im * tw_im, a_re * tw_im + a_im * tw_re]
        m = jnp.dot(jnp.concatenate(rows, axis=0).astype(BF16), f2_ref[...], preferred_element_type=F32)
        rows = []
        for i, (k_re, k_im) in enumerate(ks):
            m_re, m_im = split(m, i)
            x_re, x_im = m_re[:, :j] - m_im[:, j:], m_re[:, j:] + m_im[:, :j]
            rows += [x_re * k_re - x_im * k_im, x_re * k_im + x_im * k_re]
        m = jnp.dot(jnp.concatenate(rows, axis=0).astype(BF16), f2_ref[...], preferred_element_type=F32)
        outs = []
        for i in range(len(xs)):
            m_re, m_im = split(m, i)
            b_re, b_im = m_re[:, :j] + m_im[:, j:], m_im[:, :j] - m_re[:, j:]
            bp = jnp.concatenate([b_re * tw_re + b_im * tw_im, b_im * tw_re - b_re * tw_im], axis=0).astype(BF16)
            outs.append(jnp.dot(f1i_ref[...], bp, preferred_element_type=F32) * n_inv)
        return outs

    cbase = pl.program_id(0) * cb
    for c0 in range(0, cb, cs):
        chans = [(ci, b) for ci in range(c0, c0 + cs) for b in range(batch)]
        segs = []
        for sgm in range(3):
            seg = []
            for ci, b in chans:
                chs = sgm * HYENA_WIDTH + cbase + ci
                x = x_ref[sgm, ci, b]
                seg.append(prev_t(x) * scw_ref[chs] + x * scw_ref[3 * HYENA_WIDTH + chs]
                           + next_t(x) * scw_ref[6 * HYENA_WIDTH + chs] + scb_ref[chs])
            segs.append(seg)
        z = segs[0]
        for o in range(HYENA_ORDER):
            conv = long_conv(z, [(kf_ref[o, ci, 0], kf_ref[o, ci, 1]) for ci, _ in chans])
            z = [segs[o + 1][i] * (conv[i] + z[i] * hb_ref[o * HYENA_WIDTH + cbase + ci])
                 for i, (ci, _) in enumerate(chans)]
        for i, (ci, b) in enumerate(chans):
            o_ref[ci, b] = z[i] * hg_ref[ci, b]


def _hyena(hi_t, hg_t, kf, tabs, short_w, short_b, hy_bias, batch, length):
    rn, rh, j = tabs["rn"], tabs["rh"], DFT_INNER
    cb = HYENA_CH_BLOCK
    x5 = hi_t.reshape(3, HYENA_WIDTH, batch, rh, j)
    g4 = hg_t.reshape(HYENA_WIDTH, batch, rh, j)
    smem = pl.BlockSpec(memory_space=pltpu.SMEM)
    const = lambda shape: pl.BlockSpec(shape, lambda c: (0,) * len(shape))
    out = pl.pallas_call(
        functools.partial(_hyena_kernel, batch=batch, rn=rn, rh=rh, cb=cb, cs=HYENA_CH_SUB),
        grid=(HYENA_WIDTH // cb,),
        in_specs=[smem, smem, smem,
                  pl.BlockSpec((3, cb, batch, rh, j), lambda c: (0, c, 0, 0, 0)),
                  pl.BlockSpec((cb, batch, rh, j), lambda c: (c, 0, 0, 0)),
                  pl.BlockSpec((HYENA_ORDER, cb, 2, rn, j), lambda c: (0, c, 0, 0, 0)),
                  const((2 * rn, rh)), const((rh, 2 * rn)), const((rn, j)), const((rn, j)), const((j, 2 * j))],
        out_specs=pl.BlockSpec((cb, batch, rh, j), lambda c: (c, 0, 0, 0)),
        out_shape=jax.ShapeDtypeStruct(g4.shape, F32),
        compiler_params=_cparams(("parallel",)),
        name="hyena_conv",
    )(short_w.astype(F32).reshape(-1), short_b.astype(F32), hy_bias.astype(F32).reshape(-1),
      x5, g4, kf, tabs["f1_top"], tabs["f1_inv"], tabs["tw_re"], tabs["tw_im"], tabs["f2_cat"])
    return out.reshape(HYENA_WIDTH, batch * length)


def _rms_rows(x, g):
    return x * lax.rsqrt(jnp.mean(x * x, axis=-1, keepdims=True) + EPS) * g


def _rms_cols(x, g):
    return x * lax.rsqrt(jnp.mean(x * x, axis=0, keepdims=True) + EPS) * g


def _post_kernel(ao_ref, ga_ref, mq_ref, mk_ref, mv_ref, y_ref, u_ref, sg_ref, hy_ref, x_ref,
                 d_ref, wglu_ref, g_attn_ref, g_ssm_ref, g_hy_ref, g_mem_ref, post_g_ref, wo_ref, out_ref):
    attn_n = _rms_cols(ao_ref[...] * ga_ref[...], g_attn_ref[...])
    s = jnp.dot(mq_ref[...], mk_ref[0], preferred_element_type=F32)
    ps = []
    for h in range(MEM_HEADS):
        sh = s[:, MEM_TOKENS * h:MEM_TOKENS * (h + 1)]
        e = jnp.exp(sh - jnp.max(sh, axis=-1, keepdims=True))
        ps.append(e * (1.0 / jnp.sum(e, axis=-1, keepdims=True)))
    p = jnp.concatenate(ps, axis=1).astype(BF16)
    cross_n = _rms_rows(jnp.dot(p, mv_ref[0], preferred_element_type=F32), g_mem_ref[...])
    y = y_ref[...] + d_ref[...] * u_ref[...]
    g = y * (0.5 * (1.0 + jnp.tanh(math.sqrt(2.0 / math.pi) * (y + 0.044715 * (y * y * y)))))
    gz = jnp.dot(wglu_ref[...], g.astype(BF16), preferred_element_type=F32)
    ssm_n = _rms_cols(g * _sigmoid(gz) * sg_ref[...], g_ssm_ref[...])
    hy_n = _rms_cols(hy_ref[...], g_hy_ref[...])
    o1, o2, o3 = ATTN_WIDTH, ATTN_WIDTH + SSM_WIDTH, ATTN_WIDTH + SSM_WIDTH + HYENA_WIDTH
    mixed = (jnp.dot(attn_n.T.astype(BF16), wo_ref[0:o1], preferred_element_type=F32)
             + jnp.dot(ssm_n.T.astype(BF16), wo_ref[o1:o2], preferred_element_type=F32)
             + jnp.dot(hy_n.T.astype(BF16), wo_ref[o2:o3], preferred_element_type=F32)
             + jnp.dot(cross_n.astype(BF16), wo_ref[o3:], preferred_element_type=F32))
    out_ref[...] = x_ref[...] + _rms_rows(mixed, post_g_ref[...])


def _post(ao, ga, mq, mk_bd, mv_bd, y_t, u_t, sg_t, hy_t, x2, d, wglu_t, g_attn, g_ssm, g_hy, g_mem, post_g, wo,
          length):
    n = x2.shape[0]
    tm = TOKEN_TILE
    tiles_per_seq = length // tm
    tok_spec = lambda w: pl.BlockSpec((tm, w), lambda i: (i, 0))
    ch_spec = lambda w: pl.BlockSpec((w, tm), lambda i: (0, i))
    const = lambda shape: pl.BlockSpec(shape, lambda i: (0,) * len(shape))
    hm = MEM_HEADS * MEM_TOKENS
    return pl.pallas_call(
        _post_kernel,
        grid=(n // tm,),
        in_specs=[ch_spec(ATTN_WIDTH), ch_spec(ATTN_WIDTH), tok_spec(MEM_WIDTH),
                  pl.BlockSpec((1, MEM_WIDTH, hm), lambda i: (i // tiles_per_seq, 0, 0)),
                  pl.BlockSpec((1, hm, MEM_WIDTH), lambda i: (i // tiles_per_seq, 0, 0)),
                  ch_spec(SSM_WIDTH), ch_spec(SSM_WIDTH), ch_spec(SSM_WIDTH), ch_spec(HYENA_WIDTH),
                  tok_spec(D_MODEL),
                  const((SSM_WIDTH, 1)), const((SSM_WIDTH, SSM_WIDTH)),
                  const((ATTN_WIDTH, 1)), const((SSM_WIDTH, 1)), const((HYENA_WIDTH, 1)), const((1, MEM_WIDTH)),
                  const((1, D_MODEL)), const((MIX_WIDTH, D_MODEL))],
        out_specs=tok_spec(D_MODEL),
        out_shape=jax.ShapeDtypeStruct((n, D_MODEL), F32),
        compiler_params=_cparams(("parallel",)),
        name="post",
    )(ao, ga, mq, mk_bd, mv_bd, y_t, u_t, sg_t, hy_t, x2, d, wglu_t, g_attn, g_ssm, g_hy, g_mem, post_g, wo)


def _rope_tables(length):
    rows = length // GRID_W
    row = jnp.broadcast_to(jnp.arange(rows, dtype=F32)[:, None], (rows, GRID_W)).reshape(length)
    col = jnp.broadcast_to(jnp.arange(GRID_W, dtype=F32)[None, :], (rows, GRID_W)).reshape(length)
    inv_freq = ROPE_THETA ** (-jnp.arange(ROPE_FREQS, dtype=F32) / ROPE_FREQS)
    ang = jnp.stack([row[:, None] * inv_freq, col[:, None] * inv_freq], axis=1)
    ang = jnp.broadcast_to(ang[:, :, None, :], (length, 2, 2, ROPE_FREQS)).reshape(length, HEAD_DIM)
    cos1, sin1 = jnp.cos(ang), jnp.sin(ang)
    low1 = (jnp.arange(HEAD_DIM) % (2 * ROPE_FREQS)) < ROPE_FREQS
    cos_t, sin_t = cos1.T, jnp.where(low1, -sin1, sin1).T
    cos = jnp.concatenate([cos1, cos1], axis=1)
    sin = jnp.concatenate([sin1, sin1], axis=1)
    low = jnp.concatenate([low1, low1])
    return cos, jnp.where(low, -sin, 0.0), jnp.where(low, 0.0, sin), cos_t, sin_t


def _block_diag_heads(mk, mv, batch):
    mk4 = mk.reshape(batch, MEM_TOKENS, MEM_HEADS, HEAD_DIM)
    mv4 = mv.reshape(batch, MEM_TOKENS, MEM_HEADS, HEAD_DIM)
    eye = jnp.eye(MEM_HEADS, dtype=mk.dtype)
    k_bd = jnp.einsum('bmhd,hg->bhdgm', mk4, eye).reshape(batch, MEM_WIDTH, MEM_HEADS * MEM_TOKENS)
    v_bd = jnp.einsum('bmhd,hg->bhmgd', mv4, eye).reshape(batch, MEM_HEADS * MEM_TOKENS, MEM_WIDTH)
    return k_bd.astype(BF16), v_bd.astype(BF16)


def _layer_weights(layer, p):
    w_in = p["w_in"][layer]
    a, kv = ATTN_WIDTH, KV_WIDTH
    k0, v0, g0, mq0 = a, a + kv, a + 2 * kv, w_in.shape[1] - MEM_WIDTH
    w_tok = jnp.concatenate([w_in[:, k0:v0], w_in[:, mq0:]], axis=1).astype(BF16)
    w_ch_t = jnp.concatenate([w_in[:, :k0], w_in[:, v0:mq0]], axis=1).T.astype(BF16)
    bg = p["branch_norm"][layer].astype(F32)
    o1, o2, o3 = a, a + SSM_WIDTH, a + SSM_WIDTH + HYENA_WIDTH
    head_id = jnp.arange(kv) // HEAD_DIM
    return dict(
        w_tok=w_tok, w_ch_t=w_ch_t,
        pre_g=p["pre_norm"][layer].astype(F32)[None, :], post_g=p["post_norm"][layer].astype(F32)[None, :],
        qg=p["q_norm"][layer].astype(F32)[:, None],
        kg=jnp.tile(p["k_norm"][layer].astype(F32), ATTN_KV_HEADS)[None, :],
        ones=(head_id[:, None] == head_id[None, :]).astype(BF16),
        mem_g=p["mem_norm"][layer].astype(F32)[None, :], w_mem_kv=p["w_mem_kv"][layer].astype(BF16),
        ssm=_ssm_tables(p["ssm_a_re"][layer], p["ssm_a_im"][layer], p["ssm_log_step"][layer], p["ssm_b_re"][layer],
                        p["ssm_b_im"][layer], p["ssm_c_re"][layer], p["ssm_c_im"][layer]),
        d=p["ssm_d"][layer].astype(F32)[:, None], wglu_t=p["ssm_w_glu"][layer].T.astype(BF16),
        g_attn=bg[:o1, None], g_ssm=bg[o1:o2, None], g_hy=bg[o2:o3, None], g_mem=bg[None, o3:],
        wo=p["w_out"][layer].astype(BF16),
    )


def _mixer_layer(x2, mem2, lw, kf, tabs, rope, p, layer, batch, length):
    q_t, k, v_t, ga, mq, su_t, sg_t, hi_t, hg_t = _inproj(x2, lw["pre_g"], lw["w_tok"], lw["w_ch_t"], rope,
                                                          lw["qg"], lw["kg"], lw["ones"], length)
    ao = _flash_attention(q_t, k, v_t, batch, length)
    mem_kv = _memkv(mem2, lw["mem_g"], lw["w_mem_kv"])
    mk_bd, mv_bd = _block_diag_heads(mem_kv[:, :MEM_WIDTH], mem_kv[:, MEM_WIDTH:], batch)
    y_t = _ssm_scan(su_t, *lw["ssm"], batch, length)
    hy_t = _hyena(hi_t, hg_t, kf, tabs, p["hyena_short_w"][layer], p["hyena_short_b"][layer],
                  p["hyena_bias"][layer], batch, length)
    return _post(ao, ga, mq, mk_bd, mv_bd, y_t, su_t, sg_t, hy_t, x2, lw["d"], lw["wglu_t"],
                 lw["g_attn"], lw["g_ssm"], lw["g_hy"], lw["g_mem"], lw["post_g"], lw["wo"], length)


def _run_group(x, mem, weights, p):
    batch, length, _ = x.shape
    rope = _rope_tables(length)
    tabs = _dft_tables(length)
    x2 = x.reshape(batch * length, D_MODEL)
    mem2 = mem.reshape(batch * MEM_TOKENS, D_MODEL)
    for layer in range(DEPTH):
        kf = _hyena_filters(length, tabs, p["hyena_ffn_w1"][layer], p["hyena_ffn_b1"][layer],
                            p["hyena_ffn_w2"][layer], p["hyena_ffn_b2"][layer], p["hyena_ffn_w3"][layer],
                            p["hyena_log_decay"][layer])
        x2 = _mixer_layer(x2, mem2, weights[layer], kf, tabs, rope, p, layer, batch, length)
    return x2.reshape(batch, length, D_MODEL)


def kernel(x_prompt, x_sample, mem_prompt, mem_sample, pre_norm, post_norm, w_in, q_norm, k_norm, mem_norm, w_mem_kv, ssm_a_re, ssm_a_im, ssm_log_step, ssm_b_re, ssm_b_im, ssm_c_re, ssm_c_im, ssm_d, ssm_w_glu, hyena_short_w, hyena_short_b, hyena_ffn_w1, hyena_ffn_b1, hyena_ffn_w2, hyena_ffn_b2, hyena_ffn_w3, hyena_log_decay, hyena_bias, branch_norm, w_out):
    p = dict(pre_norm=pre_norm, post_norm=post_norm, w_in=w_in, q_norm=q_norm, k_norm=k_norm, mem_norm=mem_norm,
             w_mem_kv=w_mem_kv, ssm_a_re=ssm_a_re, ssm_a_im=ssm_a_im, ssm_log_step=ssm_log_step, ssm_b_re=ssm_b_re,
             ssm_b_im=ssm_b_im, ssm_c_re=ssm_c_re, ssm_c_im=ssm_c_im, ssm_d=ssm_d, ssm_w_glu=ssm_w_glu,
             hyena_short_w=hyena_short_w, hyena_short_b=hyena_short_b, hyena_ffn_w1=hyena_ffn_w1,
             hyena_ffn_b1=hyena_ffn_b1, hyena_ffn_w2=hyena_ffn_w2, hyena_ffn_b2=hyena_ffn_b2,
             hyena_ffn_w3=hyena_ffn_w3, hyena_log_decay=hyena_log_decay, hyena_bias=hyena_bias,
             branch_norm=branch_norm, w_out=w_out)
    weights = [_layer_weights(layer, p) for layer in range(DEPTH)]
    return (_run_group(x_prompt, mem_prompt, weights, p), _run_group(x_sample, mem_sample, weights, p))
```

```python
import functools
import math

import jax
import jax.numpy as jnp
import numpy as np
from jax import lax
from jax.experimental import pallas as pl
from jax.experimental.pallas import tpu as pltpu

F32 = jnp.float32
BF16 = jnp.bfloat16

D_MODEL = 1024
DEPTH = 2
GRID_W = 64
HEAD_DIM = 64
ATTN_HEADS = 8
ATTN_KV_HEADS = 2
ATTN_GROUP = ATTN_HEADS // ATTN_KV_HEADS
ATTN_WIDTH = ATTN_HEADS * HEAD_DIM
KV_WIDTH = ATTN_KV_HEADS * HEAD_DIM
ROPE_THETA = 10000.0
ROPE_FREQS = HEAD_DIM // 4
SSM_GROUP = 16
SSM_GROUPS = 24
SSM_WIDTH = SSM_GROUP * SSM_GROUPS
SSM_STATE = 64
HYENA_WIDTH = 384
HYENA_ORDER = 2
FILTER_BANDS = 16
FILTER_HIDDEN = 64
MEM_TOKENS = 256
MEM_HEADS = 4
MEM_WIDTH = MEM_HEADS * HEAD_DIM
MIX_WIDTH = ATTN_WIDTH + SSM_WIDTH + HYENA_WIDTH + MEM_WIDTH
EPS = 1e-6

TOK_WIDTH = KV_WIDTH + MEM_WIDTH
CH_WIDTH = 2 * ATTN_WIDTH + KV_WIDTH + 2 * SSM_WIDTH + (HYENA_ORDER + 2) * HYENA_WIDTH
V_ROWS = HEAD_DIM + 16
Q_SCALE = HEAD_DIM ** -0.5 * math.log2(math.e)

LANES = 128
SUBLANES = 8
VMEM_LIMIT = 56 * 1024 * 1024
TOKEN_TILE = 512
ATTN_Q_TILE = 256
ATTN_K_TILE = 512
SSM_CHUNK = LANES
DFT_INNER = 256
HYENA_CH_BLOCK = 8
HYENA_CH_SUB = 8
FILTER_LANE_TILE = 2048


def _cparams(sem):
    return pltpu.CompilerParams(dimension_semantics=sem, vmem_limit_bytes=VMEM_LIMIT)


def _silu(x):
    return x * (1.0 / (1.0 + jnp.exp(-x)))


def _sigmoid(x):
    return 1.0 / (1.0 + jnp.exp(-x))


def _nt_dot(a, b):
    return lax.dot_general(a, b, (((1,), (1,)), ((), ())), preferred_element_type=F32)


def _rope_128(xn, cos, s_lo, s_hi):
    outs = []
    for c in range(xn.shape[1] // LANES):
        xc = xn[:, LANES * c:LANES * (c + 1)]
        outs.append(xc * cos + pltpu.roll(xc, LANES - ROPE_FREQS, 1) * s_lo + pltpu.roll(xc, ROPE_FREQS, 1) * s_hi)
    return outs[0] if len(outs) == 1 else jnp.concatenate(outs, axis=1)


def _inproj_kernel(x_ref, pre_g_ref, wtok_ref, wch_ref, cos_ref, slo_ref, shi_ref, cost_ref, sint_ref,
                   qg_ref, kg_ref, ones_ref,
                   qt_ref, k_ref, vt_ref, ga_ref, mq_ref, su_ref, sg_ref, hi_ref, hg_ref):
    tm = x_ref.shape[0]
    x = x_ref[...]
    h = x * lax.rsqrt(jnp.mean(x * x, axis=-1, keepdims=True) + EPS) * pre_g_ref[...]
    hb = h.astype(BF16)
    tok = jnp.dot(hb, wtok_ref[...], preferred_element_type=F32)
    k = tok[:, 0:KV_WIDTH]
    mq = tok[:, KV_WIDTH:]
    k_ms = jnp.dot((k * k).astype(BF16), ones_ref[...], preferred_element_type=F32) * (1.0 / HEAD_DIM)
    kn = k * lax.rsqrt(k_ms + EPS) * kg_ref[...]
    kr = _rope_128(kn, cos_ref[...], slo_ref[...], shi_ref[...])
    for j in range(ATTN_KV_HEADS):
        k_ref[j] = kr[:, HEAD_DIM * j:HEAD_DIM * (j + 1)].astype(BF16)
    mq_ref[...] = (mq * (HEAD_DIM ** -0.5)).astype(BF16)
    ch = _nt_dot(wch_ref[...], hb)
    q3 = ch[0:ATTN_WIDTH].reshape(ATTN_HEADS, HEAD_DIM, tm)
    qn = q3 * lax.rsqrt(jnp.mean(q3 * q3, axis=1, keepdims=True) + EPS) * qg_ref[...][None]
    f = ROPE_FREQS
    rot = jnp.concatenate([qn[:, f:2 * f], qn[:, 0:f], qn[:, 3 * f:4 * f], qn[:, 2 * f:3 * f]], axis=1)
    qr = (qn * cost_ref[...][None] + rot * sint_ref[...][None]) * Q_SCALE
    qt_ref[...] = qr.reshape(ATTN_WIDTH, tm).astype(BF16)
    o = ATTN_WIDTH
    ones_row = (lax.broadcasted_iota(jnp.int32, (V_ROWS - HEAD_DIM, tm), 0) == 0).astype(BF16)
    for j in range(ATTN_KV_HEADS):
        vt_ref[j, 0:HEAD_DIM] = ch[o + HEAD_DIM * j:o + HEAD_DIM * (j + 1)].astype(BF16)
        vt_ref[j, HEAD_DIM:V_ROWS] = ones_row
    o += KV_WIDTH
    ga_ref[...] = _silu(ch[o:o + ATTN_WIDTH])
    o += ATTN_WIDTH
    su_ref[...] = ch[o:o + SSM_WIDTH]
    sg_ref[...] = _silu(ch[o + SSM_WIDTH:o + 2 * SSM_WIDTH])
    o += 2 * SSM_WIDTH
    hi_ref[...] = ch[o:o + 3 * HYENA_WIDTH]
    hg_ref[...] = _silu(ch[o + 3 * HYENA_WIDTH:])


def _inproj(x2, pre_g, w_tok, w_ch_t, rope, qg, kg, ones, length):
    cos, s_lo, s_hi, cos_t, sin_t = rope
    n = x2.shape[0]
    tm = TOKEN_TILE
    nt = n // tm
    tiles_per_seq = length // tm
    tok_spec = lambda w: pl.BlockSpec((tm, w), lambda i: (i, 0))
    ch_spec = lambda w: pl.BlockSpec((w, tm), lambda i: (0, i))
    const = lambda shape: pl.BlockSpec(shape, lambda i: (0,) * len(shape))
    pos_spec = pl.BlockSpec((tm, LANES), lambda i: (i % tiles_per_seq, 0))
    pos_t_spec = pl.BlockSpec((HEAD_DIM, tm), lambda i: (0, i % tiles_per_seq))
    return pl.pallas_call(
        _inproj_kernel,
        grid=(nt,),
        in_specs=[tok_spec(D_MODEL), const((1, D_MODEL)), const((D_MODEL, TOK_WIDTH)), const((CH_WIDTH, D_MODEL)),
                  pos_spec, pos_spec, pos_spec, pos_t_spec, pos_t_spec,
                  const((HEAD_DIM, 1)), const((1, KV_WIDTH)), const((KV_WIDTH, KV_WIDTH))],
        out_specs=[ch_spec(ATTN_WIDTH),
                   pl.BlockSpec((ATTN_KV_HEADS, tm, HEAD_DIM), lambda i: (0, i, 0)),
                   pl.BlockSpec((ATTN_KV_HEADS, V_ROWS, tm), lambda i: (0, 0, i)),
                   ch_spec(ATTN_WIDTH), tok_spec(MEM_WIDTH),
                   ch_spec(SSM_WIDTH), ch_spec(SSM_WIDTH), ch_spec(3 * HYENA_WIDTH), ch_spec(HYENA_WIDTH)],
        out_shape=[jax.ShapeDtypeStruct((ATTN_WIDTH, n), BF16),
                   jax.ShapeDtypeStruct((ATTN_KV_HEADS, n, HEAD_DIM), BF16),
                   jax.ShapeDtypeStruct((ATTN_KV_HEADS, V_ROWS, n), BF16),
                   jax.ShapeDtypeStruct((ATTN_WIDTH, n), F32),
                   jax.ShapeDtypeStruct((n, MEM_WIDTH), BF16),
                   jax.ShapeDtypeStruct((SSM_WIDTH, n), F32),
                   jax.ShapeDtypeStruct((SSM_WIDTH, n), F32),
                   jax.ShapeDtypeStruct((3 * HYENA_WIDTH, n), F32),
                   jax.ShapeDtypeStruct((HYENA_WIDTH, n), F32)],
        compiler_params=_cparams(("parallel",)),
        name="inproj",
    )(x2, pre_g, w_tok, w_ch_t, cos, s_lo, s_hi, cos_t, sin_t, qg, kg, ones)


def _flash_kernel(qt_ref, k_ref, vt_ref, o_ref, acc_s, s_s, *, tq, tk, n_kv):
    w = ATTN_GROUP * tq
    q4t = jnp.concatenate([qt_ref[HEAD_DIM * i:HEAD_DIM * (i + 1), :] for i in range(ATTN_GROUP)], axis=1)
    acc_s[...] = jnp.zeros(acc_s.shape, F32)

    def scores(c, slot):
        start = pl.multiple_of(c * tk, tk)
        s_s[slot] = jnp.dot(k_ref[0, pl.ds(start, tk), :], q4t, preferred_element_type=F32)

    def consume(c, slot, m):
        start = pl.multiple_of(c * tk, tk)
        m_new = jnp.maximum(m, jnp.max(s_s[slot], axis=0, keepdims=True))
        p = jnp.exp2(s_s[slot] - m_new).astype(BF16)
        pv = jnp.dot(vt_ref[0, :, pl.ds(start, tk)], p, preferred_element_type=F32)
        acc_s[...] = jnp.exp2(m - m_new) * acc_s[...] + pv
        return m_new

    scores(0, 0)

    def body(cc, m):
        c = 2 * cc
        scores(c + 1, 1)
        m = consume(c, 0, m)
        scores(jnp.minimum(c + 2, n_kv - 1), 0)
        return consume(c + 1, 1, m)

    lax.fori_loop(0, n_kv // 2, body, jnp.full((1, w), -jnp.inf, F32), unroll=2)
    o = acc_s[0:HEAD_DIM] * (1.0 / acc_s[HEAD_DIM:HEAD_DIM + 1])
    for i in range(ATTN_GROUP):
        o_ref[HEAD_DIM * i:HEAD_DIM * (i + 1), :] = o[:, i * tq:(i + 1) * tq]


def _flash_attention(q_t, k, v_t, batch, length):
    n = q_t.shape[1]
    tq, tk = ATTN_Q_TILE, ATTN_K_TILE
    nq, nk = length // tq, length // tk
    gw = ATTN_GROUP * HEAD_DIM
    return pl.pallas_call(
        functools.partial(_flash_kernel, tq=tq, tk=tk, n_kv=nk),
        grid=(batch, ATTN_KV_HEADS, nq),
        in_specs=[pl.BlockSpec((gw, tq), lambda b, h, i: (h, b * nq + i)),
                  pl.BlockSpec((1, length, HEAD_DIM), lambda b, h, i: (h, b, 0)),
                  pl.BlockSpec((1, V_ROWS, length), lambda b, h, i: (h, 0, b))],
        out_specs=pl.BlockSpec((gw, tq), lambda b, h, i: (h, b * nq + i)),
        out_shape=jax.ShapeDtypeStruct((ATTN_WIDTH, n), F32),
        scratch_shapes=[pltpu.VMEM((V_ROWS, ATTN_GROUP * tq), F32),
                        pltpu.VMEM((2, tk, ATTN_GROUP * tq), F32)],
        compiler_params=_cparams(("parallel", "parallel", "parallel")),
        name="flash_attn",
    )(q_t, k, v_t)


def _memkv_kernel(mem_ref, g_ref, w_ref, kv_ref):
    m = mem_ref[...]
    mn = m * lax.rsqrt(jnp.mean(m * m, axis=-1, keepdims=True) + EPS) * g_ref[...]
    kv_ref[...] = jnp.dot(mn.astype(BF16), w_ref[...], preferred_element_type=F32)


def _memkv(mem2, mem_g, w_kv):
    rows = mem2.shape[0]
    return pl.pallas_call(
        _memkv_kernel,
        grid=(rows // MEM_TOKENS,),
        in_specs=[pl.BlockSpec((MEM_TOKENS, D_MODEL), lambda i: (i, 0)),
                  pl.BlockSpec((1, D_MODEL), lambda i: (0, 0)),
                  pl.BlockSpec((D_MODEL, 2 * MEM_WIDTH), lambda i: (0, 0))],
        out_specs=pl.BlockSpec((MEM_TOKENS, 2 * MEM_WIDTH), lambda i: (i, 0)),
        out_shape=jax.ShapeDtypeStruct((rows, 2 * MEM_WIDTH), F32),
        compiler_params=_cparams(("parallel",)),
        name="mem_kv",
    )(mem2, mem_g, w_kv)


def _ssm_kernel(u_ref, kf_ref, kb_ref, p_ref, q_ref, a1_ref, a2_ref, y_ref, g_s, s_s, x_s, h_s, *, batch, n_chunks):
    t = SSM_CHUNK
    rows = batch * n_chunks
    half = 2 * SSM_STATE
    causal = lax.broadcasted_iota(jnp.int32, (t, t), 1) >= lax.broadcasted_iota(jnp.int32, (t, t), 0)

    def build(cp, carry):
        r0 = pl.multiple_of(cp * t, t)
        kf_rows, kb_rows = kf_ref[0, cp], kb_ref[0, cp]
        for c in range(SSM_GROUP):
            lo = pltpu.roll(jnp.broadcast_to(kf_rows[c:c + 1], (t, t)), 0, 1, stride=1, stride_axis=0)
            up = pltpu.roll(jnp.broadcast_to(kb_rows[c:c + 1], (t, t)), 0, 1, stride=1, stride_axis=0)
            g_s[pl.ds(r0, t), c * t:(c + 1) * t] = jnp.where(causal, lo, up).astype(BF16)
        return carry

    lax.fori_loop(0, SSM_GROUP, build, 0)
    u = jnp.concatenate([u_ref[0, c].reshape(rows, t) for c in range(SSM_GROUP)], axis=1).astype(BF16)
    y_intra = jnp.dot(u, g_s[...], preferred_element_type=F32)
    s_all = jnp.dot(u, p_ref[0], preferred_element_type=F32)
    s_s[...] = s_all
    x_s[...] = jnp.concatenate([pltpu.roll(s_all[:, :half], SSM_STATE, 1),
                                pltpu.roll(s_all[:, half:], SSM_STATE, 1)], axis=1)
    a1f, a2f = a1_ref[0, :, :half], a2_ref[0, :, :half]
    a1b, a2b = a1_ref[0, :, half:], a2_ref[0, :, half:]

    sub = SUBLANES
    n_blocks = n_chunks // sub

    def step(kb, carry):
        new = []
        for b in range(batch):
            hf, gf, hb, gb = carry[4 * b:4 * b + 4]
            base_f = pl.multiple_of(b * n_chunks + kb * sub, sub)
            base_b = pl.multiple_of(b * n_chunks + (n_blocks - 1 - kb) * sub, sub)
            sf, xf = s_s[pl.ds(base_f, sub), 0:half], x_s[pl.ds(base_f, sub), 0:half]
            sb, xb = s_s[pl.ds(base_b, sub), half:2 * half], x_s[pl.ds(base_b, sub), half:2 * half]
            hf_rows, hb_rows = [], [None] * sub
            for i in range(sub):
                hf_rows.append(hf)
                hf, gf = a1f * hf + a2f * gf + sf[i:i + 1], a1f * gf - a2f * hf + xf[i:i + 1]
            for i in range(sub - 1, -1, -1):
                hb_rows[i] = hb
                hb, gb = a1b * hb + a2b * gb + sb[i:i + 1], a1b * gb - a2b * hb + xb[i:i + 1]
            h_s[pl.ds(base_f, sub), 0:half] = jnp.concatenate(hf_rows, axis=0)
            h_s[pl.ds(base_b, sub), half:2 * half] = jnp.concatenate(hb_rows, axis=0)
            new += [hf, gf, hb, gb]
        return tuple(new)

    zero = jnp.zeros((1, half), F32)
    lax.fori_loop(0, n_blocks, step, (zero,) * (4 * batch))
    y = y_intra + jnp.dot(h_s[...].astype(BF16), q_ref[0], preferred_element_type=F32)
    for c in range(SSM_GROUP):
        y_ref[0, c] = y[:, c * t:(c + 1) * t].reshape(batch, n_chunks, t)


def _ssm_scan(u_t, kf, kb, p_mat, q_mat, a1, a2, batch, length):
    t = SSM_CHUNK
    nk = length // t
    gt = SSM_GROUP * t
    u5 = u_t.reshape(SSM_GROUPS, SSM_GROUP, batch, nk, t)
    blk = (1, SSM_GROUP, batch, nk, t)
    lag_spec = pl.BlockSpec((1, SSM_GROUP, SSM_GROUP, t), lambda g: (g, 0, 0, 0))
    y5 = pl.pallas_call(
        functools.partial(_ssm_kernel, batch=batch, n_chunks=nk),
        grid=(SSM_GROUPS,),
        in_specs=[pl.BlockSpec(blk, lambda g: (g, 0, 0, 0, 0)), lag_spec, lag_spec,
                  pl.BlockSpec((1, gt, 4 * SSM_STATE), lambda g: (g, 0, 0)),
                  pl.BlockSpec((1, 4 * SSM_STATE, gt), lambda g: (g, 0, 0)),
                  pl.BlockSpec((1, 1, 4 * SSM_STATE), lambda g: (g, 0, 0)),
                  pl.BlockSpec((1, 1, 4 * SSM_STATE), lambda g: (g, 0, 0))],
        out_specs=pl.BlockSpec(blk, lambda g: (g, 0, 0, 0, 0)),
        out_shape=jax.ShapeDtypeStruct(u5.shape, F32),
        scratch_shapes=[pltpu.VMEM((gt, gt), BF16)] + [pltpu.VMEM((batch * nk, 4 * SSM_STATE), F32)] * 3,
        compiler_params=_cparams(("parallel",)),
        name="ssm_scan",
    )(u5, kf, kb, p_mat, q_mat, a1, a2)
    return y5.reshape(SSM_WIDTH, batch * length)


def _ssm_tables(a_re, a_im, log_step, b_re, b_im, c_re, c_im):
    t = SSM_CHUNK
    hi = lax.Precision.HIGHEST
    lam = lax.complex(a_re.astype(F32), a_im.astype(F32))
    step = jnp.exp(log_step.astype(F32))[..., None]
    ls = lam * step
    a_bar = jnp.exp(ls)
    b_bar = ((a_bar - 1.0) / lam)[..., None] * lax.complex(b_re.astype(F32), b_im.astype(F32))
    c = lax.complex(c_re.astype(F32), c_im.astype(F32))
    tau = jnp.arange(t + 1, dtype=F32)
    pw = jnp.exp(ls[..., None] * tau)
    kern = jnp.einsum('dgcp,dgpt,dgpe->dgtce', c, pw[..., :t], b_bar, precision=hi).real
    kf = kern[0].at[:, 0].add(kern[1][:, 0]).transpose(0, 3, 2, 1)
    kb = jnp.roll(kern[1][:, ::-1], 1, axis=1).transpose(0, 3, 2, 1)
    pw_f, pw_b = pw[0], pw[1]
    pf = pw_f[:, :, t - 1 - jnp.arange(t)][..., None] * b_bar[0][:, :, None, :]
    pb = pw_b[:, :, :t][..., None] * b_bar[1][:, :, None, :]
    to_rows = lambda z: z.transpose(0, 3, 2, 1).reshape(SSM_GROUPS, SSM_GROUP * t, SSM_STATE)
    p_mat = jnp.concatenate([to_rows(pf.real), to_rows(pf.imag), to_rows(pb.real), to_rows(pb.imag)], axis=-1)
    qf = c[0].transpose(0, 2, 1)[..., None] * pw_f[:, :, 1:][:, :, None, :]
    qb = c[1].transpose(0, 2, 1)[..., None] * pw_b[:, :, t - jnp.arange(t)][:, :, None, :]
    to_cols = lambda z: z.reshape(SSM_GROUPS, SSM_STATE, SSM_GROUP * t)
    q_mat = jnp.concatenate([to_cols(qf.real), -to_cols(qf.imag), to_cols(qb.real), -to_cols(qb.imag)], axis=1)
    at = pw[..., t]
    a1 = jnp.concatenate([at[0].real, at[0].real, at[1].real, at[1].real], axis=-1)[:, None, :]
    a2 = jnp.concatenate([-at[0].imag, at[0].imag, -at[1].imag, at[1].imag], axis=-1)[:, None, :]
    return kf, kb, p_mat.astype(BF16), q_mat.astype(BF16), a1, a2


def _dft_tables(length):
    j = DFT_INNER
    n = 2 * length
    rn = n // j
    rh = rn // 2
    odd = 2 * jnp.arange(rh, dtype=jnp.int32) + 1
    r = jnp.arange(rn, dtype=jnp.int32)
    ang1 = (math.pi / rn) * ((odd[:, None] * r[None, :]) % (2 * rn)).astype(F32)
    c1, s1 = jnp.cos(ang1), jnp.sin(ang1)
    f1_full = jnp.concatenate([c1, -s1], axis=0)
    f1_top = f1_full[:, :rh]
    f1_inv = jnp.concatenate([c1[:, :rh].T, -s1[:, :rh].T], axis=1)
    jj = jnp.arange(j, dtype=jnp.int32)
    angt = (math.pi / n) * (odd[:, None] * jj[None, :]).astype(F32)
    tw_re, tw_im = jnp.cos(angt), -jnp.sin(angt)
    ang2 = (2.0 * math.pi / j) * ((jj[:, None] * jj[None, :]) % j).astype(F32)
    f2_cat = jnp.concatenate([jnp.cos(ang2), -jnp.sin(ang2)], axis=1)
    return dict(f1_full=f1_full.astype(BF16), f1_top=f1_top.astype(BF16), f1_inv=f1_inv.astype(BF16),
                tw_re=tw_re, tw_im=tw_im, f2_cat=f2_cat.astype(BF16), rn=rn, rh=rh)


def _dft_forward(a, tw_re, tw_im, f2_cat, rk):
    j = DFT_INNER
    a_re, a_im = a[:rk], a[rk:]
    ap = jnp.concatenate([a_re * tw_re - a_im * tw_im, a_re * tw_im + a_im * tw_re], axis=0).astype(BF16)
    m = jnp.dot(ap, f2_cat, preferred_element_type=F32)
    return m[:rk, :j] - m[rk:, j:], m[:rk, j:] + m[rk:, :j]


def _filter_hidden_kernel(bands_ref, w1t_ref, w1c_ref, w1s_ref, b1_ref, w2t_ref, b2_ref, hid_ref, *, length, tl):
    hi = lax.Precision.HIGHEST
    base = pl.program_id(0) * tl
    idx = (lax.broadcasted_iota(jnp.int32, (1, tl), 1) + base).astype(F32)
    for d in range(2):
        pos = idx if d == 0 else float(length) - idx
        tt = pos / float(length)
        wpos = (2.0 * math.pi / length) * pos
        arg = bands_ref[...] * wpos
        h1 = (w1t_ref[...] * tt
              + jnp.dot(w1c_ref[...], jnp.cos(arg), preferred_element_type=F32, precision=hi)
              - jnp.dot(w1s_ref[...], jnp.sin(arg), preferred_element_type=F32, precision=hi))
        h1 = jnp.sin(h1 + b1_ref[...])
        h2 = jnp.dot(w2t_ref[...], h1, preferred_element_type=F32, precision=hi)
        hid_ref[d] = jnp.sin(h2 + b2_ref[...])


def _filter_taps_kernel(hid_ref, w3t_ref, decay_ref, filt_ref, *, length, tl):
    hi = lax.Precision.HIGHEST
    d = pl.program_id(0) % 2
    base = pl.program_id(1) * tl
    idx = (lax.broadcasted_iota(jnp.int32, (1, tl), 1) + base).astype(F32)
    pos = jnp.where(d == 0, idx, float(length) - idx)
    tt = pos / float(length)
    f = jnp.dot(w3t_ref[...], hid_ref[0], preferred_element_type=F32, precision=hi)
    f = f * jnp.exp(-tt * decay_ref[...])
    filt_ref[...] = jnp.where(jnp.logical_and(d == 1, idx == 0.0), 0.0, f)


def _filter_dft_kernel(filt_ref, f1_ref, twre_ref, twim_ref, f2_ref, kf_ref, *, rk, cb):
    for c in range(cb):
        r = jnp.concatenate([filt_ref[0, 0, c], -filt_ref[0, 1, c]], axis=0).astype(BF16)
        a = jnp.dot(f1_ref[...], r, preferred_element_type=F32)
        x_re, x_im = _dft_forward(a, twre_ref[...], twim_ref[...], f2_ref[...], rk)
        kf_ref[0, c, 0] = x_re
        kf_ref[0, c, 1] = x_im


def _hyena_filters(length, tabs, w1, b1, w2, b2, w3, log_decay):
    tl = min(FILTER_LANE_TILE, length)
    nl = length // tl
    rn, rh, j = tabs["rn"], tabs["rh"], DFT_INNER
    bands = jnp.linspace(1e-4, FILTER_BANDS - 1, FILTER_BANDS, dtype=F32)[:, None]
    w1f = w1.astype(F32)
    const = lambda shape: pl.BlockSpec(shape, lambda *_: (0,) * len(shape))
    hid = pl.pallas_call(
        functools.partial(_filter_hidden_kernel, length=length, tl=tl),
        grid=(nl,),
        in_specs=[const((FILTER_BANDS, 1)), const((FILTER_HIDDEN, 1)), const((FILTER_HIDDEN, FILTER_BANDS)),
                  const((FILTER_HIDDEN, FILTER_BANDS)), const((FILTER_HIDDEN, 1)),
                  const((FILTER_HIDDEN, FILTER_HIDDEN)), const((FILTER_HIDDEN, 1))],
        out_specs=pl.BlockSpec((2, FILTER_HIDDEN, tl), lambda i: (0, 0, i)),
        out_shape=jax.ShapeDtypeStruct((2, FILTER_HIDDEN, length), F32),
        compiler_params=_cparams(("parallel",)),
        name="hyena_filter_hidden",
    )(bands, w1f[0:1].T, w1f[1:1 + FILTER_BANDS].T, w1f[1 + FILTER_BANDS:].T, b1.astype(F32)[:, None],
      w2.astype(F32).T, b2.astype(F32)[:, None])
    n_od = HYENA_ORDER * 2
    w3t = w3.astype(F32).T.reshape(n_od, HYENA_WIDTH, FILTER_HIDDEN)
    decay = jnp.exp(log_decay.astype(F32)).reshape(n_od, HYENA_WIDTH, 1)
    filt = pl.pallas_call(
        functools.partial(_filter_taps_kernel, length=length, tl=tl),
        grid=(n_od, nl),
        in_specs=[pl.BlockSpec((1, FILTER_HIDDEN, tl), lambda od, i: (od % 2, 0, i)),
                  pl.BlockSpec((None, HYENA_WIDTH, FILTER_HIDDEN), lambda od, i: (od, 0, 0)),
                  pl.BlockSpec((None, HYENA_WIDTH, 1), lambda od, i: (od, 0, 0))],
        out_specs=pl.BlockSpec((None, HYENA_WIDTH, tl), lambda od, i: (od, 0, i)),
        out_shape=jax.ShapeDtypeStruct((n_od, HYENA_WIDTH, length), F32),
        compiler_params=_cparams(("parallel", "parallel")),
        name="hyena_filter_taps",
    )(hid, w3t, decay)
    filt6 = filt.reshape(HYENA_ORDER, 2, HYENA_WIDTH, rh, j)
    cb = HYENA_CH_BLOCK
    return pl.pallas_call(
        functools.partial(_filter_dft_kernel, rk=rh, cb=cb),
        grid=(HYENA_ORDER, HYENA_WIDTH // cb),
        in_specs=[pl.BlockSpec((1, 2, cb, rh, j), lambda o, c: (o, 0, c, 0, 0)),
                  const((rn, rn)), const((rh, j)), const((rh, j)), const((j, 2 * j))],
        out_specs=pl.BlockSpec((1, cb, 2, rh, j), lambda o, c: (o, c, 0, 0, 0)),
        out_shape=jax.ShapeDtypeStruct((HYENA_ORDER, HYENA_WIDTH, 2, rh, j), F32),
        compiler_params=_cparams(("parallel", "parallel")),
        name="hyena_filter_dft",
    )(filt6, tabs["f1_full"], tabs["tw_re"], tabs["tw_im"], tabs["f2_cat"])


def _hyena_kernel(scw_ref, scb_ref, hb_ref, x_ref, hg_ref, kf_ref, f1_ref, f1i_ref, twre_ref, twim_ref, f2_ref,
                  o_ref, *, batch, rn, rh, cb, cs):
    j = DFT_INNER
    n_inv = 2.0 / (rn * j)
    rk = rh
    lane = lax.broadcasted_iota(jnp.int32, (rh, j), 1)
    row = lax.broadcasted_iota(jnp.int32, (rh, j), 0)
    first = jnp.logical_and(lane == 0, row == 0)
    last = jnp.logical_and(lane == j - 1, row == rh - 1)
    tw_re, tw_im = twre_ref[...], twim_ref[...]

    def prev_t(x):
        p = pltpu.roll(x, 1, 1)
        return jnp.where(first, 0.0, jnp.where(lane == 0, pltpu.roll(p, 1, 0), p))

    def next_t(x):
        p = pltpu.roll(x, j - 1, 1)
        return jnp.where(last, 0.0, jnp.where(lane == j - 1, pltpu.roll(p, rh - 1, 0), p))

    def split(m, i):
        return m[2 * rk * i:2 * rk * i + rk], m[2 * rk * i + rk:2 * rk * (i + 1)]

    def long_conv(xs, ks):
        rows = []
        for x in xs:
            a = jnp.dot(f1_ref[...], x.astype(BF16), preferred_element_type=F32)
            a_re, a_im = a[:rk], a[rk:]
            rows += [a_re * tw_re - a_im * tw_im, a_re * tw_im + a_im * tw_re]
        m = jnp.dot(jnp.concatenate(rows, axis=0).astype(BF16), f2_ref[...], preferred_element_type=F32)
        rows = []
        for i, (k_re, k_im) in enumerate(ks):
            m_re, m_im = split(m, i)
            x_re, x_im = m_re[:, :j] - m_im[:, j:], m_re[:, j:] + m_im[:, :j]
            rows += [x_re * k_re - x_im * k_im, x_re * k_im + x_im * k_re]
        m = jnp.dot(jnp.concatenate(rows, axis=0).astype(BF16), f2_ref[...], preferred_element_type=F32)
        outs = []
        for i in range(len(xs)):
            m_re, m_im = split(m, i)
            b_re, b_im = m_re[:, :j] + m_im[:, j:], m_im[:, :j] - m_re[:, j:]
            bp = jnp.concatenate([b_re * tw_re + b_im * tw_im, b_im * tw_re - b_re * tw_im], axis=0).astype(BF16)
            outs.append(jnp.dot(f1i_ref[...], bp, preferred_element_type=F32) * n_inv)
        return outs

    cbase = pl.program_id(0) * cb
    for c0 in range(0, cb, cs):
        chans = [(ci, b) for ci in range(c0, c0 + cs) for b in range(batch)]
        segs = []
        for sgm in range(3):
            seg = []
            for ci, b in chans:
                chs = sgm * HYENA_WIDTH + cbase + ci
                x = x_ref[sgm, ci, b]
                seg.append(prev_t(x) * scw_ref[chs] + x * scw_ref[3 * HYENA_WIDTH + chs]
                           + next_t(x) * scw_ref[6 * HYENA_WIDTH + chs] + scb_ref[chs])
            segs.append(seg)
        z = segs[0]
        for o in range(HYENA_ORDER):
            conv = long_conv(z, [(kf_ref[o, ci, 0], kf_ref[o, ci, 1]) for ci, _ in chans])
            z = [segs[o + 1][i] * (conv[i] + z[i] * hb_ref[o * HYENA_WIDTH + cbase + ci])
                 for i, (ci, _) in enumerate(chans)]
        for i, (ci, b) in enumerate(chans):
            o_ref[ci, b] = z[i] * hg_ref[ci, b]


def _hyena(hi_t, hg_t, kf, tabs, short_w, short_b, hy_bias, batch, length):
    rn, rh, j = tabs["rn"], tabs["rh"], DFT_INNER
    cb = HYENA_CH_BLOCK
    x5 = hi_t.reshape(3, HYENA_WIDTH, batch, rh, j)
    g4 = hg_t.reshape(HYENA_WIDTH, batch, rh, j)
    smem = pl.BlockSpec(memory_space=pltpu.SMEM)
    const = lambda shape: pl.BlockSpec(shape, lambda c: (0,) * len(shape))
    out = pl.pallas_call(
        functools.partial(_hyena_kernel, batch=batch, rn=rn, rh=rh, cb=cb, cs=HYENA_CH_SUB),
        grid=(HYENA_WIDTH // cb,),
        in_specs=[smem, smem, smem,
                  pl.BlockSpec((3, cb, batch, rh, j), lambda c: (0, c, 0, 0, 0)),
                  pl.BlockSpec((cb, batch, rh, j), lambda c: (c, 0, 0, 0)),
                  pl.BlockSpec((HYENA_ORDER, cb, 2, rh, j), lambda c: (0, c, 0, 0, 0)),
                  const((rn, rh)), const((rh, rn)), const((rh, j)), const((rh, j)), const((j, 2 * j))],
        out_specs=pl.BlockSpec((cb, batch, rh, j), lambda c: (c, 0, 0, 0)),
        out_shape=jax.ShapeDtypeStruct(g4.shape, F32),
        compiler_params=_cparams(("parallel",)),
        name="hyena_conv",
    )(short_w.astype(F32).reshape(-1), short_b.astype(F32), hy_bias.astype(F32).reshape(-1),
      x5, g4, kf, tabs["f1_top"], tabs["f1_inv"], tabs["tw_re"], tabs["tw_im"], tabs["f2_cat"])
    return out.reshape(HYENA_WIDTH, batch * length)


def _rms_rows(x, g):
    return x * lax.rsqrt(jnp.mean(x * x, axis=-1, keepdims=True) + EPS) * g


def _rms_cols(x, g):
    return x * lax.rsqrt(jnp.mean(x * x, axis=0, keepdims=True) + EPS) * g


def _post_kernel(ao_ref, ga_ref, mq_ref, mk_ref, mv_ref, y_ref, u_ref, sg_ref, hy_ref, x_ref,
                 d_ref, wglu_ref, g_attn_ref, g_ssm_ref, g_hy_ref, g_mem_ref, post_g_ref, wo_ref, out_ref):
    attn_n = _rms_cols(ao_ref[...] * ga_ref[...], g_attn_ref[...])
    s = jnp.dot(mq_ref[...], mk_ref[0], preferred_element_type=F32)
    ps = []
    for h in range(MEM_HEADS):
        sh = s[:, MEM_TOKENS * h:MEM_TOKENS * (h + 1)]
        e = jnp.exp(sh - jnp.max(sh, axis=-1, keepdims=True))
        ps.append(e * (1.0 / jnp.sum(e, axis=-1, keepdims=True)))
    p = jnp.concatenate(ps, axis=1).astype(BF16)
    cross_n = _rms_rows(jnp.dot(p, mv_ref[0], preferred_element_type=F32), g_mem_ref[...])
    y = y_ref[...] + d_ref[...] * u_ref[...]
    g = y * (0.5 * (1.0 + jnp.tanh(math.sqrt(2.0 / math.pi) * (y + 0.044715 * (y * y * y)))))
    gz = jnp.dot(wglu_ref[...], g.astype(BF16), preferred_element_type=F32)
    ssm_n = _rms_cols(g * _sigmoid(gz) * sg_ref[...], g_ssm_ref[...])
    hy_n = _rms_cols(hy_ref[...], g_hy_ref[...])
    o1, o2, o3 = ATTN_WIDTH, ATTN_WIDTH + SSM_WIDTH, ATTN_WIDTH + SSM_WIDTH + HYENA_WIDTH
    mixed = (jnp.dot(attn_n.T.astype(BF16), wo_ref[0:o1], preferred_element_type=F32)
             + jnp.dot(ssm_n.T.astype(BF16), wo_ref[o1:o2], preferred_element_type=F32)
             + jnp.dot(hy_n.T.astype(BF16), wo_ref[o2:o3], preferred_element_type=F32)
             + jnp.dot(cross_n.astype(BF16), wo_ref[o3:], preferred_element_type=F32))
    out_ref[...] = x_ref[...] + _rms_rows(mixed, post_g_ref[...])


def _post(ao, ga, mq, mk_bd, mv_bd, y_t, u_t, sg_t, hy_t, x2, d, wglu_t, g_attn, g_ssm, g_hy, g_mem, post_g, wo,
          length):
    n = x2.shape[0]
    tm = TOKEN_TILE
    tiles_per_seq = length // tm
    tok_spec = lambda w: pl.BlockSpec((tm, w), lambda i: (i, 0))
    ch_spec = lambda w: pl.BlockSpec((w, tm), lambda i: (0, i))
    const = lambda shape: pl.BlockSpec(shape, lambda i: (0,) * len(shape))
    hm = MEM_HEADS * MEM_TOKENS
    return pl.pallas_call(
        _post_kernel,
        grid=(n // tm,),
        in_specs=[ch_spec(ATTN_WIDTH), ch_spec(ATTN_WIDTH), tok_spec(MEM_WIDTH),
                  pl.BlockSpec((1, MEM_WIDTH, hm), lambda i: (i // tiles_per_seq, 0, 0)),
                  pl.BlockSpec((1, hm, MEM_WIDTH), lambda i: (i // tiles_per_seq, 0, 0)),
                  ch_spec(SSM_WIDTH), ch_spec(SSM_WIDTH), ch_spec(SSM_WIDTH), ch_spec(HYENA_WIDTH),
                  tok_spec(D_MODEL),
                  const((SSM_WIDTH, 1)), const((SSM_WIDTH, SSM_WIDTH)),
                  const((ATTN_WIDTH, 1)), const((SSM_WIDTH, 1)), const((HYENA_WIDTH, 1)), const((1, MEM_WIDTH)),
                  const((1, D_MODEL)), const((MIX_WIDTH, D_MODEL))],
        out_specs=tok_spec(D_MODEL),
        out_shape=jax.ShapeDtypeStruct((n, D_MODEL), F32),
        compiler_params=_cparams(("parallel",)),
        name="post",
    )(ao, ga, mq, mk_bd, mv_bd, y_t, u_t, sg_t, hy_t, x2, d, wglu_t, g_attn, g_ssm, g_hy, g_mem, post_g, wo)


def _rope_tables(length):
    rows = length // GRID_W
    row = jnp.broadcast_to(jnp.arange(rows, dtype=F32)[:, None], (rows, GRID_W)).reshape(length)
    col = jnp.broadcast_to(jnp.arange(GRID_W, dtype=F32)[None, :], (rows, GRID_W)).reshape(length)
    inv_freq = ROPE_THETA ** (-jnp.arange(ROPE_FREQS, dtype=F32) / ROPE_FREQS)
    ang = jnp.stack([row[:, None] * inv_freq, col[:, None] * inv_freq], axis=1)
    ang = jnp.broadcast_to(ang[:, :, None, :], (length, 2, 2, ROPE_FREQS)).reshape(length, HEAD_DIM)
    cos1, sin1 = jnp.cos(ang), jnp.sin(ang)
    low1 = (jnp.arange(HEAD_DIM) % (2 * ROPE_FREQS)) < ROPE_FREQS
    cos_t, sin_t = cos1.T, jnp.where(low1, -sin1, sin1).T
    cos = jnp.concatenate([cos1, cos1], axis=1)
    sin = jnp.concatenate([sin1, sin1], axis=1)
    low = jnp.concatenate([low1, low1])
    return cos, jnp.where(low, -sin, 0.0), jnp.where(low, 0.0, sin), cos_t, sin_t


def _block_diag_heads(mk, mv, batch):
    mk4 = mk.reshape(batch, MEM_TOKENS, MEM_HEADS, HEAD_DIM)
    mv4 = mv.reshape(batch, MEM_TOKENS, MEM_HEADS, HEAD_DIM)
    eye = jnp.eye(MEM_HEADS, dtype=mk.dtype)
    k_bd = jnp.einsum('bmhd,hg->bhdgm', mk4, eye).reshape(batch, MEM_WIDTH, MEM_HEADS * MEM_TOKENS)
    v_bd = jnp.einsum('bmhd,hg->bhmgd', mv4, eye).reshape(batch, MEM_HEADS * MEM_TOKENS, MEM_WIDTH)
    return k_bd.astype(BF16), v_bd.astype(BF16)


def _layer_weights(layer, p):
    w_in = p["w_in"][layer]
    a, kv = ATTN_WIDTH, KV_WIDTH
    k0, v0, g0, mq0 = a, a + kv, a + 2 * kv, w_in.shape[1] - MEM_WIDTH
    w_tok = jnp.concatenate([w_in[:, k0:v0], w_in[:, mq0:]], axis=1).astype(BF16)
    w_ch_t = jnp.concatenate([w_in[:, :k0], w_in[:, v0:mq0]], axis=1).T.astype(BF16)
    bg = p["branch_norm"][layer].astype(F32)
    o1, o2, o3 = a, a + SSM_WIDTH, a + SSM_WIDTH + HYENA_WIDTH
    head_id = jnp.arange(kv) // HEAD_DIM
    return dict(
        w_tok=w_tok, w_ch_t=w_ch_t,
        pre_g=p["pre_norm"][layer].astype(F32)[None, :], post_g=p["post_norm"][layer].astype(F32)[None, :],
        qg=p["q_norm"][layer].astype(F32)[:, None],
        kg=jnp.tile(p["k_norm"][layer].astype(F32), ATTN_KV_HEADS)[None, :],
        ones=(head_id[:, None] == head_id[None, :]).astype(BF16),
        mem_g=p["mem_norm"][layer].astype(F32)[None, :], w_mem_kv=p["w_mem_kv"][layer].astype(BF16),
        ssm=_ssm_tables(p["ssm_a_re"][layer], p["ssm_a_im"][layer], p["ssm_log_step"][layer], p["ssm_b_re"][layer],
                        p["ssm_b_im"][layer], p["ssm_c_re"][layer], p["ssm_c_im"][layer]),
        d=p["ssm_d"][layer].astype(F32)[:, None], wglu_t=p["ssm_w_glu"][layer].T.astype(BF16),
        g_attn=bg[:o1, None], g_ssm=bg[o1:o2, None], g_hy=bg[o2:o3, None], g_mem=bg[None, o3:],
        wo=p["w_out"][layer].astype(BF16),
    )


def _mixer_layer(x2, mem2, lw, kf, tabs, rope, p, layer, batch, length):
    q_t, k, v_t, ga, mq, su_t, sg_t, hi_t, hg_t = _inproj(x2, lw["pre_g"], lw["w_tok"], lw["w_ch_t"], rope,
                                                          lw["qg"], lw["kg"], lw["ones"], length)
    ao = _flash_attention(q_t, k, v_t, batch, length)
    mem_kv = _memkv(mem2, lw["mem_g"], lw["w_mem_kv"])
    mk_bd, mv_bd = _block_diag_heads(mem_kv[:, :MEM_WIDTH], mem_kv[:, MEM_WIDTH:], batch)
    y_t = _ssm_scan(su_t, *lw["ssm"], batch, length)
    hy_t = _hyena(hi_t, hg_t, kf, tabs, p["hyena_short_w"][layer], p["hyena_short_b"][layer],
                  p["hyena_bias"][layer], batch, length)
    return _post(ao, ga, mq, mk_bd, mv_bd, y_t, su_t, sg_t, hy_t, x2, lw["d"], lw["wglu_t"],
                 lw["g_attn"], lw["g_ssm"], lw["g_hy"], lw["g_mem"], lw["post_g"], lw["wo"], length)


def _run_group(x, mem, weights, p):
    batch, length, _ = x.shape
    rope = _rope_tables(length)
    tabs = _dft_tables(length)
    x2 = x.reshape(batch * length, D_MODEL)
    mem2 = mem.reshape(batch * MEM_TOKENS, D_MODEL)
    for layer in range(DEPTH):
        kf = _hyena_filters(length, tabs, p["hyena_ffn_w1"][layer], p["hyena_ffn_b1"][layer],
                            p["hyena_ffn_w2"][layer], p["hyena_ffn_b2"][layer], p["hyena_ffn_w3"][layer],
                            p["hyena_log_decay"][layer])
        x2 = _mixer_layer(x2, mem2, weights[layer], kf, tabs, rope, p, layer, batch, length)
    return x2.reshape(batch, length, D_MODEL)


def kernel(x_prompt, x_sample, mem_prompt, mem_sample, pre_norm, post_norm, w_in, q_norm, k_norm, mem_norm, w_mem_kv, ssm_a_re, ssm_a_im, ssm_log_step, ssm_b_re, ssm_b_im, ssm_c_re, ssm_c_im, ssm_d, ssm_w_glu, hyena_short_w, hyena_short_b, hyena_ffn_w1, hyena_ffn_b1, hyena_ffn_w2, hyena_ffn_b2, hyena_ffn_w3, hyena_log_decay, hyena_bias, branch_norm, w_out):
    p = dict(pre_norm=pre_norm, post_norm=post_norm, w_in=w_in, q_norm=q_norm, k_norm=k_norm, mem_norm=mem_norm,
             w_mem_kv=w_mem_kv, ssm_a_re=ssm_a_re, ssm_a_im=ssm_a_im, ssm_log_step=ssm_log_step, ssm_b_re=ssm_b_re,
             ssm_b_im=ssm_b_im, ssm_c_re=ssm_c_re, ssm_c_im=ssm_c_im, ssm_d=ssm_d, ssm_w_glu=ssm_w_glu,
             hyena_short_w=hyena_short_w, hyena_short_b=hyena_short_b, hyena_ffn_w1=hyena_ffn_w1,
             hyena_ffn_b1=hyena_ffn_b1, hyena_ffn_w2=hyena_ffn_w2, hyena_ffn_b2=hyena_ffn_b2,
             hyena_ffn_w3=hyena_ffn_w3, hyena_log_decay=hyena_log_decay, hyena_bias=hyena_bias,
             branch_norm=branch_norm, w_out=w_out)
    weights = [_layer_weights(layer, p) for layer in range(DEPTH)]
    return (_run_group(x_prompt, mem_prompt, weights, p), _run_group(x_sample, mem_sample, weights, p))
```

```python
import functools
import math

import jax
import jax.numpy as jnp
import numpy as np
from jax import lax
from jax.experimental import pallas as pl
from jax.experimental.pallas import tpu as pltpu

F32 = jnp.float32
BF16 = jnp.bfloat16

D_MODEL = 1024
DEPTH = 2
GRID_W = 64
HEAD_DIM = 64
ATTN_HEADS = 8
ATTN_KV_HEADS = 2
ATTN_GROUP = ATTN_HEADS // ATTN_KV_HEADS
ATTN_WIDTH = ATTN_HEADS * HEAD_DIM
KV_WIDTH = ATTN_KV_HEADS * HEAD_DIM
ROPE_THETA = 10000.0
ROPE_FREQS = HEAD_DIM // 4
SSM_GROUP = 16
SSM_GROUPS = 24
SSM_WIDTH = SSM_GROUP * SSM_GROUPS
SSM_STATE = 64
HYENA_WIDTH = 384
HYENA_ORDER = 2
FILTER_BANDS = 16
FILTER_HIDDEN = 64
MEM_TOKENS = 256
MEM_HEADS = 4
MEM_WIDTH = MEM_HEADS * HEAD_DIM
MIX_WIDTH = ATTN_WIDTH + SSM_WIDTH + HYENA_WIDTH + MEM_WIDTH
EPS = 1e-6

TOK_WIDTH = KV_WIDTH + MEM_WIDTH
CH_WIDTH = 2 * ATTN_WIDTH + KV_WIDTH + 2 * SSM_WIDTH + (HYENA_ORDER + 2) * HYENA_WIDTH
V_ROWS = HEAD_DIM + 16
Q_SCALE = HEAD_DIM ** -0.5 * math.log2(math.e)

LANES = 128
SUBLANES = 8
VMEM_LIMIT = 56 * 1024 * 1024
TOKEN_TILE = 512
ATTN_Q_TILE = 256
ATTN_K_TILE = 512
SSM_CHUNK = LANES
DFT_INNER = 256
HYENA_CH_BLOCK = 8
HYENA_CH_SUB = 8
FILTER_LANE_TILE = 2048


def _cparams(sem):
    return pltpu.CompilerParams(dimension_semantics=sem, vmem_limit_bytes=VMEM_LIMIT)


def _silu(x):
    return x * (1.0 / (1.0 + jnp.exp(-x)))


def _sigmoid(x):
    return 1.0 / (1.0 + jnp.exp(-x))


def _nt_dot(a, b):
    return lax.dot_general(a, b, (((1,), (1,)), ((), ())), preferred_element_type=F32)


def _rope_128(xn, cos, s_lo, s_hi):
    outs = []
    for c in range(xn.shape[1] // LANES):
        xc = xn[:, LANES * c:LANES * (c + 1)]
        outs.append(xc * cos + pltpu.roll(xc, LANES - ROPE_FREQS, 1) * s_lo + pltpu.roll(xc, ROPE_FREQS, 1) * s_hi)
    return outs[0] if len(outs) == 1 else jnp.concatenate(outs, axis=1)


def _inproj_kernel(x_ref, pre_g_ref, wtok_ref, wch_ref, cos_ref, slo_ref, shi_ref, cost_ref, sint_ref,
                   qg_ref, kg_ref, ones_ref,
                   qt_ref, k_ref, vt_ref, ga_ref, mq_ref, su_ref, sg_ref, hi_ref, hg_ref):
    tm = x_ref.shape[0]
    x = x_ref[...]
    h = x * lax.rsqrt(jnp.mean(x * x, axis=-1, keepdims=True) + EPS) * pre_g_ref[...]
    hb = h.astype(BF16)
    tok = jnp.dot(hb, wtok_ref[...], preferred_element_type=F32)
    k = tok[:, 0:KV_WIDTH]
    mq = tok[:, KV_WIDTH:]
    k_ms = jnp.dot((k * k).astype(BF16), ones_ref[...], preferred_element_type=F32) * (1.0 / HEAD_DIM)
    kn = k * lax.rsqrt(k_ms + EPS) * kg_ref[...]
    kr = _rope_128(kn, cos_ref[...], slo_ref[...], shi_ref[...])
    for j in range(ATTN_KV_HEADS):
        k_ref[j] = kr[:, HEAD_DIM * j:HEAD_DIM * (j + 1)].astype(BF16)
    mq_ref[...] = (mq * (HEAD_DIM ** -0.5)).astype(BF16)
    ch = _nt_dot(wch_ref[...], hb)
    q3 = ch[0:ATTN_WIDTH].reshape(ATTN_HEADS, HEAD_DIM, tm)
    qn = q3 * lax.rsqrt(jnp.mean(q3 * q3, axis=1, keepdims=True) + EPS) * qg_ref[...][None]
    f = ROPE_FREQS
    rot = jnp.concatenate([qn[:, f:2 * f], qn[:, 0:f], qn[:, 3 * f:4 * f], qn[:, 2 * f:3 * f]], axis=1)
    qr = (qn * cost_ref[...][None] + rot * sint_ref[...][None]) * Q_SCALE
    qt_ref[...] = qr.reshape(ATTN_WIDTH, tm).astype(BF16)
    o = ATTN_WIDTH
    ones_row = (lax.broadcasted_iota(jnp.int32, (V_ROWS - HEAD_DIM, tm), 0) == 0).astype(BF16)
    for j in range(ATTN_KV_HEADS):
        vt_ref[j, 0:HEAD_DIM] = ch[o + HEAD_DIM * j:o + HEAD_DIM * (j + 1)].astype(BF16)
        vt_ref[j, HEAD_DIM:V_ROWS] = ones_row
    o += KV_WIDTH
    ga_ref[...] = _silu(ch[o:o + ATTN_WIDTH])
    o += ATTN_WIDTH
    su_ref[...] = ch[o:o + SSM_WIDTH]
    sg_ref[...] = _silu(ch[o + SSM_WIDTH:o + 2 * SSM_WIDTH])
    o += 2 * SSM_WIDTH
    hi_ref[...] = ch[o:o + 3 * HYENA_WIDTH]
    hg_ref[...] = _silu(ch[o + 3 * HYENA_WIDTH:])


def _inproj(x2, pre_g, w_tok, w_ch_t, rope, qg, kg, ones, length):
    cos, s_lo, s_hi, cos_t, sin_t = rope
    n = x2.shape[0]
    tm = TOKEN_TILE
    nt = n // tm
    tiles_per_seq = length // tm
    tok_spec = lambda w: pl.BlockSpec((tm, w), lambda i: (i, 0))
    ch_spec = lambda w: pl.BlockSpec((w, tm), lambda i: (0, i))
    const = lambda shape: pl.BlockSpec(shape, lambda i: (0,) * len(shape))
    pos_spec = pl.BlockSpec((tm, LANES), lambda i: (i % tiles_per_seq, 0))
    pos_t_spec = pl.BlockSpec((HEAD_DIM, tm), lambda i: (0, i % tiles_per_seq))
    return pl.pallas_call(
        _inproj_kernel,
        grid=(nt,),
        in_specs=[tok_spec(D_MODEL), const((1, D_MODEL)), const((D_MODEL, TOK_WIDTH)), const((CH_WIDTH, D_MODEL)),
                  pos_spec, pos_spec, pos_spec, pos_t_spec, pos_t_spec,
                  const((HEAD_DIM, 1)), const((1, KV_WIDTH)), const((KV_WIDTH, KV_WIDTH))],
        out_specs=[ch_spec(ATTN_WIDTH),
                   pl.BlockSpec((ATTN_KV_HEADS, tm, HEAD_DIM), lambda i: (0, i, 0)),
                   pl.BlockSpec((ATTN_KV_HEADS, V_ROWS, tm), lambda i: (0, 0, i)),
                   ch_spec(ATTN_WIDTH), tok_spec(MEM_WIDTH),
                   ch_spec(SSM_WIDTH), ch_spec(SSM_WIDTH), ch_spec(3 * HYENA_WIDTH), ch_spec(HYENA_WIDTH)],
        out_shape=[jax.ShapeDtypeStruct((ATTN_WIDTH, n), BF16),
                   jax.ShapeDtypeStruct((ATTN_KV_HEADS, n, HEAD_DIM), BF16),
                   jax.ShapeDtypeStruct((ATTN_KV_HEADS, V_ROWS, n), BF16),
                   jax.ShapeDtypeStruct((ATTN_WIDTH, n), F32),
                   jax.ShapeDtypeStruct((n, MEM_WIDTH), BF16),
                   jax.ShapeDtypeStruct((SSM_WIDTH, n), F32),
                   jax.ShapeDtypeStruct((SSM_WIDTH, n), F32),
                   jax.ShapeDtypeStruct((3 * HYENA_WIDTH, n), F32),
                   jax.ShapeDtypeStruct((HYENA_WIDTH, n), F32)],
        compiler_params=_cparams(("parallel",)),
        name="inproj",
    )(x2, pre_g, w_tok, w_ch_t, cos, s_lo, s_hi, cos_t, sin_t, qg, kg, ones)


def _flash_kernel(qt_ref, k_ref, vt_ref, o_ref, acc_s, s_s, *, tq, tk, n_kv):
    w = ATTN_GROUP * tq
    q4t = jnp.concatenate([qt_ref[HEAD_DIM * i:HEAD_DIM * (i + 1), :] for i in range(ATTN_GROUP)], axis=1)
    acc_s[...] = jnp.zeros(acc_s.shape, F32)

    def scores(c, slot):
        start = pl.multiple_of(c * tk, tk)
        s = jnp.dot(k_ref[0, pl.ds(start, tk), :], q4t, preferred_element_type=F32)
        s_s[slot] = s
        return jnp.max(s, axis=0, keepdims=True)

    def consume(c, slot, m, m_chunk):
        start = pl.multiple_of(c * tk, tk)
        m_new = jnp.maximum(m, m_chunk)
        p = jnp.exp2(s_s[slot] - m_new).astype(BF16)
        pv = jnp.dot(vt_ref[0, :, pl.ds(start, tk)], p, preferred_element_type=F32)
        acc_s[...] = jnp.exp2(m - m_new) * acc_s[...] + pv
        return m_new

    def body(cc, carry):
        m, mc0 = carry
        c = 2 * cc
        mc1 = scores(c + 1, 1)
        m = consume(c, 0, m, mc0)
        mc0 = scores(c + 2, 0)
        return consume(c + 1, 1, m, mc1), mc0

    m, mc0 = lax.fori_loop(0, n_kv // 2 - 1, body, (jnp.full((1, w), -jnp.inf, F32), scores(0, 0)), unroll=2)
    mc1 = scores(n_kv - 1, 1)
    consume(n_kv - 1, 1, consume(n_kv - 2, 0, m, mc0), mc1)
    o = acc_s[0:HEAD_DIM] * (1.0 / acc_s[HEAD_DIM:HEAD_DIM + 1])
    for i in range(ATTN_GROUP):
        o_ref[HEAD_DIM * i:HEAD_DIM * (i + 1), :] = o[:, i * tq:(i + 1) * tq]


def _flash_attention(q_t, k, v_t, batch, length):
    n = q_t.shape[1]
    tq, tk = ATTN_Q_TILE, ATTN_K_TILE
    nq, nk = length // tq, length // tk
    gw = ATTN_GROUP * HEAD_DIM
    return pl.pallas_call(
        functools.partial(_flash_kernel, tq=tq, tk=tk, n_kv=nk),
        grid=(batch, ATTN_KV_HEADS, nq),
        in_specs=[pl.BlockSpec((gw, tq), lambda b, h, i: (h, b * nq + i)),
                  pl.BlockSpec((1, length, HEAD_DIM), lambda b, h, i: (h, b, 0)),
                  pl.BlockSpec((1, V_ROWS, length), lambda b, h, i: (h, 0, b))],
        out_specs=pl.BlockSpec((gw, tq), lambda b, h, i: (h, b * nq + i)),
        out_shape=jax.ShapeDtypeStruct((ATTN_WIDTH, n), F32),
        scratch_shapes=[pltpu.VMEM((V_ROWS, ATTN_GROUP * tq), F32),
                        pltpu.VMEM((2, tk, ATTN_GROUP * tq), F32)],
        compiler_params=_cparams(("parallel", "parallel", "parallel")),
        name="flash_attn",
    )(q_t, k, v_t)


def _memkv_kernel(mem_ref, g_ref, w_ref, kv_ref):
    m = mem_ref[...]
    mn = m * lax.rsqrt(jnp.mean(m * m, axis=-1, keepdims=True) + EPS) * g_ref[...]
    kv_ref[...] = jnp.dot(mn.astype(BF16), w_ref[...], preferred_element_type=F32)


def _memkv(mem2, mem_g, w_kv):
    rows = mem2.shape[0]
    return pl.pallas_call(
        _memkv_kernel,
        grid=(rows // MEM_TOKENS,),
        in_specs=[pl.BlockSpec((MEM_TOKENS, D_MODEL), lambda i: (i, 0)),
                  pl.BlockSpec((1, D_MODEL), lambda i: (0, 0)),
                  pl.BlockSpec((D_MODEL, 2 * MEM_WIDTH), lambda i: (0, 0))],
        out_specs=pl.BlockSpec((MEM_TOKENS, 2 * MEM_WIDTH), lambda i: (i, 0)),
        out_shape=jax.ShapeDtypeStruct((rows, 2 * MEM_WIDTH), F32),
        compiler_params=_cparams(("parallel",)),
        name="mem_kv",
    )(mem2, mem_g, w_kv)


def _ssm_kernel(u_ref, kf_ref, kb_ref, p_ref, q_ref, a1_ref, a2_ref, y_ref, g_s, s_s, x_s, h_s, *, batch, n_chunks):
    t = SSM_CHUNK
    rows = batch * n_chunks
    half = 2 * SSM_STATE
    causal = lax.broadcasted_iota(jnp.int32, (t, t), 1) >= lax.broadcasted_iota(jnp.int32, (t, t), 0)

    def build(cp, carry):
        r0 = pl.multiple_of(cp * t, t)
        kf_rows, kb_rows = kf_ref[0, cp], kb_ref[0, cp]
        for c in range(SSM_GROUP):
            lo = pltpu.roll(jnp.broadcast_to(kf_rows[c:c + 1], (t, t)), 0, 1, stride=1, stride_axis=0)
            up = pltpu.roll(jnp.broadcast_to(kb_rows[c:c + 1], (t, t)), 0, 1, stride=1, stride_axis=0)
            g_s[pl.ds(r0, t), c * t:(c + 1) * t] = jnp.where(causal, lo, up).astype(BF16)
        return carry

    lax.fori_loop(0, SSM_GROUP, build, 0)
    u = jnp.concatenate([u_ref[0, c].reshape(rows, t) for c in range(SSM_GROUP)], axis=1).astype(BF16)
    y_intra = jnp.dot(u, g_s[...], preferred_element_type=F32)
    s_all = jnp.dot(u, p_ref[0], preferred_element_type=F32)
    s_s[...] = s_all
    x_s[...] = jnp.concatenate([pltpu.roll(s_all[:, :half], SSM_STATE, 1),
                                pltpu.roll(s_all[:, half:], SSM_STATE, 1)], axis=1)
    a1f, a2f = a1_ref[0, :, :half], a2_ref[0, :, :half]
    a1b, a2b = a1_ref[0, :, half:], a2_ref[0, :, half:]

    sub = SUBLANES
    n_blocks = n_chunks // sub

    def step(kb, carry):
        new = []
        for b in range(batch):
            hf, gf, hb, gb = carry[4 * b:4 * b + 4]
            base_f = pl.multiple_of(b * n_chunks + kb * sub, sub)
            base_b = pl.multiple_of(b * n_chunks + (n_blocks - 1 - kb) * sub, sub)
            sf, xf = s_s[pl.ds(base_f, sub), 0:half], x_s[pl.ds(base_f, sub), 0:half]
            sb, xb = s_s[pl.ds(base_b, sub), half:2 * half], x_s[pl.ds(base_b, sub), half:2 * half]
            hf_rows, hb_rows = [], [None] * sub
            for i in range(sub):
                hf_rows.append(hf)
                hf, gf = a1f * hf + a2f * gf + sf[i:i + 1], a1f * gf - a2f * hf + xf[i:i + 1]
            for i in range(sub - 1, -1, -1):
                hb_rows[i] = hb
                hb, gb = a1b * hb + a2b * gb + sb[i:i + 1], a1b * gb - a2b * hb + xb[i:i + 1]
            h_s[pl.ds(base_f, sub), 0:half] = jnp.concatenate(hf_rows, axis=0)
            h_s[pl.ds(base_b, sub), half:2 * half] = jnp.concatenate(hb_rows, axis=0)
            new += [hf, gf, hb, gb]
        return tuple(new)

    zero = jnp.zeros((1, half), F32)
    lax.fori_loop(0, n_blocks, step, (zero,) * (4 * batch))
    y = y_intra + jnp.dot(h_s[...].astype(BF16), q_ref[0], preferred_element_type=F32)
    for c in range(SSM_GROUP):
        y_ref[0, c] = y[:, c * t:(c + 1) * t].reshape(batch, n_chunks, t)


def _ssm_scan(u_t, kf, kb, p_mat, q_mat, a1, a2, batch, length):
    t = SSM_CHUNK
    nk = length // t
    gt = SSM_GROUP * t
    u5 = u_t.reshape(SSM_GROUPS, SSM_GROUP, batch, nk, t)
    blk = (1, SSM_GROUP, batch, nk, t)
    lag_spec = pl.BlockSpec((1, SSM_GROUP, SSM_GROUP, t), lambda g: (g, 0, 0, 0))
    y5 = pl.pallas_call(
        functools.partial(_ssm_kernel, batch=batch, n_chunks=nk),
        grid=(SSM_GROUPS,),
        in_specs=[pl.BlockSpec(blk, lambda g: (g, 0, 0, 0, 0)), lag_spec, lag_spec,
                  pl.BlockSpec((1, gt, 4 * SSM_STATE), lambda g: (g, 0, 0)),
                  pl.BlockSpec((1, 4 * SSM_STATE, gt), lambda g: (g, 0, 0)),
                  pl.BlockSpec((1, 1, 4 * SSM_STATE), lambda g: (g, 0, 0)),
                  pl.BlockSpec((1, 1, 4 * SSM_STATE), lambda g: (g, 0, 0))],
        out_specs=pl.BlockSpec(blk, lambda g: (g, 0, 0, 0, 0)),
        out_shape=jax.ShapeDtypeStruct(u5.shape, F32),
        scratch_shapes=[pltpu.VMEM((gt, gt), BF16)] + [pltpu.VMEM((batch * nk, 4 * SSM_STATE), F32)] * 3,
        compiler_params=_cparams(("parallel",)),
        name="ssm_scan",
    )(u5, kf, kb, p_mat, q_mat, a1, a2)
    return y5.reshape(SSM_WIDTH, batch * length)


def _ssm_tables(a_re, a_im, log_step, b_re, b_im, c_re, c_im):
    t = SSM_CHUNK
    hi = lax.Precision.HIGHEST
    lam = lax.complex(a_re.astype(F32), a_im.astype(F32))
    step = jnp.exp(log_step.astype(F32))[..., None]
    ls = lam * step
    a_bar = jnp.exp(ls)
    b_bar = ((a_bar - 1.0) / lam)[..., None] * lax.complex(b_re.astype(F32), b_im.astype(F32))
    c = lax.complex(c_re.astype(F32), c_im.astype(F32))
    tau = jnp.arange(t + 1, dtype=F32)
    pw = jnp.exp(ls[..., None] * tau)
    kern = jnp.einsum('dgcp,dgpt,dgpe->dgtce', c, pw[..., :t], b_bar, precision=hi).real
    kf = kern[0].at[:, 0].add(kern[1][:, 0]).transpose(0, 3, 2, 1)
    kb = jnp.roll(kern[1][:, ::-1], 1, axis=1).transpose(0, 3, 2, 1)
    pw_f, pw_b = pw[0], pw[1]
    pf = pw_f[:, :, t - 1 - jnp.arange(t)][..., None] * b_bar[0][:, :, None, :]
    pb = pw_b[:, :, :t][..., None] * b_bar[1][:, :, None, :]
    to_rows = lambda z: z.transpose(0, 3, 2, 1).reshape(SSM_GROUPS, SSM_GROUP * t, SSM_STATE)
    p_mat = jnp.concatenate([to_rows(pf.real), to_rows(pf.imag), to_rows(pb.real), to_rows(pb.imag)], axis=-1)
    qf = c[0].transpose(0, 2, 1)[..., None] * pw_f[:, :, 1:][:, :, None, :]
    qb = c[1].transpose(0, 2, 1)[..., None] * pw_b[:, :, t - jnp.arange(t)][:, :, None, :]
    to_cols = lambda z: z.reshape(SSM_GROUPS, SSM_STATE, SSM_GROUP * t)
    q_mat = jnp.concatenate([to_cols(qf.real), -to_cols(qf.imag), to_cols(qb.real), -to_cols(qb.imag)], axis=1)
    at = pw[..., t]
    a1 = jnp.concatenate([at[0].real, at[0].real, at[1].real, at[1].real], axis=-1)[:, None, :]
    a2 = jnp.concatenate([-at[0].imag, at[0].imag, -at[1].imag, at[1].imag], axis=-1)[:, None, :]
    return kf, kb, p_mat.astype(BF16), q_mat.astype(BF16), a1, a2


def _dft_tables(length):
    j = DFT_INNER
    n = 2 * length
    rn = n // j
    rh = rn // 2
    odd = 2 * jnp.arange(rh, dtype=jnp.int32) + 1
    r = jnp.arange(rn, dtype=jnp.int32)
    ang1 = (math.pi / rn) * ((odd[:, None] * r[None, :]) % (2 * rn)).astype(F32)
    c1, s1 = jnp.cos(ang1), jnp.sin(ang1)
    f1_full = jnp.concatenate([c1, -s1], axis=0)
    f1_top = f1_full[:, :rh]
    f1_inv = jnp.concatenate([c1[:, :rh].T, -s1[:, :rh].T], axis=1)
    jj = jnp.arange(j, dtype=jnp.int32)
    angt = (math.pi / n) * (odd[:, None] * jj[None, :]).astype(F32)
    tw_re, tw_im = jnp.cos(angt), -jnp.sin(angt)
    ang2 = (2.0 * math.pi / j) * ((jj[:, None] * jj[None, :]) % j).astype(F32)
    f2_cat = jnp.concatenate([jnp.cos(ang2), -jnp.sin(ang2)], axis=1)
    return dict(f1_full=f1_full.astype(BF16), f1_top=f1_top.astype(BF16), f1_inv=f1_inv.astype(BF16),
                tw_re=tw_re, tw_im=tw_im, f2_cat=f2_cat.astype(BF16), rn=rn, rh=rh)


def _dft_forward(a, tw_re, tw_im, f2_cat, rk):
    j = DFT_INNER
    a_re, a_im = a[:rk], a[rk:]
    ap = jnp.concatenate([a_re * tw_re - a_im * tw_im, a_re * tw_im + a_im * tw_re], axis=0).astype(BF16)
    m = jnp.dot(ap, f2_cat, preferred_element_type=F32)
    return m[:rk, :j] - m[rk:, j:], m[:rk, j:] + m[rk:, :j]


def _filter_hidden_kernel(bands_ref, w1t_ref, w1c_ref, w1s_ref, b1_ref, w2t_ref, b2_ref, hid_ref, *, length, tl):
    hi = lax.Precision.HIGHEST
    base = pl.program_id(0) * tl
    idx = (lax.broadcasted_iota(jnp.int32, (1, tl), 1) + base).astype(F32)
    for d in range(2):
        pos = idx if d == 0 else float(length) - idx
        tt = pos / float(length)
        wpos = (2.0 * math.pi / length) * pos
        arg = bands_ref[...] * wpos
        h1 = (w1t_ref[...] * tt
              + jnp.dot(w1c_ref[...], jnp.cos(arg), preferred_element_type=F32, precision=hi)
              - jnp.dot(w1s_ref[...], jnp.sin(arg), preferred_element_type=F32, precision=hi))
        h1 = jnp.sin(h1 + b1_ref[...])
        h2 = jnp.dot(w2t_ref[...], h1, preferred_element_type=F32, precision=hi)
        hid_ref[d] = jnp.sin(h2 + b2_ref[...])


def _filter_taps_kernel(hid_ref, w3t_ref, decay_ref, filt_ref, *, length, tl):
    hi = lax.Precision.HIGHEST
    d = pl.program_id(0) % 2
    base = pl.program_id(1) * tl
    idx = (lax.broadcasted_iota(jnp.int32, (1, tl), 1) + base).astype(F32)
    pos = jnp.where(d == 0, idx, float(length) - idx)
    tt = pos / float(length)
    f = jnp.dot(w3t_ref[...], hid_ref[0], preferred_element_type=F32, precision=hi)
    f = f * jnp.exp(-tt * decay_ref[...])
    filt_ref[...] = jnp.where(jnp.logical_and(d == 1, idx == 0.0), 0.0, f)


def _filter_dft_kernel(filt_ref, f1_ref, twre_ref, twim_ref, f2_ref, kf_ref, *, rk, cb):
    for c in range(cb):
        r = jnp.concatenate([filt_ref[0, 0, c], -filt_ref[0, 1, c]], axis=0).astype(BF16)
        a = jnp.dot(f1_ref[...], r, preferred_element_type=F32)
        x_re, x_im = _dft_forward(a, twre_ref[...], twim_ref[...], f2_ref[...], rk)
        kf_ref[0, c, 0] = x_re
        kf_ref[0, c, 1] = x_im


def _hyena_filters(length, tabs, w1, b1, w2, b2, w3, log_decay):
    tl = min(FILTER_LANE_TILE, length)
    nl = length // tl
    rn, rh, j = tabs["rn"], tabs["rh"], DFT_INNER
    bands = jnp.linspace(1e-4, FILTER_BANDS - 1, FILTER_BANDS, dtype=F32)[:, None]
    w1f = w1.astype(F32)
    const = lambda shape: pl.BlockSpec(shape, lambda *_: (0,) * len(shape))
    hid = pl.pallas_call(
        functools.partial(_filter_hidden_kernel, length=length, tl=tl),
        grid=(nl,),
        in_specs=[const((FILTER_BANDS, 1)), const((FILTER_HIDDEN, 1)), const((FILTER_HIDDEN, FILTER_BANDS)),
                  const((FILTER_HIDDEN, FILTER_BANDS)), const((FILTER_HIDDEN, 1)),
                  const((FILTER_HIDDEN, FILTER_HIDDEN)), const((FILTER_HIDDEN, 1))],
        out_specs=pl.BlockSpec((2, FILTER_HIDDEN, tl), lambda i: (0, 0, i)),
        out_shape=jax.ShapeDtypeStruct((2, FILTER_HIDDEN, length), F32),
        compiler_params=_cparams(("parallel",)),
        name="hyena_filter_hidden",
    )(bands, w1f[0:1].T, w1f[1:1 + FILTER_BANDS].T, w1f[1 + FILTER_BANDS:].T, b1.astype(F32)[:, None],
      w2.astype(F32).T, b2.astype(F32)[:, None])
    n_od = HYENA_ORDER * 2
    w3t = w3.astype(F32).T.reshape(n_od, HYENA_WIDTH, FILTER_HIDDEN)
    decay = jnp.exp(log_decay.astype(F32)).reshape(n_od, HYENA_WIDTH, 1)
    filt = pl.pallas_call(
        functools.partial(_filter_taps_kernel, length=length, tl=tl),
        grid=(n_od, nl),
        in_specs=[pl.BlockSpec((1, FILTER_HIDDEN, tl), lambda od, i: (od % 2, 0, i)),
                  pl.BlockSpec((None, HYENA_WIDTH, FILTER_HIDDEN), lambda od, i: (od, 0, 0)),
                  pl.BlockSpec((None, HYENA_WIDTH, 1), lambda od, i: (od, 0, 0))],
        out_specs=pl.BlockSpec((None, HYENA_WIDTH, tl), lambda od, i: (od, 0, i)),
        out_shape=jax.ShapeDtypeStruct((n_od, HYENA_WIDTH, length), F32),
        compiler_params=_cparams(("parallel", "parallel")),
        name="hyena_filter_taps",
    )(hid, w3t, decay)
    filt6 = filt.reshape(HYENA_ORDER, 2, HYENA_WIDTH, rh, j)
    cb = HYENA_CH_BLOCK
    return pl.pallas_call(
        functools.partial(_filter_dft_kernel, rk=rh, cb=cb),
        grid=(HYENA_ORDER, HYENA_WIDTH // cb),
        in_specs=[pl.BlockSpec((1, 2, cb, rh, j), lambda o, c: (o, 0, c, 0, 0)),
                  const((rn, rn)), const((rh, j)), const((rh, j)), const((j, 2 * j))],
        out_specs=pl.BlockSpec((1, cb, 2, rh, j), lambda o, c: (o, c, 0, 0, 0)),
        out_shape=jax.ShapeDtypeStruct((HYENA_ORDER, HYENA_WIDTH, 2, rh, j), F32),
        compiler_params=_cparams(("parallel", "parallel")),
        name="hyena_filter_dft",
    )(filt6, tabs["f1_full"], tabs["tw_re"], tabs["tw_im"], tabs["f2_cat"])


def _hyena_kernel(scw_ref, scb_ref, hb_ref, x_ref, hg_ref, kf_ref, f1_ref, f1i_ref, twre_ref, twim_ref, f2_ref,
                  o_ref, *, batch, rn, rh, cb, cs):
    j = DFT_INNER
    n_inv = 2.0 / (rn * j)
    rk = rh
    lane = lax.broadcasted_iota(jnp.int32, (rh, j), 1)
    row = lax.broadcasted_iota(jnp.int32, (rh, j), 0)
    first = jnp.logical_and(lane == 0, row == 0)
    last = jnp.logical_and(lane == j - 1, row == rh - 1)
    tw_re, tw_im = twre_ref[...], twim_ref[...]

    def prev_t(x):
        p = pltpu.roll(x, 1, 1)
        return jnp.where(first, 0.0, jnp.where(lane == 0, pltpu.roll(p, 1, 0), p))

    def next_t(x):
        p = pltpu.roll(x, j - 1, 1)
        return jnp.where(last, 0.0, jnp.where(lane == j - 1, pltpu.roll(p, rh - 1, 0), p))

    def split(m, i):
        return m[2 * rk * i:2 * rk * i + rk], m[2 * rk * i + rk:2 * rk * (i + 1)]

    def long_conv(xs, ks):
        rows = []
        for x in xs:
            a = jnp.dot(f1_ref[...], x.astype(BF16), preferred_element_type=F32)
            a_re, a_im = a[:rk], a[rk:]
            rows += [a_re * tw_re - a_im * tw_im, a_re * tw_im + a_im * tw_re]
        m = jnp.dot(jnp.concatenate(rows, axis=0).astype(BF16), f2_ref[...], preferred_element_type=F32)
        rows = []
        for i, (k_re, k_im) in enumerate(ks):
            m_re, m_im = split(m, i)
            x_re, x_im = m_re[:, :j] - m_im[:, j:], m_re[:, j:] + m_im[:, :j]
            rows += [x_re * k_re - x_im * k_im, x_re * k_im + x_im * k_re]
        m = jnp.dot(jnp.concatenate(rows, axis=0).astype(BF16), f2_ref[...], preferred_element_type=F32)
        outs = []
        for i in range(len(xs)):
            m_re, m_im = split(m, i)
            b_re, b_im = m_re[:, :j] + m_im[:, j:], m_im[:, :j] - m_re[:, j:]
            bp = jnp.concatenate([b_re * tw_re + b_im * tw_im, b_im * tw_re - b_re * tw_im], axis=0).astype(BF16)
            outs.append(jnp.dot(f1i_ref[...], bp, preferred_element_type=F32) * n_inv)
        return outs

    cbase = pl.program_id(0) * cb
    for c0 in range(0, cb, cs):
        chans = [(ci, b) for ci in range(c0, c0 + cs) for b in range(batch)]
        segs = []
        for sgm in range(3):
            seg = []
            for ci, b in chans:
                chs = sgm * HYENA_WIDTH + cbase + ci
                x = x_ref[sgm, ci, b]
                seg.append(prev_t(x) * scw_ref[chs] + x * scw_ref[3 * HYENA_WIDTH + chs]
                           + next_t(x) * scw_ref[6 * HYENA_WIDTH + chs] + scb_ref[chs])
            segs.append(seg)
        z = segs[0]
        for o in range(HYENA_ORDER):
            conv = long_conv(z, [(kf_ref[o, ci, 0], kf_ref[o, ci, 1]) for ci, _ in chans])
            z = [segs[o + 1][i] * (conv[i] + z[i] * hb_ref[o * HYENA_WIDTH + cbase + ci])
                 for i, (ci, _) in enumerate(chans)]
        for i, (ci, b) in enumerate(chans):
            o_ref[ci, b] = z[i] * hg_ref[ci, b]


def _hyena(hi_t, hg_t, kf, tabs, short_w, short_b, hy_bias, batch, length):
    rn, rh, j = tabs["rn"], tabs["rh"], DFT_INNER
    cb = HYENA_CH_BLOCK
    x5 = hi_t.reshape(3, HYENA_WIDTH, batch, rh, j)
    g4 = hg_t.reshape(HYENA_WIDTH, batch, rh, j)
    smem = pl.BlockSpec(memory_space=pltpu.SMEM)
    const = lambda shape: pl.BlockSpec(shape, lambda c: (0,) * len(shape))
    out = pl.pallas_call(
        functools.partial(_hyena_kernel, batch=batch, rn=rn, rh=rh, cb=cb, cs=HYENA_CH_SUB),
        grid=(HYENA_WIDTH // cb,),
        in_specs=[smem, smem, smem,
                  pl.BlockSpec((3, cb, batch, rh, j), lambda c: (0, c, 0, 0, 0)),
                  pl.BlockSpec((cb, batch, rh, j), lambda c: (c, 0, 0, 0)),
                  pl.BlockSpec((HYENA_ORDER, cb, 2, rh, j), lambda c: (0, c, 0, 0, 0)),
                  const((rn, rh)), const((rh, rn)), const((rh, j)), const((rh, j)), const((j, 2 * j))],
        out_specs=pl.BlockSpec((cb, batch, rh, j), lambda c: (c, 0, 0, 0)),
        out_shape=jax.ShapeDtypeStruct(g4.shape, F32),
        compiler_params=_cparams(("parallel",)),
        name="hyena_conv",
    )(short_w.astype(F32).reshape(-1), short_b.astype(F32), hy_bias.astype(F32).reshape(-1),
      x5, g4, kf, tabs["f1_top"], tabs["f1_inv"], tabs["tw_re"], tabs["tw_im"], tabs["f2_cat"])
    return out.reshape(HYENA_WIDTH, batch * length)


def _rms_rows(x, g):
    return x * lax.rsqrt(jnp.mean(x * x, axis=-1, keepdims=True) + EPS) * g


def _rms_cols(x, g):
    return x * lax.rsqrt(jnp.mean(x * x, axis=0, keepdims=True) + EPS) * g


def _post_kernel(ao_ref, ga_ref, mq_ref, mk_ref, mv_ref, y_ref, u_ref, sg_ref, hy_ref, x_ref,
                 d_ref, wglu_ref, g_attn_ref, g_ssm_ref, g_hy_ref, g_mem_ref, post_g_ref, wo_ref, out_ref):
    attn_n = _rms_cols(ao_ref[...] * ga_ref[...], g_attn_ref[...])
    s = jnp.dot(mq_ref[...], mk_ref[0], preferred_element_type=F32)
    ps = []
    for h in range(MEM_HEADS):
        sh = s[:, MEM_TOKENS * h:MEM_TOKENS * (h + 1)]
        e = jnp.exp(sh - jnp.max(sh, axis=-1, keepdims=True))
        ps.append(e * (1.0 / jnp.sum(e, axis=-1, keepdims=True)))
    p = jnp.concatenate(ps, axis=1).astype(BF16)
    cross_n = _rms_rows(jnp.dot(p, mv_ref[0], preferred_element_type=F32), g_mem_ref[...])
    y = y_ref[...] + d_ref[...] * u_ref[...]
    g = y * (0.5 * (1.0 + jnp.tanh(math.sqrt(2.0 / math.pi) * (y + 0.044715 * (y * y * y)))))
    gz = jnp.dot(wglu_ref[...], g.astype(BF16), preferred_element_type=F32)
    ssm_n = _rms_cols(g * _sigmoid(gz) * sg_ref[...], g_ssm_ref[...])
    hy_n = _rms_cols(hy_ref[...], g_hy_ref[...])
    o1, o2, o3 = ATTN_WIDTH, ATTN_WIDTH + SSM_WIDTH, ATTN_WIDTH + SSM_WIDTH + HYENA_WIDTH
    mixed = (jnp.dot(attn_n.T.astype(BF16), wo_ref[0:o1], preferred_element_type=F32)
             + jnp.dot(ssm_n.T.astype(BF16), wo_ref[o1:o2], preferred_element_type=F32)
             + jnp.dot(hy_n.T.astype(BF16), wo_ref[o2:o3], preferred_element_type=F32)
             + jnp.dot(cross_n.astype(BF16), wo_ref[o3:], preferred_element_type=F32))
    out_ref[...] = x_ref[...] + _rms_rows(mixed, post_g_ref[...])


def _post(ao, ga, mq, mk_bd, mv_bd, y_t, u_t, sg_t, hy_t, x2, d, wglu_t, g_attn, g_ssm, g_hy, g_mem, post_g, wo,
          length):
    n = x2.shape[0]
    tm = TOKEN_TILE
    tiles_per_seq = length // tm
    tok_spec = lambda w: pl.BlockSpec((tm, w), lambda i: (i, 0))
    ch_spec = lambda w: pl.BlockSpec((w, tm), lambda i: (0, i))
    const = lambda shape: pl.BlockSpec(shape, lambda i: (0,) * len(shape))
    hm = MEM_HEADS * MEM_TOKENS
    return pl.pallas_call(
        _post_kernel,
        grid=(n // tm,),
        in_specs=[ch_spec(ATTN_WIDTH), ch_spec(ATTN_WIDTH), tok_spec(MEM_WIDTH),
                  pl.BlockSpec((1, MEM_WIDTH, hm), lambda i: (i // tiles_per_seq, 0, 0)),
                  pl.BlockSpec((1, hm, MEM_WIDTH), lambda i: (i // tiles_per_seq, 0, 0)),
                  ch_spec(SSM_WIDTH), ch_spec(SSM_WIDTH), ch_spec(SSM_WIDTH), ch_spec(HYENA_WIDTH),
                  tok_spec(D_MODEL),
                  const((SSM_WIDTH, 1)), const((SSM_WIDTH, SSM_WIDTH)),
                  const((ATTN_WIDTH, 1)), const((SSM_WIDTH, 1)), const((HYENA_WIDTH, 1)), const((1, MEM_WIDTH)),
                  const((1, D_MODEL)), const((MIX_WIDTH, D_MODEL))],
        out_specs=tok_spec(D_MODEL),
        out_shape=jax.ShapeDtypeStruct((n, D_MODEL), F32),
        compiler_params=_cparams(("parallel",)),
        name="post",
    )(ao, ga, mq, mk_bd, mv_bd, y_t, u_t, sg_t, hy_t, x2, d, wglu_t, g_attn, g_ssm, g_hy, g_mem, post_g, wo)


def _rope_tables(length):
    rows = length // GRID_W
    row = jnp.broadcast_to(jnp.arange(rows, dtype=F32)[:, None], (rows, GRID_W)).reshape(length)
    col = jnp.broadcast_to(jnp.arange(GRID_W, dtype=F32)[None, :], (rows, GRID_W)).reshape(length)
    inv_freq = ROPE_THETA ** (-jnp.arange(ROPE_FREQS, dtype=F32) / ROPE_FREQS)
    ang = jnp.stack([row[:, None] * inv_freq, col[:, None] * inv_freq], axis=1)
    ang = jnp.broadcast_to(ang[:, :, None, :], (length, 2, 2, ROPE_FREQS)).reshape(length, HEAD_DIM)
    cos1, sin1 = jnp.cos(ang), jnp.sin(ang)
    low1 = (jnp.arange(HEAD_DIM) % (2 * ROPE_FREQS)) < ROPE_FREQS
    cos_t, sin_t = cos1.T, jnp.where(low1, -sin1, sin1).T
    cos = jnp.concatenate([cos1, cos1], axis=1)
    sin = jnp.concatenate([sin1, sin1], axis=1)
    low = jnp.concatenate([low1, low1])
    return cos, jnp.where(low, -sin, 0.0), jnp.where(low, 0.0, sin), cos_t, sin_t


def _block_diag_heads(mk, mv, batch):
    mk4 = mk.reshape(batch, MEM_TOKENS, MEM_HEADS, HEAD_DIM)
    mv4 = mv.reshape(batch, MEM_TOKENS, MEM_HEADS, HEAD_DIM)
    eye = jnp.eye(MEM_HEADS, dtype=mk.dtype)
    k_bd = jnp.einsum('bmhd,hg->bhdgm', mk4, eye).reshape(batch, MEM_WIDTH, MEM_HEADS * MEM_TOKENS)
    v_bd = jnp.einsum('bmhd,hg->bhmgd', mv4, eye).reshape(batch, MEM_HEADS * MEM_TOKENS, MEM_WIDTH)
    return k_bd.astype(BF16), v_bd.astype(BF16)


def _layer_weights(layer, p):
    w_in = p["w_in"][layer]
    a, kv = ATTN_WIDTH, KV_WIDTH
    k0, v0, g0, mq0 = a, a + kv, a + 2 * kv, w_in.shape[1] - MEM_WIDTH
    w_tok = jnp.concatenate([w_in[:, k0:v0], w_in[:, mq0:]], axis=1).astype(BF16)
    w_ch_t = jnp.concatenate([w_in[:, :k0], w_in[:, v0:mq0]], axis=1).T.astype(BF16)
    bg = p["branch_norm"][layer].astype(F32)
    o1, o2, o3 = a, a + SSM_WIDTH, a + SSM_WIDTH + HYENA_WIDTH
    head_id = jnp.arange(kv) // HEAD_DIM
    return dict(
        w_tok=w_tok, w_ch_t=w_ch_t,
        pre_g=p["pre_norm"][layer].astype(F32)[None, :], post_g=p["post_norm"][layer].astype(F32)[None, :],
        qg=p["q_norm"][layer].astype(F32)[:, None],
        kg=jnp.tile(p["k_norm"][layer].astype(F32), ATTN_KV_HEADS)[None, :],
        ones=(head_id[:, None] == head_id[None, :]).astype(BF16),
        mem_g=p["mem_norm"][layer].astype(F32)[None, :], w_mem_kv=p["w_mem_kv"][layer].astype(BF16),
        ssm=_ssm_tables(p["ssm_a_re"][layer], p["ssm_a_im"][layer], p["ssm_log_step"][layer], p["ssm_b_re"][layer],
                        p["ssm_b_im"][layer], p["ssm_c_re"][layer], p["ssm_c_im"][layer]),
        d=p["ssm_d"][layer].astype(F32)[:, None], wglu_t=p["ssm_w_glu"][layer].T.astype(BF16),
        g_attn=bg[:o1, None], g_ssm=bg[o1:o2, None], g_hy=bg[o2:o3, None], g_mem=bg[None, o3:],
        wo=p["w_out"][layer].astype(BF16),
    )


def _mixer_layer(x2, mem2, lw, kf, tabs, rope, p, layer, batch, length):
    q_t, k, v_t, ga, mq, su_t, sg_t, hi_t, hg_t = _inproj(x2, lw["pre_g"], lw["w_tok"], lw["w_ch_t"], rope,
                                                          lw["qg"], lw["kg"], lw["ones"], length)
    ao = _flash_attention(q_t, k, v_t, batch, length)
    mem_kv = _memkv(mem2, lw["mem_g"], lw["w_mem_kv"])
    mk_bd, mv_bd = _block_diag_heads(mem_kv[:, :MEM_WIDTH], mem_kv[:, MEM_WIDTH:], batch)
    y_t = _ssm_scan(su_t, *lw["ssm"], batch, length)
    hy_t = _hyena(hi_t, hg_t, kf, tabs, p["hyena_short_w"][layer], p["hyena_short_b"][layer],
                  p["hyena_bias"][layer], batch, length)
    return _post(ao, ga, mq, mk_bd, mv_bd, y_t, su_t, sg_t, hy_t, x2, lw["d"], lw["wglu_t"],
                 lw["g_attn"], lw["g_ssm"], lw["g_hy"], lw["g_mem"], lw["post_g"], lw["wo"], length)


def _run_group(x, mem, weights, p):
    batch, length, _ = x.shape
    rope = _rope_tables(length)
    tabs = _dft_tables(length)
    x2 = x.reshape(batch * length, D_MODEL)
    mem2 = mem.reshape(batch * MEM_TOKENS, D_MODEL)
    for layer in range(DEPTH):
        kf = _hyena_filters(length, tabs, p["hyena_ffn_w1"][layer], p["hyena_ffn_b1"][layer],
                            p["hyena_ffn_w2"][layer], p["hyena_ffn_b2"][layer], p["hyena_ffn_w3"][layer],
                            p["hyena_log_decay"][layer])
        x2 = _mixer_layer(x2, mem2, weights[layer], kf, tabs, rope, p, layer, batch, length)
    return x2.reshape(batch, length, D_MODEL)


def kernel(x_prompt, x_sample, mem_prompt, mem_sample, pre_norm, post_norm, w_in, q_norm, k_norm, mem_norm, w_mem_kv, ssm_a_re, ssm_a_im, ssm_log_step, ssm_b_re, ssm_b_im, ssm_c_re, ssm_c_im, ssm_d, ssm_w_glu, hyena_short_w, hyena_short_b, hyena_ffn_w1, hyena_ffn_b1, hyena_ffn_w2, hyena_ffn_b2, hyena_ffn_w3, hyena_log_decay, hyena_bias, branch_norm, w_out):
    p = dict(pre_norm=pre_norm, post_norm=post_norm, w_in=w_in, q_norm=q_norm, k_norm=k_norm, mem_norm=mem_norm,
             w_mem_kv=w_mem_kv, ssm_a_re=ssm_a_re, ssm_a_im=ssm_a_im, ssm_log_step=ssm_log_step, ssm_b_re=ssm_b_re,
             ssm_b_im=ssm_b_im, ssm_c_re=ssm_c_re, ssm_c_im=ssm_c_im, ssm_d=ssm_d, ssm_w_glu=ssm_w_glu,
             hyena_short_w=hyena_short_w, hyena_short_b=hyena_short_b, hyena_ffn_w1=hyena_ffn_w1,
             hyena_ffn_b1=hyena_ffn_b1, hyena_ffn_w2=hyena_ffn_w2, hyena_ffn_b2=hyena_ffn_b2,
             hyena_ffn_w3=hyena_ffn_w3, hyena_log_decay=hyena_log_decay, hyena_bias=hyena_bias,
             branch_norm=branch_norm, w_out=w_out)
    weights = [_layer_weights(layer, p) for layer in range(DEPTH)]
    return (_run_group(x_prompt, mem_prompt, weights, p), _run_group(x_sample, mem_sample, weights, p))
```

```python
import functools
import math

import jax
import jax.numpy as jnp
import numpy as np
from jax import lax
from jax.experimental import pallas as pl
from jax.experimental.pallas import tpu as pltpu

F32 = jnp.float32
BF16 = jnp.bfloat16

D_MODEL = 1024
DEPTH = 2
GRID_W = 64
HEAD_DIM = 64
ATTN_HEADS = 8
ATTN_KV_HEADS = 2
ATTN_GROUP = ATTN_HEADS // ATTN_KV_HEADS
ATTN_WIDTH = ATTN_HEADS * HEAD_DIM
KV_WIDTH = ATTN_KV_HEADS * HEAD_DIM
ROPE_THETA = 10000.0
ROPE_FREQS = HEAD_DIM // 4
SSM_GROUP = 16
SSM_GROUPS = 24
SSM_WIDTH = SSM_GROUP * SSM_GROUPS
SSM_STATE = 64
HYENA_WIDTH = 384
HYENA_ORDER = 2
FILTER_BANDS = 16
FILTER_HIDDEN = 64
MEM_TOKENS = 256
MEM_HEADS = 4
MEM_WIDTH = MEM_HEADS * HEAD_DIM
MIX_WIDTH = ATTN_WIDTH + SSM_WIDTH + HYENA_WIDTH + MEM_WIDTH
EPS = 1e-6

TOK_WIDTH = KV_WIDTH + MEM_WIDTH
CH_WIDTH = 2 * ATTN_WIDTH + KV_WIDTH + 2 * SSM_WIDTH + (HYENA_ORDER + 2) * HYENA_WIDTH
V_ROWS = HEAD_DIM + 16
Q_SCALE = HEAD_DIM ** -0.5 * math.log2(math.e)
SOFTMAX_STATIC_BOUND = 60.0

LANES = 128
SUBLANES = 8
VMEM_LIMIT = 56 * 1024 * 1024
TOKEN_TILE = 512
ATTN_Q_TILE = 256
ATTN_K_TILE = 512
SSM_CHUNK = LANES
DFT_INNER = 256
HYENA_CH_BLOCK = 8
HYENA_CH_SUB = 8
FILTER_LANE_TILE = 2048


def _cparams(sem):
    return pltpu.CompilerParams(dimension_semantics=sem, vmem_limit_bytes=VMEM_LIMIT)


def _silu(x):
    return x * (1.0 / (1.0 + jnp.exp(-x)))


def _sigmoid(x):
    return 1.0 / (1.0 + jnp.exp(-x))


def _nt_dot(a, b):
    return lax.dot_general(a, b, (((1,), (1,)), ((), ())), preferred_element_type=F32)


def _rope_128(xn, cos, s_lo, s_hi):
    outs = []
    for c in range(xn.shape[1] // LANES):
        xc = xn[:, LANES * c:LANES * (c + 1)]
        outs.append(xc * cos + pltpu.roll(xc, LANES - ROPE_FREQS, 1) * s_lo + pltpu.roll(xc, ROPE_FREQS, 1) * s_hi)
    return outs[0] if len(outs) == 1 else jnp.concatenate(outs, axis=1)


def _inproj_kernel(x_ref, pre_g_ref, wtok_ref, wch_ref, cos_ref, slo_ref, shi_ref, cost_ref, sint_ref,
                   qg_ref, kg_ref, ones_ref,
                   qt_ref, k_ref, vt_ref, ga_ref, mq_ref, su_ref, sg_ref, hi_ref, hg_ref):
    tm = x_ref.shape[0]
    x = x_ref[...]
    h = x * lax.rsqrt(jnp.mean(x * x, axis=-1, keepdims=True) + EPS) * pre_g_ref[...]
    hb = h.astype(BF16)
    tok = jnp.dot(hb, wtok_ref[...], preferred_element_type=F32)
    k = tok[:, 0:KV_WIDTH]
    mq = tok[:, KV_WIDTH:]
    k_ms = jnp.dot((k * k).astype(BF16), ones_ref[...], preferred_element_type=F32) * (1.0 / HEAD_DIM)
    kn = k * lax.rsqrt(k_ms + EPS) * kg_ref[...]
    kr = _rope_128(kn, cos_ref[...], slo_ref[...], shi_ref[...])
    for j in range(ATTN_KV_HEADS):
        k_ref[j] = kr[:, HEAD_DIM * j:HEAD_DIM * (j + 1)].astype(BF16)
    mq_ref[...] = (mq * (HEAD_DIM ** -0.5)).astype(BF16)
    ch = _nt_dot(wch_ref[...], hb)
    q3 = ch[0:ATTN_WIDTH].reshape(ATTN_HEADS, HEAD_DIM, tm)
    qn = q3 * lax.rsqrt(jnp.mean(q3 * q3, axis=1, keepdims=True) + EPS) * qg_ref[...][None]
    f = ROPE_FREQS
    rot = jnp.concatenate([qn[:, f:2 * f], qn[:, 0:f], qn[:, 3 * f:4 * f], qn[:, 2 * f:3 * f]], axis=1)
    qr = (qn * cost_ref[...][None] + rot * sint_ref[...][None]) * Q_SCALE
    qt_ref[...] = qr.reshape(ATTN_WIDTH, tm).astype(BF16)
    o = ATTN_WIDTH
    ones_row = (lax.broadcasted_iota(jnp.int32, (V_ROWS - HEAD_DIM, tm), 0) == 0).astype(BF16)
    for j in range(ATTN_KV_HEADS):
        vt_ref[j, 0:HEAD_DIM] = ch[o + HEAD_DIM * j:o + HEAD_DIM * (j + 1)].astype(BF16)
        vt_ref[j, HEAD_DIM:V_ROWS] = ones_row
    o += KV_WIDTH
    ga_ref[...] = _silu(ch[o:o + ATTN_WIDTH])
    o += ATTN_WIDTH
    su_ref[...] = ch[o:o + SSM_WIDTH]
    sg_ref[...] = _silu(ch[o + SSM_WIDTH:o + 2 * SSM_WIDTH])
    o += 2 * SSM_WIDTH
    hi_ref[...] = ch[o:o + 3 * HYENA_WIDTH]
    hg_ref[...] = _silu(ch[o + 3 * HYENA_WIDTH:])


def _inproj(x2, pre_g, w_tok, w_ch_t, rope, qg, kg, ones, length):
    cos, s_lo, s_hi, cos_t, sin_t = rope
    n = x2.shape[0]
    tm = TOKEN_TILE
    nt = n // tm
    tiles_per_seq = length // tm
    tok_spec = lambda w: pl.BlockSpec((tm, w), lambda i: (i, 0))
    ch_spec = lambda w: pl.BlockSpec((w, tm), lambda i: (0, i))
    const = lambda shape: pl.BlockSpec(shape, lambda i: (0,) * len(shape))
    pos_spec = pl.BlockSpec((tm, LANES), lambda i: (i % tiles_per_seq, 0))
    pos_t_spec = pl.BlockSpec((HEAD_DIM, tm), lambda i: (0, i % tiles_per_seq))
    return pl.pallas_call(
        _inproj_kernel,
        grid=(nt,),
        in_specs=[tok_spec(D_MODEL), const((1, D_MODEL)), const((D_MODEL, TOK_WIDTH)), const((CH_WIDTH, D_MODEL)),
                  pos_spec, pos_spec, pos_spec, pos_t_spec, pos_t_spec,
                  const((HEAD_DIM, 1)), const((1, KV_WIDTH)), const((KV_WIDTH, KV_WIDTH))],
        out_specs=[ch_spec(ATTN_WIDTH),
                   pl.BlockSpec((ATTN_KV_HEADS, tm, HEAD_DIM), lambda i: (0, i, 0)),
                   pl.BlockSpec((ATTN_KV_HEADS, V_ROWS, tm), lambda i: (0, 0, i)),
                   ch_spec(ATTN_WIDTH), tok_spec(MEM_WIDTH),
                   ch_spec(SSM_WIDTH), ch_spec(SSM_WIDTH), ch_spec(3 * HYENA_WIDTH), ch_spec(HYENA_WIDTH)],
        out_shape=[jax.ShapeDtypeStruct((ATTN_WIDTH, n), BF16),
                   jax.ShapeDtypeStruct((ATTN_KV_HEADS, n, HEAD_DIM), BF16),
                   jax.ShapeDtypeStruct((ATTN_KV_HEADS, V_ROWS, n), BF16),
                   jax.ShapeDtypeStruct((ATTN_WIDTH, n), F32),
                   jax.ShapeDtypeStruct((n, MEM_WIDTH), BF16),
                   jax.ShapeDtypeStruct((SSM_WIDTH, n), F32),
                   jax.ShapeDtypeStruct((SSM_WIDTH, n), F32),
                   jax.ShapeDtypeStruct((3 * HYENA_WIDTH, n), F32),
                   jax.ShapeDtypeStruct((HYENA_WIDTH, n), F32)],
        compiler_params=_cparams(("parallel",)),
        name="inproj",
    )(x2, pre_g, w_tok, w_ch_t, cos, s_lo, s_hi, cos_t, sin_t, qg, kg, ones)


def _flash_kernel(qt_ref, k_ref, vt_ref, o_ref, acc_s, s_s, kmax_s, *, tq, tk, n_kv):
    w = ATTN_GROUP * tq
    q4t = jnp.concatenate([qt_ref[HEAD_DIM * i:HEAD_DIM * (i + 1), :] for i in range(ATTN_GROUP)], axis=1)

    def chunk(ref_slice, c):
        return ref_slice(pl.ds(pl.multiple_of(c * tk, tk), tk))

    k_chunk = lambda c: chunk(lambda d: k_ref[0, d, :], c)
    v_chunk = lambda c: chunk(lambda d: vt_ref[0, :, d], c)

    @pl.when(pl.program_id(2) == 0)
    def _():
        def key_norm(c, mx):
            kc = k_chunk(c).astype(F32)
            return jnp.maximum(mx, jnp.max(jnp.sum(kc * kc, axis=1, keepdims=True), axis=0, keepdims=True))

        mx = lax.fori_loop(0, n_kv, key_norm, jnp.zeros((1, 1), F32))
        kmax_s[...] = jnp.broadcast_to(jnp.sqrt(mx), kmax_s.shape)

    qf = q4t.astype(F32)
    bound = jnp.sqrt(jnp.sum(qf * qf, axis=0, keepdims=True)) * kmax_s[:, 0:1] * (1.0 + 2.0 ** -10)
    bound_max = jnp.max(bound)

    def finish():
        o = acc_s[0:HEAD_DIM] * (1.0 / acc_s[HEAD_DIM:HEAD_DIM + 1])
        for i in range(ATTN_GROUP):
            o_ref[HEAD_DIM * i:HEAD_DIM * (i + 1), :] = o[:, i * tq:(i + 1) * tq]

    @pl.when(bound_max <= SOFTMAX_STATIC_BOUND)
    def _():
        acc_s[...] = jnp.zeros(acc_s.shape, F32)

        def body(c, carry):
            s = jnp.dot(k_chunk(c), q4t, preferred_element_type=F32)
            p = jnp.exp2(s - bound).astype(BF16)
            acc_s[...] += jnp.dot(v_chunk(c), p, preferred_element_type=F32)
            return carry

        lax.fori_loop(0, n_kv, body, 0, unroll=4)
        finish()

    @pl.when(bound_max > SOFTMAX_STATIC_BOUND)
    def _():
        acc_s[...] = jnp.zeros(acc_s.shape, F32)

        def scores(c, slot):
            s = jnp.dot(k_chunk(c), q4t, preferred_element_type=F32)
            s_s[slot] = s
            return jnp.max(s, axis=0, keepdims=True)

        def consume(c, slot, m, m_chunk):
            m_new = jnp.maximum(m, m_chunk)
            p = jnp.exp2(s_s[slot] - m_new).astype(BF16)
            pv = jnp.dot(v_chunk(c), p, preferred_element_type=F32)
            acc_s[...] = jnp.exp2(m - m_new) * acc_s[...] + pv
            return m_new

        def body(cc, carry):
            m, mc0 = carry
            c = 2 * cc
            mc1 = scores(c + 1, 1)
            m = consume(c, 0, m, mc0)
            mc0 = scores(c + 2, 0)
            return consume(c + 1, 1, m, mc1), mc0

        m, mc0 = lax.fori_loop(0, n_kv // 2 - 1, body, (jnp.full((1, w), -jnp.inf, F32), scores(0, 0)), unroll=2)
        mc1 = scores(n_kv - 1, 1)
        consume(n_kv - 1, 1, consume(n_kv - 2, 0, m, mc0), mc1)
        finish()


def _flash_attention(q_t, k, v_t, batch, length):
    n = q_t.shape[1]
    tq, tk = ATTN_Q_TILE, ATTN_K_TILE
    nq, nk = length // tq, length // tk
    gw = ATTN_GROUP * HEAD_DIM
    return pl.pallas_call(
        functools.partial(_flash_kernel, tq=tq, tk=tk, n_kv=nk),
        grid=(batch, ATTN_KV_HEADS, nq),
        in_specs=[pl.BlockSpec((gw, tq), lambda b, h, i: (h, b * nq + i)),
                  pl.BlockSpec((1, length, HEAD_DIM), lambda b, h, i: (h, b, 0)),
                  pl.BlockSpec((1, V_ROWS, length), lambda b, h, i: (h, 0, b))],
        out_specs=pl.BlockSpec((gw, tq), lambda b, h, i: (h, b * nq + i)),
        out_shape=jax.ShapeDtypeStruct((ATTN_WIDTH, n), F32),
        scratch_shapes=[pltpu.VMEM((V_ROWS, ATTN_GROUP * tq), F32),
                        pltpu.VMEM((2, tk, ATTN_GROUP * tq), F32),
                        pltpu.VMEM((1, LANES), F32)],
        compiler_params=_cparams(("parallel", "parallel", "arbitrary")),
        name="flash_attn",
    )(q_t, k, v_t)


def _memkv_kernel(mem_ref, g_ref, w_ref, kv_ref):
    m = mem_ref[...]
    mn = m * lax.rsqrt(jnp.mean(m * m, axis=-1, keepdims=True) + EPS) * g_ref[...]
    kv_ref[...] = jnp.dot(mn.astype(BF16), w_ref[...], preferred_element_type=F32)


def _memkv(mem2, mem_g, w_kv):
    rows = mem2.shape[0]
    return pl.pallas_call(
        _memkv_kernel,
        grid=(rows // MEM_TOKENS,),
        in_specs=[pl.BlockSpec((MEM_TOKENS, D_MODEL), lambda i: (i, 0)),
                  pl.BlockSpec((1, D_MODEL), lambda i: (0, 0)),
                  pl.BlockSpec((D_MODEL, 2 * MEM_WIDTH), lambda i: (0, 0))],
        out_specs=pl.BlockSpec((MEM_TOKENS, 2 * MEM_WIDTH), lambda i: (i, 0)),
        out_shape=jax.ShapeDtypeStruct((rows, 2 * MEM_WIDTH), F32),
        compiler_params=_cparams(("parallel",)),
        name="mem_kv",
    )(mem2, mem_g, w_kv)


def _ssm_kernel(u_ref, kf_ref, kb_ref, p_ref, q_ref, a1_ref, a2_ref, y_ref, g_s, s_s, x_s, h_s, *, batch, n_chunks):
    t = SSM_CHUNK
    rows = batch * n_chunks
    half = 2 * SSM_STATE
    causal = lax.broadcasted_iota(jnp.int32, (t, t), 1) >= lax.broadcasted_iota(jnp.int32, (t, t), 0)

    def build(cp, carry):
        r0 = pl.multiple_of(cp * t, t)
        kf_rows, kb_rows = kf_ref[0, cp], kb_ref[0, cp]
        for c in range(SSM_GROUP):
            lo = pltpu.roll(jnp.broadcast_to(kf_rows[c:c + 1], (t, t)), 0, 1, stride=1, stride_axis=0)
            up = pltpu.roll(jnp.broadcast_to(kb_rows[c:c + 1], (t, t)), 0, 1, stride=1, stride_axis=0)
            g_s[pl.ds(r0, t), c * t:(c + 1) * t] = jnp.where(causal, lo, up).astype(BF16)
        return carry

    lax.fori_loop(0, SSM_GROUP, build, 0)
    u = jnp.concatenate([u_ref[0, c].reshape(rows, t) for c in range(SSM_GROUP)], axis=1).astype(BF16)
    y_intra = jnp.dot(u, g_s[...], preferred_element_type=F32)
    s_all = jnp.dot(u, p_ref[0], preferred_element_type=F32)
    s_s[...] = s_all
    x_s[...] = jnp.concatenate([pltpu.roll(s_all[:, :half], SSM_STATE, 1),
                                pltpu.roll(s_all[:, half:], SSM_STATE, 1)], axis=1)
    a1f, a2f = a1_ref[0, :, :half], a2_ref[0, :, :half]
    a1b, a2b = a1_ref[0, :, half:], a2_ref[0, :, half:]

    sub = SUBLANES
    n_blocks = n_chunks // sub

    def step(kb, carry):
        new = []
        for b in range(batch):
            hf, gf, hb, gb = carry[4 * b:4 * b + 4]
            base_f = pl.multiple_of(b * n_chunks + kb * sub, sub)
            base_b = pl.multiple_of(b * n_chunks + (n_blocks - 1 - kb) * sub, sub)
            sf, xf = s_s[pl.ds(base_f, sub), 0:half], x_s[pl.ds(base_f, sub), 0:half]
            sb, xb = s_s[pl.ds(base_b, sub), half:2 * half], x_s[pl.ds(base_b, sub), half:2 * half]
            hf_rows, hb_rows = [], [None] * sub
            for i in range(sub):
                hf_rows.append(hf)
                hf, gf = a1f * hf + a2f * gf + sf[i:i + 1], a1f * gf - a2f * hf + xf[i:i + 1]
            for i in range(sub - 1, -1, -1):
                hb_rows[i] = hb
                hb, gb = a1b * hb + a2b * gb + sb[i:i + 1], a1b * gb - a2b * hb + xb[i:i + 1]
            h_s[pl.ds(base_f, sub), 0:half] = jnp.concatenate(hf_rows, axis=0)
            h_s[pl.ds(base_b, sub), half:2 * half] = jnp.concatenate(hb_rows, axis=0)
            new += [hf, gf, hb, gb]
        return tuple(new)

    zero = jnp.zeros((1, half), F32)
    lax.fori_loop(0, n_blocks, step, (zero,) * (4 * batch))
    y = y_intra + jnp.dot(h_s[...].astype(BF16), q_ref[0], preferred_element_type=F32)
    for c in range(SSM_GROUP):
        y_ref[0, c] = y[:, c * t:(c + 1) * t].reshape(batch, n_chunks, t)


def _ssm_scan(u_t, kf, kb, p_mat, q_mat, a1, a2, batch, length):
    t = SSM_CHUNK
    nk = length // t
    gt = SSM_GROUP * t
    u5 = u_t.reshape(SSM_GROUPS, SSM_GROUP, batch, nk, t)
    blk = (1, SSM_GROUP, batch, nk, t)
    lag_spec = pl.BlockSpec((1, SSM_GROUP, SSM_GROUP, t), lambda g: (g, 0, 0, 0))
    y5 = pl.pallas_call(
        functools.partial(_ssm_kernel, batch=batch, n_chunks=nk),
        grid=(SSM_GROUPS,),
        in_specs=[pl.BlockSpec(blk, lambda g: (g, 0, 0, 0, 0)), lag_spec, lag_spec,
                  pl.BlockSpec((1, gt, 4 * SSM_STATE), lambda g: (g, 0, 0)),
                  pl.BlockSpec((1, 4 * SSM_STATE, gt), lambda g: (g, 0, 0)),
                  pl.BlockSpec((1, 1, 4 * SSM_STATE), lambda g: (g, 0, 0)),
                  pl.BlockSpec((1, 1, 4 * SSM_STATE), lambda g: (g, 0, 0))],
        out_specs=pl.BlockSpec(blk, lambda g: (g, 0, 0, 0, 0)),
        out_shape=jax.ShapeDtypeStruct(u5.shape, F32),
        scratch_shapes=[pltpu.VMEM((gt, gt), BF16)] + [pltpu.VMEM((batch * nk, 4 * SSM_STATE), F32)] * 3,
        compiler_params=_cparams(("parallel",)),
        name="ssm_scan",
    )(u5, kf, kb, p_mat, q_mat, a1, a2)
    return y5.reshape(SSM_WIDTH, batch * length)


def _ssm_tables(a_re, a_im, log_step, b_re, b_im, c_re, c_im):
    t = SSM_CHUNK
    hi = lax.Precision.HIGHEST
    lam = lax.complex(a_re.astype(F32), a_im.astype(F32))
    step = jnp.exp(log_step.astype(F32))[..., None]
    ls = lam * step
    a_bar = jnp.exp(ls)
    b_bar = ((a_bar - 1.0) / lam)[..., None] * lax.complex(b_re.astype(F32), b_im.astype(F32))
    c = lax.complex(c_re.astype(F32), c_im.astype(F32))
    tau = jnp.arange(t + 1, dtype=F32)
    pw = jnp.exp(ls[..., None] * tau)
    kern = jnp.einsum('dgcp,dgpt,dgpe->dgtce', c, pw[..., :t], b_bar, precision=hi).real
    kf = kern[0].at[:, 0].add(kern[1][:, 0]).transpose(0, 3, 2, 1)
    kb = jnp.roll(kern[1][:, ::-1], 1, axis=1).transpose(0, 3, 2, 1)
    pw_f, pw_b = pw[0], pw[1]
    pf = pw_f[:, :, t - 1 - jnp.arange(t)][..., None] * b_bar[0][:, :, None, :]
    pb = pw_b[:, :, :t][..., None] * b_bar[1][:, :, None, :]
    to_rows = lambda z: z.transpose(0, 3, 2, 1).reshape(SSM_GROUPS, SSM_GROUP * t, SSM_STATE)
    p_mat = jnp.concatenate([to_rows(pf.real), to_rows(pf.imag), to_rows(pb.real), to_rows(pb.imag)], axis=-1)
    qf = c[0].transpose(0, 2, 1)[..., None] * pw_f[:, :, 1:][:, :, None, :]
    qb = c[1].transpose(0, 2, 1)[..., None] * pw_b[:, :, t - jnp.arange(t)][:, :, None, :]
    to_cols = lambda z: z.reshape(SSM_GROUPS, SSM_STATE, SSM_GROUP * t)
    q_mat = jnp.concatenate([to_cols(qf.real), -to_cols(qf.imag), to_cols(qb.real), -to_cols(qb.imag)], axis=1)
    at = pw[..., t]
    a1 = jnp.concatenate([at[0].real, at[0].real, at[1].real, at[1].real], axis=-1)[:, None, :]
    a2 = jnp.concatenate([-at[0].imag, at[0].imag, -at[1].imag, at[1].imag], axis=-1)[:, None, :]
    return kf, kb, p_mat.astype(BF16), q_mat.astype(BF16), a1, a2


def _dft_tables(length):
    j = DFT_INNER
    n = 2 * length
    rn = n // j
    rh = rn // 2
    odd = 2 * jnp.arange(rh, dtype=jnp.int32) + 1
    r = jnp.arange(rn, dtype=jnp.int32)
    ang1 = (math.pi / rn) * ((odd[:, None] * r[None, :]) % (2 * rn)).astype(F32)
    c1, s1 = jnp.cos(ang1), jnp.sin(ang1)
    f1_full = jnp.concatenate([c1, -s1], axis=0)
    f1_top = f1_full[:, :rh]
    f1_inv = jnp.concatenate([c1[:, :rh].T, -s1[:, :rh].T], axis=1)
    jj = jnp.arange(j, dtype=jnp.int32)
    angt = (math.pi / n) * (odd[:, None] * jj[None, :]).astype(F32)
    tw_re, tw_im = jnp.cos(angt), -jnp.sin(angt)
    ang2 = (2.0 * math.pi / j) * ((jj[:, None] * jj[None, :]) % j).astype(F32)
    f2_cat = jnp.concatenate([jnp.cos(ang2), -jnp.sin(ang2)], axis=1)
    return dict(f1_full=f1_full.astype(BF16), f1_top=f1_top.astype(BF16), f1_inv=f1_inv.astype(BF16),
                tw_re=tw_re, tw_im=tw_im, f2_cat=f2_cat.astype(BF16), rn=rn, rh=rh)


def _dft_forward(a, tw_re, tw_im, f2_cat, rk):
    j = DFT_INNER
    a_re, a_im = a[:rk], a[rk:]
    ap = jnp.concatenate([a_re * tw_re - a_im * tw_im, a_re * tw_im + a_im * tw_re], axis=0).astype(BF16)
    m = jnp.dot(ap, f2_cat, preferred_element_type=F32)
    return m[:rk, :j] - m[rk:, j:], m[:rk, j:] + m[rk:, :j]


def _filter_hidden_kernel(bands_ref, w1t_ref, w1c_ref, w1s_ref, b1_ref, w2t_ref, b2_ref, hid_ref, *, length, tl):
    hi = lax.Precision.HIGHEST
    base = pl.program_id(0) * tl
    idx = (lax.broadcasted_iota(jnp.int32, (1, tl), 1) + base).astype(F32)
    for d in range(2):
        pos = idx if d == 0 else float(length) - idx
        tt = pos / float(length)
        wpos = (2.0 * math.pi / length) * pos
        arg = bands_ref[...] * wpos
        h1 = (w1t_ref[...] * tt
              + jnp.dot(w1c_ref[...], jnp.cos(arg), preferred_element_type=F32, precision=hi)
              - jnp.dot(w1s_ref[...], jnp.sin(arg), preferred_element_type=F32, precision=hi))
        h1 = jnp.sin(h1 + b1_ref[...])
        h2 = jnp.dot(w2t_ref[...], h1, preferred_element_type=F32, precision=hi)
        hid_ref[d] = jnp.sin(h2 + b2_ref[...])


def _filter_taps_kernel(hid_ref, w3t_ref, decay_ref, filt_ref, *, length, tl):
    hi = lax.Precision.HIGHEST
    d = pl.program_id(0) % 2
    base = pl.program_id(1) * tl
    idx = (lax.broadcasted_iota(jnp.int32, (1, tl), 1) + base).astype(F32)
    pos = jnp.where(d == 0, idx, float(length) - idx)
    tt = pos / float(length)
    f = jnp.dot(w3t_ref[...], hid_ref[0], preferred_element_type=F32, precision=hi)
    f = f * jnp.exp(-tt * decay_ref[...])
    filt_ref[...] = jnp.where(jnp.logical_and(d == 1, idx == 0.0), 0.0, f)


def _filter_dft_kernel(filt_ref, f1_ref, twre_ref, twim_ref, f2_ref, kf_ref, *, rk, cb):
    for c in range(cb):
        r = jnp.concatenate([filt_ref[0, 0, c], -filt_ref[0, 1, c]], axis=0).astype(BF16)
        a = jnp.dot(f1_ref[...], r, preferred_element_type=F32)
        x_re, x_im = _dft_forward(a, twre_ref[...], twim_ref[...], f2_ref[...], rk)
        kf_ref[0, c, 0] = x_re
        kf_ref[0, c, 1] = x_im


def _hyena_filters(length, tabs, w1, b1, w2, b2, w3, log_decay):
    tl = min(FILTER_LANE_TILE, length)
    nl = length // tl
    rn, rh, j = tabs["rn"], tabs["rh"], DFT_INNER
    bands = jnp.linspace(1e-4, FILTER_BANDS - 1, FILTER_BANDS, dtype=F32)[:, None]
    w1f = w1.astype(F32)
    const = lambda shape: pl.BlockSpec(shape, lambda *_: (0,) * len(shape))
    hid = pl.pallas_call(
        functools.partial(_filter_hidden_kernel, length=length, tl=tl),
        grid=(nl,),
        in_specs=[const((FILTER_BANDS, 1)), const((FILTER_HIDDEN, 1)), const((FILTER_HIDDEN, FILTER_BANDS)),
                  const((FILTER_HIDDEN, FILTER_BANDS)), const((FILTER_HIDDEN, 1)),
                  const((FILTER_HIDDEN, FILTER_HIDDEN)), const((FILTER_HIDDEN, 1))],
        out_specs=pl.BlockSpec((2, FILTER_HIDDEN, tl), lambda i: (0, 0, i)),
        out_shape=jax.ShapeDtypeStruct((2, FILTER_HIDDEN, length), F32),
        compiler_params=_cparams(("parallel",)),
        name="hyena_filter_hidden",
    )(bands, w1f[0:1].T, w1f[1:1 + FILTER_BANDS].T, w1f[1 + FILTER_BANDS:].T, b1.astype(F32)[:, None],
      w2.astype(F32).T, b2.astype(F32)[:, None])
    n_od = HYENA_ORDER * 2
    w3t = w3.astype(F32).T.reshape(n_od, HYENA_WIDTH, FILTER_HIDDEN)
    decay = jnp.exp(log_decay.astype(F32)).reshape(n_od, HYENA_WIDTH, 1)
    filt = pl.pallas_call(
        functools.partial(_filter_taps_kernel, length=length, tl=tl),
        grid=(n_od, nl),
        in_specs=[pl.BlockSpec((1, FILTER_HIDDEN, tl), lambda od, i: (od % 2, 0, i)),
                  pl.BlockSpec((None, HYENA_WIDTH, FILTER_HIDDEN), lambda od, i: (od, 0, 0)),
                  pl.BlockSpec((None, HYENA_WIDTH, 1), lambda od, i: (od, 0, 0))],
        out_specs=pl.BlockSpec((None, HYENA_WIDTH, tl), lambda od, i: (od, 0, i)),
        out_shape=jax.ShapeDtypeStruct((n_od, HYENA_WIDTH, length), F32),
        compiler_params=_cparams(("parallel", "parallel")),
        name="hyena_filter_taps",
    )(hid, w3t, decay)
    filt6 = filt.reshape(HYENA_ORDER, 2, HYENA_WIDTH, rh, j)
    cb = HYENA_CH_BLOCK
    return pl.pallas_call(
        functools.partial(_filter_dft_kernel, rk=rh, cb=cb),
        grid=(HYENA_ORDER, HYENA_WIDTH // cb),
        in_specs=[pl.BlockSpec((1, 2, cb, rh, j), lambda o, c: (o, 0, c, 0, 0)),
                  const((rn, rn)), const((rh, j)), const((rh, j)), const((j, 2 * j))],
        out_specs=pl.BlockSpec((1, cb, 2, rh, j), lambda o, c: (o, c, 0, 0, 0)),
        out_shape=jax.ShapeDtypeStruct((HYENA_ORDER, HYENA_WIDTH, 2, rh, j), F32),
        compiler_params=_cparams(("parallel", "parallel")),
        name="hyena_filter_dft",
    )(filt6, tabs["f1_full"], tabs["tw_re"], tabs["tw_im"], tabs["f2_cat"])


def _hyena_kernel(scw_ref, scb_ref, hb_ref, x_ref, hg_ref, kf_ref, f1_ref, f1i_ref, twre_ref, twim_ref, f2_ref,
                  o_ref, *, batch, rn, rh, cb, cs):
    j = DFT_INNER
    n_inv = 2.0 / (rn * j)
    rk = rh
    lane = lax.broadcasted_iota(jnp.int32, (rh, j), 1)
    row = lax.broadcasted_iota(jnp.int32, (rh, j), 0)
    first = jnp.logical_and(lane == 0, row == 0)
    last = jnp.logical_and(lane == j - 1, row == rh - 1)
    tw_re, tw_im = twre_ref[...], twim_ref[...]

    def prev_t(x):
        p = pltpu.roll(x, 1, 1)
        return jnp.where(first, 0.0, jnp.where(lane == 0, pltpu.roll(p, 1, 0), p))

    def next_t(x):
        p = pltpu.roll(x, j - 1, 1)
        return jnp.where(last, 0.0, jnp.where(lane == j - 1, pltpu.roll(p, rh - 1, 0), p))

    def split(m, i):
        return m[2 * rk * i:2 * rk * i + rk], m[2 * rk * i + rk:2 * rk * (i + 1)]

    def long_conv(xs, ks):
        rows = []
        for x in xs:
            a = jnp.dot(f1_ref[...], x.astype(BF16), preferred_element_type=F32)
            a_re, a_im = a[:rk], a[rk:]
            rows += [a_re * tw_re - a_im * tw_im, a_re * tw_im + a_im * tw_re]
        m = jnp.dot(jnp.concatenate(rows, axis=0).astype(BF16), f2_ref[...], preferred_element_type=F32)
        rows = []
        for i, (k_re, k_im) in enumerate(ks):
            m_re, m_im = split(m, i)
            x_re, x_im = m_re[:, :j] - m_im[:, j:], m_re[:, j:] + m_im[:, :j]
            rows += [x_re * k_re - x_im * k_im, x_re * k_im + x_im * k_re]
        m = jnp.dot(jnp.concatenate(rows, axis=0).astype(BF16), f2_ref[...], preferred_element_type=F32)
        outs = []
        for i in range(len(xs)):
            m_re, m_im = split(m, i)
            b_re, b_im = m_re[:, :j] + m_im[:, j:], m_im[:, :j] - m_re[:, j:]
            bp = jnp.concatenate([b_re * tw_re + b_im * tw_im, b_im * tw_re - b_re * tw_im], axis=0).astype(BF16)
            outs.append(jnp.dot(f1i_ref[...], bp, preferred_element_type=F32) * n_inv)
        return outs

    cbase = pl.program_id(0) * cb
    for c0 in range(0, cb, cs):
        chans = [(ci, b) for ci in range(c0, c0 + cs) for b in range(batch)]
        segs = []
        for sgm in range(3):
            seg = []
            for ci, b in chans:
                chs = sgm * HYENA_WIDTH + cbase + ci
                x = x_ref[sgm, ci, b]
                seg.append(prev_t(x) * scw_ref[chs] + x * scw_ref[3 * HYENA_WIDTH + chs]
                           + next_t(x) * scw_ref[6 * HYENA_WIDTH + chs] + scb_ref[chs])
            segs.append(seg)
        z = segs[0]
        for o in range(HYENA_ORDER):
            conv = long_conv(z, [(kf_ref[o, ci, 0], kf_ref[o, ci, 1]) for ci, _ in chans])
            z = [segs[o + 1][i] * (conv[i] + z[i] * hb_ref[o * HYENA_WIDTH + cbase + ci])
                 for i, (ci, _) in enumerate(chans)]
        for i, (ci, b) in enumerate(chans):
            o_ref[ci, b] = z[i] * hg_ref[ci, b]


def _hyena(hi_t, hg_t, kf, tabs, short_w, short_b, hy_bias, batch, length):
    rn, rh, j = tabs["rn"], tabs["rh"], DFT_INNER
    cb = HYENA_CH_BLOCK
    x5 = hi_t.reshape(3, HYENA_WIDTH, batch, rh, j)
    g4 = hg_t.reshape(HYENA_WIDTH, batch, rh, j)
    smem = pl.BlockSpec(memory_space=pltpu.SMEM)
    const = lambda shape: pl.BlockSpec(shape, lambda c: (0,) * len(shape))
    out = pl.pallas_call(
        functools.partial(_hyena_kernel, batch=batch, rn=rn, rh=rh, cb=cb, cs=HYENA_CH_SUB),
        grid=(HYENA_WIDTH // cb,),
        in_specs=[smem, smem, smem,
                  pl.BlockSpec((3, cb, batch, rh, j), lambda c: (0, c, 0, 0, 0)),
                  pl.BlockSpec((cb, batch, rh, j), lambda c: (c, 0, 0, 0)),
                  pl.BlockSpec((HYENA_ORDER, cb, 2, rh, j), lambda c: (0, c, 0, 0, 0)),
                  const((rn, rh)), const((rh, rn)), const((rh, j)), const((rh, j)), const((j, 2 * j))],
        out_specs=pl.BlockSpec((cb, batch, rh, j), lambda c: (c, 0, 0, 0)),
        out_shape=jax.ShapeDtypeStruct(g4.shape, F32),
        compiler_params=_cparams(("parallel",)),
        name="hyena_conv",
    )(short_w.astype(F32).reshape(-1), short_b.astype(F32), hy_bias.astype(F32).reshape(-1),
      x5, g4, kf, tabs["f1_top"], tabs["f1_inv"], tabs["tw_re"], tabs["tw_im"], tabs["f2_cat"])
    return out.reshape(HYENA_WIDTH, batch * length)


def _rms_rows(x, g):
    return x * lax.rsqrt(jnp.mean(x * x, axis=-1, keepdims=True) + EPS) * g


def _rms_cols(x, g):
    return x * lax.rsqrt(jnp.mean(x * x, axis=0, keepdims=True) + EPS) * g


def _post_kernel(ao_ref, ga_ref, mq_ref, mk_ref, mv_ref, y_ref, u_ref, sg_ref, hy_ref, x_ref,
                 d_ref, wglu_ref, g_attn_ref, g_ssm_ref, g_hy_ref, g_mem_ref, post_g_ref, wo_ref, out_ref):
    attn_n = _rms_cols(ao_ref[...] * ga_ref[...], g_attn_ref[...])
    s = jnp.dot(mq_ref[...], mk_ref[0], preferred_element_type=F32)
    ps = []
    for h in range(MEM_HEADS):
        sh = s[:, MEM_TOKENS * h:MEM_TOKENS * (h + 1)]
        e = jnp.exp(sh - jnp.max(sh, axis=-1, keepdims=True))
        ps.append(e * (1.0 / jnp.sum(e, axis=-1, keepdims=True)))
    p = jnp.concatenate(ps, axis=1).astype(BF16)
    cross_n = _rms_rows(jnp.dot(p, mv_ref[0], preferred_element_type=F32), g_mem_ref[...])
    y = y_ref[...] + d_ref[...] * u_ref[...]
    g = y * (0.5 * (1.0 + jnp.tanh(math.sqrt(2.0 / math.pi) * (y + 0.044715 * (y * y * y)))))
    gz = jnp.dot(wglu_ref[...], g.astype(BF16), preferred_element_type=F32)
    ssm_n = _rms_cols(g * _sigmoid(gz) * sg_ref[...], g_ssm_ref[...])
    hy_n = _rms_cols(hy_ref[...], g_hy_ref[...])
    o1, o2, o3 = ATTN_WIDTH, ATTN_WIDTH + SSM_WIDTH, ATTN_WIDTH + SSM_WIDTH + HYENA_WIDTH
    mixed = (jnp.dot(attn_n.T.astype(BF16), wo_ref[0:o1], preferred_element_type=F32)
             + jnp.dot(ssm_n.T.astype(BF16), wo_ref[o1:o2], preferred_element_type=F32)
             + jnp.dot(hy_n.T.astype(BF16), wo_ref[o2:o3], preferred_element_type=F32)
             + jnp.dot(cross_n.astype(BF16), wo_ref[o3:], preferred_element_type=F32))
    out_ref[...] = x_ref[...] + _rms_rows(mixed, post_g_ref[...])


def _post(ao, ga, mq, mk_bd, mv_bd, y_t, u_t, sg_t, hy_t, x2, d, wglu_t, g_attn, g_ssm, g_hy, g_mem, post_g, wo,
          length):
    n = x2.shape[0]
    tm = TOKEN_TILE
    tiles_per_seq = length // tm
    tok_spec = lambda w: pl.BlockSpec((tm, w), lambda i: (i, 0))
    ch_spec = lambda w: pl.BlockSpec((w, tm), lambda i: (0, i))
    const = lambda shape: pl.BlockSpec(shape, lambda i: (0,) * len(shape))
    hm = MEM_HEADS * MEM_TOKENS
    return pl.pallas_call(
        _post_kernel,
        grid=(n // tm,),
        in_specs=[ch_spec(ATTN_WIDTH), ch_spec(ATTN_WIDTH), tok_spec(MEM_WIDTH),
                  pl.BlockSpec((1, MEM_WIDTH, hm), lambda i: (i // tiles_per_seq, 0, 0)),
                  pl.BlockSpec((1, hm, MEM_WIDTH), lambda i: (i // tiles_per_seq, 0, 0)),
                  ch_spec(SSM_WIDTH), ch_spec(SSM_WIDTH), ch_spec(SSM_WIDTH), ch_spec(HYENA_WIDTH),
                  tok_spec(D_MODEL),
                  const((SSM_WIDTH, 1)), const((SSM_WIDTH, SSM_WIDTH)),
                  const((ATTN_WIDTH, 1)), const((SSM_WIDTH, 1)), const((HYENA_WIDTH, 1)), const((1, MEM_WIDTH)),
                  const((1, D_MODEL)), const((MIX_WIDTH, D_MODEL))],
        out_specs=tok_spec(D_MODEL),
        out_shape=jax.ShapeDtypeStruct((n, D_MODEL), F32),
        compiler_params=_cparams(("parallel",)),
        name="post",
    )(ao, ga, mq, mk_bd, mv_bd, y_t, u_t, sg_t, hy_t, x2, d, wglu_t, g_attn, g_ssm, g_hy, g_mem, post_g, wo)


def _rope_tables(length):
    rows = length // GRID_W
    row = jnp.broadcast_to(jnp.arange(rows, dtype=F32)[:, None], (rows, GRID_W)).reshape(length)
    col = jnp.broadcast_to(jnp.arange(GRID_W, dtype=F32)[None, :], (rows, GRID_W)).reshape(length)
    inv_freq = ROPE_THETA ** (-jnp.arange(ROPE_FREQS, dtype=F32) / ROPE_FREQS)
    ang = jnp.stack([row[:, None] * inv_freq, col[:, None] * inv_freq], axis=1)
    ang = jnp.broadcast_to(ang[:, :, None, :], (length, 2, 2, ROPE_FREQS)).reshape(length, HEAD_DIM)
    cos1, sin1 = jnp.cos(ang), jnp.sin(ang)
    low1 = (jnp.arange(HEAD_DIM) % (2 * ROPE_FREQS)) < ROPE_FREQS
    cos_t, sin_t = cos1.T, jnp.where(low1, -sin1, sin1).T
    cos = jnp.concatenate([cos1, cos1], axis=1)
    sin = jnp.concatenate([sin1, sin1], axis=1)
    low = jnp.concatenate([low1, low1])
    return cos, jnp.where(low, -sin, 0.0), jnp.where(low, 0.0, sin), cos_t, sin_t


def _block_diag_heads(mk, mv, batch):
    mk4 = mk.reshape(batch, MEM_TOKENS, MEM_HEADS, HEAD_DIM)
    mv4 = mv.reshape(batch, MEM_TOKENS, MEM_HEADS, HEAD_DIM)
    eye = jnp.eye(MEM_HEADS, dtype=mk.dtype)
    k_bd = jnp.einsum('bmhd,hg->bhdgm', mk4, eye).reshape(batch, MEM_WIDTH, MEM_HEADS * MEM_TOKENS)
    v_bd = jnp.einsum('bmhd,hg->bhmgd', mv4, eye).reshape(batch, MEM_HEADS * MEM_TOKENS, MEM_WIDTH)
    return k_bd.astype(BF16), v_bd.astype(BF16)


def _layer_weights(layer, p):
    w_in = p["w_in"][layer]
    a, kv = ATTN_WIDTH, KV_WIDTH
    k0, v0, g0, mq0 = a, a + kv, a + 2 * kv, w_in.shape[1] - MEM_WIDTH
    w_tok = jnp.concatenate([w_in[:, k0:v0], w_in[:, mq0:]], axis=1).astype(BF16)
    w_ch_t = jnp.concatenate([w_in[:, :k0], w_in[:, v0:mq0]], axis=1).T.astype(BF16)
    bg = p["branch_norm"][layer].astype(F32)
    o1, o2, o3 = a, a + SSM_WIDTH, a + SSM_WIDTH + HYENA_WIDTH
    head_id = jnp.arange(kv) // HEAD_DIM
    return dict(
        w_tok=w_tok, w_ch_t=w_ch_t,
        pre_g=p["pre_norm"][layer].astype(F32)[None, :], post_g=p["post_norm"][layer].astype(F32)[None, :],
        qg=p["q_norm"][layer].astype(F32)[:, None],
        kg=jnp.tile(p["k_norm"][layer].astype(F32), ATTN_KV_HEADS)[None, :],
        ones=(head_id[:, None] == head_id[None, :]).astype(BF16),
        mem_g=p["mem_norm"][layer].astype(F32)[None, :], w_mem_kv=p["w_mem_kv"][layer].astype(BF16),
        ssm=_ssm_tables(p["ssm_a_re"][layer], p["ssm_a_im"][layer], p["ssm_log_step"][layer], p["ssm_b_re"][layer],
                        p["ssm_b_im"][layer], p["ssm_c_re"][layer], p["ssm_c_im"][layer]),
        d=p["ssm_d"][layer].astype(F32)[:, None], wglu_t=p["ssm_w_glu"][layer].T.astype(BF16),
        g_attn=bg[:o1, None], g_ssm=bg[o1:o2, None], g_hy=bg[o2:o3, None], g_mem=bg[None, o3:],
        wo=p["w_out"][layer].astype(BF16),
    )


def _mixer_layer(x2, mem2, lw, kf, tabs, rope, p, layer, batch, length):
    q_t, k, v_t, ga, mq, su_t, sg_t, hi_t, hg_t = _inproj(x2, lw["pre_g"], lw["w_tok"], lw["w_ch_t"], rope,
                                                          lw["qg"], lw["kg"], lw["ones"], length)
    ao = _flash_attention(q_t, k, v_t, batch, length)
    mem_kv = _memkv(mem2, lw["mem_g"], lw["w_mem_kv"])
    mk_bd, mv_bd = _block_diag_heads(mem_kv[:, :MEM_WIDTH], mem_kv[:, MEM_WIDTH:], batch)
    y_t = _ssm_scan(su_t, *lw["ssm"], batch, length)
    hy_t = _hyena(hi_t, hg_t, kf, tabs, p["hyena_short_w"][layer], p["hyena_short_b"][layer],
                  p["hyena_bias"][layer], batch, length)
    return _post(ao, ga, mq, mk_bd, mv_bd, y_t, su_t, sg_t, hy_t, x2, lw["d"], lw["wglu_t"],
                 lw["g_attn"], lw["g_ssm"], lw["g_hy"], lw["g_mem"], lw["post_g"], lw["wo"], length)


def _run_group(x, mem, weights, p):
    batch, length, _ = x.shape
    rope = _rope_tables(length)
    tabs = _dft_tables(length)
    x2 = x.reshape(batch * length, D_MODEL)
    mem2 = mem.reshape(batch * MEM_TOKENS, D_MODEL)
    for layer in range(DEPTH):
        kf = _hyena_filters(length, tabs, p["hyena_ffn_w1"][layer], p["hyena_ffn_b1"][layer],
                            p["hyena_ffn_w2"][layer], p["hyena_ffn_b2"][layer], p["hyena_ffn_w3"][layer],
                            p["hyena_log_decay"][layer])
        x2 = _mixer_layer(x2, mem2, weights[layer], kf, tabs, rope, p, layer, batch, length)
    return x2.reshape(batch, length, D_MODEL)


def kernel(x_prompt, x_sample, mem_prompt, mem_sample, pre_norm, post_norm, w_in, q_norm, k_norm, mem_norm, w_mem_kv, ssm_a_re, ssm_a_im, ssm_log_step, ssm_b_re, ssm_b_im, ssm_c_re, ssm_c_im, ssm_d, ssm_w_glu, hyena_short_w, hyena_short_b, hyena_ffn_w1, hyena_ffn_b1, hyena_ffn_w2, hyena_ffn_b2, hyena_ffn_w3, hyena_log_decay, hyena_bias, branch_norm, w_out):
    p = dict(pre_norm=pre_norm, post_norm=post_norm, w_in=w_in, q_norm=q_norm, k_norm=k_norm, mem_norm=mem_norm,
             w_mem_kv=w_mem_kv, ssm_a_re=ssm_a_re, ssm_a_im=ssm_a_im, ssm_log_step=ssm_log_step, ssm_b_re=ssm_b_re,
             ssm_b_im=ssm_b_im, ssm_c_re=ssm_c_re, ssm_c_im=ssm_c_im, ssm_d=ssm_d, ssm_w_glu=ssm_w_glu,
             hyena_short_w=hyena_short_w, hyena_short_b=hyena_short_b, hyena_ffn_w1=hyena_ffn_w1,
             hyena_ffn_b1=hyena_ffn_b1, hyena_ffn_w2=hyena_ffn_w2, hyena_ffn_b2=hyena_ffn_b2,
             hyena_ffn_w3=hyena_ffn_w3, hyena_log_decay=hyena_log_decay, hyena_bias=hyena_bias,
             branch_norm=branch_norm, w_out=w_out)
    weights = [_layer_weights(layer, p) for layer in range(DEPTH)]
    return (_run_group(x_prompt, mem_prompt, weights, p), _run_group(x_sample, mem_sample, weights, p))
```

```python
import functools
import math

import jax
import jax.numpy as jnp
import numpy as np
from jax import lax
from jax.experimental import pallas as pl
from jax.experimental.pallas import tpu as pltpu

F32 = jnp.float32
BF16 = jnp.bfloat16

D_MODEL = 1024
DEPTH = 2
GRID_W = 64
HEAD_DIM = 64
ATTN_HEADS = 8
ATTN_KV_HEADS = 2
ATTN_GROUP = ATTN_HEADS // ATTN_KV_HEADS
ATTN_WIDTH = ATTN_HEADS * HEAD_DIM
KV_WIDTH = ATTN_KV_HEADS * HEAD_DIM
ROPE_THETA = 10000.0
ROPE_FREQS = HEAD_DIM // 4
SSM_GROUP = 16
SSM_GROUPS = 24
SSM_WIDTH = SSM_GROUP * SSM_GROUPS
SSM_STATE = 64
HYENA_WIDTH = 384
HYENA_ORDER = 2
FILTER_BANDS = 16
FILTER_HIDDEN = 64
MEM_TOKENS = 256
MEM_HEADS = 4
MEM_WIDTH = MEM_HEADS * HEAD_DIM
MIX_WIDTH = ATTN_WIDTH + SSM_WIDTH + HYENA_WIDTH + MEM_WIDTH
EPS = 1e-6

TOK_WIDTH = KV_WIDTH + MEM_WIDTH
CH_WIDTH = 2 * ATTN_WIDTH + KV_WIDTH + 2 * SSM_WIDTH + (HYENA_ORDER + 2) * HYENA_WIDTH
V_ROWS = HEAD_DIM + 16
Q_SCALE = HEAD_DIM ** -0.5 * math.log2(math.e)
SOFTMAX_STATIC_BOUND = 60.0

LANES = 128
SUBLANES = 8
VMEM_LIMIT = 56 * 1024 * 1024
TOKEN_TILE = 512
ATTN_Q_TILE = 256
ATTN_K_TILE = 512
SSM_CHUNK = LANES
DFT_INNER = 256
HYENA_CH_BLOCK = 8
HYENA_CH_SUB = 8
FILTER_LANE_TILE = 2048


def _cparams(sem):
    return pltpu.CompilerParams(dimension_semantics=sem, vmem_limit_bytes=VMEM_LIMIT)


def _silu(x):
    return x * (1.0 / (1.0 + jnp.exp(-x)))


def _sigmoid(x):
    return 1.0 / (1.0 + jnp.exp(-x))


def _nt_dot(a, b):
    return lax.dot_general(a, b, (((1,), (1,)), ((), ())), preferred_element_type=F32)


def _rope_128(xn, cos, s_lo, s_hi):
    outs = []
    for c in range(xn.shape[1] // LANES):
        xc = xn[:, LANES * c:LANES * (c + 1)]
        outs.append(xc * cos + pltpu.roll(xc, LANES - ROPE_FREQS, 1) * s_lo + pltpu.roll(xc, ROPE_FREQS, 1) * s_hi)
    return outs[0] if len(outs) == 1 else jnp.concatenate(outs, axis=1)


def _inproj_kernel(x_ref, pre_g_ref, wtok_ref, wch_ref, cos_ref, slo_ref, shi_ref, cost_ref, sint_ref,
                   qg_ref, kg_ref, ones_ref,
                   qt_ref, k_ref, vt_ref, ga_ref, mq_ref, su_ref, sg_ref, hi_ref, hg_ref):
    tm = x_ref.shape[0]
    x = x_ref[...]
    h = x * lax.rsqrt(jnp.mean(x * x, axis=-1, keepdims=True) + EPS) * pre_g_ref[...]
    hb = h.astype(BF16)
    tok = jnp.dot(hb, wtok_ref[...], preferred_element_type=F32)
    k = tok[:, 0:KV_WIDTH]
    mq = tok[:, KV_WIDTH:]
    k_ms = jnp.dot((k * k).astype(BF16), ones_ref[...], preferred_element_type=F32) * (1.0 / HEAD_DIM)
    kn = k * lax.rsqrt(k_ms + EPS) * kg_ref[...]
    kr = _rope_128(kn, cos_ref[...], slo_ref[...], shi_ref[...])
    for j in range(ATTN_KV_HEADS):
        k_ref[j] = kr[:, HEAD_DIM * j:HEAD_DIM * (j + 1)].astype(BF16)
    mq_ref[...] = (mq * (HEAD_DIM ** -0.5)).astype(BF16)
    ch = _nt_dot(wch_ref[...], hb)
    q3 = ch[0:ATTN_WIDTH].reshape(ATTN_HEADS, HEAD_DIM, tm)
    qn = q3 * lax.rsqrt(jnp.mean(q3 * q3, axis=1, keepdims=True) + EPS) * qg_ref[...][None]
    f = ROPE_FREQS
    rot = jnp.concatenate([qn[:, f:2 * f], qn[:, 0:f], qn[:, 3 * f:4 * f], qn[:, 2 * f:3 * f]], axis=1)
    qr = (qn * cost_ref[...][None] + rot * sint_ref[...][None]) * Q_SCALE
    qt_ref[...] = qr.reshape(ATTN_WIDTH, tm).astype(BF16)
    o = ATTN_WIDTH
    ones_row = (lax.broadcasted_iota(jnp.int32, (V_ROWS - HEAD_DIM, tm), 0) == 0).astype(BF16)
    for j in range(ATTN_KV_HEADS):
        vt_ref[j, 0:HEAD_DIM] = ch[o + HEAD_DIM * j:o + HEAD_DIM * (j + 1)].astype(BF16)
        vt_ref[j, HEAD_DIM:V_ROWS] = ones_row
    o += KV_WIDTH
    ga_ref[...] = _silu(ch[o:o + ATTN_WIDTH])
    o += ATTN_WIDTH
    su_ref[...] = ch[o:o + SSM_WIDTH]
    sg_ref[...] = _silu(ch[o + SSM_WIDTH:o + 2 * SSM_WIDTH])
    o += 2 * SSM_WIDTH
    hi_ref[...] = ch[o:o + 3 * HYENA_WIDTH]
    hg_ref[...] = _silu(ch[o + 3 * HYENA_WIDTH:])


def _inproj(x2, pre_g, w_tok, w_ch_t, rope, qg, kg, ones, length):
    cos, s_lo, s_hi, cos_t, sin_t = rope
    n = x2.shape[0]
    tm = TOKEN_TILE
    nt = n // tm
    tiles_per_seq = length // tm
    tok_spec = lambda w: pl.BlockSpec((tm, w), lambda i: (i, 0))
    ch_spec = lambda w: pl.BlockSpec((w, tm), lambda i: (0, i))
    const = lambda shape: pl.BlockSpec(shape, lambda i: (0,) * len(shape))
    pos_spec = pl.BlockSpec((tm, LANES), lambda i: (i % tiles_per_seq, 0))
    pos_t_spec = pl.BlockSpec((HEAD_DIM, tm), lambda i: (0, i % tiles_per_seq))
    return pl.pallas_call(
        _inproj_kernel,
        grid=(nt,),
        in_specs=[tok_spec(D_MODEL), const((1, D_MODEL)), const((D_MODEL, TOK_WIDTH)), const((CH_WIDTH, D_MODEL)),
                  pos_spec, pos_spec, pos_spec, pos_t_spec, pos_t_spec,
                  const((HEAD_DIM, 1)), const((1, KV_WIDTH)), const((KV_WIDTH, KV_WIDTH))],
        out_specs=[ch_spec(ATTN_WIDTH),
                   pl.BlockSpec((ATTN_KV_HEADS, tm, HEAD_DIM), lambda i: (0, i, 0)),
                   pl.BlockSpec((ATTN_KV_HEADS, V_ROWS, tm), lambda i: (0, 0, i)),
                   ch_spec(ATTN_WIDTH), tok_spec(MEM_WIDTH),
                   ch_spec(SSM_WIDTH), ch_spec(SSM_WIDTH), ch_spec(3 * HYENA_WIDTH), ch_spec(HYENA_WIDTH)],
        out_shape=[jax.ShapeDtypeStruct((ATTN_WIDTH, n), BF16),
                   jax.ShapeDtypeStruct((ATTN_KV_HEADS, n, HEAD_DIM), BF16),
                   jax.ShapeDtypeStruct((ATTN_KV_HEADS, V_ROWS, n), BF16),
                   jax.ShapeDtypeStruct((ATTN_WIDTH, n), F32),
                   jax.ShapeDtypeStruct((n, MEM_WIDTH), BF16),
                   jax.ShapeDtypeStruct((SSM_WIDTH, n), F32),
                   jax.ShapeDtypeStruct((SSM_WIDTH, n), F32),
                   jax.ShapeDtypeStruct((3 * HYENA_WIDTH, n), F32),
                   jax.ShapeDtypeStruct((HYENA_WIDTH, n), F32)],
        compiler_params=_cparams(("parallel",)),
        name="inproj",
    )(x2, pre_g, w_tok, w_ch_t, cos, s_lo, s_hi, cos_t, sin_t, qg, kg, ones)


def _flash_kernel(qt_ref, k_ref, vt_ref, o_ref, acc_s, s_s, kmax_s, *, tq, tk, n_kv):
    w = ATTN_GROUP * tq
    q4t = jnp.concatenate([qt_ref[HEAD_DIM * i:HEAD_DIM * (i + 1), :] for i in range(ATTN_GROUP)], axis=1)

    def chunk(ref_slice, c):
        return ref_slice(pl.ds(pl.multiple_of(c * tk, tk), tk))

    k_chunk = lambda c: chunk(lambda d: k_ref[0, d, :], c)
    v_chunk = lambda c: chunk(lambda d: vt_ref[0, :, d], c)

    @pl.when(pl.program_id(2) == 0)
    def _():
        def key_norm(c, mx):
            kc = k_chunk(c).astype(F32)
            return jnp.maximum(mx, jnp.max(jnp.sum(kc * kc, axis=1, keepdims=True), axis=0, keepdims=True))

        mx = lax.fori_loop(0, n_kv, key_norm, jnp.zeros((1, 1), F32))
        kmax_s[...] = jnp.broadcast_to(jnp.sqrt(mx), kmax_s.shape)

    qf = q4t.astype(F32)
    bound = jnp.sqrt(jnp.sum(qf * qf, axis=0, keepdims=True)) * kmax_s[:, 0:1] * (1.0 + 2.0 ** -10)
    bound_max = jnp.max(bound)

    def finish():
        o = acc_s[0:HEAD_DIM] * (1.0 / acc_s[HEAD_DIM:HEAD_DIM + 1])
        for i in range(ATTN_GROUP):
            o_ref[HEAD_DIM * i:HEAD_DIM * (i + 1), :] = o[:, i * tq:(i + 1) * tq]

    @pl.when(bound_max <= SOFTMAX_STATIC_BOUND)
    def _():
        acc_s[...] = jnp.zeros(acc_s.shape, F32)

        def body(c, carry):
            s = jnp.dot(k_chunk(c), q4t, preferred_element_type=F32)
            p = jnp.exp2(s - bound).astype(BF16)
            acc_s[...] += jnp.dot(v_chunk(c), p, preferred_element_type=F32)
            return carry

        lax.fori_loop(0, n_kv, body, 0, unroll=8)
        finish()

    @pl.when(bound_max > SOFTMAX_STATIC_BOUND)
    def _():
        acc_s[...] = jnp.zeros(acc_s.shape, F32)

        def scores(c, slot):
            s = jnp.dot(k_chunk(c), q4t, preferred_element_type=F32)
            s_s[slot] = s
            return jnp.max(s, axis=0, keepdims=True)

        def consume(c, slot, m, m_chunk):
            m_new = jnp.maximum(m, m_chunk)
            p = jnp.exp2(s_s[slot] - m_new).astype(BF16)
            pv = jnp.dot(v_chunk(c), p, preferred_element_type=F32)
            acc_s[...] = jnp.exp2(m - m_new) * acc_s[...] + pv
            return m_new

        def body(cc, carry):
            m, mc0 = carry
            c = 2 * cc
            mc1 = scores(c + 1, 1)
            m = consume(c, 0, m, mc0)
            mc0 = scores(c + 2, 0)
            return consume(c + 1, 1, m, mc1), mc0

        m, mc0 = lax.fori_loop(0, n_kv // 2 - 1, body, (jnp.full((1, w), -jnp.inf, F32), scores(0, 0)), unroll=2)
        mc1 = scores(n_kv - 1, 1)
        consume(n_kv - 1, 1, consume(n_kv - 2, 0, m, mc0), mc1)
        finish()


def _flash_attention(q_t, k, v_t, batch, length):
    n = q_t.shape[1]
    tq, tk = ATTN_Q_TILE, ATTN_K_TILE
    nq, nk = length // tq, length // tk
    gw = ATTN_GROUP * HEAD_DIM
    return pl.pallas_call(
        functools.partial(_flash_kernel, tq=tq, tk=tk, n_kv=nk),
        grid=(batch, ATTN_KV_HEADS, nq),
        in_specs=[pl.BlockSpec((gw, tq), lambda b, h, i: (h, b * nq + i)),
                  pl.BlockSpec((1, length, HEAD_DIM), lambda b, h, i: (h, b, 0)),
                  pl.BlockSpec((1, V_ROWS, length), lambda b, h, i: (h, 0, b))],
        out_specs=pl.BlockSpec((gw, tq), lambda b, h, i: (h, b * nq + i)),
        out_shape=jax.ShapeDtypeStruct((ATTN_WIDTH, n), F32),
        scratch_shapes=[pltpu.VMEM((V_ROWS, ATTN_GROUP * tq), F32),
                        pltpu.VMEM((2, tk, ATTN_GROUP * tq), F32),
                        pltpu.VMEM((1, LANES), F32)],
        compiler_params=_cparams(("parallel", "parallel", "arbitrary")),
        name="flash_attn",
    )(q_t, k, v_t)


def _memkv_kernel(mem_ref, g_ref, w_ref, kv_ref):
    m = mem_ref[...]
    mn = m * lax.rsqrt(jnp.mean(m * m, axis=-1, keepdims=True) + EPS) * g_ref[...]
    kv_ref[...] = jnp.dot(mn.astype(BF16), w_ref[...], preferred_element_type=F32)


def _memkv(mem2, mem_g, w_kv):
    rows = mem2.shape[0]
    return pl.pallas_call(
        _memkv_kernel,
        grid=(rows // MEM_TOKENS,),
        in_specs=[pl.BlockSpec((MEM_TOKENS, D_MODEL), lambda i: (i, 0)),
                  pl.BlockSpec((1, D_MODEL), lambda i: (0, 0)),
                  pl.BlockSpec((D_MODEL, 2 * MEM_WIDTH), lambda i: (0, 0))],
        out_specs=pl.BlockSpec((MEM_TOKENS, 2 * MEM_WIDTH), lambda i: (i, 0)),
        out_shape=jax.ShapeDtypeStruct((rows, 2 * MEM_WIDTH), F32),
        compiler_params=_cparams(("parallel",)),
        name="mem_kv",
    )(mem2, mem_g, w_kv)


def _ssm_kernel(u_ref, kf_ref, kb_ref, p_ref, q_ref, a1_ref, a2_ref, y_ref, g_s, s_s, x_s, h_s, *, batch, n_chunks):
    t = SSM_CHUNK
    rows = batch * n_chunks
    half = 2 * SSM_STATE
    causal = lax.broadcasted_iota(jnp.int32, (t, t), 1) >= lax.broadcasted_iota(jnp.int32, (t, t), 0)

    def build(cp, carry):
        r0 = pl.multiple_of(cp * t, t)
        kf_rows, kb_rows = kf_ref[0, cp], kb_ref[0, cp]
        for c in range(SSM_GROUP):
            lo = pltpu.roll(jnp.broadcast_to(kf_rows[c:c + 1], (t, t)), 0, 1, stride=1, stride_axis=0)
            up = pltpu.roll(jnp.broadcast_to(kb_rows[c:c + 1], (t, t)), 0, 1, stride=1, stride_axis=0)
            g_s[pl.ds(r0, t), c * t:(c + 1) * t] = jnp.where(causal, lo, up).astype(BF16)
        return carry

    lax.fori_loop(0, SSM_GROUP, build, 0)
    u = jnp.concatenate([u_ref[0, c].reshape(rows, t) for c in range(SSM_GROUP)], axis=1).astype(BF16)
    y_intra = jnp.dot(u, g_s[...], preferred_element_type=F32)
    s_all = jnp.dot(u, p_ref[0], preferred_element_type=F32)
    s_s[...] = s_all
    x_s[...] = jnp.concatenate([pltpu.roll(s_all[:, :half], SSM_STATE, 1),
                                pltpu.roll(s_all[:, half:], SSM_STATE, 1)], axis=1)
    a1f, a2f = a1_ref[0, :, :half], a2_ref[0, :, :half]
    a1b, a2b = a1_ref[0, :, half:], a2_ref[0, :, half:]

    sub = SUBLANES
    n_blocks = n_chunks // sub

    def step(kb, carry):
        new = []
        for b in range(batch):
            hf, gf, hb, gb = carry[4 * b:4 * b + 4]
            base_f = pl.multiple_of(b * n_chunks + kb * sub, sub)
            base_b = pl.multiple_of(b * n_chunks + (n_blocks - 1 - kb) * sub, sub)
            sf, xf = s_s[pl.ds(base_f, sub), 0:half], x_s[pl.ds(base_f, sub), 0:half]
            sb, xb = s_s[pl.ds(base_b, sub), half:2 * half], x_s[pl.ds(base_b, sub), half:2 * half]
            hf_rows, hb_rows = [], [None] * sub
            for i in range(sub):
                hf_rows.append(hf)
                hf, gf = a1f * hf + a2f * gf + sf[i:i + 1], a1f * gf - a2f * hf + xf[i:i + 1]
            for i in range(sub - 1, -1, -1):
                hb_rows[i] = hb
                hb, gb = a1b * hb + a2b * gb + sb[i:i + 1], a1b * gb - a2b * hb + xb[i:i + 1]
            h_s[pl.ds(base_f, sub), 0:half] = jnp.concatenate(hf_rows, axis=0)
            h_s[pl.ds(base_b, sub), half:2 * half] = jnp.concatenate(hb_rows, axis=0)
            new += [hf, gf, hb, gb]
        return tuple(new)

    zero = jnp.zeros((1, half), F32)
    lax.fori_loop(0, n_blocks, step, (zero,) * (4 * batch))
    y = y_intra + jnp.dot(h_s[...].astype(BF16), q_ref[0], preferred_element_type=F32)
    for c in range(SSM_GROUP):
        y_ref[0, c] = y[:, c * t:(c + 1) * t].reshape(batch, n_chunks, t)


def _ssm_scan(u_t, kf, kb, p_mat, q_mat, a1, a2, batch, length):
    t = SSM_CHUNK
    nk = length // t
    gt = SSM_GROUP * t
    u5 = u_t.reshape(SSM_GROUPS, SSM_GROUP, batch, nk, t)
    blk = (1, SSM_GROUP, batch, nk, t)
    lag_spec = pl.BlockSpec((1, SSM_GROUP, SSM_GROUP, t), lambda g: (g, 0, 0, 0))
    y5 = pl.pallas_call(
        functools.partial(_ssm_kernel, batch=batch, n_chunks=nk),
        grid=(SSM_GROUPS,),
        in_specs=[pl.BlockSpec(blk, lambda g: (g, 0, 0, 0, 0)), lag_spec, lag_spec,
                  pl.BlockSpec((1, gt, 4 * SSM_STATE), lambda g: (g, 0, 0)),
                  pl.BlockSpec((1, 4 * SSM_STATE, gt), lambda g: (g, 0, 0)),
                  pl.BlockSpec((1, 1, 4 * SSM_STATE), lambda g: (g, 0, 0)),
                  pl.BlockSpec((1, 1, 4 * SSM_STATE), lambda g: (g, 0, 0))],
        out_specs=pl.BlockSpec(blk, lambda g: (g, 0, 0, 0, 0)),
        out_shape=jax.ShapeDtypeStruct(u5.shape, F32),
        scratch_shapes=[pltpu.VMEM((gt, gt), BF16)] + [pltpu.VMEM((batch * nk, 4 * SSM_STATE), F32)] * 3,
        compiler_params=_cparams(("parallel",)),
        name="ssm_scan",
    )(u5, kf, kb, p_mat, q_mat, a1, a2)
    return y5.reshape(SSM_WIDTH, batch * length)


def _ssm_tables(a_re, a_im, log_step, b_re, b_im, c_re, c_im):
    t = SSM_CHUNK
    hi = lax.Precision.HIGHEST
    lam = lax.complex(a_re.astype(F32), a_im.astype(F32))
    step = jnp.exp(log_step.astype(F32))[..., None]
    ls = lam * step
    a_bar = jnp.exp(ls)
    b_bar = ((a_bar - 1.0) / lam)[..., None] * lax.complex(b_re.astype(F32), b_im.astype(F32))
    c = lax.complex(c_re.astype(F32), c_im.astype(F32))
    tau = jnp.arange(t + 1, dtype=F32)
    pw = jnp.exp(ls[..., None] * tau)
    kern = jnp.einsum('dgcp,dgpt,dgpe->dgtce', c, pw[..., :t], b_bar, precision=hi).real
    kf = kern[0].at[:, 0].add(kern[1][:, 0]).transpose(0, 3, 2, 1)
    kb = jnp.roll(kern[1][:, ::-1], 1, axis=1).transpose(0, 3, 2, 1)
    pw_f, pw_b = pw[0], pw[1]
    pf = pw_f[:, :, t - 1 - jnp.arange(t)][..., None] * b_bar[0][:, :, None, :]
    pb = pw_b[:, :, :t][..., None] * b_bar[1][:, :, None, :]
    to_rows = lambda z: z.transpose(0, 3, 2, 1).reshape(SSM_GROUPS, SSM_GROUP * t, SSM_STATE)
    p_mat = jnp.concatenate([to_rows(pf.real), to_rows(pf.imag), to_rows(pb.real), to_rows(pb.imag)], axis=-1)
    qf = c[0].transpose(0, 2, 1)[..., None] * pw_f[:, :, 1:][:, :, None, :]
    qb = c[1].transpose(0, 2, 1)[..., None] * pw_b[:, :, t - jnp.arange(t)][:, :, None, :]
    to_cols = lambda z: z.reshape(SSM_GROUPS, SSM_STATE, SSM_GROUP * t)
    q_mat = jnp.concatenate([to_cols(qf.real), -to_cols(qf.imag), to_cols(qb.real), -to_cols(qb.imag)], axis=1)
    at = pw[..., t]
    a1 = jnp.concatenate([at[0].real, at[0].real, at[1].real, at[1].real], axis=-1)[:, None, :]
    a2 = jnp.concatenate([-at[0].imag, at[0].imag, -at[1].imag, at[1].imag], axis=-1)[:, None, :]
    return kf, kb, p_mat.astype(BF16), q_mat.astype(BF16), a1, a2


def _dft_tables(length):
    j = DFT_INNER
    n = 2 * length
    rn = n // j
    rh = rn // 2
    odd = 2 * jnp.arange(rh, dtype=jnp.int32) + 1
    r = jnp.arange(rn, dtype=jnp.int32)
    ang1 = (math.pi / rn) * ((odd[:, None] * r[None, :]) % (2 * rn)).astype(F32)
    c1, s1 = jnp.cos(ang1), jnp.sin(ang1)
    f1_full = jnp.concatenate([c1, -s1], axis=0)
    f1_top = f1_full[:, :rh]
    f1_inv = jnp.concatenate([c1[:, :rh].T, -s1[:, :rh].T], axis=1)
    jj = jnp.arange(j, dtype=jnp.int32)
    angt = (math.pi / n) * (odd[:, None] * jj[None, :]).astype(F32)
    tw_re, tw_im = jnp.cos(angt), -jnp.sin(angt)
    ang2 = (2.0 * math.pi / j) * ((jj[:, None] * jj[None, :]) % j).astype(F32)
    f2_cat = jnp.concatenate([jnp.cos(ang2), -jnp.sin(ang2)], axis=1)
    return dict(f1_full=f1_full.astype(BF16), f1_top=f1_top.astype(BF16), f1_inv=f1_inv.astype(BF16),
                tw_re=tw_re, tw_im=tw_im, f2_cat=f2_cat.astype(BF16), rn=rn, rh=rh)


def _dft_forward(a, tw_re, tw_im, f2_cat, rk):
    j = DFT_INNER
    a_re, a_im = a[:rk], a[rk:]
    ap = jnp.concatenate([a_re * tw_re - a_im * tw_im, a_re * tw_im + a_im * tw_re], axis=0).astype(BF16)
    m = jnp.dot(ap, f2_cat, preferred_element_type=F32)
    return m[:rk, :j] - m[rk:, j:], m[:rk, j:] + m[rk:, :j]


def _filter_hidden_kernel(bands_ref, w1t_ref, w1c_ref, w1s_ref, b1_ref, w2t_ref, b2_ref, hid_ref, *, length, tl):
    hi = lax.Precision.HIGHEST
    base = pl.program_id(0) * tl
    idx = (lax.broadcasted_iota(jnp.int32, (1, tl), 1) + base).astype(F32)
    for d in range(2):
        pos = idx if d == 0 else float(length) - idx
        tt = pos / float(length)
        wpos = (2.0 * math.pi / length) * pos
        arg = bands_ref[...] * wpos
        h1 = (w1t_ref[...] * tt
              + jnp.dot(w1c_ref[...], jnp.cos(arg), preferred_element_type=F32, precision=hi)
              - jnp.dot(w1s_ref[...], jnp.sin(arg), preferred_element_type=F32, precision=hi))
        h1 = jnp.sin(h1 + b1_ref[...])
        h2 = jnp.dot(w2t_ref[...], h1, preferred_element_type=F32, precision=hi)
        hid_ref[d] = jnp.sin(h2 + b2_ref[...])


def _filter_taps_kernel(hid_ref, w3t_ref, decay_ref, filt_ref, *, length, tl):
    hi = lax.Precision.HIGHEST
    d = pl.program_id(0) % 2
    base = pl.program_id(1) * tl
    idx = (lax.broadcasted_iota(jnp.int32, (1, tl), 1) + base).astype(F32)
    pos = jnp.where(d == 0, idx, float(length) - idx)
    tt = pos / float(length)
    f = jnp.dot(w3t_ref[...], hid_ref[0], preferred_element_type=F32, precision=hi)
    f = f * jnp.exp(-tt * decay_ref[...])
    f = jnp.where(jnp.logical_and(d == 1, idx == 0.0), 0.0, f)
    filt_ref[...] = f.astype(filt_ref.dtype)


def _filter_dft_kernel(filt_ref, f1_ref, twre_ref, twim_ref, f2_ref, kf_ref, *, rk, cb):
    for c in range(cb):
        r = jnp.concatenate([filt_ref[0, 0, c], -filt_ref[0, 1, c]], axis=0)
        a = jnp.dot(f1_ref[...], r, preferred_element_type=F32)
        x_re, x_im = _dft_forward(a, twre_ref[...], twim_ref[...], f2_ref[...], rk)
        kf_ref[0, c, 0] = x_re.astype(kf_ref.dtype)
        kf_ref[0, c, 1] = x_im.astype(kf_ref.dtype)


def _hyena_filters(length, tabs, w1, b1, w2, b2, w3, log_decay):
    tl = min(FILTER_LANE_TILE, length)
    nl = length // tl
    rn, rh, j = tabs["rn"], tabs["rh"], DFT_INNER
    bands = jnp.linspace(1e-4, FILTER_BANDS - 1, FILTER_BANDS, dtype=F32)[:, None]
    w1f = w1.astype(F32)
    const = lambda shape: pl.BlockSpec(shape, lambda *_: (0,) * len(shape))
    hid = pl.pallas_call(
        functools.partial(_filter_hidden_kernel, length=length, tl=tl),
        grid=(nl,),
        in_specs=[const((FILTER_BANDS, 1)), const((FILTER_HIDDEN, 1)), const((FILTER_HIDDEN, FILTER_BANDS)),
                  const((FILTER_HIDDEN, FILTER_BANDS)), const((FILTER_HIDDEN, 1)),
                  const((FILTER_HIDDEN, FILTER_HIDDEN)), const((FILTER_HIDDEN, 1))],
        out_specs=pl.BlockSpec((2, FILTER_HIDDEN, tl), lambda i: (0, 0, i)),
        out_shape=jax.ShapeDtypeStruct((2, FILTER_HIDDEN, length), F32),
        compiler_params=_cparams(("parallel",)),
        name="hyena_filter_hidden",
    )(bands, w1f[0:1].T, w1f[1:1 + FILTER_BANDS].T, w1f[1 + FILTER_BANDS:].T, b1.astype(F32)[:, None],
      w2.astype(F32).T, b2.astype(F32)[:, None])
    n_od = HYENA_ORDER * 2
    w3t = w3.astype(F32).T.reshape(n_od, HYENA_WIDTH, FILTER_HIDDEN)
    decay = jnp.exp(log_decay.astype(F32)).reshape(n_od, HYENA_WIDTH, 1)
    filt = pl.pallas_call(
        functools.partial(_filter_taps_kernel, length=length, tl=tl),
        grid=(n_od, nl),
        in_specs=[pl.BlockSpec((1, FILTER_HIDDEN, tl), lambda od, i: (od % 2, 0, i)),
                  pl.BlockSpec((None, HYENA_WIDTH, FILTER_HIDDEN), lambda od, i: (od, 0, 0)),
                  pl.BlockSpec((None, HYENA_WIDTH, 1), lambda od, i: (od, 0, 0))],
        out_specs=pl.BlockSpec((None, HYENA_WIDTH, tl), lambda od, i: (od, 0, i)),
        out_shape=jax.ShapeDtypeStruct((n_od, HYENA_WIDTH, length), BF16),
        compiler_params=_cparams(("parallel", "parallel")),
        name="hyena_filter_taps",
    )(hid, w3t, decay)
    filt6 = filt.reshape(HYENA_ORDER, 2, HYENA_WIDTH, rh, j)
    cb = HYENA_CH_BLOCK
    return pl.pallas_call(
        functools.partial(_filter_dft_kernel, rk=rh, cb=cb),
        grid=(HYENA_ORDER, HYENA_WIDTH // cb),
        in_specs=[pl.BlockSpec((1, 2, cb, rh, j), lambda o, c: (o, 0, c, 0, 0)),
                  const((rn, rn)), const((rh, j)), const((rh, j)), const((j, 2 * j))],
        out_specs=pl.BlockSpec((1, cb, 2, rh, j), lambda o, c: (o, c, 0, 0, 0)),
        out_shape=jax.ShapeDtypeStruct((HYENA_ORDER, HYENA_WIDTH, 2, rh, j), BF16),
        compiler_params=_cparams(("parallel", "parallel")),
        name="hyena_filter_dft",
    )(filt6, tabs["f1_full"], tabs["tw_re"], tabs["tw_im"], tabs["f2_cat"])


def _hyena_kernel(scw_ref, scb_ref, hb_ref, x_ref, hg_ref, kf_ref, f1_ref, f1i_ref, twre_ref, twim_ref, f2_ref,
                  o_ref, *, batch, rn, rh, cb, cs):
    j = DFT_INNER
    n_inv = 2.0 / (rn * j)
    rk = rh
    lane = lax.broadcasted_iota(jnp.int32, (rh, j), 1)
    row = lax.broadcasted_iota(jnp.int32, (rh, j), 0)
    first = jnp.logical_and(lane == 0, row == 0)
    last = jnp.logical_and(lane == j - 1, row == rh - 1)
    tw_re, tw_im = twre_ref[...], twim_ref[...]

    def prev_t(x):
        p = pltpu.roll(x, 1, 1)
        return jnp.where(first, 0.0, jnp.where(lane == 0, pltpu.roll(p, 1, 0), p))

    def next_t(x):
        p = pltpu.roll(x, j - 1, 1)
        return jnp.where(last, 0.0, jnp.where(lane == j - 1, pltpu.roll(p, rh - 1, 0), p))

    def split(m, i):
        return m[2 * rk * i:2 * rk * i + rk], m[2 * rk * i + rk:2 * rk * (i + 1)]

    def long_conv(xs, ks):
        rows = []
        for x in xs:
            a = jnp.dot(f1_ref[...], x.astype(BF16), preferred_element_type=F32)
            a_re, a_im = a[:rk], a[rk:]
            rows += [a_re * tw_re - a_im * tw_im, a_re * tw_im + a_im * tw_re]
        m = jnp.dot(jnp.concatenate(rows, axis=0).astype(BF16), f2_ref[...], preferred_element_type=F32)
        rows = []
        for i, (k_re, k_im) in enumerate(ks):
            m_re, m_im = split(m, i)
            x_re, x_im = m_re[:, :j] - m_im[:, j:], m_re[:, j:] + m_im[:, :j]
            rows += [x_re * k_re - x_im * k_im, x_re * k_im + x_im * k_re]
        m = jnp.dot(jnp.concatenate(rows, axis=0).astype(BF16), f2_ref[...], preferred_element_type=F32)
        outs = []
        for i in range(len(xs)):
            m_re, m_im = split(m, i)
            b_re, b_im = m_re[:, :j] + m_im[:, j:], m_im[:, :j] - m_re[:, j:]
            bp = jnp.concatenate([b_re * tw_re + b_im * tw_im, b_im * tw_re - b_re * tw_im], axis=0).astype(BF16)
            outs.append(jnp.dot(f1i_ref[...], bp, preferred_element_type=F32) * n_inv)
        return outs

    cbase = pl.program_id(0) * cb
    for c0 in range(0, cb, cs):
        chans = [(ci, b) for ci in range(c0, c0 + cs) for b in range(batch)]
        segs = []
        for sgm in range(3):
            seg = []
            for ci, b in chans:
                chs = sgm * HYENA_WIDTH + cbase + ci
                x = x_ref[sgm, ci, b]
                seg.append(prev_t(x) * scw_ref[chs] + x * scw_ref[3 * HYENA_WIDTH + chs]
                           + next_t(x) * scw_ref[6 * HYENA_WIDTH + chs] + scb_ref[chs])
            segs.append(seg)
        z = segs[0]
        for o in range(HYENA_ORDER):
            conv = long_conv(z, [(kf_ref[o, ci, 0].astype(F32), kf_ref[o, ci, 1].astype(F32)) for ci, _ in chans])
            z = [segs[o + 1][i] * (conv[i] + z[i] * hb_ref[o * HYENA_WIDTH + cbase + ci])
                 for i, (ci, _) in enumerate(chans)]
        for i, (ci, b) in enumerate(chans):
            o_ref[ci, b] = z[i] * hg_ref[ci, b]


def _hyena(hi_t, hg_t, kf, tabs, short_w, short_b, hy_bias, batch, length):
    rn, rh, j = tabs["rn"], tabs["rh"], DFT_INNER
    cb = HYENA_CH_BLOCK
    x5 = hi_t.reshape(3, HYENA_WIDTH, batch, rh, j)
    g4 = hg_t.reshape(HYENA_WIDTH, batch, rh, j)
    smem = pl.BlockSpec(memory_space=pltpu.SMEM)
    const = lambda shape: pl.BlockSpec(shape, lambda c: (0,) * len(shape))
    out = pl.pallas_call(
        functools.partial(_hyena_kernel, batch=batch, rn=rn, rh=rh, cb=cb, cs=HYENA_CH_SUB),
        grid=(HYENA_WIDTH // cb,),
        in_specs=[smem, smem, smem,
                  pl.BlockSpec((3, cb, batch, rh, j), lambda c: (0, c, 0, 0, 0)),
                  pl.BlockSpec((cb, batch, rh, j), lambda c: (c, 0, 0, 0)),
                  pl.BlockSpec((HYENA_ORDER, cb, 2, rh, j), lambda c: (0, c, 0, 0, 0)),
                  const((rn, rh)), const((rh, rn)), const((rh, j)), const((rh, j)), const((j, 2 * j))],
        out_specs=pl.BlockSpec((cb, batch, rh, j), lambda c: (c, 0, 0, 0)),
        out_shape=jax.ShapeDtypeStruct(g4.shape, F32),
        compiler_params=_cparams(("parallel",)),
        name="hyena_conv",
    )(short_w.astype(F32).reshape(-1), short_b.astype(F32), hy_bias.astype(F32).reshape(-1),
      x5, g4, kf, tabs["f1_top"], tabs["f1_inv"], tabs["tw_re"], tabs["tw_im"], tabs["f2_cat"])
    return out.reshape(HYENA_WIDTH, batch * length)


def _rms_rows(x, g):
    return x * lax.rsqrt(jnp.mean(x * x, axis=-1, keepdims=True) + EPS) * g


def _rms_cols(x, g):
    return x * lax.rsqrt(jnp.mean(x * x, axis=0, keepdims=True) + EPS) * g


def _post_kernel(ao_ref, ga_ref, mq_ref, mk_ref, mv_ref, y_ref, u_ref, sg_ref, hy_ref, x_ref,
                 d_ref, wglu_ref, g_attn_ref, g_ssm_ref, g_hy_ref, g_mem_ref, post_g_ref, wo_ref, out_ref):
    attn_n = _rms_cols(ao_ref[...] * ga_ref[...], g_attn_ref[...])
    s = jnp.dot(mq_ref[...], mk_ref[0], preferred_element_type=F32)
    ps = []
    for h in range(MEM_HEADS):
        sh = s[:, MEM_TOKENS * h:MEM_TOKENS * (h + 1)]
        e = jnp.exp(sh - jnp.max(sh, axis=-1, keepdims=True))
        ps.append(e * (1.0 / jnp.sum(e, axis=-1, keepdims=True)))
    p = jnp.concatenate(ps, axis=1).astype(BF16)
    cross_n = _rms_rows(jnp.dot(p, mv_ref[0], preferred_element_type=F32), g_mem_ref[...])
    y = y_ref[...] + d_ref[...] * u_ref[...]
    g = y * (0.5 * (1.0 + jnp.tanh(math.sqrt(2.0 / math.pi) * (y + 0.044715 * (y * y * y)))))
    gz = jnp.dot(wglu_ref[...], g.astype(BF16), preferred_element_type=F32)
    ssm_n = _rms_cols(g * _sigmoid(gz) * sg_ref[...], g_ssm_ref[...])
    hy_n = _rms_cols(hy_ref[...], g_hy_ref[...])
    o1, o2, o3 = ATTN_WIDTH, ATTN_WIDTH + SSM_WIDTH, ATTN_WIDTH + SSM_WIDTH + HYENA_WIDTH
    mixed = (jnp.dot(attn_n.T.astype(BF16), wo_ref[0:o1], preferred_element_type=F32)
             + jnp.dot(ssm_n.T.astype(BF16), wo_ref[o1:o2], preferred_element_type=F32)
             + jnp.dot(hy_n.T.astype(BF16), wo_ref[o2:o3], preferred_element_type=F32)
             + jnp.dot(cross_n.astype(BF16), wo_ref[o3:], preferred_element_type=F32))
    out_ref[...] = x_ref[...] + _rms_rows(mixed, post_g_ref[...])


def _post(ao, ga, mq, mk_bd, mv_bd, y_t, u_t, sg_t, hy_t, x2, d, wglu_t, g_attn, g_ssm, g_hy, g_mem, post_g, wo,
          length):
    n = x2.shape[0]
    tm = TOKEN_TILE
    tiles_per_seq = length // tm
    tok_spec = lambda w: pl.BlockSpec((tm, w), lambda i: (i, 0))
    ch_spec = lambda w: pl.BlockSpec((w, tm), lambda i: (0, i))
    const = lambda shape: pl.BlockSpec(shape, lambda i: (0,) * len(shape))
    hm = MEM_HEADS * MEM_TOKENS
    return pl.pallas_call(
        _post_kernel,
        grid=(n // tm,),
        in_specs=[ch_spec(ATTN_WIDTH), ch_spec(ATTN_WIDTH), tok_spec(MEM_WIDTH),
                  pl.BlockSpec((1, MEM_WIDTH, hm), lambda i: (i // tiles_per_seq, 0, 0)),
                  pl.BlockSpec((1, hm, MEM_WIDTH), lambda i: (i // tiles_per_seq, 0, 0)),
                  ch_spec(SSM_WIDTH), ch_spec(SSM_WIDTH), ch_spec(SSM_WIDTH), ch_spec(HYENA_WIDTH),
                  tok_spec(D_MODEL),
                  const((SSM_WIDTH, 1)), const((SSM_WIDTH, SSM_WIDTH)),
                  const((ATTN_WIDTH, 1)), const((SSM_WIDTH, 1)), const((HYENA_WIDTH, 1)), const((1, MEM_WIDTH)),
                  const((1, D_MODEL)), const((MIX_WIDTH, D_MODEL))],
        out_specs=tok_spec(D_MODEL),
        out_shape=jax.ShapeDtypeStruct((n, D_MODEL), F32),
        compiler_params=_cparams(("parallel",)),
        name="post",
    )(ao, ga, mq, mk_bd, mv_bd, y_t, u_t, sg_t, hy_t, x2, d, wglu_t, g_attn, g_ssm, g_hy, g_mem, post_g, wo)


def _rope_tables(length):
    rows = length // GRID_W
    row = jnp.broadcast_to(jnp.arange(rows, dtype=F32)[:, None], (rows, GRID_W)).reshape(length)
    col = jnp.broadcast_to(jnp.arange(GRID_W, dtype=F32)[None, :], (rows, GRID_W)).reshape(length)
    inv_freq = ROPE_THETA ** (-jnp.arange(ROPE_FREQS, dtype=F32) / ROPE_FREQS)
    ang = jnp.stack([row[:, None] * inv_freq, col[:, None] * inv_freq], axis=1)
    ang = jnp.broadcast_to(ang[:, :, None, :], (length, 2, 2, ROPE_FREQS)).reshape(length, HEAD_DIM)
    cos1, sin1 = jnp.cos(ang), jnp.sin(ang)
    low1 = (jnp.arange(HEAD_DIM) % (2 * ROPE_FREQS)) < ROPE_FREQS
    cos_t, sin_t = cos1.T, jnp.where(low1, -sin1, sin1).T
    cos = jnp.concatenate([cos1, cos1], axis=1)
    sin = jnp.concatenate([sin1, sin1], axis=1)
    low = jnp.concatenate([low1, low1])
    return cos, jnp.where(low, -sin, 0.0), jnp.where(low, 0.0, sin), cos_t, sin_t


def _block_diag_heads(mk, mv, batch):
    mk4 = mk.reshape(batch, MEM_TOKENS, MEM_HEADS, HEAD_DIM)
    mv4 = mv.reshape(batch, MEM_TOKENS, MEM_HEADS, HEAD_DIM)
    eye = jnp.eye(MEM_HEADS, dtype=mk.dtype)
    k_bd = jnp.einsum('bmhd,hg->bhdgm', mk4, eye).reshape(batch, MEM_WIDTH, MEM_HEADS * MEM_TOKENS)
    v_bd = jnp.einsum('bmhd,hg->bhmgd', mv4, eye).reshape(batch, MEM_HEADS * MEM_TOKENS, MEM_WIDTH)
    return k_bd.astype(BF16), v_bd.astype(BF16)


def _layer_weights(layer, p):
    w_in = p["w_in"][layer]
    a, kv = ATTN_WIDTH, KV_WIDTH
    k0, v0, g0, mq0 = a, a + kv, a + 2 * kv, w_in.shape[1] - MEM_WIDTH
    w_tok = jnp.concatenate([w_in[:, k0:v0], w_in[:, mq0:]], axis=1).astype(BF16)
    w_ch_t = jnp.concatenate([w_in[:, :k0], w_in[:, v0:mq0]], axis=1).T.astype(BF16)
    bg = p["branch_norm"][layer].astype(F32)
    o1, o2, o3 = a, a + SSM_WIDTH, a + SSM_WIDTH + HYENA_WIDTH
    head_id = jnp.arange(kv) // HEAD_DIM
    return dict(
        w_tok=w_tok, w_ch_t=w_ch_t,
        pre_g=p["pre_norm"][layer].astype(F32)[None, :], post_g=p["post_norm"][layer].astype(F32)[None, :],
        qg=p["q_norm"][layer].astype(F32)[:, None],
        kg=jnp.tile(p["k_norm"][layer].astype(F32), ATTN_KV_HEADS)[None, :],
        ones=(head_id[:, None] == head_id[None, :]).astype(BF16),
        mem_g=p["mem_norm"][layer].astype(F32)[None, :], w_mem_kv=p["w_mem_kv"][layer].astype(BF16),
        ssm=_ssm_tables(p["ssm_a_re"][layer], p["ssm_a_im"][layer], p["ssm_log_step"][layer], p["ssm_b_re"][layer],
                        p["ssm_b_im"][layer], p["ssm_c_re"][layer], p["ssm_c_im"][layer]),
        d=p["ssm_d"][layer].astype(F32)[:, None], wglu_t=p["ssm_w_glu"][layer].T.astype(BF16),
        g_attn=bg[:o1, None], g_ssm=bg[o1:o2, None], g_hy=bg[o2:o3, None], g_mem=bg[None, o3:],
        wo=p["w_out"][layer].astype(BF16),
    )


def _mixer_layer(x2, mem2, lw, kf, tabs, rope, p, layer, batch, length):
    q_t, k, v_t, ga, mq, su_t, sg_t, hi_t, hg_t = _inproj(x2, lw["pre_g"], lw["w_tok"], lw["w_ch_t"], rope,
                                                          lw["qg"], lw["kg"], lw["ones"], length)
    ao = _flash_attention(q_t, k, v_t, batch, length)
    mem_kv = _memkv(mem2, lw["mem_g"], lw["w_mem_kv"])
    mk_bd, mv_bd = _block_diag_heads(mem_kv[:, :MEM_WIDTH], mem_kv[:, MEM_WIDTH:], batch)
    y_t = _ssm_scan(su_t, *lw["ssm"], batch, length)
    hy_t = _hyena(hi_t, hg_t, kf, tabs, p["hyena_short_w"][layer], p["hyena_short_b"][layer],
                  p["hyena_bias"][layer], batch, length)
    return _post(ao, ga, mq, mk_bd, mv_bd, y_t, su_t, sg_t, hy_t, x2, lw["d"], lw["wglu_t"],
                 lw["g_attn"], lw["g_ssm"], lw["g_hy"], lw["g_mem"], lw["post_g"], lw["wo"], length)


def _run_group(x, mem, weights, p):
    batch, length, _ = x.shape
    rope = _rope_tables(length)
    tabs = _dft_tables(length)
    x2 = x.reshape(batch * length, D_MODEL)
    mem2 = mem.reshape(batch * MEM_TOKENS, D_MODEL)
    for layer in range(DEPTH):
        kf = _hyena_filters(length, tabs, p["hyena_ffn_w1"][layer], p["hyena_ffn_b1"][layer],
                            p["hyena_ffn_w2"][layer], p["hyena_ffn_b2"][layer], p["hyena_ffn_w3"][layer],
                            p["hyena_log_decay"][layer])
        x2 = _mixer_layer(x2, mem2, weights[layer], kf, tabs, rope, p, layer, batch, length)
    return x2.reshape(batch, length, D_MODEL)


def kernel(x_prompt, x_sample, mem_prompt, mem_sample, pre_norm, post_norm, w_in, q_norm, k_norm, mem_norm, w_mem_kv, ssm_a_re, ssm_a_im, ssm_log_step, ssm_b_re, ssm_b_im, ssm_c_re, ssm_c_im, ssm_d, ssm_w_glu, hyena_short_w, hyena_short_b, hyena_ffn_w1, hyena_ffn_b1, hyena_ffn_w2, hyena_ffn_b2, hyena_ffn_w3, hyena_log_decay, hyena_bias, branch_norm, w_out):
    p = dict(pre_norm=pre_norm, post_norm=post_norm, w_in=w_in, q_norm=q_norm, k_norm=k_norm, mem_norm=mem_norm,
             w_mem_kv=w_mem_kv, ssm_a_re=ssm_a_re, ssm_a_im=ssm_a_im, ssm_log_step=ssm_log_step, ssm_b_re=ssm_b_re,
             ssm_b_im=ssm_b_im, ssm_c_re=ssm_c_re, ssm_c_im=ssm_c_im, ssm_d=ssm_d, ssm_w_glu=ssm_w_glu,
             hyena_short_w=hyena_short_w, hyena_short_b=hyena_short_b, hyena_ffn_w1=hyena_ffn_w1,
             hyena_ffn_b1=hyena_ffn_b1, hyena_ffn_w2=hyena_ffn_w2, hyena_ffn_b2=hyena_ffn_b2,
             hyena_ffn_w3=hyena_ffn_w3, hyena_log_decay=hyena_log_decay, hyena_bias=hyena_bias,
             branch_norm=branch_norm, w_out=w_out)
    weights = [_layer_weights(layer, p) for layer in range(DEPTH)]
    return (_run_group(x_prompt, mem_prompt, weights, p), _run_group(x_sample, mem_sample, weights, p))
```

```python
import functools
import math

import jax
import jax.numpy as jnp
import numpy as np
from jax import lax
from jax.experimental import pallas as pl
from jax.experimental.pallas import tpu as pltpu

F32 = jnp.float32
BF16 = jnp.bfloat16

D_MODEL = 1024
DEPTH = 2
GRID_W = 64
HEAD_DIM = 64
ATTN_HEADS = 8
ATTN_KV_HEADS = 2
ATTN_GROUP = ATTN_HEADS // ATTN_KV_HEADS
ATTN_WIDTH = ATTN_HEADS * HEAD_DIM
KV_WIDTH = ATTN_KV_HEADS * HEAD_DIM
ROPE_THETA = 10000.0
ROPE_FREQS = HEAD_DIM // 4
SSM_GROUP = 16
SSM_GROUPS = 24
SSM_WIDTH = SSM_GROUP * SSM_GROUPS
SSM_STATE = 64
HYENA_WIDTH = 384
HYENA_ORDER = 2
FILTER_BANDS = 16
FILTER_HIDDEN = 64
MEM_TOKENS = 256
MEM_HEADS = 4
MEM_WIDTH = MEM_HEADS * HEAD_DIM
MIX_WIDTH = ATTN_WIDTH + SSM_WIDTH + HYENA_WIDTH + MEM_WIDTH
EPS = 1e-6

TOK_WIDTH = KV_WIDTH + MEM_WIDTH
CH_WIDTH = 2 * ATTN_WIDTH + KV_WIDTH + 2 * SSM_WIDTH + (HYENA_ORDER + 2) * HYENA_WIDTH
V_ROWS = HEAD_DIM + 16
Q_SCALE = HEAD_DIM ** -0.5 * math.log2(math.e)
SOFTMAX_STATIC_BOUND = 60.0

LANES = 128
SUBLANES = 8
VMEM_LIMIT = 56 * 1024 * 1024
TOKEN_TILE = 512
ATTN_Q_TILE = 256
ATTN_K_TILE = 512
SSM_CHUNK = LANES
DFT_INNER = 256
HYENA_CH_BLOCK = 8
HYENA_CH_SUB = 8
FILTER_LANE_TILE = 2048


def _cparams(sem):
    return pltpu.CompilerParams(dimension_semantics=sem, vmem_limit_bytes=VMEM_LIMIT)


def _silu(x):
    return x * (1.0 / (1.0 + jnp.exp(-x)))


def _sigmoid(x):
    return 1.0 / (1.0 + jnp.exp(-x))


def _nt_dot(a, b):
    return lax.dot_general(a, b, (((1,), (1,)), ((), ())), preferred_element_type=F32)


def _rope_128(xn, cos, s_lo, s_hi):
    outs = []
    for c in range(xn.shape[1] // LANES):
        xc = xn[:, LANES * c:LANES * (c + 1)]
        outs.append(xc * cos + pltpu.roll(xc, LANES - ROPE_FREQS, 1) * s_lo + pltpu.roll(xc, ROPE_FREQS, 1) * s_hi)
    return outs[0] if len(outs) == 1 else jnp.concatenate(outs, axis=1)


def _inproj_kernel(x_ref, pre_g_ref, wtok_ref, wch_ref, cos_ref, slo_ref, shi_ref, cost_ref, sint_ref,
                   qg_ref, kg_ref, ones_ref,
                   qt_ref, k_ref, vt_ref, ga_ref, mq_ref, su_ref, sg_ref, hi_ref, hg_ref):
    tm = x_ref.shape[0]
    x = x_ref[...]
    h = x * lax.rsqrt(jnp.mean(x * x, axis=-1, keepdims=True) + EPS) * pre_g_ref[...]
    hb = h.astype(BF16)
    tok = jnp.dot(hb, wtok_ref[...], preferred_element_type=F32)
    k = tok[:, 0:KV_WIDTH]
    mq = tok[:, KV_WIDTH:]
    k_ms = jnp.dot((k * k).astype(BF16), ones_ref[...], preferred_element_type=F32) * (1.0 / HEAD_DIM)
    kn = k * lax.rsqrt(k_ms + EPS) * kg_ref[...]
    kr = _rope_128(kn, cos_ref[...], slo_ref[...], shi_ref[...])
    for j in range(ATTN_KV_HEADS):
        k_ref[j] = kr[:, HEAD_DIM * j:HEAD_DIM * (j + 1)].astype(BF16)
    mq_ref[...] = (mq * (HEAD_DIM ** -0.5)).astype(BF16)
    ch = _nt_dot(wch_ref[...], hb)
    q3 = ch[0:ATTN_WIDTH].reshape(ATTN_HEADS, HEAD_DIM, tm)
    qn = q3 * lax.rsqrt(jnp.mean(q3 * q3, axis=1, keepdims=True) + EPS) * qg_ref[...][None]
    f = ROPE_FREQS
    rot = jnp.concatenate([qn[:, f:2 * f], qn[:, 0:f], qn[:, 3 * f:4 * f], qn[:, 2 * f:3 * f]], axis=1)
    qr = (qn * cost_ref[...][None] + rot * sint_ref[...][None]) * Q_SCALE
    qt_ref[...] = qr.reshape(ATTN_WIDTH, tm).astype(BF16)
    o = ATTN_WIDTH
    ones_row = (lax.broadcasted_iota(jnp.int32, (V_ROWS - HEAD_DIM, tm), 0) == 0).astype(BF16)
    for j in range(ATTN_KV_HEADS):
        vt_ref[j, 0:HEAD_DIM] = ch[o + HEAD_DIM * j:o + HEAD_DIM * (j + 1)].astype(BF16)
        vt_ref[j, HEAD_DIM:V_ROWS] = ones_row
    o += KV_WIDTH
    ga_ref[...] = _silu(ch[o:o + ATTN_WIDTH])
    o += ATTN_WIDTH
    su_ref[...] = ch[o:o + SSM_WIDTH]
    sg_ref[...] = _silu(ch[o + SSM_WIDTH:o + 2 * SSM_WIDTH])
    o += 2 * SSM_WIDTH
    hi_ref[...] = ch[o:o + 3 * HYENA_WIDTH]
    hg_ref[...] = _silu(ch[o + 3 * HYENA_WIDTH:])


def _inproj(x2, pre_g, w_tok, w_ch_t, rope, qg, kg, ones, length):
    cos, s_lo, s_hi, cos_t, sin_t = rope
    n = x2.shape[0]
    tm = TOKEN_TILE
    nt = n // tm
    tiles_per_seq = length // tm
    tok_spec = lambda w: pl.BlockSpec((tm, w), lambda i: (i, 0))
    ch_spec = lambda w: pl.BlockSpec((w, tm), lambda i: (0, i))
    const = lambda shape: pl.BlockSpec(shape, lambda i: (0,) * len(shape))
    pos_spec = pl.BlockSpec((tm, LANES), lambda i: (i % tiles_per_seq, 0))
    pos_t_spec = pl.BlockSpec((HEAD_DIM, tm), lambda i: (0, i % tiles_per_seq))
    return pl.pallas_call(
        _inproj_kernel,
        grid=(nt,),
        in_specs=[tok_spec(D_MODEL), const((1, D_MODEL)), const((D_MODEL, TOK_WIDTH)), const((CH_WIDTH, D_MODEL)),
                  pos_spec, pos_spec, pos_spec, pos_t_spec, pos_t_spec,
                  const((HEAD_DIM, 1)), const((1, KV_WIDTH)), const((KV_WIDTH, KV_WIDTH))],
        out_specs=[ch_spec(ATTN_WIDTH),
                   pl.BlockSpec((ATTN_KV_HEADS, tm, HEAD_DIM), lambda i: (0, i, 0)),
                   pl.BlockSpec((ATTN_KV_HEADS, V_ROWS, tm), lambda i: (0, 0, i)),
                   ch_spec(ATTN_WIDTH), tok_spec(MEM_WIDTH),
                   ch_spec(SSM_WIDTH), ch_spec(SSM_WIDTH), ch_spec(3 * HYENA_WIDTH), ch_spec(HYENA_WIDTH)],
        out_shape=[jax.ShapeDtypeStruct((ATTN_WIDTH, n), BF16),
                   jax.ShapeDtypeStruct((ATTN_KV_HEADS, n, HEAD_DIM), BF16),
                   jax.ShapeDtypeStruct((ATTN_KV_HEADS, V_ROWS, n), BF16),
                   jax.ShapeDtypeStruct((ATTN_WIDTH, n), F32),
                   jax.ShapeDtypeStruct((n, MEM_WIDTH), BF16),
                   jax.ShapeDtypeStruct((SSM_WIDTH, n), F32),
                   jax.ShapeDtypeStruct((SSM_WIDTH, n), F32),
                   jax.ShapeDtypeStruct((3 * HYENA_WIDTH, n), F32),
                   jax.ShapeDtypeStruct((HYENA_WIDTH, n), F32)],
        compiler_params=_cparams(("parallel",)),
        name="inproj",
    )(x2, pre_g, w_tok, w_ch_t, cos, s_lo, s_hi, cos_t, sin_t, qg, kg, ones)


def _flash_kernel(qt_ref, k_ref, vt_ref, o_ref, acc_s, s_s, kmax_s, *, tq, tk, n_kv):
    w = ATTN_GROUP * tq
    q4t = jnp.concatenate([qt_ref[HEAD_DIM * i:HEAD_DIM * (i + 1), :] for i in range(ATTN_GROUP)], axis=1)

    def chunk(ref_slice, c):
        return ref_slice(pl.ds(pl.multiple_of(c * tk, tk), tk))

    k_chunk = lambda c: chunk(lambda d: k_ref[0, d, :], c)
    v_chunk = lambda c: chunk(lambda d: vt_ref[0, :, d], c)

    @pl.when(pl.program_id(2) == 0)
    def _():
        def key_norm(c, mx):
            kc = k_chunk(c).astype(F32)
            return jnp.maximum(mx, jnp.max(jnp.sum(kc * kc, axis=1, keepdims=True), axis=0, keepdims=True))

        mx = lax.fori_loop(0, n_kv, key_norm, jnp.zeros((1, 1), F32))
        kmax_s[...] = jnp.broadcast_to(jnp.sqrt(mx), kmax_s.shape)

    qf = q4t.astype(F32)
    bound = jnp.sqrt(jnp.sum(qf * qf, axis=0, keepdims=True)) * kmax_s[:, 0:1] * (1.0 + 2.0 ** -10)
    bound_max = jnp.max(bound)

    def finish():
        o = acc_s[0:HEAD_DIM] * (1.0 / acc_s[HEAD_DIM:HEAD_DIM + 1])
        for i in range(ATTN_GROUP):
            o_ref[HEAD_DIM * i:HEAD_DIM * (i + 1), :] = o[:, i * tq:(i + 1) * tq]

    @pl.when(bound_max <= SOFTMAX_STATIC_BOUND)
    def _():
        acc_s[...] = jnp.zeros(acc_s.shape, F32)

        def body(c, carry):
            s = jnp.dot(k_chunk(c), q4t, preferred_element_type=F32)
            p = jnp.exp2(s - bound).astype(BF16)
            acc_s[...] += jnp.dot(v_chunk(c), p, preferred_element_type=F32)
            return carry

        lax.fori_loop(0, n_kv, body, 0, unroll=8)
        finish()

    @pl.when(jnp.logical_not(bound_max <= SOFTMAX_STATIC_BOUND))
    def _():
        acc_s[...] = jnp.zeros(acc_s.shape, F32)

        def scores(c, slot):
            s = jnp.dot(k_chunk(c), q4t, preferred_element_type=F32)
            s_s[slot] = s
            return jnp.max(s, axis=0, keepdims=True)

        def consume(c, slot, m, m_chunk):
            m_new = jnp.maximum(m, m_chunk)
            p = jnp.exp2(s_s[slot] - m_new).astype(BF16)
            pv = jnp.dot(v_chunk(c), p, preferred_element_type=F32)
            acc_s[...] = jnp.exp2(m - m_new) * acc_s[...] + pv
            return m_new

        def body(cc, carry):
            m, mc0 = carry
            c = 2 * cc
            mc1 = scores(c + 1, 1)
            m = consume(c, 0, m, mc0)
            mc0 = scores(c + 2, 0)
            return consume(c + 1, 1, m, mc1), mc0

        m, mc0 = lax.fori_loop(0, n_kv // 2 - 1, body, (jnp.full((1, w), -jnp.inf, F32), scores(0, 0)), unroll=2)
        mc1 = scores(n_kv - 1, 1)
        consume(n_kv - 1, 1, consume(n_kv - 2, 0, m, mc0), mc1)
        finish()


def _flash_attention(q_t, k, v_t, batch, length):
    n = q_t.shape[1]
    tq, tk = ATTN_Q_TILE, ATTN_K_TILE
    nq, nk = length // tq, length // tk
    gw = ATTN_GROUP * HEAD_DIM
    return pl.pallas_call(
        functools.partial(_flash_kernel, tq=tq, tk=tk, n_kv=nk),
        grid=(batch, ATTN_KV_HEADS, nq),
        in_specs=[pl.BlockSpec((gw, tq), lambda b, h, i: (h, b * nq + i)),
                  pl.BlockSpec((1, length, HEAD_DIM), lambda b, h, i: (h, b, 0)),
                  pl.BlockSpec((1, V_ROWS, length), lambda b, h, i: (h, 0, b))],
        out_specs=pl.BlockSpec((gw, tq), lambda b, h, i: (h, b * nq + i)),
        out_shape=jax.ShapeDtypeStruct((ATTN_WIDTH, n), F32),
        scratch_shapes=[pltpu.VMEM((V_ROWS, ATTN_GROUP * tq), F32),
                        pltpu.VMEM((2, tk, ATTN_GROUP * tq), F32),
                        pltpu.VMEM((1, LANES), F32)],
        compiler_params=_cparams(("parallel", "parallel", "arbitrary")),
        name="flash_attn",
    )(q_t, k, v_t)


def _memkv_kernel(mem_ref, g_ref, w_ref, kv_ref):
    m = mem_ref[...]
    mn = m * lax.rsqrt(jnp.mean(m * m, axis=-1, keepdims=True) + EPS) * g_ref[...]
    kv_ref[...] = jnp.dot(mn.astype(BF16), w_ref[...], preferred_element_type=F32)


def _memkv(mem2, mem_g, w_kv):
    rows = mem2.shape[0]
    return pl.pallas_call(
        _memkv_kernel,
        grid=(rows // MEM_TOKENS,),
        in_specs=[pl.BlockSpec((MEM_TOKENS, D_MODEL), lambda i: (i, 0)),
                  pl.BlockSpec((1, D_MODEL), lambda i: (0, 0)),
                  pl.BlockSpec((D_MODEL, 2 * MEM_WIDTH), lambda i: (0, 0))],
        out_specs=pl.BlockSpec((MEM_TOKENS, 2 * MEM_WIDTH), lambda i: (i, 0)),
        out_shape=jax.ShapeDtypeStruct((rows, 2 * MEM_WIDTH), F32),
        compiler_params=_cparams(("parallel",)),
        name="mem_kv",
    )(mem2, mem_g, w_kv)


def _ssm_kernel(u_ref, kf_ref, kb_ref, p_ref, q_ref, a1_ref, a2_ref, y_ref, g_s, s_s, x_s, h_s, *, batch, n_chunks):
    t = SSM_CHUNK
    rows = batch * n_chunks
    half = 2 * SSM_STATE
    causal = lax.broadcasted_iota(jnp.int32, (t, t), 1) >= lax.broadcasted_iota(jnp.int32, (t, t), 0)

    def build(cp, carry):
        r0 = pl.multiple_of(cp * t, t)
        kf_rows, kb_rows = kf_ref[0, cp], kb_ref[0, cp]
        for c in range(SSM_GROUP):
            lo = pltpu.roll(jnp.broadcast_to(kf_rows[c:c + 1], (t, t)), 0, 1, stride=1, stride_axis=0)
            up = pltpu.roll(jnp.broadcast_to(kb_rows[c:c + 1], (t, t)), 0, 1, stride=1, stride_axis=0)
            g_s[pl.ds(r0, t), c * t:(c + 1) * t] = jnp.where(causal, lo, up).astype(BF16)
        return carry

    lax.fori_loop(0, SSM_GROUP, build, 0)
    u = jnp.concatenate([u_ref[0, c].reshape(rows, t) for c in range(SSM_GROUP)], axis=1).astype(BF16)
    y_intra = jnp.dot(u, g_s[...], preferred_element_type=F32)
    s_all = jnp.dot(u, p_ref[0], preferred_element_type=F32)
    s_s[...] = s_all
    x_s[...] = jnp.concatenate([pltpu.roll(s_all[:, :half], SSM_STATE, 1),
                                pltpu.roll(s_all[:, half:], SSM_STATE, 1)], axis=1)
    a1f, a2f = a1_ref[0, :, :half], a2_ref[0, :, :half]
    a1b, a2b = a1_ref[0, :, half:], a2_ref[0, :, half:]

    sub = SUBLANES
    n_blocks = n_chunks // sub

    def step(kb, carry):
        new = []
        for b in range(batch):
            hf, gf, hb, gb = carry[4 * b:4 * b + 4]
            base_f = pl.multiple_of(b * n_chunks + kb * sub, sub)
            base_b = pl.multiple_of(b * n_chunks + (n_blocks - 1 - kb) * sub, sub)
            sf, xf = s_s[pl.ds(base_f, sub), 0:half], x_s[pl.ds(base_f, sub), 0:half]
            sb, xb = s_s[pl.ds(base_b, sub), half:2 * half], x_s[pl.ds(base_b, sub), half:2 * half]
            hf_rows, hb_rows = [], [None] * sub
            for i in range(sub):
                hf_rows.append(hf)
                hf, gf = a1f * hf + a2f * gf + sf[i:i + 1], a1f * gf - a2f * hf + xf[i:i + 1]
            for i in range(sub - 1, -1, -1):
                hb_rows[i] = hb
                hb, gb = a1b * hb + a2b * gb + sb[i:i + 1], a1b * gb - a2b * hb + xb[i:i + 1]
            h_s[pl.ds(base_f, sub), 0:half] = jnp.concatenate(hf_rows, axis=0)
            h_s[pl.ds(base_b, sub), half:2 * half] = jnp.concatenate(hb_rows, axis=0)
            new += [hf, gf, hb, gb]
        return tuple(new)

    zero = jnp.zeros((1, half), F32)
    lax.fori_loop(0, n_blocks, step, (zero,) * (4 * batch))
    y = y_intra + jnp.dot(h_s[...].astype(BF16), q_ref[0], preferred_element_type=F32)
    for c in range(SSM_GROUP):
        y_ref[0, c] = y[:, c * t:(c + 1) * t].reshape(batch, n_chunks, t)


def _ssm_scan(u_t, kf, kb, p_mat, q_mat, a1, a2, batch, length):
    t = SSM_CHUNK
    nk = length // t
    gt = SSM_GROUP * t
    u5 = u_t.reshape(SSM_GROUPS, SSM_GROUP, batch, nk, t)
    blk = (1, SSM_GROUP, batch, nk, t)
    lag_spec = pl.BlockSpec((1, SSM_GROUP, SSM_GROUP, t), lambda g: (g, 0, 0, 0))
    y5 = pl.pallas_call(
        functools.partial(_ssm_kernel, batch=batch, n_chunks=nk),
        grid=(SSM_GROUPS,),
        in_specs=[pl.BlockSpec(blk, lambda g: (g, 0, 0, 0, 0)), lag_spec, lag_spec,
                  pl.BlockSpec((1, gt, 4 * SSM_STATE), lambda g: (g, 0, 0)),
                  pl.BlockSpec((1, 4 * SSM_STATE, gt), lambda g: (g, 0, 0)),
                  pl.BlockSpec((1, 1, 4 * SSM_STATE), lambda g: (g, 0, 0)),
                  pl.BlockSpec((1, 1, 4 * SSM_STATE), lambda g: (g, 0, 0))],
        out_specs=pl.BlockSpec(blk, lambda g: (g, 0, 0, 0, 0)),
        out_shape=jax.ShapeDtypeStruct(u5.shape, F32),
        scratch_shapes=[pltpu.VMEM((gt, gt), BF16)] + [pltpu.VMEM((batch * nk, 4 * SSM_STATE), F32)] * 3,
        compiler_params=_cparams(("parallel",)),
        name="ssm_scan",
    )(u5, kf, kb, p_mat, q_mat, a1, a2)
    return y5.reshape(SSM_WIDTH, batch * length)


def _ssm_tables(a_re, a_im, log_step, b_re, b_im, c_re, c_im):
    t = SSM_CHUNK
    hi = lax.Precision.HIGHEST
    lam = lax.complex(a_re.astype(F32), a_im.astype(F32))
    step = jnp.exp(log_step.astype(F32))[..., None]
    ls = lam * step
    a_bar = jnp.exp(ls)
    b_bar = ((a_bar - 1.0) / lam)[..., None] * lax.complex(b_re.astype(F32), b_im.astype(F32))
    c = lax.complex(c_re.astype(F32), c_im.astype(F32))
    tau = jnp.arange(t + 1, dtype=F32)
    pw = jnp.exp(ls[..., None] * tau)
    kern = jnp.einsum('dgcp,dgpt,dgpe->dgtce', c, pw[..., :t], b_bar, precision=hi).real
    kf = kern[0].at[:, 0].add(kern[1][:, 0]).transpose(0, 3, 2, 1)
    kb = jnp.roll(kern[1][:, ::-1], 1, axis=1).transpose(0, 3, 2, 1)
    pw_f, pw_b = pw[0], pw[1]
    pf = pw_f[:, :, t - 1 - jnp.arange(t)][..., None] * b_bar[0][:, :, None, :]
    pb = pw_b[:, :, :t][..., None] * b_bar[1][:, :, None, :]
    to_rows = lambda z: z.transpose(0, 3, 2, 1).reshape(SSM_GROUPS, SSM_GROUP * t, SSM_STATE)
    p_mat = jnp.concatenate([to_rows(pf.real), to_rows(pf.imag), to_rows(pb.real), to_rows(pb.imag)], axis=-1)
    qf = c[0].transpose(0, 2, 1)[..., None] * pw_f[:, :, 1:][:, :, None, :]
    qb = c[1].transpose(0, 2, 1)[..., None] * pw_b[:, :, t - jnp.arange(t)][:, :, None, :]
    to_cols = lambda z: z.reshape(SSM_GROUPS, SSM_STATE, SSM_GROUP * t)
    q_mat = jnp.concatenate([to_cols(qf.real), -to_cols(qf.imag), to_cols(qb.real), -to_cols(qb.imag)], axis=1)
    at = pw[..., t]
    a1 = jnp.concatenate([at[0].real, at[0].real, at[1].real, at[1].real], axis=-1)[:, None, :]
    a2 = jnp.concatenate([-at[0].imag, at[0].imag, -at[1].imag, at[1].imag], axis=-1)[:, None, :]
    return kf, kb, p_mat.astype(BF16), q_mat.astype(BF16), a1, a2


def _dft_tables(length):
    j = DFT_INNER
    n = 2 * length
    rn = n // j
    rh = rn // 2
    odd = 2 * jnp.arange(rh, dtype=jnp.int32) + 1
    r = jnp.arange(rn, dtype=jnp.int32)
    ang1 = (math.pi / rn) * ((odd[:, None] * r[None, :]) % (2 * rn)).astype(F32)
    c1, s1 = jnp.cos(ang1), jnp.sin(ang1)
    f1_full = jnp.concatenate([c1, -s1], axis=0)
    f1_top = f1_full[:, :rh]
    f1_inv = jnp.concatenate([c1[:, :rh].T, -s1[:, :rh].T], axis=1)
    jj = jnp.arange(j, dtype=jnp.int32)
    angt = (math.pi / n) * (odd[:, None] * jj[None, :]).astype(F32)
    tw_re, tw_im = jnp.cos(angt), -jnp.sin(angt)
    ang2 = (2.0 * math.pi / j) * ((jj[:, None] * jj[None, :]) % j).astype(F32)
    f2_cat = jnp.concatenate([jnp.cos(ang2), -jnp.sin(ang2)], axis=1)
    return dict(f1_full=f1_full.astype(BF16), f1_top=f1_top.astype(BF16), f1_inv=f1_inv.astype(BF16),
                tw_re=tw_re, tw_im=tw_im, f2_cat=f2_cat.astype(BF16), rn=rn, rh=rh)


def _dft_forward(a, tw_re, tw_im, f2_cat, rk):
    j = DFT_INNER
    a_re, a_im = a[:rk], a[rk:]
    ap = jnp.concatenate([a_re * tw_re - a_im * tw_im, a_re * tw_im + a_im * tw_re], axis=0).astype(BF16)
    m = jnp.dot(ap, f2_cat, preferred_element_type=F32)
    return m[:rk, :j] - m[rk:, j:], m[:rk, j:] + m[rk:, :j]


def _filter_hidden_kernel(bands_ref, w1t_ref, w1c_ref, w1s_ref, b1_ref, w2t_ref, b2_ref, hid_ref, *, length, tl):
    hi = lax.Precision.HIGHEST
    base = pl.program_id(0) * tl
    idx = (lax.broadcasted_iota(jnp.int32, (1, tl), 1) + base).astype(F32)
    for d in range(2):
        pos = idx if d == 0 else float(length) - idx
        tt = pos / float(length)
        wpos = (2.0 * math.pi / length) * pos
        arg = bands_ref[...] * wpos
        h1 = (w1t_ref[...] * tt
              + jnp.dot(w1c_ref[...], jnp.cos(arg), preferred_element_type=F32, precision=hi)
              - jnp.dot(w1s_ref[...], jnp.sin(arg), preferred_element_type=F32, precision=hi))
        h1 = jnp.sin(h1 + b1_ref[...])
        h2 = jnp.dot(w2t_ref[...], h1, preferred_element_type=F32, precision=hi)
        hid_ref[d] = jnp.sin(h2 + b2_ref[...])


def _filter_taps_kernel(hid_ref, w3t_ref, decay_ref, filt_ref, *, length, tl):
    hi = lax.Precision.HIGHEST
    d = pl.program_id(0) % 2
    base = pl.program_id(1) * tl
    idx = (lax.broadcasted_iota(jnp.int32, (1, tl), 1) + base).astype(F32)
    pos = jnp.where(d == 0, idx, float(length) - idx)
    tt = pos / float(length)
    f = jnp.dot(w3t_ref[...], hid_ref[0], preferred_element_type=F32, precision=hi)
    f = f * jnp.exp(-tt * decay_ref[...])
    f = jnp.where(jnp.logical_and(d == 1, idx == 0.0), 0.0, f)
    filt_ref[...] = f.astype(filt_ref.dtype)


def _filter_dft_kernel(filt_ref, f1_ref, twre_ref, twim_ref, f2_ref, kf_ref, *, rk, cb):
    for c in range(cb):
        r = jnp.concatenate([filt_ref[0, 0, c], -filt_ref[0, 1, c]], axis=0)
        a = jnp.dot(f1_ref[...], r, preferred_element_type=F32)
        x_re, x_im = _dft_forward(a, twre_ref[...], twim_ref[...], f2_ref[...], rk)
        kf_ref[0, c, 0] = x_re.astype(kf_ref.dtype)
        kf_ref[0, c, 1] = x_im.astype(kf_ref.dtype)


def _hyena_filters(length, tabs, w1, b1, w2, b2, w3, log_decay):
    tl = min(FILTER_LANE_TILE, length)
    nl = length // tl
    rn, rh, j = tabs["rn"], tabs["rh"], DFT_INNER
    bands = jnp.linspace(1e-4, FILTER_BANDS - 1, FILTER_BANDS, dtype=F32)[:, None]
    w1f = w1.astype(F32)
    const = lambda shape: pl.BlockSpec(shape, lambda *_: (0,) * len(shape))
    hid = pl.pallas_call(
        functools.partial(_filter_hidden_kernel, length=length, tl=tl),
        grid=(nl,),
        in_specs=[const((FILTER_BANDS, 1)), const((FILTER_HIDDEN, 1)), const((FILTER_HIDDEN, FILTER_BANDS)),
                  const((FILTER_HIDDEN, FILTER_BANDS)), const((FILTER_HIDDEN, 1)),
                  const((FILTER_HIDDEN, FILTER_HIDDEN)), const((FILTER_HIDDEN, 1))],
        out_specs=pl.BlockSpec((2, FILTER_HIDDEN, tl), lambda i: (0, 0, i)),
        out_shape=jax.ShapeDtypeStruct((2, FILTER_HIDDEN, length), F32),
        compiler_params=_cparams(("parallel",)),
        name="hyena_filter_hidden",
    )(bands, w1f[0:1].T, w1f[1:1 + FILTER_BANDS].T, w1f[1 + FILTER_BANDS:].T, b1.astype(F32)[:, None],
      w2.astype(F32).T, b2.astype(F32)[:, None])
    n_od = HYENA_ORDER * 2
    w3t = w3.astype(F32).T.reshape(n_od, HYENA_WIDTH, FILTER_HIDDEN)
    decay = jnp.exp(log_decay.astype(F32)).reshape(n_od, HYENA_WIDTH, 1)
    filt = pl.pallas_call(
        functools.partial(_filter_taps_kernel, length=length, tl=tl),
        grid=(n_od, nl),
        in_specs=[pl.BlockSpec((1, FILTER_HIDDEN, tl), lambda od, i: (od % 2, 0, i)),
                  pl.BlockSpec((None, HYENA_WIDTH, FILTER_HIDDEN), lambda od, i: (od, 0, 0)),
                  pl.BlockSpec((None, HYENA_WIDTH, 1), lambda od, i: (od, 0, 0))],
        out_specs=pl.BlockSpec((None, HYENA_WIDTH, tl), lambda od, i: (od, 0, i)),
        out_shape=jax.ShapeDtypeStruct((n_od, HYENA_WIDTH, length), BF16),
        compiler_params=_cparams(("parallel", "parallel")),
        name="hyena_filter_taps",
    )(hid, w3t, decay)
    filt6 = filt.reshape(HYENA_ORDER, 2, HYENA_WIDTH, rh, j)
    cb = HYENA_CH_BLOCK
    return pl.pallas_call(
        functools.partial(_filter_dft_kernel, rk=rh, cb=cb),
        grid=(HYENA_ORDER, HYENA_WIDTH // cb),
        in_specs=[pl.BlockSpec((1, 2, cb, rh, j), lambda o, c: (o, 0, c, 0, 0)),
                  const((rn, rn)), const((rh, j)), const((rh, j)), const((j, 2 * j))],
        out_specs=pl.BlockSpec((1, cb, 2, rh, j), lambda o, c: (o, c, 0, 0, 0)),
        out_shape=jax.ShapeDtypeStruct((HYENA_ORDER, HYENA_WIDTH, 2, rh, j), BF16),
        compiler_params=_cparams(("parallel", "parallel")),
        name="hyena_filter_dft",
    )(filt6, tabs["f1_full"], tabs["tw_re"], tabs["tw_im"], tabs["f2_cat"])


def _hyena_kernel(scw_ref, scb_ref, hb_ref, x_ref, hg_ref, kf_ref, f1_ref, f1i_ref, twre_ref, twim_ref, f2_ref,
                  o_ref, *, batch, rn, rh, cb, cs):
    j = DFT_INNER
    n_inv = 2.0 / (rn * j)
    rk = rh
    lane = lax.broadcasted_iota(jnp.int32, (rh, j), 1)
    row = lax.broadcasted_iota(jnp.int32, (rh, j), 0)
    first = jnp.logical_and(lane == 0, row == 0)
    last = jnp.logical_and(lane == j - 1, row == rh - 1)
    tw_re, tw_im = twre_ref[...], twim_ref[...]

    def prev_t(x):
        p = pltpu.roll(x, 1, 1)
        return jnp.where(first, 0.0, jnp.where(lane == 0, pltpu.roll(p, 1, 0), p))

    def next_t(x):
        p = pltpu.roll(x, j - 1, 1)
        return jnp.where(last, 0.0, jnp.where(lane == j - 1, pltpu.roll(p, rh - 1, 0), p))

    def split(m, i):
        return m[2 * rk * i:2 * rk * i + rk], m[2 * rk * i + rk:2 * rk * (i + 1)]

    def long_conv(xs, ks):
        rows = []
        for x in xs:
            a = jnp.dot(f1_ref[...], x.astype(BF16), preferred_element_type=F32)
            a_re, a_im = a[:rk], a[rk:]
            rows += [a_re * tw_re - a_im * tw_im, a_re * tw_im + a_im * tw_re]
        m = jnp.dot(jnp.concatenate(rows, axis=0).astype(BF16), f2_ref[...], preferred_element_type=F32)
        rows = []
        for i, (k_re, k_im) in enumerate(ks):
            m_re, m_im = split(m, i)
            x_re, x_im = m_re[:, :j] - m_im[:, j:], m_re[:, j:] + m_im[:, :j]
            rows += [x_re * k_re - x_im * k_im, x_re * k_im + x_im * k_re]
        m = jnp.dot(jnp.concatenate(rows, axis=0).astype(BF16), f2_ref[...], preferred_element_type=F32)
        outs = []
        for i in range(len(xs)):
            m_re, m_im = split(m, i)
            b_re, b_im = m_re[:, :j] + m_im[:, j:], m_im[:, :j] - m_re[:, j:]
            bp = jnp.concatenate([b_re * tw_re + b_im * tw_im, b_im * tw_re - b_re * tw_im], axis=0).astype(BF16)
            outs.append(jnp.dot(f1i_ref[...], bp, preferred_element_type=F32) * n_inv)
        return outs

    cbase = pl.program_id(0) * cb
    for c0 in range(0, cb, cs):
        chans = [(ci, b) for ci in range(c0, c0 + cs) for b in range(batch)]
        segs = []
        for sgm in range(3):
            seg = []
            for ci, b in chans:
                chs = sgm * HYENA_WIDTH + cbase + ci
                x = x_ref[sgm, ci, b]
                seg.append(prev_t(x) * scw_ref[chs] + x * scw_ref[3 * HYENA_WIDTH + chs]
                           + next_t(x) * scw_ref[6 * HYENA_WIDTH + chs] + scb_ref[chs])
            segs.append(seg)
        z = segs[0]
        for o in range(HYENA_ORDER):
            conv = long_conv(z, [(kf_ref[o, ci, 0].astype(F32), kf_ref[o, ci, 1].astype(F32)) for ci, _ in chans])
            z = [segs[o + 1][i] * (conv[i] + z[i] * hb_ref[o * HYENA_WIDTH + cbase + ci])
                 for i, (ci, _) in enumerate(chans)]
        for i, (ci, b) in enumerate(chans):
            o_ref[ci, b] = z[i] * hg_ref[ci, b]


def _hyena(hi_t, hg_t, kf, tabs, short_w, short_b, hy_bias, batch, length):
    rn, rh, j = tabs["rn"], tabs["rh"], DFT_INNER
    cb = HYENA_CH_BLOCK
    x5 = hi_t.reshape(3, HYENA_WIDTH, batch, rh, j)
    g4 = hg_t.reshape(HYENA_WIDTH, batch, rh, j)
    smem = pl.BlockSpec(memory_space=pltpu.SMEM)
    const = lambda shape: pl.BlockSpec(shape, lambda c: (0,) * len(shape))
    out = pl.pallas_call(
        functools.partial(_hyena_kernel, batch=batch, rn=rn, rh=rh, cb=cb, cs=HYENA_CH_SUB),
        grid=(HYENA_WIDTH // cb,),
        in_specs=[smem, smem, smem,
                  pl.BlockSpec((3, cb, batch, rh, j), lambda c: (0, c, 0, 0, 0)),
                  pl.BlockSpec((cb, batch, rh, j), lambda c: (c, 0, 0, 0)),
                  pl.BlockSpec((HYENA_ORDER, cb, 2, rh, j), lambda c: (0, c, 0, 0, 0)),
                  const((rn, rh)), const((rh, rn)), const((rh, j)), const((rh, j)), const((j, 2 * j))],
        out_specs=pl.BlockSpec((cb, batch, rh, j), lambda c: (c, 0, 0, 0)),
        out_shape=jax.ShapeDtypeStruct(g4.shape, F32),
        compiler_params=_cparams(("parallel",)),
        name="hyena_conv",
    )(short_w.astype(F32).reshape(-1), short_b.astype(F32), hy_bias.astype(F32).reshape(-1),
      x5, g4, kf, tabs["f1_top"], tabs["f1_inv"], tabs["tw_re"], tabs["tw_im"], tabs["f2_cat"])
    return out.reshape(HYENA_WIDTH, batch * length)


def _rms_rows(x, g):
    return x * lax.rsqrt(jnp.mean(x * x, axis=-1, keepdims=True) + EPS) * g


def _rms_cols(x, g):
    return x * lax.rsqrt(jnp.mean(x * x, axis=0, keepdims=True) + EPS) * g


def _post_kernel(ao_ref, ga_ref, mq_ref, mk_ref, mv_ref, y_ref, u_ref, sg_ref, hy_ref, x_ref,
                 d_ref, wglu_ref, g_attn_ref, g_ssm_ref, g_hy_ref, g_mem_ref, post_g_ref, wo_ref, out_ref):
    attn_n = _rms_cols(ao_ref[...] * ga_ref[...], g_attn_ref[...])
    s = jnp.dot(mq_ref[...], mk_ref[0], preferred_element_type=F32)
    ps = []
    for h in range(MEM_HEADS):
        sh = s[:, MEM_TOKENS * h:MEM_TOKENS * (h + 1)]
        e = jnp.exp(sh - jnp.max(sh, axis=-1, keepdims=True))
        ps.append(e * (1.0 / jnp.sum(e, axis=-1, keepdims=True)))
    p = jnp.concatenate(ps, axis=1).astype(BF16)
    cross_n = _rms_rows(jnp.dot(p, mv_ref[0], preferred_element_type=F32), g_mem_ref[...])
    y = y_ref[...] + d_ref[...] * u_ref[...]
    g = y * (0.5 * (1.0 + jnp.tanh(math.sqrt(2.0 / math.pi) * (y + 0.044715 * (y * y * y)))))
    gz = jnp.dot(wglu_ref[...], g.astype(BF16), preferred_element_type=F32)
    ssm_n = _rms_cols(g * _sigmoid(gz) * sg_ref[...], g_ssm_ref[...])
    hy_n = _rms_cols(hy_ref[...], g_hy_ref[...])
    o1, o2, o3 = ATTN_WIDTH, ATTN_WIDTH + SSM_WIDTH, ATTN_WIDTH + SSM_WIDTH + HYENA_WIDTH
    mixed = (jnp.dot(attn_n.T.astype(BF16), wo_ref[0:o1], preferred_element_type=F32)
             + jnp.dot(ssm_n.T.astype(BF16), wo_ref[o1:o2], preferred_element_type=F32)
             + jnp.dot(hy_n.T.astype(BF16), wo_ref[o2:o3], preferred_element_type=F32)
             + jnp.dot(cross_n.astype(BF16), wo_ref[o3:], preferred_element_type=F32))
    out_ref[...] = x_ref[...] + _rms_rows(mixed, post_g_ref[...])


def _post(ao, ga, mq, mk_bd, mv_bd, y_t, u_t, sg_t, hy_t, x2, d, wglu_t, g_attn, g_ssm, g_hy, g_mem, post_g, wo,
          length):
    n = x2.shape[0]
    tm = TOKEN_TILE
    tiles_per_seq = length // tm
    tok_spec = lambda w: pl.BlockSpec((tm, w), lambda i: (i, 0))
    ch_spec = lambda w: pl.BlockSpec((w, tm), lambda i: (0, i))
    const = lambda shape: pl.BlockSpec(shape, lambda i: (0,) * len(shape))
    hm = MEM_HEADS * MEM_TOKENS
    return pl.pallas_call(
        _post_kernel,
        grid=(n // tm,),
        in_specs=[ch_spec(ATTN_WIDTH), ch_spec(ATTN_WIDTH), tok_spec(MEM_WIDTH),
                  pl.BlockSpec((1, MEM_WIDTH, hm), lambda i: (i // tiles_per_seq, 0, 0)),
                  pl.BlockSpec((1, hm, MEM_WIDTH), lambda i: (i // tiles_per_seq, 0, 0)),
                  ch_spec(SSM_WIDTH), ch_spec(SSM_WIDTH), ch_spec(SSM_WIDTH), ch_spec(HYENA_WIDTH),
                  tok_spec(D_MODEL),
                  const((SSM_WIDTH, 1)), const((SSM_WIDTH, SSM_WIDTH)),
                  const((ATTN_WIDTH, 1)), const((SSM_WIDTH, 1)), const((HYENA_WIDTH, 1)), const((1, MEM_WIDTH)),
                  const((1, D_MODEL)), const((MIX_WIDTH, D_MODEL))],
        out_specs=tok_spec(D_MODEL),
        out_shape=jax.ShapeDtypeStruct((n, D_MODEL), F32),
        compiler_params=_cparams(("parallel",)),
        name="post",
    )(ao, ga, mq, mk_bd, mv_bd, y_t, u_t, sg_t, hy_t, x2, d, wglu_t, g_attn, g_ssm, g_hy, g_mem, post_g, wo)


def _rope_tables(length):
    rows = length // GRID_W
    row = jnp.broadcast_to(jnp.arange(rows, dtype=F32)[:, None], (rows, GRID_W)).reshape(length)
    col = jnp.broadcast_to(jnp.arange(GRID_W, dtype=F32)[None, :], (rows, GRID_W)).reshape(length)
    inv_freq = ROPE_THETA ** (-jnp.arange(ROPE_FREQS, dtype=F32) / ROPE_FREQS)
    ang = jnp.stack([row[:, None] * inv_freq, col[:, None] * inv_freq], axis=1)
    ang = jnp.broadcast_to(ang[:, :, None, :], (length, 2, 2, ROPE_FREQS)).reshape(length, HEAD_DIM)
    cos1, sin1 = jnp.cos(ang), jnp.sin(ang)
    low1 = (jnp.arange(HEAD_DIM) % (2 * ROPE_FREQS)) < ROPE_FREQS
    cos_t, sin_t = cos1.T, jnp.where(low1, -sin1, sin1).T
    cos = jnp.concatenate([cos1, cos1], axis=1)
    sin = jnp.concatenate([sin1, sin1], axis=1)
    low = jnp.concatenate([low1, low1])
    return cos, jnp.where(low, -sin, 0.0), jnp.where(low, 0.0, sin), cos_t, sin_t


def _block_diag_heads(mk, mv, batch):
    mk4 = mk.reshape(batch, MEM_TOKENS, MEM_HEADS, HEAD_DIM)
    mv4 = mv.reshape(batch, MEM_TOKENS, MEM_HEADS, HEAD_DIM)
    eye = jnp.eye(MEM_HEADS, dtype=mk.dtype)
    k_bd = jnp.einsum('bmhd,hg->bhdgm', mk4, eye).reshape(batch, MEM_WIDTH, MEM_HEADS * MEM_TOKENS)
    v_bd = jnp.einsum('bmhd,hg->bhmgd', mv4, eye).reshape(batch, MEM_HEADS * MEM_TOKENS, MEM_WIDTH)
    return k_bd.astype(BF16), v_bd.astype(BF16)


def _layer_weights(layer, p):
    w_in = p["w_in"][layer]
    a, kv = ATTN_WIDTH, KV_WIDTH
    k0, v0, g0, mq0 = a, a + kv, a + 2 * kv, w_in.shape[1] - MEM_WIDTH
    w_tok = jnp.concatenate([w_in[:, k0:v0], w_in[:, mq0:]], axis=1).astype(BF16)
    w_ch_t = jnp.concatenate([w_in[:, :k0], w_in[:, v0:mq0]], axis=1).T.astype(BF16)
    bg = p["branch_norm"][layer].astype(F32)
    o1, o2, o3 = a, a + SSM_WIDTH, a + SSM_WIDTH + HYENA_WIDTH
    head_id = jnp.arange(kv) // HEAD_DIM
    return dict(
        w_tok=w_tok, w_ch_t=w_ch_t,
        pre_g=p["pre_norm"][layer].astype(F32)[None, :], post_g=p["post_norm"][layer].astype(F32)[None, :],
        qg=p["q_norm"][layer].astype(F32)[:, None],
        kg=jnp.tile(p["k_norm"][layer].astype(F32), ATTN_KV_HEADS)[None, :],
        ones=(head_id[:, None] == head_id[None, :]).astype(BF16),
        mem_g=p["mem_norm"][layer].astype(F32)[None, :], w_mem_kv=p["w_mem_kv"][layer].astype(BF16),
        ssm=_ssm_tables(p["ssm_a_re"][layer], p["ssm_a_im"][layer], p["ssm_log_step"][layer], p["ssm_b_re"][layer],
                        p["ssm_b_im"][layer], p["ssm_c_re"][layer], p["ssm_c_im"][layer]),
        d=p["ssm_d"][layer].astype(F32)[:, None], wglu_t=p["ssm_w_glu"][layer].T.astype(BF16),
        g_attn=bg[:o1, None], g_ssm=bg[o1:o2, None], g_hy=bg[o2:o3, None], g_mem=bg[None, o3:],
        wo=p["w_out"][layer].astype(BF16),
    )


def _mixer_layer(x2, mem2, lw, kf, tabs, rope, p, layer, batch, length):
    q_t, k, v_t, ga, mq, su_t, sg_t, hi_t, hg_t = _inproj(x2, lw["pre_g"], lw["w_tok"], lw["w_ch_t"], rope,
                                                          lw["qg"], lw["kg"], lw["ones"], length)
    ao = _flash_attention(q_t, k, v_t, batch, length)
    mem_kv = _memkv(mem2, lw["mem_g"], lw["w_mem_kv"])
    mk_bd, mv_bd = _block_diag_heads(mem_kv[:, :MEM_WIDTH], mem_kv[:, MEM_WIDTH:], batch)
    y_t = _ssm_scan(su_t, *lw["ssm"], batch, length)
    hy_t = _hyena(hi_t, hg_t, kf, tabs, p["hyena_short_w"][layer], p["hyena_short_b"][layer],
                  p["hyena_bias"][layer], batch, length)
    return _post(ao, ga, mq, mk_bd, mv_bd, y_t, su_t, sg_t, hy_t, x2, lw["d"], lw["wglu_t"],
                 lw["g_attn"], lw["g_ssm"], lw["g_hy"], lw["g_mem"], lw["post_g"], lw["wo"], length)


def _run_group(x, mem, weights, p):
    batch, length, _ = x.shape
    rope = _rope_tables(length)
    tabs = _dft_tables(length)
    x2 = x.reshape(batch * length, D_MODEL)
    mem2 = mem.reshape(batch * MEM_TOKENS, D_MODEL)
    for layer in range(DEPTH):
        kf = _hyena_filters(length, tabs, p["hyena_ffn_w1"][layer], p["hyena_ffn_b1"][layer],
                            p["hyena_ffn_w2"][layer], p["hyena_ffn_b2"][layer], p["hyena_ffn_w3"][layer],
                            p["hyena_log_decay"][layer])
        x2 = _mixer_layer(x2, mem2, weights[layer], kf, tabs, rope, p, layer, batch, length)
    return x2.reshape(batch, length, D_MODEL)


def kernel(x_prompt, x_sample, mem_prompt, mem_sample, pre_norm, post_norm, w_in, q_norm, k_norm, mem_norm, w_mem_kv, ssm_a_re, ssm_a_im, ssm_log_step, ssm_b_re, ssm_b_im, ssm_c_re, ssm_c_im, ssm_d, ssm_w_glu, hyena_short_w, hyena_short_b, hyena_ffn_w1, hyena_ffn_b1, hyena_ffn_w2, hyena_ffn_b2, hyena_ffn_w3, hyena_log_decay, hyena_bias, branch_norm, w_out):
    p = dict(pre_norm=pre_norm, post_norm=post_norm, w_in=w_in, q_norm=q_norm, k_norm=k_norm, mem_norm=mem_norm,
             w_mem_kv=w_mem_kv, ssm_a_re=ssm_a_re, ssm_a_im=ssm_a_im, ssm_log_step=ssm_log_step, ssm_b_re=ssm_b_re,
             ssm_b_im=ssm_b_im, ssm_c_re=ssm_c_re, ssm_c_im=ssm_c_im, ssm_d=ssm_d, ssm_w_glu=ssm_w_glu,
             hyena_short_w=hyena_short_w, hyena_short_b=hyena_short_b, hyena_ffn_w1=hyena_ffn_w1,
             hyena_ffn_b1=hyena_ffn_b1, hyena_ffn_w2=hyena_ffn_w2, hyena_ffn_b2=hyena_ffn_b2,
             hyena_ffn_w3=hyena_ffn_w3, hyena_log_decay=hyena_log_decay, hyena_bias=hyena_bias,
             branch_norm=branch_norm, w_out=w_out)
    weights = [_layer_weights(layer, p) for layer in range(DEPTH)]
    return (_run_group(x_prompt, mem_prompt, weights, p), _run_group(x_sample, mem_sample, weights, p))
```

```python
import functools
import math

import jax
import jax.numpy as jnp
import numpy as np
from jax import lax
from jax.experimental import pallas as pl
from jax.experimental.pallas import tpu as pltpu

F32 = jnp.float32
BF16 = jnp.bfloat16

D_MODEL = 1024
DEPTH = 2
GRID_W = 64
HEAD_DIM = 64
ATTN_HEADS = 8
ATTN_KV_HEADS = 2
ATTN_GROUP = ATTN_HEADS // ATTN_KV_HEADS
ATTN_WIDTH = ATTN_HEADS * HEAD_DIM
KV_WIDTH = ATTN_KV_HEADS * HEAD_DIM
ROPE_THETA = 10000.0
ROPE_FREQS = HEAD_DIM // 4
SSM_GROUP = 16
SSM_GROUPS = 24
SSM_WIDTH = SSM_GROUP * SSM_GROUPS
SSM_STATE = 64
HYENA_WIDTH = 384
HYENA_ORDER = 2
FILTER_BANDS = 16
FILTER_HIDDEN = 64
MEM_TOKENS = 256
MEM_HEADS = 4
MEM_WIDTH = MEM_HEADS * HEAD_DIM
MIX_WIDTH = ATTN_WIDTH + SSM_WIDTH + HYENA_WIDTH + MEM_WIDTH
EPS = 1e-6

TOK_WIDTH = KV_WIDTH + MEM_WIDTH
CH_WIDTH = 2 * ATTN_WIDTH + KV_WIDTH + 2 * SSM_WIDTH + (HYENA_ORDER + 2) * HYENA_WIDTH
V_ROWS = HEAD_DIM + 16
Q_SCALE = HEAD_DIM ** -0.5 * math.log2(math.e)
SOFTMAX_STATIC_BOUND = 60.0

LANES = 128
SUBLANES = 8
VMEM_LIMIT = 56 * 1024 * 1024
TOKEN_TILE = 512
ATTN_Q_TILE = 256
ATTN_K_TILE = 512
SSM_CHUNK = LANES
DFT_INNER = 256
HYENA_CH_BLOCK = 8
HYENA_CH_SUB = 8
FILTER_LANE_TILE = 2048


def _cparams(sem):
    return pltpu.CompilerParams(dimension_semantics=sem, vmem_limit_bytes=VMEM_LIMIT)


def _silu(x):
    return x * (1.0 / (1.0 + jnp.exp(-x)))


def _sigmoid(x):
    return 1.0 / (1.0 + jnp.exp(-x))


def _nt_dot(a, b):
    return lax.dot_general(a, b, (((1,), (1,)), ((), ())), preferred_element_type=F32)


def _rope_128(xn, cos, s_lo, s_hi):
    outs = []
    for c in range(xn.shape[1] // LANES):
        xc = xn[:, LANES * c:LANES * (c + 1)]
        outs.append(xc * cos + pltpu.roll(xc, LANES - ROPE_FREQS, 1) * s_lo + pltpu.roll(xc, ROPE_FREQS, 1) * s_hi)
    return outs[0] if len(outs) == 1 else jnp.concatenate(outs, axis=1)


def _inproj_kernel(x_ref, pre_g_ref, wtok_ref, wch_ref, cos_ref, slo_ref, shi_ref, cost_ref, sint_ref,
                   qg_ref, kg_ref, ones_ref,
                   qt_ref, k_ref, vt_ref, ga_ref, mq_ref, su_ref, sg_ref, hi_ref, hg_ref):
    tm = x_ref.shape[0]
    x = x_ref[...]
    h = x * lax.rsqrt(jnp.mean(x * x, axis=-1, keepdims=True) + EPS) * pre_g_ref[...]
    hb = h.astype(BF16)
    tok = jnp.dot(hb, wtok_ref[...], preferred_element_type=F32)
    k = tok[:, 0:KV_WIDTH]
    mq = tok[:, KV_WIDTH:]
    k_ms = jnp.dot((k * k).astype(BF16), ones_ref[...], preferred_element_type=F32) * (1.0 / HEAD_DIM)
    kn = k * lax.rsqrt(k_ms + EPS) * kg_ref[...]
    kr = _rope_128(kn, cos_ref[...], slo_ref[...], shi_ref[...])
    for j in range(ATTN_KV_HEADS):
        k_ref[j] = kr[:, HEAD_DIM * j:HEAD_DIM * (j + 1)].astype(BF16)
    mq_ref[...] = (mq * (HEAD_DIM ** -0.5)).astype(BF16)
    ch = _nt_dot(wch_ref[...], hb)
    q3 = ch[0:ATTN_WIDTH].reshape(ATTN_HEADS, HEAD_DIM, tm)
    qn = q3 * lax.rsqrt(jnp.mean(q3 * q3, axis=1, keepdims=True) + EPS) * qg_ref[...][None]
    f = ROPE_FREQS
    rot = jnp.concatenate([qn[:, f:2 * f], qn[:, 0:f], qn[:, 3 * f:4 * f], qn[:, 2 * f:3 * f]], axis=1)
    qr = (qn * cost_ref[...][None] + rot * sint_ref[...][None]) * Q_SCALE
    qt_ref[...] = qr.reshape(ATTN_WIDTH, tm).astype(BF16)
    o = ATTN_WIDTH
    ones_row = (lax.broadcasted_iota(jnp.int32, (V_ROWS - HEAD_DIM, tm), 0) == 0).astype(BF16)
    for j in range(ATTN_KV_HEADS):
        vt_ref[j, 0:HEAD_DIM] = ch[o + HEAD_DIM * j:o + HEAD_DIM * (j + 1)].astype(BF16)
        vt_ref[j, HEAD_DIM:V_ROWS] = ones_row
    o += KV_WIDTH
    ga_ref[...] = _silu(ch[o:o + ATTN_WIDTH])
    o += ATTN_WIDTH
    su_ref[...] = ch[o:o + SSM_WIDTH]
    sg_ref[...] = _silu(ch[o + SSM_WIDTH:o + 2 * SSM_WIDTH])
    o += 2 * SSM_WIDTH
    hi_ref[...] = ch[o:o + 3 * HYENA_WIDTH]
    hg_ref[...] = _silu(ch[o + 3 * HYENA_WIDTH:])


def _inproj(x2, pre_g, w_tok, w_ch_t, rope, qg, kg, ones, length):
    cos, s_lo, s_hi, cos_t, sin_t = rope
    n = x2.shape[0]
    tm = TOKEN_TILE
    nt = n // tm
    tiles_per_seq = length // tm
    tok_spec = lambda w: pl.BlockSpec((tm, w), lambda i: (i, 0))
    ch_spec = lambda w: pl.BlockSpec((w, tm), lambda i: (0, i))
    const = lambda shape: pl.BlockSpec(shape, lambda i: (0,) * len(shape))
    pos_spec = pl.BlockSpec((tm, LANES), lambda i: (i % tiles_per_seq, 0))
    pos_t_spec = pl.BlockSpec((HEAD_DIM, tm), lambda i: (0, i % tiles_per_seq))
    return pl.pallas_call(
        _inproj_kernel,
        grid=(nt,),
        in_specs=[tok_spec(D_MODEL), const((1, D_MODEL)), const((D_MODEL, TOK_WIDTH)), const((CH_WIDTH, D_MODEL)),
                  pos_spec, pos_spec, pos_spec, pos_t_spec, pos_t_spec,
                  const((HEAD_DIM, 1)), const((1, KV_WIDTH)), const((KV_WIDTH, KV_WIDTH))],
        out_specs=[ch_spec(ATTN_WIDTH),
                   pl.BlockSpec((ATTN_KV_HEADS, tm, HEAD_DIM), lambda i: (0, i, 0)),
                   pl.BlockSpec((ATTN_KV_HEADS, V_ROWS, tm), lambda i: (0, 0, i)),
                   ch_spec(ATTN_WIDTH), tok_spec(MEM_WIDTH),
                   ch_spec(SSM_WIDTH), ch_spec(SSM_WIDTH), ch_spec(3 * HYENA_WIDTH), ch_spec(HYENA_WIDTH)],
        out_shape=[jax.ShapeDtypeStruct((ATTN_WIDTH, n), BF16),
                   jax.ShapeDtypeStruct((ATTN_KV_HEADS, n, HEAD_DIM), BF16),
                   jax.ShapeDtypeStruct((ATTN_KV_HEADS, V_ROWS, n), BF16),
                   jax.ShapeDtypeStruct((ATTN_WIDTH, n), F32),
                   jax.ShapeDtypeStruct((n, MEM_WIDTH), BF16),
                   jax.ShapeDtypeStruct((SSM_WIDTH, n), F32),
                   jax.ShapeDtypeStruct((SSM_WIDTH, n), F32),
                   jax.ShapeDtypeStruct((3 * HYENA_WIDTH, n), F32),
                   jax.ShapeDtypeStruct((HYENA_WIDTH, n), F32)],
        compiler_params=_cparams(("parallel",)),
        name="inproj",
    )(x2, pre_g, w_tok, w_ch_t, cos, s_lo, s_hi, cos_t, sin_t, qg, kg, ones)


def _flash_kernel(qt_ref, k_ref, vt_ref, o_ref, acc_s, s_s, kmax_s, *, tq, tk, n_kv):
    w = ATTN_GROUP * tq
    q4t = jnp.concatenate([qt_ref[HEAD_DIM * i:HEAD_DIM * (i + 1), :] for i in range(ATTN_GROUP)], axis=1)

    def chunk(ref_slice, c):
        return ref_slice(pl.ds(pl.multiple_of(c * tk, tk), tk))

    k_chunk = lambda c: chunk(lambda d: k_ref[0, d, :], c)
    v_chunk = lambda c: chunk(lambda d: vt_ref[0, :, d], c)

    @pl.when(pl.program_id(2) == 0)
    def _():
        def key_norm(c, mx):
            kc = k_chunk(c).astype(F32)
            return jnp.maximum(mx, jnp.max(jnp.sum(kc * kc, axis=1, keepdims=True), axis=0, keepdims=True))

        mx = lax.fori_loop(0, n_kv, key_norm, jnp.zeros((1, 1), F32))
        kmax_s[...] = jnp.broadcast_to(jnp.sqrt(mx), kmax_s.shape)

    qf = q4t.astype(F32)
    bound = jnp.sqrt(jnp.sum(qf * qf, axis=0, keepdims=True)) * kmax_s[:, 0:1] * (1.0 + 2.0 ** -10)
    bound_max = jnp.max(bound)

    def finish():
        o = acc_s[0:HEAD_DIM] * (1.0 / acc_s[HEAD_DIM:HEAD_DIM + 1])
        for i in range(ATTN_GROUP):
            o_ref[HEAD_DIM * i:HEAD_DIM * (i + 1), :] = o[:, i * tq:(i + 1) * tq]

    @pl.when(bound_max <= SOFTMAX_STATIC_BOUND)
    def _():
        acc_s[...] = jnp.zeros(acc_s.shape, F32)

        def body(c, carry):
            s = jnp.dot(k_chunk(c), q4t, preferred_element_type=F32)
            p = jnp.exp2(s - bound).astype(BF16)
            acc_s[...] += jnp.dot(v_chunk(c), p, preferred_element_type=F32)
            return carry

        lax.fori_loop(0, n_kv, body, 0, unroll=8)
        finish()

    @pl.when(jnp.logical_not(bound_max <= SOFTMAX_STATIC_BOUND))
    def _():
        acc_s[...] = jnp.zeros(acc_s.shape, F32)

        def scores(c, slot):
            s = jnp.dot(k_chunk(c), q4t, preferred_element_type=F32)
            s_s[slot] = s
            return jnp.max(s, axis=0, keepdims=True)

        def consume(c, slot, m, m_chunk):
            m_new = jnp.maximum(m, m_chunk)
            p = jnp.exp2(s_s[slot] - m_new).astype(BF16)
            pv = jnp.dot(v_chunk(c), p, preferred_element_type=F32)
            acc_s[...] = jnp.exp2(m - m_new) * acc_s[...] + pv
            return m_new

        def body(cc, carry):
            m, mc0 = carry
            c = 2 * cc
            mc1 = scores(c + 1, 1)
            m = consume(c, 0, m, mc0)
            mc0 = scores(c + 2, 0)
            return consume(c + 1, 1, m, mc1), mc0

        m, mc0 = lax.fori_loop(0, n_kv // 2 - 1, body, (jnp.full((1, w), -jnp.inf, F32), scores(0, 0)), unroll=2)
        mc1 = scores(n_kv - 1, 1)
        consume(n_kv - 1, 1, consume(n_kv - 2, 0, m, mc0), mc1)
        finish()


def _flash_attention(q_t, k, v_t, batch, length):
    n = q_t.shape[1]
    tq, tk = ATTN_Q_TILE, ATTN_K_TILE
    nq, nk = length // tq, length // tk
    gw = ATTN_GROUP * HEAD_DIM
    return pl.pallas_call(
        functools.partial(_flash_kernel, tq=tq, tk=tk, n_kv=nk),
        grid=(batch, ATTN_KV_HEADS, nq),
        in_specs=[pl.BlockSpec((gw, tq), lambda b, h, i: (h, b * nq + i)),
                  pl.BlockSpec((1, length, HEAD_DIM), lambda b, h, i: (h, b, 0)),
                  pl.BlockSpec((1, V_ROWS, length), lambda b, h, i: (h, 0, b))],
        out_specs=pl.BlockSpec((gw, tq), lambda b, h, i: (h, b * nq + i)),
        out_shape=jax.ShapeDtypeStruct((ATTN_WIDTH, n), F32),
        scratch_shapes=[pltpu.VMEM((V_ROWS, ATTN_GROUP * tq), F32),
                        pltpu.VMEM((2, tk, ATTN_GROUP * tq), F32),
                        pltpu.VMEM((1, LANES), F32)],
        compiler_params=_cparams(("parallel", "parallel", "arbitrary")),
        name="flash_attn",
    )(q_t, k, v_t)


def _memkv_kernel(mem_ref, g_ref, w_ref, kv_ref):
    m = mem_ref[...]
    mn = m * lax.rsqrt(jnp.mean(m * m, axis=-1, keepdims=True) + EPS) * g_ref[...]
    kv_ref[...] = jnp.dot(mn.astype(BF16), w_ref[...], preferred_element_type=F32)


def _memkv(mem2, mem_g, w_kv):
    rows = mem2.shape[0]
    return pl.pallas_call(
        _memkv_kernel,
        grid=(rows // MEM_TOKENS,),
        in_specs=[pl.BlockSpec((MEM_TOKENS, D_MODEL), lambda i: (i, 0)),
                  pl.BlockSpec((1, D_MODEL), lambda i: (0, 0)),
                  pl.BlockSpec((D_MODEL, 2 * MEM_WIDTH), lambda i: (0, 0))],
        out_specs=pl.BlockSpec((MEM_TOKENS, 2 * MEM_WIDTH), lambda i: (i, 0)),
        out_shape=jax.ShapeDtypeStruct((rows, 2 * MEM_WIDTH), F32),
        compiler_params=_cparams(("parallel",)),
        name="mem_kv",
    )(mem2, mem_g, w_kv)


def _toeplitz_kernel(kf_ref, kb_ref, g_ref):
    t = SSM_CHUNK
    causal = lax.broadcasted_iota(jnp.int32, (t, t), 1) >= lax.broadcasted_iota(jnp.int32, (t, t), 0)

    def build(cp, carry):
        r0 = pl.multiple_of(cp * t, t)
        kf_rows, kb_rows = kf_ref[0, cp], kb_ref[0, cp]
        for c in range(SSM_GROUP):
            lo = pltpu.roll(jnp.broadcast_to(kf_rows[c:c + 1], (t, t)), 0, 1, stride=1, stride_axis=0)
            up = pltpu.roll(jnp.broadcast_to(kb_rows[c:c + 1], (t, t)), 0, 1, stride=1, stride_axis=0)
            g_ref[0, pl.ds(r0, t), c * t:(c + 1) * t] = jnp.where(causal, lo, up).astype(BF16)
        return carry

    lax.fori_loop(0, SSM_GROUP, build, 0)


def _toeplitz(kf, kb):
    gt = SSM_GROUP * SSM_CHUNK
    lag_spec = pl.BlockSpec((1, SSM_GROUP, SSM_GROUP, SSM_CHUNK), lambda g: (g, 0, 0, 0))
    return pl.pallas_call(
        _toeplitz_kernel,
        grid=(SSM_GROUPS,),
        in_specs=[lag_spec, lag_spec],
        out_specs=pl.BlockSpec((1, gt, gt), lambda g: (g, 0, 0)),
        out_shape=jax.ShapeDtypeStruct((SSM_GROUPS, gt, gt), BF16),
        compiler_params=_cparams(("parallel",)),
        name="ssm_toeplitz",
    )(kf, kb)


def _ssm_kernel(u_ref, g_ref, p_ref, q_ref, a1_ref, a2_ref, y_ref, s_s, x_s, h_s, *, batch, n_chunks):
    t = SSM_CHUNK
    rows = batch * n_chunks
    half = 2 * SSM_STATE
    u = jnp.concatenate([u_ref[0, c].reshape(rows, t) for c in range(SSM_GROUP)], axis=1).astype(BF16)
    y_intra = jnp.dot(u, g_ref[0], preferred_element_type=F32)
    s_all = jnp.dot(u, p_ref[0], preferred_element_type=F32)
    s_s[...] = s_all
    x_s[...] = jnp.concatenate([pltpu.roll(s_all[:, :half], SSM_STATE, 1),
                                pltpu.roll(s_all[:, half:], SSM_STATE, 1)], axis=1)
    a1f, a2f = a1_ref[0, :, :half], a2_ref[0, :, :half]
    a1b, a2b = a1_ref[0, :, half:], a2_ref[0, :, half:]

    sub = SUBLANES
    n_blocks = n_chunks // sub

    def step(kb, carry):
        new = []
        for b in range(batch):
            hf, gf, hb, gb = carry[4 * b:4 * b + 4]
            base_f = pl.multiple_of(b * n_chunks + kb * sub, sub)
            base_b = pl.multiple_of(b * n_chunks + (n_blocks - 1 - kb) * sub, sub)
            sf, xf = s_s[pl.ds(base_f, sub), 0:half], x_s[pl.ds(base_f, sub), 0:half]
            sb, xb = s_s[pl.ds(base_b, sub), half:2 * half], x_s[pl.ds(base_b, sub), half:2 * half]
            hf_rows, hb_rows = [], [None] * sub
            for i in range(sub):
                hf_rows.append(hf)
                hf, gf = a1f * hf + a2f * gf + sf[i:i + 1], a1f * gf - a2f * hf + xf[i:i + 1]
            for i in range(sub - 1, -1, -1):
                hb_rows[i] = hb
                hb, gb = a1b * hb + a2b * gb + sb[i:i + 1], a1b * gb - a2b * hb + xb[i:i + 1]
            h_s[pl.ds(base_f, sub), 0:half] = jnp.concatenate(hf_rows, axis=0)
            h_s[pl.ds(base_b, sub), half:2 * half] = jnp.concatenate(hb_rows, axis=0)
            new += [hf, gf, hb, gb]
        return tuple(new)

    zero = jnp.zeros((1, half), F32)
    lax.fori_loop(0, n_blocks, step, (zero,) * (4 * batch))
    y = y_intra + jnp.dot(h_s[...].astype(BF16), q_ref[0], preferred_element_type=F32)
    for c in range(SSM_GROUP):
        y_ref[0, c] = y[:, c * t:(c + 1) * t].reshape(batch, n_chunks, t)


def _ssm_scan(u_t, g_mat, p_mat, q_mat, a1, a2, batch, length):
    t = SSM_CHUNK
    nk = length // t
    gt = SSM_GROUP * t
    u5 = u_t.reshape(SSM_GROUPS, SSM_GROUP, batch, nk, t)
    blk = (1, SSM_GROUP, batch, nk, t)
    y5 = pl.pallas_call(
        functools.partial(_ssm_kernel, batch=batch, n_chunks=nk),
        grid=(SSM_GROUPS,),
        in_specs=[pl.BlockSpec(blk, lambda g: (g, 0, 0, 0, 0)),
                  pl.BlockSpec((1, gt, gt), lambda g: (g, 0, 0)),
                  pl.BlockSpec((1, gt, 4 * SSM_STATE), lambda g: (g, 0, 0)),
                  pl.BlockSpec((1, 4 * SSM_STATE, gt), lambda g: (g, 0, 0)),
                  pl.BlockSpec((1, 1, 4 * SSM_STATE), lambda g: (g, 0, 0)),
                  pl.BlockSpec((1, 1, 4 * SSM_STATE), lambda g: (g, 0, 0))],
        out_specs=pl.BlockSpec(blk, lambda g: (g, 0, 0, 0, 0)),
        out_shape=jax.ShapeDtypeStruct(u5.shape, F32),
        scratch_shapes=[pltpu.VMEM((batch * nk, 4 * SSM_STATE), F32)] * 3,
        compiler_params=_cparams(("parallel",)),
        name="ssm_scan",
    )(u5, g_mat, p_mat, q_mat, a1, a2)
    return y5.reshape(SSM_WIDTH, batch * length)


def _ssm_tables(a_re, a_im, log_step, b_re, b_im, c_re, c_im):
    t = SSM_CHUNK
    hi = lax.Precision.HIGHEST
    lam = lax.complex(a_re.astype(F32), a_im.astype(F32))
    step = jnp.exp(log_step.astype(F32))[..., None]
    ls = lam * step
    a_bar = jnp.exp(ls)
    b_bar = ((a_bar - 1.0) / lam)[..., None] * lax.complex(b_re.astype(F32), b_im.astype(F32))
    c = lax.complex(c_re.astype(F32), c_im.astype(F32))
    tau = jnp.arange(t + 1, dtype=F32)
    pw = jnp.exp(ls[..., None] * tau)
    kern = jnp.einsum('dgcp,dgpt,dgpe->dgtce', c, pw[..., :t], b_bar, precision=hi).real
    kf = kern[0].at[:, 0].add(kern[1][:, 0]).transpose(0, 3, 2, 1)
    kb = jnp.roll(kern[1][:, ::-1], 1, axis=1).transpose(0, 3, 2, 1)
    pw_f, pw_b = pw[0], pw[1]
    pf = pw_f[:, :, t - 1 - jnp.arange(t)][..., None] * b_bar[0][:, :, None, :]
    pb = pw_b[:, :, :t][..., None] * b_bar[1][:, :, None, :]
    to_rows = lambda z: z.transpose(0, 3, 2, 1).reshape(SSM_GROUPS, SSM_GROUP * t, SSM_STATE)
    p_mat = jnp.concatenate([to_rows(pf.real), to_rows(pf.imag), to_rows(pb.real), to_rows(pb.imag)], axis=-1)
    qf = c[0].transpose(0, 2, 1)[..., None] * pw_f[:, :, 1:][:, :, None, :]
    qb = c[1].transpose(0, 2, 1)[..., None] * pw_b[:, :, t - jnp.arange(t)][:, :, None, :]
    to_cols = lambda z: z.reshape(SSM_GROUPS, SSM_STATE, SSM_GROUP * t)
    q_mat = jnp.concatenate([to_cols(qf.real), -to_cols(qf.imag), to_cols(qb.real), -to_cols(qb.imag)], axis=1)
    at = pw[..., t]
    a1 = jnp.concatenate([at[0].real, at[0].real, at[1].real, at[1].real], axis=-1)[:, None, :]
    a2 = jnp.concatenate([-at[0].imag, at[0].imag, -at[1].imag, at[1].imag], axis=-1)[:, None, :]
    return _toeplitz(kf, kb), p_mat.astype(BF16), q_mat.astype(BF16), a1, a2


def _dft_tables(length):
    j = DFT_INNER
    n = 2 * length
    rn = n // j
    rh = rn // 2
    odd = 2 * jnp.arange(rh, dtype=jnp.int32) + 1
    r = jnp.arange(rn, dtype=jnp.int32)
    ang1 = (math.pi / rn) * ((odd[:, None] * r[None, :]) % (2 * rn)).astype(F32)
    c1, s1 = jnp.cos(ang1), jnp.sin(ang1)
    f1_full = jnp.concatenate([c1, -s1], axis=0)
    f1_top = f1_full[:, :rh]
    f1_inv = jnp.concatenate([c1[:, :rh].T, -s1[:, :rh].T], axis=1)
    jj = jnp.arange(j, dtype=jnp.int32)
    angt = (math.pi / n) * (odd[:, None] * jj[None, :]).astype(F32)
    tw_re, tw_im = jnp.cos(angt), -jnp.sin(angt)
    ang2 = (2.0 * math.pi / j) * ((jj[:, None] * jj[None, :]) % j).astype(F32)
    f2_cat = jnp.concatenate([jnp.cos(ang2), -jnp.sin(ang2)], axis=1)
    return dict(f1_full=f1_full.astype(BF16), f1_top=f1_top.astype(BF16), f1_inv=f1_inv.astype(BF16),
                tw_re=tw_re, tw_im=tw_im, f2_cat=f2_cat.astype(BF16), rn=rn, rh=rh)


def _dft_forward(a, tw_re, tw_im, f2_cat, rk):
    j = DFT_INNER
    a_re, a_im = a[:rk], a[rk:]
    ap = jnp.concatenate([a_re * tw_re - a_im * tw_im, a_re * tw_im + a_im * tw_re], axis=0).astype(BF16)
    m = jnp.dot(ap, f2_cat, preferred_element_type=F32)
    return m[:rk, :j] - m[rk:, j:], m[:rk, j:] + m[rk:, :j]


def _filter_hidden_kernel(bands_ref, w1t_ref, w1c_ref, w1s_ref, b1_ref, w2t_ref, b2_ref, hid_ref, *, length, tl):
    hi = lax.Precision.HIGHEST
    base = pl.program_id(0) * tl
    pos = (lax.broadcasted_iota(jnp.int32, (1, tl), 1) + base).astype(F32)
    tt = pos / float(length)
    wpos = (2.0 * math.pi / length) * pos
    arg = bands_ref[...] * wpos
    h1 = (w1t_ref[...] * tt
          + jnp.dot(w1c_ref[...], jnp.cos(arg), preferred_element_type=F32, precision=hi)
          - jnp.dot(w1s_ref[...], jnp.sin(arg), preferred_element_type=F32, precision=hi))
    h1 = jnp.sin(h1 + b1_ref[...])
    h2 = jnp.dot(w2t_ref[...], h1, preferred_element_type=F32, precision=hi)
    hid_ref[...] = jnp.sin(h2 + b2_ref[...])


def _filter_taps_kernel(hid_ref, w3t_ref, decay_ref, filt_ref, *, length, tl):
    d = pl.program_id(0) % 2
    base = pl.program_id(1) * tl
    idx = (lax.broadcasted_iota(jnp.int32, (1, tl), 1) + base).astype(F32)
    pos = jnp.where(d == 0, idx, float(length) - idx)
    tt = pos / float(length)
    f = jnp.dot(w3t_ref[...], hid_ref[0], preferred_element_type=F32)
    f = f * jnp.exp(-tt * decay_ref[...])
    f = jnp.where(jnp.logical_and(d == 1, idx == 0.0), 0.0, f)
    filt_ref[...] = f.astype(filt_ref.dtype)


def _filter_dft_kernel(filt_ref, f1_ref, twre_ref, twim_ref, f2_ref, kf_ref, *, rk, cb):
    for c in range(cb):
        r = jnp.concatenate([filt_ref[0, 0, c], -filt_ref[0, 1, c]], axis=0)
        a = jnp.dot(f1_ref[...], r, preferred_element_type=F32)
        x_re, x_im = _dft_forward(a, twre_ref[...], twim_ref[...], f2_ref[...], rk)
        kf_ref[0, c, 0] = x_re.astype(kf_ref.dtype)
        kf_ref[0, c, 1] = x_im.astype(kf_ref.dtype)


def _hyena_filters(length, tabs, w1, b1, w2, b2, w3, log_decay):
    tl = min(FILTER_LANE_TILE, length)
    nl = length // tl
    rn, rh, j = tabs["rn"], tabs["rh"], DFT_INNER
    bands = jnp.linspace(1e-4, FILTER_BANDS - 1, FILTER_BANDS, dtype=F32)[:, None]
    w1f = w1.astype(F32)
    const = lambda shape: pl.BlockSpec(shape, lambda *_: (0,) * len(shape))
    hid = pl.pallas_call(
        functools.partial(_filter_hidden_kernel, length=length, tl=tl),
        grid=(nl,),
        in_specs=[const((FILTER_BANDS, 1)), const((FILTER_HIDDEN, 1)), const((FILTER_HIDDEN, FILTER_BANDS)),
                  const((FILTER_HIDDEN, FILTER_BANDS)), const((FILTER_HIDDEN, 1)),
                  const((FILTER_HIDDEN, FILTER_HIDDEN)), const((FILTER_HIDDEN, 1))],
        out_specs=pl.BlockSpec((FILTER_HIDDEN, tl), lambda i: (0, i)),
        out_shape=jax.ShapeDtypeStruct((FILTER_HIDDEN, length), F32),
        compiler_params=_cparams(("parallel",)),
        name="hyena_filter_hidden",
    )(bands, w1f[0:1].T, w1f[1:1 + FILTER_BANDS].T, w1f[1 + FILTER_BANDS:].T, b1.astype(F32)[:, None],
      w2.astype(F32).T, b2.astype(F32)[:, None])
    hid = jnp.stack([hid, jnp.concatenate([hid[:, :1], hid[:, :0:-1]], axis=1)]).astype(BF16)
    n_od = HYENA_ORDER * 2
    w3t = w3.T.reshape(n_od, HYENA_WIDTH, FILTER_HIDDEN).astype(BF16)
    decay = jnp.exp(log_decay.astype(F32)).reshape(n_od, HYENA_WIDTH, 1)
    filt = pl.pallas_call(
        functools.partial(_filter_taps_kernel, length=length, tl=tl),
        grid=(n_od, nl),
        in_specs=[pl.BlockSpec((1, FILTER_HIDDEN, tl), lambda od, i: (od % 2, 0, i)),
                  pl.BlockSpec((None, HYENA_WIDTH, FILTER_HIDDEN), lambda od, i: (od, 0, 0)),
                  pl.BlockSpec((None, HYENA_WIDTH, 1), lambda od, i: (od, 0, 0))],
        out_specs=pl.BlockSpec((None, HYENA_WIDTH, tl), lambda od, i: (od, 0, i)),
        out_shape=jax.ShapeDtypeStruct((n_od, HYENA_WIDTH, length), BF16),
        compiler_params=_cparams(("parallel", "parallel")),
        name="hyena_filter_taps",
    )(hid, w3t, decay)
    filt6 = filt.reshape(HYENA_ORDER, 2, HYENA_WIDTH, rh, j)
    cb = HYENA_CH_BLOCK
    return pl.pallas_call(
        functools.partial(_filter_dft_kernel, rk=rh, cb=cb),
        grid=(HYENA_ORDER, HYENA_WIDTH // cb),
        in_specs=[pl.BlockSpec((1, 2, cb, rh, j), lambda o, c: (o, 0, c, 0, 0)),
                  const((rn, rn)), const((rh, j)), const((rh, j)), const((j, 2 * j))],
        out_specs=pl.BlockSpec((1, cb, 2, rh, j), lambda o, c: (o, c, 0, 0, 0)),
        out_shape=jax.ShapeDtypeStruct((HYENA_ORDER, HYENA_WIDTH, 2, rh, j), BF16),
        compiler_params=_cparams(("parallel", "parallel")),
        name="hyena_filter_dft",
    )(filt6, tabs["f1_full"], tabs["tw_re"], tabs["tw_im"], tabs["f2_cat"])


def _hyena_kernel(scw_ref, scb_ref, hb_ref, x_ref, hg_ref, kf_ref, f1_ref, f1i_ref, twre_ref, twim_ref, f2_ref,
                  o_ref, *, batch, rn, rh, cb, cs):
    j = DFT_INNER
    n_inv = 2.0 / (rn * j)
    rk = rh
    lane = lax.broadcasted_iota(jnp.int32, (rh, j), 1)
    row = lax.broadcasted_iota(jnp.int32, (rh, j), 0)
    first = jnp.logical_and(lane == 0, row == 0)
    last = jnp.logical_and(lane == j - 1, row == rh - 1)
    tw_re, tw_im = twre_ref[...], twim_ref[...]

    def prev_t(x):
        p = pltpu.roll(x, 1, 1)
        return jnp.where(first, 0.0, jnp.where(lane == 0, pltpu.roll(p, 1, 0), p))

    def next_t(x):
        p = pltpu.roll(x, j - 1, 1)
        return jnp.where(last, 0.0, jnp.where(lane == j - 1, pltpu.roll(p, rh - 1, 0), p))

    def split(m, i):
        return m[2 * rk * i:2 * rk * i + rk], m[2 * rk * i + rk:2 * rk * (i + 1)]

    def long_conv(xs, ks):
        rows = []
        for x in xs:
            a = jnp.dot(f1_ref[...], x.astype(BF16), preferred_element_type=F32)
            a_re, a_im = a[:rk], a[rk:]
            rows += [a_re * tw_re - a_im * tw_im, a_re * tw_im + a_im * tw_re]
        m = jnp.dot(jnp.concatenate(rows, axis=0).astype(BF16), f2_ref[...], preferred_element_type=F32)
        rows = []
        for i, (k_re, k_im) in enumerate(ks):
            m_re, m_im = split(m, i)
            x_re, x_im = m_re[:, :j] - m_im[:, j:], m_re[:, j:] + m_im[:, :j]
            rows += [x_re * k_re - x_im * k_im, x_re * k_im + x_im * k_re]
        m = jnp.dot(jnp.concatenate(rows, axis=0).astype(BF16), f2_ref[...], preferred_element_type=F32)
        outs = []
        for i in range(len(xs)):
            m_re, m_im = split(m, i)
            b_re, b_im = m_re[:, :j] + m_im[:, j:], m_im[:, :j] - m_re[:, j:]
            bp = jnp.concatenate([b_re * tw_re + b_im * tw_im, b_im * tw_re - b_re * tw_im], axis=0).astype(BF16)
            outs.append(jnp.dot(f1i_ref[...], bp, preferred_element_type=F32) * n_inv)
        return outs

    cbase = pl.program_id(0) * cb
    for c0 in range(0, cb, cs):
        chans = [(ci, b) for ci in range(c0, c0 + cs) for b in range(batch)]
        segs = []
        for sgm in range(3):
            seg = []
            for ci, b in chans:
                chs = sgm * HYENA_WIDTH + cbase + ci
                x = x_ref[sgm, ci, b]
                seg.append(prev_t(x) * scw_ref[chs] + x * scw_ref[3 * HYENA_WIDTH + chs]
                           + next_t(x) * scw_ref[6 * HYENA_WIDTH + chs] + scb_ref[chs])
            segs.append(seg)
        z = segs[0]
        for o in range(HYENA_ORDER):
            conv = long_conv(z, [(kf_ref[o, ci, 0].astype(F32), kf_ref[o, ci, 1].astype(F32)) for ci, _ in chans])
            z = [segs[o + 1][i] * (conv[i] + z[i] * hb_ref[o * HYENA_WIDTH + cbase + ci])
                 for i, (ci, _) in enumerate(chans)]
        for i, (ci, b) in enumerate(chans):
            o_ref[ci, b] = z[i] * hg_ref[ci, b]


def _hyena(hi_t, hg_t, kf, tabs, short_w, short_b, hy_bias, batch, length):
    rn, rh, j = tabs["rn"], tabs["rh"], DFT_INNER
    cb = HYENA_CH_BLOCK
    x5 = hi_t.reshape(3, HYENA_WIDTH, batch, rh, j)
    g4 = hg_t.reshape(HYENA_WIDTH, batch, rh, j)
    smem = pl.BlockSpec(memory_space=pltpu.SMEM)
    const = lambda shape: pl.BlockSpec(shape, lambda c: (0,) * len(shape))
    out = pl.pallas_call(
        functools.partial(_hyena_kernel, batch=batch, rn=rn, rh=rh, cb=cb, cs=HYENA_CH_SUB),
        grid=(HYENA_WIDTH // cb,),
        in_specs=[smem, smem, smem,
                  pl.BlockSpec((3, cb, batch, rh, j), lambda c: (0, c, 0, 0, 0)),
                  pl.BlockSpec((cb, batch, rh, j), lambda c: (c, 0, 0, 0)),
                  pl.BlockSpec((HYENA_ORDER, cb, 2, rh, j), lambda c: (0, c, 0, 0, 0)),
                  const((rn, rh)), const((rh, rn)), const((rh, j)), const((rh, j)), const((j, 2 * j))],
        out_specs=pl.BlockSpec((cb, batch, rh, j), lambda c: (c, 0, 0, 0)),
        out_shape=jax.ShapeDtypeStruct(g4.shape, F32),
        compiler_params=_cparams(("parallel",)),
        name="hyena_conv",
    )(short_w.astype(F32).reshape(-1), short_b.astype(F32), hy_bias.astype(F32).reshape(-1),
      x5, g4, kf, tabs["f1_top"], tabs["f1_inv"], tabs["tw_re"], tabs["tw_im"], tabs["f2_cat"])
    return out.reshape(HYENA_WIDTH, batch * length)


def _rms_rows(x, g):
    return x * lax.rsqrt(jnp.mean(x * x, axis=-1, keepdims=True) + EPS) * g


def _rms_cols(x, g):
    return x * lax.rsqrt(jnp.mean(x * x, axis=0, keepdims=True) + EPS) * g


def _post_kernel(ao_ref, ga_ref, mq_ref, mk_ref, mv_ref, y_ref, u_ref, sg_ref, hy_ref, x_ref,
                 d_ref, wglu_ref, g_attn_ref, g_ssm_ref, g_hy_ref, g_mem_ref, post_g_ref, wo_ref, out_ref):
    attn_n = _rms_cols(ao_ref[...] * ga_ref[...], g_attn_ref[...])
    s = jnp.dot(mq_ref[...], mk_ref[0], preferred_element_type=F32)
    ps = []
    for h in range(MEM_HEADS):
        sh = s[:, MEM_TOKENS * h:MEM_TOKENS * (h + 1)]
        e = jnp.exp(sh - jnp.max(sh, axis=-1, keepdims=True))
        ps.append(e * (1.0 / jnp.sum(e, axis=-1, keepdims=True)))
    p = jnp.concatenate(ps, axis=1).astype(BF16)
    cross_n = _rms_rows(jnp.dot(p, mv_ref[0], preferred_element_type=F32), g_mem_ref[...])
    y = y_ref[...] + d_ref[...] * u_ref[...]
    g = y * (0.5 * (1.0 + jnp.tanh(math.sqrt(2.0 / math.pi) * (y + 0.044715 * (y * y * y)))))
    gz = jnp.dot(wglu_ref[...], g.astype(BF16), preferred_element_type=F32)
    ssm_n = _rms_cols(g * _sigmoid(gz) * sg_ref[...], g_ssm_ref[...])
    hy_n = _rms_cols(hy_ref[...], g_hy_ref[...])
    o1, o2, o3 = ATTN_WIDTH, ATTN_WIDTH + SSM_WIDTH, ATTN_WIDTH + SSM_WIDTH + HYENA_WIDTH
    mixed = (jnp.dot(attn_n.T.astype(BF16), wo_ref[0:o1], preferred_element_type=F32)
             + jnp.dot(ssm_n.T.astype(BF16), wo_ref[o1:o2], preferred_element_type=F32)
             + jnp.dot(hy_n.T.astype(BF16), wo_ref[o2:o3], preferred_element_type=F32)
             + jnp.dot(cross_n.astype(BF16), wo_ref[o3:], preferred_element_type=F32))
    out_ref[...] = x_ref[...] + _rms_rows(mixed, post_g_ref[...])


def _post(ao, ga, mq, mk_bd, mv_bd, y_t, u_t, sg_t, hy_t, x2, d, wglu_t, g_attn, g_ssm, g_hy, g_mem, post_g, wo,
          length):
    n = x2.shape[0]
    tm = TOKEN_TILE
    tiles_per_seq = length // tm
    tok_spec = lambda w: pl.BlockSpec((tm, w), lambda i: (i, 0))
    ch_spec = lambda w: pl.BlockSpec((w, tm), lambda i: (0, i))
    const = lambda shape: pl.BlockSpec(shape, lambda i: (0,) * len(shape))
    hm = MEM_HEADS * MEM_TOKENS
    return pl.pallas_call(
        _post_kernel,
        grid=(n // tm,),
        in_specs=[ch_spec(ATTN_WIDTH), ch_spec(ATTN_WIDTH), tok_spec(MEM_WIDTH),
                  pl.BlockSpec((1, MEM_WIDTH, hm), lambda i: (i // tiles_per_seq, 0, 0)),
                  pl.BlockSpec((1, hm, MEM_WIDTH), lambda i: (i // tiles_per_seq, 0, 0)),
                  ch_spec(SSM_WIDTH), ch_spec(SSM_WIDTH), ch_spec(SSM_WIDTH), ch_spec(HYENA_WIDTH),
                  tok_spec(D_MODEL),
                  const((SSM_WIDTH, 1)), const((SSM_WIDTH, SSM_WIDTH)),
                  const((ATTN_WIDTH, 1)), const((SSM_WIDTH, 1)), const((HYENA_WIDTH, 1)), const((1, MEM_WIDTH)),
                  const((1, D_MODEL)), const((MIX_WIDTH, D_MODEL))],
        out_specs=tok_spec(D_MODEL),
        out_shape=jax.ShapeDtypeStruct((n, D_MODEL), F32),
        compiler_params=_cparams(("parallel",)),
        name="post",
    )(ao, ga, mq, mk_bd, mv_bd, y_t, u_t, sg_t, hy_t, x2, d, wglu_t, g_attn, g_ssm, g_hy, g_mem, post_g, wo)


def _rope_tables(length):
    rows = length // GRID_W
    row = jnp.broadcast_to(jnp.arange(rows, dtype=F32)[:, None], (rows, GRID_W)).reshape(length)
    col = jnp.broadcast_to(jnp.arange(GRID_W, dtype=F32)[None, :], (rows, GRID_W)).reshape(length)
    inv_freq = ROPE_THETA ** (-jnp.arange(ROPE_FREQS, dtype=F32) / ROPE_FREQS)
    ang = jnp.stack([row[:, None] * inv_freq, col[:, None] * inv_freq], axis=1)
    ang = jnp.broadcast_to(ang[:, :, None, :], (length, 2, 2, ROPE_FREQS)).reshape(length, HEAD_DIM)
    cos1, sin1 = jnp.cos(ang), jnp.sin(ang)
    low1 = (jnp.arange(HEAD_DIM) % (2 * ROPE_FREQS)) < ROPE_FREQS
    cos_t, sin_t = cos1.T, jnp.where(low1, -sin1, sin1).T
    cos = jnp.concatenate([cos1, cos1], axis=1)
    sin = jnp.concatenate([sin1, sin1], axis=1)
    low = jnp.concatenate([low1, low1])
    return cos, jnp.where(low, -sin, 0.0), jnp.where(low, 0.0, sin), cos_t, sin_t


def _block_diag_heads(mk, mv, batch):
    mk4 = mk.reshape(batch, MEM_TOKENS, MEM_HEADS, HEAD_DIM)
    mv4 = mv.reshape(batch, MEM_TOKENS, MEM_HEADS, HEAD_DIM)
    eye = jnp.eye(MEM_HEADS, dtype=mk.dtype)
    k_bd = jnp.einsum('bmhd,hg->bhdgm', mk4, eye).reshape(batch, MEM_WIDTH, MEM_HEADS * MEM_TOKENS)
    v_bd = jnp.einsum('bmhd,hg->bhmgd', mv4, eye).reshape(batch, MEM_HEADS * MEM_TOKENS, MEM_WIDTH)
    return k_bd.astype(BF16), v_bd.astype(BF16)


def _layer_weights(layer, p):
    w_in = p["w_in"][layer]
    a, kv = ATTN_WIDTH, KV_WIDTH
    k0, v0, g0, mq0 = a, a + kv, a + 2 * kv, w_in.shape[1] - MEM_WIDTH
    w_tok = jnp.concatenate([w_in[:, k0:v0], w_in[:, mq0:]], axis=1).astype(BF16)
    w_ch_t = jnp.concatenate([w_in[:, :k0], w_in[:, v0:mq0]], axis=1).T.astype(BF16)
    bg = p["branch_norm"][layer].astype(F32)
    o1, o2, o3 = a, a + SSM_WIDTH, a + SSM_WIDTH + HYENA_WIDTH
    head_id = jnp.arange(kv) // HEAD_DIM
    return dict(
        w_tok=w_tok, w_ch_t=w_ch_t,
        pre_g=p["pre_norm"][layer].astype(F32)[None, :], post_g=p["post_norm"][layer].astype(F32)[None, :],
        qg=p["q_norm"][layer].astype(F32)[:, None],
        kg=jnp.tile(p["k_norm"][layer].astype(F32), ATTN_KV_HEADS)[None, :],
        ones=(head_id[:, None] == head_id[None, :]).astype(BF16),
        mem_g=p["mem_norm"][layer].astype(F32)[None, :], w_mem_kv=p["w_mem_kv"][layer].astype(BF16),
        ssm=_ssm_tables(p["ssm_a_re"][layer], p["ssm_a_im"][layer], p["ssm_log_step"][layer], p["ssm_b_re"][layer],
                        p["ssm_b_im"][layer], p["ssm_c_re"][layer], p["ssm_c_im"][layer]),
        d=p["ssm_d"][layer].astype(F32)[:, None], wglu_t=p["ssm_w_glu"][layer].T.astype(BF16),
        g_attn=bg[:o1, None], g_ssm=bg[o1:o2, None], g_hy=bg[o2:o3, None], g_mem=bg[None, o3:],
        wo=p["w_out"][layer].astype(BF16),
    )


def _mixer_layer(x2, mem2, lw, kf, tabs, rope, p, layer, batch, length):
    q_t, k, v_t, ga, mq, su_t, sg_t, hi_t, hg_t = _inproj(x2, lw["pre_g"], lw["w_tok"], lw["w_ch_t"], rope,
                                                          lw["qg"], lw["kg"], lw["ones"], length)
    ao = _flash_attention(q_t, k, v_t, batch, length)
    mem_kv = _memkv(mem2, lw["mem_g"], lw["w_mem_kv"])
    mk_bd, mv_bd = _block_diag_heads(mem_kv[:, :MEM_WIDTH], mem_kv[:, MEM_WIDTH:], batch)
    y_t = _ssm_scan(su_t, *lw["ssm"], batch, length)
    hy_t = _hyena(hi_t, hg_t, kf, tabs, p["hyena_short_w"][layer], p["hyena_short_b"][layer],
                  p["hyena_bias"][layer], batch, length)
    return _post(ao, ga, mq, mk_bd, mv_bd, y_t, su_t, sg_t, hy_t, x2, lw["d"], lw["wglu_t"],
                 lw["g_attn"], lw["g_ssm"], lw["g_hy"], lw["g_mem"], lw["post_g"], lw["wo"], length)


def _run_group(x, mem, weights, p):
    batch, length, _ = x.shape
    rope = _rope_tables(length)
    tabs = _dft_tables(length)
    x2 = x.reshape(batch * length, D_MODEL)
    mem2 = mem.reshape(batch * MEM_TOKENS, D_MODEL)
    for layer in range(DEPTH):
        kf = _hyena_filters(length, tabs, p["hyena_ffn_w1"][layer], p["hyena_ffn_b1"][layer],
                            p["hyena_ffn_w2"][layer], p["hyena_ffn_b2"][layer], p["hyena_ffn_w3"][layer],
                            p["hyena_log_decay"][layer])
        x2 = _mixer_layer(x2, mem2, weights[layer], kf, tabs, rope, p, layer, batch, length)
    return x2.reshape(batch, length, D_MODEL)


def kernel(x_prompt, x_sample, mem_prompt, mem_sample, pre_norm, post_norm, w_in, q_norm, k_norm, mem_norm, w_mem_kv, ssm_a_re, ssm_a_im, ssm_log_step, ssm_b_re, ssm_b_im, ssm_c_re, ssm_c_im, ssm_d, ssm_w_glu, hyena_short_w, hyena_short_b, hyena_ffn_w1, hyena_ffn_b1, hyena_ffn_w2, hyena_ffn_b2, hyena_ffn_w3, hyena_log_decay, hyena_bias, branch_norm, w_out):
    p = dict(pre_norm=pre_norm, post_norm=post_norm, w_in=w_in, q_norm=q_norm, k_norm=k_norm, mem_norm=mem_norm,
             w_mem_kv=w_mem_kv, ssm_a_re=ssm_a_re, ssm_a_im=ssm_a_im, ssm_log_step=ssm_log_step, ssm_b_re=ssm_b_re,
             ssm_b_im=ssm_b_im, ssm_c_re=ssm_c_re, ssm_c_im=ssm_c_im, ssm_d=ssm_d, ssm_w_glu=ssm_w_glu,
             hyena_short_w=hyena_short_w, hyena_short_b=hyena_short_b, hyena_ffn_w1=hyena_ffn_w1,
             hyena_ffn_b1=hyena_ffn_b1, hyena_ffn_w2=hyena_ffn_w2, hyena_ffn_b2=hyena_ffn_b2,
             hyena_ffn_w3=hyena_ffn_w3, hyena_log_decay=hyena_log_decay, hyena_bias=hyena_bias,
             branch_norm=branch_norm, w_out=w_out)
    weights = [_layer_weights(layer, p) for layer in range(DEPTH)]
    return (_run_group(x_prompt, mem_prompt, weights, p), _run_group(x_sample, mem_sample, weights, p))
```

```python
import functools
import math

import jax
import jax.numpy as jnp
import numpy as np
from jax import lax
from jax.experimental import pallas as pl
from jax.experimental.pallas import tpu as pltpu

F32 = jnp.float32
BF16 = jnp.bfloat16

D_MODEL = 1024
DEPTH = 2
GRID_W = 64
HEAD_DIM = 64
ATTN_HEADS = 8
ATTN_KV_HEADS = 2
ATTN_GROUP = ATTN_HEADS // ATTN_KV_HEADS
ATTN_WIDTH = ATTN_HEADS * HEAD_DIM
KV_WIDTH = ATTN_KV_HEADS * HEAD_DIM
ROPE_THETA = 10000.0
ROPE_FREQS = HEAD_DIM // 4
SSM_GROUP = 16
SSM_GROUPS = 24
SSM_WIDTH = SSM_GROUP * SSM_GROUPS
SSM_STATE = 64
HYENA_WIDTH = 384
HYENA_ORDER = 2
FILTER_BANDS = 16
FILTER_HIDDEN = 64
MEM_TOKENS = 256
MEM_HEADS = 4
MEM_WIDTH = MEM_HEADS * HEAD_DIM
MIX_WIDTH = ATTN_WIDTH + SSM_WIDTH + HYENA_WIDTH + MEM_WIDTH
EPS = 1e-6

TOK_WIDTH = KV_WIDTH + MEM_WIDTH
CH_WIDTH = 2 * ATTN_WIDTH + KV_WIDTH + 2 * SSM_WIDTH + (HYENA_ORDER + 2) * HYENA_WIDTH
V_ROWS = HEAD_DIM + 16
Q_SCALE = HEAD_DIM ** -0.5 * math.log2(math.e)
SOFTMAX_STATIC_BOUND = 60.0

LANES = 128
SUBLANES = 8
VMEM_LIMIT = 56 * 1024 * 1024
TOKEN_TILE = 512
ATTN_Q_TILE = 256
ATTN_K_TILE = 512
SSM_CHUNK = LANES
DFT_INNER = 256
HYENA_CH_BLOCK = 8
HYENA_CH_SUB = 8
FILTER_LANE_TILE = 2048


def _cparams(sem):
    return pltpu.CompilerParams(dimension_semantics=sem, vmem_limit_bytes=VMEM_LIMIT)


def _silu(x):
    return x * (1.0 / (1.0 + jnp.exp(-x)))


def _sigmoid(x):
    return 1.0 / (1.0 + jnp.exp(-x))


def _nt_dot(a, b):
    return lax.dot_general(a, b, (((1,), (1,)), ((), ())), preferred_element_type=F32)


def _rope_128(xn, cos, s_lo, s_hi):
    outs = []
    for c in range(xn.shape[1] // LANES):
        xc = xn[:, LANES * c:LANES * (c + 1)]
        outs.append(xc * cos + pltpu.roll(xc, LANES - ROPE_FREQS, 1) * s_lo + pltpu.roll(xc, ROPE_FREQS, 1) * s_hi)
    return outs[0] if len(outs) == 1 else jnp.concatenate(outs, axis=1)


def _inproj_kernel(x_ref, pre_g_ref, wtok_ref, wch_ref, cos_ref, slo_ref, shi_ref, cost_ref, sint_ref,
                   qg_ref, kg_ref, ones_ref,
                   qt_ref, k_ref, vt_ref, ga_ref, mq_ref, su_ref, sg_ref, hi_ref, hg_ref):
    tm = x_ref.shape[0]
    x = x_ref[...]
    h = x * lax.rsqrt(jnp.mean(x * x, axis=-1, keepdims=True) + EPS) * pre_g_ref[...]
    hb = h.astype(BF16)
    tok = jnp.dot(hb, wtok_ref[...], preferred_element_type=F32)
    k = tok[:, 0:KV_WIDTH]
    mq = tok[:, KV_WIDTH:]
    k_ms = jnp.dot((k * k).astype(BF16), ones_ref[...], preferred_element_type=F32) * (1.0 / HEAD_DIM)
    kn = k * lax.rsqrt(k_ms + EPS) * kg_ref[...]
    kr = _rope_128(kn, cos_ref[...], slo_ref[...], shi_ref[...])
    for j in range(ATTN_KV_HEADS):
        k_ref[j] = kr[:, HEAD_DIM * j:HEAD_DIM * (j + 1)].astype(BF16)
    mq_ref[...] = (mq * (HEAD_DIM ** -0.5)).astype(BF16)
    ch = _nt_dot(wch_ref[...], hb)
    q3 = ch[0:ATTN_WIDTH].reshape(ATTN_HEADS, HEAD_DIM, tm)
    qn = q3 * lax.rsqrt(jnp.mean(q3 * q3, axis=1, keepdims=True) + EPS) * qg_ref[...][None]
    f = ROPE_FREQS
    rot = jnp.concatenate([qn[:, f:2 * f], qn[:, 0:f], qn[:, 3 * f:4 * f], qn[:, 2 * f:3 * f]], axis=1)
    qr = (qn * cost_ref[...][None] + rot * sint_ref[...][None]) * Q_SCALE
    qt_ref[...] = qr.reshape(ATTN_WIDTH, tm).astype(BF16)
    o = ATTN_WIDTH
    ones_row = (lax.broadcasted_iota(jnp.int32, (V_ROWS - HEAD_DIM, tm), 0) == 0).astype(BF16)
    for j in range(ATTN_KV_HEADS):
        vt_ref[j, 0:HEAD_DIM] = ch[o + HEAD_DIM * j:o + HEAD_DIM * (j + 1)].astype(BF16)
        vt_ref[j, HEAD_DIM:V_ROWS] = ones_row
    o += KV_WIDTH
    ga_ref[...] = _silu(ch[o:o + ATTN_WIDTH])
    o += ATTN_WIDTH
    su_ref[...] = ch[o:o + SSM_WIDTH]
    sg_ref[...] = _silu(ch[o + SSM_WIDTH:o + 2 * SSM_WIDTH])
    o += 2 * SSM_WIDTH
    hi_ref[...] = ch[o:o + 3 * HYENA_WIDTH]
    hg_ref[...] = _silu(ch[o + 3 * HYENA_WIDTH:])


def _inproj(x2, pre_g, w_tok, w_ch_t, rope, qg, kg, ones, length):
    cos, s_lo, s_hi, cos_t, sin_t = rope
    n = x2.shape[0]
    tm = TOKEN_TILE
    nt = n // tm
    tiles_per_seq = length // tm
    tok_spec = lambda w: pl.BlockSpec((tm, w), lambda i: (i, 0))
    ch_spec = lambda w: pl.BlockSpec((w, tm), lambda i: (0, i))
    const = lambda shape: pl.BlockSpec(shape, lambda i: (0,) * len(shape))
    pos_spec = pl.BlockSpec((tm, LANES), lambda i: (i % tiles_per_seq, 0))
    pos_t_spec = pl.BlockSpec((HEAD_DIM, tm), lambda i: (0, i % tiles_per_seq))
    return pl.pallas_call(
        _inproj_kernel,
        grid=(nt,),
        in_specs=[tok_spec(D_MODEL), const((1, D_MODEL)), const((D_MODEL, TOK_WIDTH)), const((CH_WIDTH, D_MODEL)),
                  pos_spec, pos_spec, pos_spec, pos_t_spec, pos_t_spec,
                  const((HEAD_DIM, 1)), const((1, KV_WIDTH)), const((KV_WIDTH, KV_WIDTH))],
        out_specs=[ch_spec(ATTN_WIDTH),
                   pl.BlockSpec((ATTN_KV_HEADS, tm, HEAD_DIM), lambda i: (0, i, 0)),
                   pl.BlockSpec((ATTN_KV_HEADS, V_ROWS, tm), lambda i: (0, 0, i)),
                   ch_spec(ATTN_WIDTH), tok_spec(MEM_WIDTH),
                   ch_spec(SSM_WIDTH), ch_spec(SSM_WIDTH), ch_spec(3 * HYENA_WIDTH), ch_spec(HYENA_WIDTH)],
        out_shape=[jax.ShapeDtypeStruct((ATTN_WIDTH, n), BF16),
                   jax.ShapeDtypeStruct((ATTN_KV_HEADS, n, HEAD_DIM), BF16),
                   jax.ShapeDtypeStruct((ATTN_KV_HEADS, V_ROWS, n), BF16),
                   jax.ShapeDtypeStruct((ATTN_WIDTH, n), F32),
                   jax.ShapeDtypeStruct((n, MEM_WIDTH), BF16),
                   jax.ShapeDtypeStruct((SSM_WIDTH, n), F32),
                   jax.ShapeDtypeStruct((SSM_WIDTH, n), F32),
                   jax.ShapeDtypeStruct((3 * HYENA_WIDTH, n), F32),
                   jax.ShapeDtypeStruct((HYENA_WIDTH, n), F32)],
        compiler_params=_cparams(("parallel",)),
        name="inproj",
    )(x2, pre_g, w_tok, w_ch_t, cos, s_lo, s_hi, cos_t, sin_t, qg, kg, ones)


def _flash_kernel(qt_ref, k_ref, vt_ref, o_ref, acc_s, s_s, kmax_s, *, tq, tk, n_kv):
    w = ATTN_GROUP * tq
    q4t = jnp.concatenate([qt_ref[HEAD_DIM * i:HEAD_DIM * (i + 1), :] for i in range(ATTN_GROUP)], axis=1)

    def chunk(ref_slice, c):
        return ref_slice(pl.ds(pl.multiple_of(c * tk, tk), tk))

    k_chunk = lambda c: chunk(lambda d: k_ref[0, d, :], c)
    v_chunk = lambda c: chunk(lambda d: vt_ref[0, :, d], c)

    @pl.when(pl.program_id(2) == 0)
    def _():
        def key_norm(c, mx):
            kc = k_chunk(c).astype(F32)
            return jnp.maximum(mx, jnp.max(jnp.sum(kc * kc, axis=1, keepdims=True), axis=0, keepdims=True))

        mx = lax.fori_loop(0, n_kv, key_norm, jnp.zeros((1, 1), F32))
        kmax_s[...] = jnp.broadcast_to(jnp.sqrt(mx), kmax_s.shape)

    qf = q4t.astype(F32)
    bound = jnp.sqrt(jnp.sum(qf * qf, axis=0, keepdims=True)) * kmax_s[:, 0:1] * (1.0 + 2.0 ** -10)
    bound_max = jnp.max(bound)

    def finish():
        o = acc_s[0:HEAD_DIM] * (1.0 / acc_s[HEAD_DIM:HEAD_DIM + 1])
        for i in range(ATTN_GROUP):
            o_ref[HEAD_DIM * i:HEAD_DIM * (i + 1), :] = o[:, i * tq:(i + 1) * tq]

    @pl.when(bound_max <= SOFTMAX_STATIC_BOUND)
    def _():
        acc_s[...] = jnp.zeros(acc_s.shape, F32)

        def body(c, carry):
            s = jnp.dot(k_chunk(c), q4t, preferred_element_type=F32)
            p = jnp.exp2(s - bound).astype(BF16)
            acc_s[...] += jnp.dot(v_chunk(c), p, preferred_element_type=F32)
            return carry

        lax.fori_loop(0, n_kv, body, 0, unroll=8)
        finish()

    @pl.when(jnp.logical_not(bound_max <= SOFTMAX_STATIC_BOUND))
    def _():
        acc_s[...] = jnp.zeros(acc_s.shape, F32)

        def scores(c, slot):
            s = jnp.dot(k_chunk(c), q4t, preferred_element_type=F32)
            s_s[slot] = s
            return jnp.max(s, axis=0, keepdims=True)

        def consume(c, slot, m, m_chunk):
            m_new = jnp.maximum(m, m_chunk)
            p = jnp.exp2(s_s[slot] - m_new).astype(BF16)
            pv = jnp.dot(v_chunk(c), p, preferred_element_type=F32)
            acc_s[...] = jnp.exp2(m - m_new) * acc_s[...] + pv
            return m_new

        def body(cc, carry):
            m, mc0 = carry
            c = 2 * cc
            mc1 = scores(c + 1, 1)
            m = consume(c, 0, m, mc0)
            mc0 = scores(c + 2, 0)
            return consume(c + 1, 1, m, mc1), mc0

        m, mc0 = lax.fori_loop(0, n_kv // 2 - 1, body, (jnp.full((1, w), -jnp.inf, F32), scores(0, 0)), unroll=2)
        mc1 = scores(n_kv - 1, 1)
        consume(n_kv - 1, 1, consume(n_kv - 2, 0, m, mc0), mc1)
        finish()


def _flash_attention(q_t, k, v_t, batch, length):
    n = q_t.shape[1]
    tq, tk = ATTN_Q_TILE, ATTN_K_TILE
    nq, nk = length // tq, length // tk
    gw = ATTN_GROUP * HEAD_DIM
    return pl.pallas_call(
        functools.partial(_flash_kernel, tq=tq, tk=tk, n_kv=nk),
        grid=(batch, ATTN_KV_HEADS, nq),
        in_specs=[pl.BlockSpec((gw, tq), lambda b, h, i: (h, b * nq + i)),
                  pl.BlockSpec((1, length, HEAD_DIM), lambda b, h, i: (h, b, 0)),
                  pl.BlockSpec((1, V_ROWS, length), lambda b, h, i: (h, 0, b))],
        out_specs=pl.BlockSpec((gw, tq), lambda b, h, i: (h, b * nq + i)),
        out_shape=jax.ShapeDtypeStruct((ATTN_WIDTH, n), F32),
        scratch_shapes=[pltpu.VMEM((V_ROWS, ATTN_GROUP * tq), F32),
                        pltpu.VMEM((2, tk, ATTN_GROUP * tq), F32),
                        pltpu.VMEM((1, LANES), F32)],
        compiler_params=_cparams(("parallel", "parallel", "arbitrary")),
        name="flash_attn",
    )(q_t, k, v_t)


def _memkv_kernel(mem_ref, g_ref, w_ref, kv_ref):
    m = mem_ref[...]
    mn = m * lax.rsqrt(jnp.mean(m * m, axis=-1, keepdims=True) + EPS) * g_ref[...]
    kv_ref[...] = jnp.dot(mn.astype(BF16), w_ref[...], preferred_element_type=F32)


def _memkv(mem2, mem_g, w_kv):
    rows = mem2.shape[0]
    return pl.pallas_call(
        _memkv_kernel,
        grid=(rows // MEM_TOKENS,),
        in_specs=[pl.BlockSpec((MEM_TOKENS, D_MODEL), lambda i: (i, 0)),
                  pl.BlockSpec((1, D_MODEL), lambda i: (0, 0)),
                  pl.BlockSpec((D_MODEL, 2 * MEM_WIDTH), lambda i: (0, 0))],
        out_specs=pl.BlockSpec((MEM_TOKENS, 2 * MEM_WIDTH), lambda i: (i, 0)),
        out_shape=jax.ShapeDtypeStruct((rows, 2 * MEM_WIDTH), F32),
        compiler_params=_cparams(("parallel",)),
        name="mem_kv",
    )(mem2, mem_g, w_kv)


def _toeplitz_kernel(kf_ref, kb_ref, g_ref):
    t = SSM_CHUNK
    causal = lax.broadcasted_iota(jnp.int32, (t, t), 1) >= lax.broadcasted_iota(jnp.int32, (t, t), 0)

    def build(cp, carry):
        r0 = pl.multiple_of(cp * t, t)
        kf_rows, kb_rows = kf_ref[0, cp], kb_ref[0, cp]
        for c in range(SSM_GROUP):
            lo = pltpu.roll(jnp.broadcast_to(kf_rows[c:c + 1], (t, t)), 0, 1, stride=1, stride_axis=0)
            up = pltpu.roll(jnp.broadcast_to(kb_rows[c:c + 1], (t, t)), 0, 1, stride=1, stride_axis=0)
            g_ref[0, pl.ds(r0, t), c * t:(c + 1) * t] = jnp.where(causal, lo, up).astype(BF16)
        return carry

    lax.fori_loop(0, SSM_GROUP, build, 0)


def _toeplitz(kf, kb):
    gt = SSM_GROUP * SSM_CHUNK
    lag_spec = pl.BlockSpec((1, SSM_GROUP, SSM_GROUP, SSM_CHUNK), lambda g: (g, 0, 0, 0))
    return pl.pallas_call(
        _toeplitz_kernel,
        grid=(SSM_GROUPS,),
        in_specs=[lag_spec, lag_spec],
        out_specs=pl.BlockSpec((1, gt, gt), lambda g: (g, 0, 0)),
        out_shape=jax.ShapeDtypeStruct((SSM_GROUPS, gt, gt), BF16),
        compiler_params=_cparams(("parallel",)),
        name="ssm_toeplitz",
    )(kf, kb)


def _ssm_kernel(u_ref, g_ref, p_ref, q_ref, a1_ref, a2_ref, y_ref, s_s, x_s, h_s, *, batch, n_chunks):
    t = SSM_CHUNK
    rows = batch * n_chunks
    half = 2 * SSM_STATE
    u = jnp.concatenate([u_ref[0, c].reshape(rows, t) for c in range(SSM_GROUP)], axis=1).astype(BF16)
    y_intra = jnp.dot(u, g_ref[0], preferred_element_type=F32)
    s_all = jnp.dot(u, p_ref[0], preferred_element_type=F32)
    s_s[...] = s_all
    x_s[...] = jnp.concatenate([pltpu.roll(s_all[:, :half], SSM_STATE, 1),
                                pltpu.roll(s_all[:, half:], SSM_STATE, 1)], axis=1)
    a1f, a2f = a1_ref[0, :, :half], a2_ref[0, :, :half]
    a1b, a2b = a1_ref[0, :, half:], a2_ref[0, :, half:]

    sub = SUBLANES
    n_blocks = n_chunks // sub

    def step(kb, carry):
        new = []
        for b in range(batch):
            hf, gf, hb, gb = carry[4 * b:4 * b + 4]
            base_f = pl.multiple_of(b * n_chunks + kb * sub, sub)
            base_b = pl.multiple_of(b * n_chunks + (n_blocks - 1 - kb) * sub, sub)
            sf, xf = s_s[pl.ds(base_f, sub), 0:half], x_s[pl.ds(base_f, sub), 0:half]
            sb, xb = s_s[pl.ds(base_b, sub), half:2 * half], x_s[pl.ds(base_b, sub), half:2 * half]
            hf_rows, hb_rows = [], [None] * sub
            for i in range(sub):
                hf_rows.append(hf)
                hf, gf = a1f * hf + a2f * gf + sf[i:i + 1], a1f * gf - a2f * hf + xf[i:i + 1]
            for i in range(sub - 1, -1, -1):
                hb_rows[i] = hb
                hb, gb = a1b * hb + a2b * gb + sb[i:i + 1], a1b * gb - a2b * hb + xb[i:i + 1]
            h_s[pl.ds(base_f, sub), 0:half] = jnp.concatenate(hf_rows, axis=0)
            h_s[pl.ds(base_b, sub), half:2 * half] = jnp.concatenate(hb_rows, axis=0)
            new += [hf, gf, hb, gb]
        return tuple(new)

    zero = jnp.zeros((1, half), F32)
    lax.fori_loop(0, n_blocks, step, (zero,) * (4 * batch))
    y = y_intra + jnp.dot(h_s[...].astype(BF16), q_ref[0], preferred_element_type=F32)
    for c in range(SSM_GROUP):
        y_ref[0, c] = y[:, c * t:(c + 1) * t].reshape(batch, n_chunks, t)


def _ssm_scan(u_t, g_mat, p_mat, q_mat, a1, a2, batch, length):
    t = SSM_CHUNK
    nk = length // t
    gt = SSM_GROUP * t
    u5 = u_t.reshape(SSM_GROUPS, SSM_GROUP, batch, nk, t)
    blk = (1, SSM_GROUP, batch, nk, t)
    y5 = pl.pallas_call(
        functools.partial(_ssm_kernel, batch=batch, n_chunks=nk),
        grid=(SSM_GROUPS,),
        in_specs=[pl.BlockSpec(blk, lambda g: (g, 0, 0, 0, 0)),
                  pl.BlockSpec((1, gt, gt), lambda g: (g, 0, 0)),
                  pl.BlockSpec((1, gt, 4 * SSM_STATE), lambda g: (g, 0, 0)),
                  pl.BlockSpec((1, 4 * SSM_STATE, gt), lambda g: (g, 0, 0)),
                  pl.BlockSpec((1, 1, 4 * SSM_STATE), lambda g: (g, 0, 0)),
                  pl.BlockSpec((1, 1, 4 * SSM_STATE), lambda g: (g, 0, 0))],
        out_specs=pl.BlockSpec(blk, lambda g: (g, 0, 0, 0, 0)),
        out_shape=jax.ShapeDtypeStruct(u5.shape, F32),
        scratch_shapes=[pltpu.VMEM((batch * nk, 4 * SSM_STATE), F32)] * 3,
        compiler_params=_cparams(("parallel",)),
        name="ssm_scan",
    )(u5, g_mat, p_mat, q_mat, a1, a2)
    return y5.reshape(SSM_WIDTH, batch * length)


def _ssm_tables(a_re, a_im, log_step, b_re, b_im, c_re, c_im):
    t = SSM_CHUNK
    hi = lax.Precision.HIGHEST
    lam = lax.complex(a_re.astype(F32), a_im.astype(F32))
    step = jnp.exp(log_step.astype(F32))[..., None]
    ls = lam * step
    a_bar = jnp.exp(ls)
    b_bar = ((a_bar - 1.0) / lam)[..., None] * lax.complex(b_re.astype(F32), b_im.astype(F32))
    c = lax.complex(c_re.astype(F32), c_im.astype(F32))
    tau = jnp.arange(t + 1, dtype=F32)
    pw = jnp.exp(ls[..., None] * tau)
    kern = jnp.einsum('dgcp,dgpt,dgpe->dgtce', c, pw[..., :t], b_bar, precision=hi).real
    kf = kern[0].at[:, 0].add(kern[1][:, 0]).transpose(0, 3, 2, 1)
    kb = jnp.roll(kern[1][:, ::-1], 1, axis=1).transpose(0, 3, 2, 1)
    pw_f, pw_b = pw[0], pw[1]
    pf = pw_f[:, :, t - 1 - jnp.arange(t)][..., None] * b_bar[0][:, :, None, :]
    pb = pw_b[:, :, :t][..., None] * b_bar[1][:, :, None, :]
    to_rows = lambda z: z.transpose(0, 3, 2, 1).reshape(SSM_GROUPS, SSM_GROUP * t, SSM_STATE)
    p_mat = jnp.concatenate([to_rows(pf.real), to_rows(pf.imag), to_rows(pb.real), to_rows(pb.imag)], axis=-1)
    qf = c[0].transpose(0, 2, 1)[..., None] * pw_f[:, :, 1:][:, :, None, :]
    qb = c[1].transpose(0, 2, 1)[..., None] * pw_b[:, :, t - jnp.arange(t)][:, :, None, :]
    to_cols = lambda z: z.reshape(SSM_GROUPS, SSM_STATE, SSM_GROUP * t)
    q_mat = jnp.concatenate([to_cols(qf.real), -to_cols(qf.imag), to_cols(qb.real), -to_cols(qb.imag)], axis=1)
    at = pw[..., t]
    a1 = jnp.concatenate([at[0].real, at[0].real, at[1].real, at[1].real], axis=-1)[:, None, :]
    a2 = jnp.concatenate([-at[0].imag, at[0].imag, -at[1].imag, at[1].imag], axis=-1)[:, None, :]
    return _toeplitz(kf, kb), p_mat.astype(BF16), q_mat.astype(BF16), a1, a2


def _dft_tables(length):
    j = DFT_INNER
    n = 2 * length
    rn = n // j
    rh = rn // 2
    odd = 2 * jnp.arange(rh, dtype=jnp.int32) + 1
    r = jnp.arange(rn, dtype=jnp.int32)
    ang1 = (math.pi / rn) * ((odd[:, None] * r[None, :]) % (2 * rn)).astype(F32)
    c1, s1 = jnp.cos(ang1), jnp.sin(ang1)
    f1_full = jnp.concatenate([c1, -s1], axis=0)
    f1_top = f1_full[:, :rh]
    f1_inv = jnp.concatenate([c1[:, :rh].T, -s1[:, :rh].T], axis=1)
    jj = jnp.arange(j, dtype=jnp.int32)
    angt = (math.pi / n) * (odd[:, None] * jj[None, :]).astype(F32)
    tw_re, tw_im = jnp.cos(angt), -jnp.sin(angt)
    ang2 = (2.0 * math.pi / j) * ((jj[:, None] * jj[None, :]) % j).astype(F32)
    c2, s2 = jnp.cos(ang2), jnp.sin(ang2)
    f2_fwd = jnp.block([[c2, -s2], [s2, c2]])
    f2_inv = jnp.block([[c2, s2], [-s2, c2]])
    return dict(f1_full=f1_full.astype(BF16), f1_top=f1_top.astype(BF16), f1_inv=f1_inv.astype(BF16),
                tw_re=tw_re, tw_im=tw_im, f2_fwd=f2_fwd.astype(BF16), f2_inv=f2_inv.astype(BF16), rn=rn, rh=rh)


def _twiddle_rows(a, tw_re, tw_im, rk):
    a_re, a_im = a[:rk], a[rk:]
    return jnp.concatenate([a_re * tw_re - a_im * tw_im, a_re * tw_im + a_im * tw_re], axis=1)


def _filter_hidden_kernel(bands_ref, w1t_ref, w1c_ref, w1s_ref, b1_ref, w2t_ref, b2_ref, hid_ref, *, length, tl):
    hi = lax.Precision.HIGHEST
    base = pl.program_id(0) * tl
    idx = (lax.broadcasted_iota(jnp.int32, (1, tl), 1) + base).astype(F32)
    for d in range(2):
        pos = idx if d == 0 else float(length) - idx
        tt = pos / float(length)
        wpos = (2.0 * math.pi / length) * pos
        arg = bands_ref[...] * wpos
        h1 = (w1t_ref[...] * tt
              + jnp.dot(w1c_ref[...], jnp.cos(arg), preferred_element_type=F32, precision=hi)
              - jnp.dot(w1s_ref[...], jnp.sin(arg), preferred_element_type=F32, precision=hi))
        h1 = jnp.sin(h1 + b1_ref[...])
        h2 = jnp.dot(w2t_ref[...], h1, preferred_element_type=F32, precision=hi)
        hid_ref[d] = jnp.sin(h2 + b2_ref[...]).astype(hid_ref.dtype)


def _filter_taps_kernel(hid_ref, w3t_ref, decay_ref, filt_ref, *, length, tl):
    d = pl.program_id(0) % 2
    base = pl.program_id(1) * tl
    idx = (lax.broadcasted_iota(jnp.int32, (1, tl), 1) + base).astype(F32)
    pos = jnp.where(d == 0, idx, float(length) - idx)
    tt = pos / float(length)
    f = jnp.dot(w3t_ref[...], hid_ref[0], preferred_element_type=F32)
    f = f * jnp.exp(-tt * decay_ref[...])
    f = jnp.where(jnp.logical_and(d == 1, idx == 0.0), 0.0, f)
    filt_ref[...] = f.astype(filt_ref.dtype)


def _filter_dft_kernel(filt_ref, f1_ref, twre_ref, twim_ref, f2_ref, kf_ref, *, rk, cb):
    for c in range(cb):
        r = jnp.concatenate([filt_ref[0, 0, c], -filt_ref[0, 1, c]], axis=0)
        a = jnp.dot(f1_ref[...], r, preferred_element_type=F32)
        ap = _twiddle_rows(a, twre_ref[...], twim_ref[...], rk).astype(BF16)
        x = jnp.dot(ap, f2_ref[...], preferred_element_type=F32)
        kf_ref[0, c, 0] = x[:, :DFT_INNER].astype(kf_ref.dtype)
        kf_ref[0, c, 1] = x[:, DFT_INNER:].astype(kf_ref.dtype)


def _hyena_filters(length, tabs, w1, b1, w2, b2, w3, log_decay):
    tl = min(FILTER_LANE_TILE, length)
    nl = length // tl
    rn, rh, j = tabs["rn"], tabs["rh"], DFT_INNER
    bands = jnp.linspace(1e-4, FILTER_BANDS - 1, FILTER_BANDS, dtype=F32)[:, None]
    w1f = w1.astype(F32)
    const = lambda shape: pl.BlockSpec(shape, lambda *_: (0,) * len(shape))
    hid = pl.pallas_call(
        functools.partial(_filter_hidden_kernel, length=length, tl=tl),
        grid=(nl,),
        in_specs=[const((FILTER_BANDS, 1)), const((FILTER_HIDDEN, 1)), const((FILTER_HIDDEN, FILTER_BANDS)),
                  const((FILTER_HIDDEN, FILTER_BANDS)), const((FILTER_HIDDEN, 1)),
                  const((FILTER_HIDDEN, FILTER_HIDDEN)), const((FILTER_HIDDEN, 1))],
        out_specs=pl.BlockSpec((2, FILTER_HIDDEN, tl), lambda i: (0, 0, i)),
        out_shape=jax.ShapeDtypeStruct((2, FILTER_HIDDEN, length), BF16),
        compiler_params=_cparams(("parallel",)),
        name="hyena_filter_hidden",
    )(bands, w1f[0:1].T, w1f[1:1 + FILTER_BANDS].T, w1f[1 + FILTER_BANDS:].T, b1.astype(F32)[:, None],
      w2.astype(F32).T, b2.astype(F32)[:, None])
    n_od = HYENA_ORDER * 2
    w3t = w3.T.reshape(n_od, HYENA_WIDTH, FILTER_HIDDEN).astype(BF16)
    decay = jnp.exp(log_decay.astype(F32)).reshape(n_od, HYENA_WIDTH, 1)
    filt = pl.pallas_call(
        functools.partial(_filter_taps_kernel, length=length, tl=tl),
        grid=(n_od, nl),
        in_specs=[pl.BlockSpec((1, FILTER_HIDDEN, tl), lambda od, i: (od % 2, 0, i)),
                  pl.BlockSpec((None, HYENA_WIDTH, FILTER_HIDDEN), lambda od, i: (od, 0, 0)),
                  pl.BlockSpec((None, HYENA_WIDTH, 1), lambda od, i: (od, 0, 0))],
        out_specs=pl.BlockSpec((None, HYENA_WIDTH, tl), lambda od, i: (od, 0, i)),
        out_shape=jax.ShapeDtypeStruct((n_od, HYENA_WIDTH, length), BF16),
        compiler_params=_cparams(("parallel", "parallel")),
        name="hyena_filter_taps",
    )(hid, w3t, decay)
    filt6 = filt.reshape(HYENA_ORDER, 2, HYENA_WIDTH, rh, j)
    cb = HYENA_CH_BLOCK
    return pl.pallas_call(
        functools.partial(_filter_dft_kernel, rk=rh, cb=cb),
        grid=(HYENA_ORDER, HYENA_WIDTH // cb),
        in_specs=[pl.BlockSpec((1, 2, cb, rh, j), lambda o, c: (o, 0, c, 0, 0)),
                  const((rn, rn)), const((rh, j)), const((rh, j)), const((2 * j, 2 * j))],
        out_specs=pl.BlockSpec((1, cb, 2, rh, j), lambda o, c: (o, c, 0, 0, 0)),
        out_shape=jax.ShapeDtypeStruct((HYENA_ORDER, HYENA_WIDTH, 2, rh, j), BF16),
        compiler_params=_cparams(("parallel", "parallel")),
        name="hyena_filter_dft",
    )(filt6, tabs["f1_full"], tabs["tw_re"], tabs["tw_im"], tabs["f2_fwd"])


def _hyena_kernel(scw_ref, scb_ref, hb_ref, x_ref, hg_ref, kf_ref, f1_ref, f1i_ref, twre_ref, twim_ref, f2_ref,
                  f2i_ref, o_ref, *, batch, rn, rh, cb, cs):
    j = DFT_INNER
    n_inv = 2.0 / (rn * j)
    rk = rh
    lane = lax.broadcasted_iota(jnp.int32, (rh, j), 1)
    row = lax.broadcasted_iota(jnp.int32, (rh, j), 0)
    first = jnp.logical_and(lane == 0, row == 0)
    last = jnp.logical_and(lane == j - 1, row == rh - 1)
    tw_re, tw_im = twre_ref[...], twim_ref[...]

    def prev_t(x):
        p = pltpu.roll(x, 1, 1)
        return jnp.where(first, 0.0, jnp.where(lane == 0, pltpu.roll(p, 1, 0), p))

    def next_t(x):
        p = pltpu.roll(x, j - 1, 1)
        return jnp.where(last, 0.0, jnp.where(lane == j - 1, pltpu.roll(p, rh - 1, 0), p))

    def split(m, i):
        return m[rk * i:rk * (i + 1), :j], m[rk * i:rk * (i + 1), j:]

    def long_conv(xs, ks):
        rows = []
        for x in xs:
            a = jnp.dot(f1_ref[...], x.astype(BF16), preferred_element_type=F32)
            rows.append(_twiddle_rows(a, tw_re, tw_im, rk))
        m = jnp.dot(jnp.concatenate(rows, axis=0).astype(BF16), f2_ref[...], preferred_element_type=F32)
        rows = []
        for i, (k_re, k_im) in enumerate(ks):
            x_re, x_im = split(m, i)
            rows.append(jnp.concatenate([x_re * k_re - x_im * k_im, x_re * k_im + x_im * k_re], axis=1))
        m = jnp.dot(jnp.concatenate(rows, axis=0).astype(BF16), f2i_ref[...], preferred_element_type=F32)
        outs = []
        for i in range(len(xs)):
            b_re, b_im = split(m, i)
            bp = jnp.concatenate([b_re * tw_re + b_im * tw_im, b_im * tw_re - b_re * tw_im], axis=0).astype(BF16)
            outs.append(jnp.dot(f1i_ref[...], bp, preferred_element_type=F32) * n_inv)
        return outs

    cbase = pl.program_id(0) * cb
    for c0 in range(0, cb, cs):
        chans = [(ci, b) for ci in range(c0, c0 + cs) for b in range(batch)]
        segs = []
        for sgm in range(3):
            seg = []
            for ci, b in chans:
                chs = sgm * HYENA_WIDTH + cbase + ci
                x = x_ref[sgm, ci, b]
                seg.append(prev_t(x) * scw_ref[chs] + x * scw_ref[3 * HYENA_WIDTH + chs]
                           + next_t(x) * scw_ref[6 * HYENA_WIDTH + chs] + scb_ref[chs])
            segs.append(seg)
        z = segs[0]
        for o in range(HYENA_ORDER):
            conv = long_conv(z, [(kf_ref[o, ci, 0].astype(F32), kf_ref[o, ci, 1].astype(F32)) for ci, _ in chans])
            z = [segs[o + 1][i] * (conv[i] + z[i] * hb_ref[o * HYENA_WIDTH + cbase + ci])
                 for i, (ci, _) in enumerate(chans)]
        for i, (ci, b) in enumerate(chans):
            o_ref[ci, b] = z[i] * hg_ref[ci, b]


def _hyena(hi_t, hg_t, kf, tabs, short_w, short_b, hy_bias, batch, length):
    rn, rh, j = tabs["rn"], tabs["rh"], DFT_INNER
    cb = HYENA_CH_BLOCK
    x5 = hi_t.reshape(3, HYENA_WIDTH, batch, rh, j)
    g4 = hg_t.reshape(HYENA_WIDTH, batch, rh, j)
    smem = pl.BlockSpec(memory_space=pltpu.SMEM)
    const = lambda shape: pl.BlockSpec(shape, lambda c: (0,) * len(shape))
    out = pl.pallas_call(
        functools.partial(_hyena_kernel, batch=batch, rn=rn, rh=rh, cb=cb, cs=HYENA_CH_SUB),
        grid=(HYENA_WIDTH // cb,),
        in_specs=[smem, smem, smem,
                  pl.BlockSpec((3, cb, batch, rh, j), lambda c: (0, c, 0, 0, 0)),
                  pl.BlockSpec((cb, batch, rh, j), lambda c: (c, 0, 0, 0)),
                  pl.BlockSpec((HYENA_ORDER, cb, 2, rh, j), lambda c: (0, c, 0, 0, 0)),
                  const((rn, rh)), const((rh, rn)), const((rh, j)), const((rh, j)),
                  const((2 * j, 2 * j)), const((2 * j, 2 * j))],
        out_specs=pl.BlockSpec((cb, batch, rh, j), lambda c: (c, 0, 0, 0)),
        out_shape=jax.ShapeDtypeStruct(g4.shape, F32),
        compiler_params=_cparams(("parallel",)),
        name="hyena_conv",
    )(short_w.astype(F32).reshape(-1), short_b.astype(F32), hy_bias.astype(F32).reshape(-1),
      x5, g4, kf, tabs["f1_top"], tabs["f1_inv"], tabs["tw_re"], tabs["tw_im"], tabs["f2_fwd"], tabs["f2_inv"])
    return out.reshape(HYENA_WIDTH, batch * length)


def _rms_rows(x, g):
    return x * lax.rsqrt(jnp.mean(x * x, axis=-1, keepdims=True) + EPS) * g


def _rms_cols(x, g):
    return x * lax.rsqrt(jnp.mean(x * x, axis=0, keepdims=True) + EPS) * g


def _post_kernel(ao_ref, ga_ref, mq_ref, mk_ref, mv_ref, y_ref, u_ref, sg_ref, hy_ref, x_ref,
                 d_ref, wglu_ref, g_attn_ref, g_ssm_ref, g_hy_ref, g_mem_ref, post_g_ref, wo_ref, out_ref):
    attn_n = _rms_cols(ao_ref[...] * ga_ref[...], g_attn_ref[...])
    s = jnp.dot(mq_ref[...], mk_ref[0], preferred_element_type=F32)
    ps = []
    for h in range(MEM_HEADS):
        sh = s[:, MEM_TOKENS * h:MEM_TOKENS * (h + 1)]
        e = jnp.exp(sh - jnp.max(sh, axis=-1, keepdims=True))
        ps.append(e * (1.0 / jnp.sum(e, axis=-1, keepdims=True)))
    p = jnp.concatenate(ps, axis=1).astype(BF16)
    cross_n = _rms_rows(jnp.dot(p, mv_ref[0], preferred_element_type=F32), g_mem_ref[...])
    y = y_ref[...] + d_ref[...] * u_ref[...]
    g = y * (0.5 * (1.0 + jnp.tanh(math.sqrt(2.0 / math.pi) * (y + 0.044715 * (y * y * y)))))
    gz = jnp.dot(wglu_ref[...], g.astype(BF16), preferred_element_type=F32)
    ssm_n = _rms_cols(g * _sigmoid(gz) * sg_ref[...], g_ssm_ref[...])
    hy_n = _rms_cols(hy_ref[...], g_hy_ref[...])
    o1, o2, o3 = ATTN_WIDTH, ATTN_WIDTH + SSM_WIDTH, ATTN_WIDTH + SSM_WIDTH + HYENA_WIDTH
    mixed = (jnp.dot(attn_n.T.astype(BF16), wo_ref[0:o1], preferred_element_type=F32)
             + jnp.dot(ssm_n.T.astype(BF16), wo_ref[o1:o2], preferred_element_type=F32)
             + jnp.dot(hy_n.T.astype(BF16), wo_ref[o2:o3], preferred_element_type=F32)
             + jnp.dot(cross_n.astype(BF16), wo_ref[o3:], preferred_element_type=F32))
    out_ref[...] = x_ref[...] + _rms_rows(mixed, post_g_ref[...])


def _post(ao, ga, mq, mk_bd, mv_bd, y_t, u_t, sg_t, hy_t, x2, d, wglu_t, g_attn, g_ssm, g_hy, g_mem, post_g, wo,
          length):
    n = x2.shape[0]
    tm = TOKEN_TILE
    tiles_per_seq = length // tm
    tok_spec = lambda w: pl.BlockSpec((tm, w), lambda i: (i, 0))
    ch_spec = lambda w: pl.BlockSpec((w, tm), lambda i: (0, i))
    const = lambda shape: pl.BlockSpec(shape, lambda i: (0,) * len(shape))
    hm = MEM_HEADS * MEM_TOKENS
    return pl.pallas_call(
        _post_kernel,
        grid=(n // tm,),
        in_specs=[ch_spec(ATTN_WIDTH), ch_spec(ATTN_WIDTH), tok_spec(MEM_WIDTH),
                  pl.BlockSpec((1, MEM_WIDTH, hm), lambda i: (i // tiles_per_seq, 0, 0)),
                  pl.BlockSpec((1, hm, MEM_WIDTH), lambda i: (i // tiles_per_seq, 0, 0)),
                  ch_spec(SSM_WIDTH), ch_spec(SSM_WIDTH), ch_spec(SSM_WIDTH), ch_spec(HYENA_WIDTH),
                  tok_spec(D_MODEL),
                  const((SSM_WIDTH, 1)), const((SSM_WIDTH, SSM_WIDTH)),
                  const((ATTN_WIDTH, 1)), const((SSM_WIDTH, 1)), const((HYENA_WIDTH, 1)), const((1, MEM_WIDTH)),
                  const((1, D_MODEL)), const((MIX_WIDTH, D_MODEL))],
        out_specs=tok_spec(D_MODEL),
        out_shape=jax.ShapeDtypeStruct((n, D_MODEL), F32),
        compiler_params=_cparams(("parallel",)),
        name="post",
    )(ao, ga, mq, mk_bd, mv_bd, y_t, u_t, sg_t, hy_t, x2, d, wglu_t, g_attn, g_ssm, g_hy, g_mem, post_g, wo)


def _rope_tables(length):
    rows = length // GRID_W
    row = jnp.broadcast_to(jnp.arange(rows, dtype=F32)[:, None], (rows, GRID_W)).reshape(length)
    col = jnp.broadcast_to(jnp.arange(GRID_W, dtype=F32)[None, :], (rows, GRID_W)).reshape(length)
    inv_freq = ROPE_THETA ** (-jnp.arange(ROPE_FREQS, dtype=F32) / ROPE_FREQS)
    ang = jnp.stack([row[:, None] * inv_freq, col[:, None] * inv_freq], axis=1)
    ang = jnp.broadcast_to(ang[:, :, None, :], (length, 2, 2, ROPE_FREQS)).reshape(length, HEAD_DIM)
    cos1, sin1 = jnp.cos(ang), jnp.sin(ang)
    low1 = (jnp.arange(HEAD_DIM) % (2 * ROPE_FREQS)) < ROPE_FREQS
    cos_t, sin_t = cos1.T, jnp.where(low1, -sin1, sin1).T
    cos = jnp.concatenate([cos1, cos1], axis=1)
    sin = jnp.concatenate([sin1, sin1], axis=1)
    low = jnp.concatenate([low1, low1])
    return cos, jnp.where(low, -sin, 0.0), jnp.where(low, 0.0, sin), cos_t, sin_t


def _block_diag_heads(mk, mv, batch):
    mk4 = mk.reshape(batch, MEM_TOKENS, MEM_HEADS, HEAD_DIM)
    mv4 = mv.reshape(batch, MEM_TOKENS, MEM_HEADS, HEAD_DIM)
    eye = jnp.eye(MEM_HEADS, dtype=mk.dtype)
    k_bd = jnp.einsum('bmhd,hg->bhdgm', mk4, eye).reshape(batch, MEM_WIDTH, MEM_HEADS * MEM_TOKENS)
    v_bd = jnp.einsum('bmhd,hg->bhmgd', mv4, eye).reshape(batch, MEM_HEADS * MEM_TOKENS, MEM_WIDTH)
    return k_bd.astype(BF16), v_bd.astype(BF16)


def _layer_weights(layer, p):
    w_in = p["w_in"][layer]
    a, kv = ATTN_WIDTH, KV_WIDTH
    k0, v0, g0, mq0 = a, a + kv, a + 2 * kv, w_in.shape[1] - MEM_WIDTH
    w_tok = jnp.concatenate([w_in[:, k0:v0], w_in[:, mq0:]], axis=1).astype(BF16)
    w_ch_t = jnp.concatenate([w_in[:, :k0], w_in[:, v0:mq0]], axis=1).T.astype(BF16)
    bg = p["branch_norm"][layer].astype(F32)
    o1, o2, o3 = a, a + SSM_WIDTH, a + SSM_WIDTH + HYENA_WIDTH
    head_id = jnp.arange(kv) // HEAD_DIM
    return dict(
        w_tok=w_tok, w_ch_t=w_ch_t,
        pre_g=p["pre_norm"][layer].astype(F32)[None, :], post_g=p["post_norm"][layer].astype(F32)[None, :],
        qg=p["q_norm"][layer].astype(F32)[:, None],
        kg=jnp.tile(p["k_norm"][layer].astype(F32), ATTN_KV_HEADS)[None, :],
        ones=(head_id[:, None] == head_id[None, :]).astype(BF16),
        mem_g=p["mem_norm"][layer].astype(F32)[None, :], w_mem_kv=p["w_mem_kv"][layer].astype(BF16),
        ssm=_ssm_tables(p["ssm_a_re"][layer], p["ssm_a_im"][layer], p["ssm_log_step"][layer], p["ssm_b_re"][layer],
                        p["ssm_b_im"][layer], p["ssm_c_re"][layer], p["ssm_c_im"][layer]),
        d=p["ssm_d"][layer].astype(F32)[:, None], wglu_t=p["ssm_w_glu"][layer].T.astype(BF16),
        g_attn=bg[:o1, None], g_ssm=bg[o1:o2, None], g_hy=bg[o2:o3, None], g_mem=bg[None, o3:],
        wo=p["w_out"][layer].astype(BF16),
    )


def _mixer_layer(x2, mem2, lw, kf, tabs, rope, p, layer, batch, length):
    q_t, k, v_t, ga, mq, su_t, sg_t, hi_t, hg_t = _inproj(x2, lw["pre_g"], lw["w_tok"], lw["w_ch_t"], rope,
                                                          lw["qg"], lw["kg"], lw["ones"], length)
    ao = _flash_attention(q_t, k, v_t, batch, length)
    mem_kv = _memkv(mem2, lw["mem_g"], lw["w_mem_kv"])
    mk_bd, mv_bd = _block_diag_heads(mem_kv[:, :MEM_WIDTH], mem_kv[:, MEM_WIDTH:], batch)
    y_t = _ssm_scan(su_t, *lw["ssm"], batch, length)
    hy_t = _hyena(hi_t, hg_t, kf, tabs, p["hyena_short_w"][layer], p["hyena_short_b"][layer],
                  p["hyena_bias"][layer], batch, length)
    return _post(ao, ga, mq, mk_bd, mv_bd, y_t, su_t, sg_t, hy_t, x2, lw["d"], lw["wglu_t"],
                 lw["g_attn"], lw["g_ssm"], lw["g_hy"], lw["g_mem"], lw["post_g"], lw["wo"], length)


def _run_group(x, mem, weights, p):
    batch, length, _ = x.shape
    rope = _rope_tables(length)
    tabs = _dft_tables(length)
    x2 = x.reshape(batch * length, D_MODEL)
    mem2 = mem.reshape(batch * MEM_TOKENS, D_MODEL)
    for layer in range(DEPTH):
        kf = _hyena_filters(length, tabs, p["hyena_ffn_w1"][layer], p["hyena_ffn_b1"][layer],
                            p["hyena_ffn_w2"][layer], p["hyena_ffn_b2"][layer], p["hyena_ffn_w3"][layer],
                            p["hyena_log_decay"][layer])
        x2 = _mixer_layer(x2, mem2, weights[layer], kf, tabs, rope, p, layer, batch, length)
    return x2.reshape(batch, length, D_MODEL)


def kernel(x_prompt, x_sample, mem_prompt, mem_sample, pre_norm, post_norm, w_in, q_norm, k_norm, mem_norm, w_mem_kv, ssm_a_re, ssm_a_im, ssm_log_step, ssm_b_re, ssm_b_im, ssm_c_re, ssm_c_im, ssm_d, ssm_w_glu, hyena_short_w, hyena_short_b, hyena_ffn_w1, hyena_ffn_b1, hyena_ffn_w2, hyena_ffn_b2, hyena_ffn_w3, hyena_log_decay, hyena_bias, branch_norm, w_out):
    p = dict(pre_norm=pre_norm, post_norm=post_norm, w_in=w_in, q_norm=q_norm, k_norm=k_norm, mem_norm=mem_norm,
             w_mem_kv=w_mem_kv, ssm_a_re=ssm_a_re, ssm_a_im=ssm_a_im, ssm_log_step=ssm_log_step, ssm_b_re=ssm_b_re,
             ssm_b_im=ssm_b_im, ssm_c_re=ssm_c_re, ssm_c_im=ssm_c_im, ssm_d=ssm_d, ssm_w_glu=ssm_w_glu,
             hyena_short_w=hyena_short_w, hyena_short_b=hyena_short_b, hyena_ffn_w1=hyena_ffn_w1,
             hyena_ffn_b1=hyena_ffn_b1, hyena_ffn_w2=hyena_ffn_w2, hyena_ffn_b2=hyena_ffn_b2,
             hyena_ffn_w3=hyena_ffn_w3, hyena_log_decay=hyena_log_decay, hyena_bias=hyena_bias,
             branch_norm=branch_norm, w_out=w_out)
    weights = [_layer_weights(layer, p) for layer in range(DEPTH)]
    return (_run_group(x_prompt, mem_prompt, weights, p), _run_group(x_sample, mem_sample, weights, p))
```

```python
import functools
import math

import jax
import jax.numpy as jnp
import numpy as np
from jax import lax
from jax.experimental import pallas as pl
from jax.experimental.pallas import tpu as pltpu

F32 = jnp.float32
BF16 = jnp.bfloat16

D_MODEL = 1024
DEPTH = 2
GRID_W = 64
HEAD_DIM = 64
ATTN_HEADS = 8
ATTN_KV_HEADS = 2
ATTN_GROUP = ATTN_HEADS // ATTN_KV_HEADS
ATTN_WIDTH = ATTN_HEADS * HEAD_DIM
KV_WIDTH = ATTN_KV_HEADS * HEAD_DIM
ROPE_THETA = 10000.0
ROPE_FREQS = HEAD_DIM // 4
SSM_GROUP = 16
SSM_GROUPS = 24
SSM_WIDTH = SSM_GROUP * SSM_GROUPS
SSM_STATE = 64
HYENA_WIDTH = 384
HYENA_ORDER = 2
FILTER_BANDS = 16
FILTER_HIDDEN = 64
MEM_TOKENS = 256
MEM_HEADS = 4
MEM_WIDTH = MEM_HEADS * HEAD_DIM
MIX_WIDTH = ATTN_WIDTH + SSM_WIDTH + HYENA_WIDTH + MEM_WIDTH
EPS = 1e-6

TOK_WIDTH = KV_WIDTH + MEM_WIDTH
CH_WIDTH = 2 * ATTN_WIDTH + KV_WIDTH + 2 * SSM_WIDTH + (HYENA_ORDER + 2) * HYENA_WIDTH
V_ROWS = HEAD_DIM + 16
Q_SCALE = HEAD_DIM ** -0.5 * math.log2(math.e)
SOFTMAX_STATIC_BOUND = 60.0

LANES = 128
SUBLANES = 8
VMEM_LIMIT = 56 * 1024 * 1024
TOKEN_TILE = 512
ATTN_Q_TILE = 256
ATTN_K_TILE = 512
SSM_CHUNK = LANES
DFT_INNER = 256
HYENA_CH_BLOCK = 8
HYENA_CH_SUB = 8
FILTER_LANE_TILE = 2048


def _cparams(sem):
    return pltpu.CompilerParams(dimension_semantics=sem, vmem_limit_bytes=VMEM_LIMIT)


def _silu(x):
    return x * (1.0 / (1.0 + jnp.exp(-x)))


def _sigmoid(x):
    return 1.0 / (1.0 + jnp.exp(-x))


def _nt_dot(a, b):
    return lax.dot_general(a, b, (((1,), (1,)), ((), ())), preferred_element_type=F32)


def _rope_128(xn, cos, s_lo, s_hi):
    outs = []
    for c in range(xn.shape[1] // LANES):
        xc = xn[:, LANES * c:LANES * (c + 1)]
        outs.append(xc * cos + pltpu.roll(xc, LANES - ROPE_FREQS, 1) * s_lo + pltpu.roll(xc, ROPE_FREQS, 1) * s_hi)
    return outs[0] if len(outs) == 1 else jnp.concatenate(outs, axis=1)


def _inproj_kernel(x_ref, pre_g_ref, wtok_ref, wch_ref, cos_ref, slo_ref, shi_ref, cost_ref, sint_ref,
                   qg_ref, kg_ref, ones_ref,
                   qt_ref, k_ref, vt_ref, ga_ref, mq_ref, su_ref, sg_ref, hi_ref, hg_ref):
    tm = x_ref.shape[0]
    x = x_ref[...]
    h = x * lax.rsqrt(jnp.mean(x * x, axis=-1, keepdims=True) + EPS) * pre_g_ref[...]
    hb = h.astype(BF16)
    tok = jnp.dot(hb, wtok_ref[...], preferred_element_type=F32)
    k = tok[:, 0:KV_WIDTH]
    mq = tok[:, KV_WIDTH:]
    k_ms = jnp.dot((k * k).astype(BF16), ones_ref[...], preferred_element_type=F32) * (1.0 / HEAD_DIM)
    kn = k * lax.rsqrt(k_ms + EPS) * kg_ref[...]
    kr = _rope_128(kn, cos_ref[...], slo_ref[...], shi_ref[...])
    for j in range(ATTN_KV_HEADS):
        k_ref[j] = kr[:, HEAD_DIM * j:HEAD_DIM * (j + 1)].astype(BF16)
    mq_ref[...] = (mq * (HEAD_DIM ** -0.5)).astype(BF16)
    ch = _nt_dot(wch_ref[...], hb)
    q3 = ch[0:ATTN_WIDTH].reshape(ATTN_HEADS, HEAD_DIM, tm)
    qn = q3 * lax.rsqrt(jnp.mean(q3 * q3, axis=1, keepdims=True) + EPS) * qg_ref[...][None]
    f = ROPE_FREQS
    rot = jnp.concatenate([qn[:, f:2 * f], qn[:, 0:f], qn[:, 3 * f:4 * f], qn[:, 2 * f:3 * f]], axis=1)
    qr = (qn * cost_ref[...][None] + rot * sint_ref[...][None]) * Q_SCALE
    qt_ref[...] = qr.reshape(ATTN_WIDTH, tm).astype(BF16)
    o = ATTN_WIDTH
    ones_row = (lax.broadcasted_iota(jnp.int32, (V_ROWS - HEAD_DIM, tm), 0) == 0).astype(BF16)
    for j in range(ATTN_KV_HEADS):
        vt_ref[j, 0:HEAD_DIM] = ch[o + HEAD_DIM * j:o + HEAD_DIM * (j + 1)].astype(BF16)
        vt_ref[j, HEAD_DIM:V_ROWS] = ones_row
    o += KV_WIDTH
    ga_ref[...] = _silu(ch[o:o + ATTN_WIDTH])
    o += ATTN_WIDTH
    su_ref[...] = ch[o:o + SSM_WIDTH]
    sg_ref[...] = _silu(ch[o + SSM_WIDTH:o + 2 * SSM_WIDTH])
    o += 2 * SSM_WIDTH
    hi_ref[...] = ch[o:o + 3 * HYENA_WIDTH]
    hg_ref[...] = _silu(ch[o + 3 * HYENA_WIDTH:])


def _inproj(x2, pre_g, w_tok, w_ch_t, rope, qg, kg, ones, length):
    cos, s_lo, s_hi, cos_t, sin_t = rope
    n = x2.shape[0]
    tm = TOKEN_TILE
    nt = n // tm
    tiles_per_seq = length // tm
    tok_spec = lambda w: pl.BlockSpec((tm, w), lambda i: (i, 0))
    ch_spec = lambda w: pl.BlockSpec((w, tm), lambda i: (0, i))
    const = lambda shape: pl.BlockSpec(shape, lambda i: (0,) * len(shape))
    pos_spec = pl.BlockSpec((tm, LANES), lambda i: (i % tiles_per_seq, 0))
    pos_t_spec = pl.BlockSpec((HEAD_DIM, tm), lambda i: (0, i % tiles_per_seq))
    return pl.pallas_call(
        _inproj_kernel,
        grid=(nt,),
        in_specs=[tok_spec(D_MODEL), const((1, D_MODEL)), const((D_MODEL, TOK_WIDTH)), const((CH_WIDTH, D_MODEL)),
                  pos_spec, pos_spec, pos_spec, pos_t_spec, pos_t_spec,
                  const((HEAD_DIM, 1)), const((1, KV_WIDTH)), const((KV_WIDTH, KV_WIDTH))],
        out_specs=[ch_spec(ATTN_WIDTH),
                   pl.BlockSpec((ATTN_KV_HEADS, tm, HEAD_DIM), lambda i: (0, i, 0)),
                   pl.BlockSpec((ATTN_KV_HEADS, V_ROWS, tm), lambda i: (0, 0, i)),
                   ch_spec(ATTN_WIDTH), tok_spec(MEM_WIDTH),
                   ch_spec(SSM_WIDTH), ch_spec(SSM_WIDTH), ch_spec(3 * HYENA_WIDTH), ch_spec(HYENA_WIDTH)],
        out_shape=[jax.ShapeDtypeStruct((ATTN_WIDTH, n), BF16),
                   jax.ShapeDtypeStruct((ATTN_KV_HEADS, n, HEAD_DIM), BF16),
                   jax.ShapeDtypeStruct((ATTN_KV_HEADS, V_ROWS, n), BF16),
                   jax.ShapeDtypeStruct((ATTN_WIDTH, n), F32),
                   jax.ShapeDtypeStruct((n, MEM_WIDTH), BF16),
                   jax.ShapeDtypeStruct((SSM_WIDTH, n), F32),
                   jax.ShapeDtypeStruct((SSM_WIDTH, n), F32),
                   jax.ShapeDtypeStruct((3 * HYENA_WIDTH, n), F32),
                   jax.ShapeDtypeStruct((HYENA_WIDTH, n), F32)],
        compiler_params=_cparams(("parallel",)),
        name="inproj",
    )(x2, pre_g, w_tok, w_ch_t, cos, s_lo, s_hi, cos_t, sin_t, qg, kg, ones)


def _flash_kernel(qt_ref, k_ref, vt_ref, o_ref, acc_s, s_s, kmax_s, *, tq, tk, n_kv):
    w = ATTN_GROUP * tq
    q4t = jnp.concatenate([qt_ref[HEAD_DIM * i:HEAD_DIM * (i + 1), :] for i in range(ATTN_GROUP)], axis=1)

    def chunk(ref_slice, c):
        return ref_slice(pl.ds(pl.multiple_of(c * tk, tk), tk))

    k_chunk = lambda c: chunk(lambda d: k_ref[0, d, :], c)
    v_chunk = lambda c: chunk(lambda d: vt_ref[0, :, d], c)

    @pl.when(pl.program_id(2) == 0)
    def _():
        def key_norm(c, mx):
            kc = k_chunk(c).astype(F32)
            return jnp.maximum(mx, jnp.max(jnp.sum(kc * kc, axis=1, keepdims=True), axis=0, keepdims=True))

        mx = lax.fori_loop(0, n_kv, key_norm, jnp.zeros((1, 1), F32))
        kmax_s[...] = jnp.broadcast_to(jnp.sqrt(mx), kmax_s.shape)

    qf = q4t.astype(F32)
    bound = jnp.sqrt(jnp.sum(qf * qf, axis=0, keepdims=True)) * kmax_s[:, 0:1] * (1.0 + 2.0 ** -10)
    bound_max = jnp.max(bound)

    def finish():
        o = acc_s[0:HEAD_DIM] * (1.0 / acc_s[HEAD_DIM:HEAD_DIM + 1])
        for i in range(ATTN_GROUP):
            o_ref[HEAD_DIM * i:HEAD_DIM * (i + 1), :] = o[:, i * tq:(i + 1) * tq]

    @pl.when(bound_max <= SOFTMAX_STATIC_BOUND)
    def _():
        acc_s[...] = jnp.zeros(acc_s.shape, F32)

        def body(c, carry):
            s = jnp.dot(k_chunk(c), q4t, preferred_element_type=F32)
            p = jnp.exp2(s - bound).astype(BF16)
            acc_s[...] += jnp.dot(v_chunk(c), p, preferred_element_type=F32)
            return carry

        lax.fori_loop(0, n_kv, body, 0, unroll=8)
        finish()

    @pl.when(jnp.logical_not(bound_max <= SOFTMAX_STATIC_BOUND))
    def _():
        acc_s[...] = jnp.zeros(acc_s.shape, F32)

        def scores(c, slot):
            s = jnp.dot(k_chunk(c), q4t, preferred_element_type=F32)
            s_s[slot] = s
            return jnp.max(s, axis=0, keepdims=True)

        def consume(c, slot, m, m_chunk):
            m_new = jnp.maximum(m, m_chunk)
            p = jnp.exp2(s_s[slot] - m_new).astype(BF16)
            pv = jnp.dot(v_chunk(c), p, preferred_element_type=F32)
            acc_s[...] = jnp.exp2(m - m_new) * acc_s[...] + pv
            return m_new

        def body(cc, carry):
            m, mc0 = carry
            c = 2 * cc
            mc1 = scores(c + 1, 1)
            m = consume(c, 0, m, mc0)
            mc0 = scores(c + 2, 0)
            return consume(c + 1, 1, m, mc1), mc0

        m, mc0 = lax.fori_loop(0, n_kv // 2 - 1, body, (jnp.full((1, w), -jnp.inf, F32), scores(0, 0)), unroll=2)
        mc1 = scores(n_kv - 1, 1)
        consume(n_kv - 1, 1, consume(n_kv - 2, 0, m, mc0), mc1)
        finish()


def _flash_attention(q_t, k, v_t, batch, length):
    n = q_t.shape[1]
    tq, tk = ATTN_Q_TILE, ATTN_K_TILE
    nq, nk = length // tq, length // tk
    gw = ATTN_GROUP * HEAD_DIM
    return pl.pallas_call(
        functools.partial(_flash_kernel, tq=tq, tk=tk, n_kv=nk),
        grid=(batch, ATTN_KV_HEADS, nq),
        in_specs=[pl.BlockSpec((gw, tq), lambda b, h, i: (h, b * nq + i)),
                  pl.BlockSpec((1, length, HEAD_DIM), lambda b, h, i: (h, b, 0)),
                  pl.BlockSpec((1, V_ROWS, length), lambda b, h, i: (h, 0, b))],
        out_specs=pl.BlockSpec((gw, tq), lambda b, h, i: (h, b * nq + i)),
        out_shape=jax.ShapeDtypeStruct((ATTN_WIDTH, n), F32),
        scratch_shapes=[pltpu.VMEM((V_ROWS, ATTN_GROUP * tq), F32),
                        pltpu.VMEM((2, tk, ATTN_GROUP * tq), F32),
                        pltpu.VMEM((1, LANES), F32)],
        compiler_params=_cparams(("parallel", "parallel", "arbitrary")),
        name="flash_attn",
    )(q_t, k, v_t)


def _memkv_kernel(mem_ref, g_ref, w_ref, kv_ref):
    m = mem_ref[...]
    mn = m * lax.rsqrt(jnp.mean(m * m, axis=-1, keepdims=True) + EPS) * g_ref[...]
    kv_ref[...] = jnp.dot(mn.astype(BF16), w_ref[...], preferred_element_type=F32)


def _memkv(mem2, mem_g, w_kv):
    rows = mem2.shape[0]
    return pl.pallas_call(
        _memkv_kernel,
        grid=(rows // MEM_TOKENS,),
        in_specs=[pl.BlockSpec((MEM_TOKENS, D_MODEL), lambda i: (i, 0)),
                  pl.BlockSpec((1, D_MODEL), lambda i: (0, 0)),
                  pl.BlockSpec((D_MODEL, 2 * MEM_WIDTH), lambda i: (0, 0))],
        out_specs=pl.BlockSpec((MEM_TOKENS, 2 * MEM_WIDTH), lambda i: (i, 0)),
        out_shape=jax.ShapeDtypeStruct((rows, 2 * MEM_WIDTH), F32),
        compiler_params=_cparams(("parallel",)),
        name="mem_kv",
    )(mem2, mem_g, w_kv)


def _toeplitz_kernel(kf_ref, kb_ref, g_ref):
    t = SSM_CHUNK
    causal = lax.broadcasted_iota(jnp.int32, (t, t), 1) >= lax.broadcasted_iota(jnp.int32, (t, t), 0)

    def build(cp, carry):
        r0 = pl.multiple_of(cp * t, t)
        kf_rows, kb_rows = kf_ref[0, cp], kb_ref[0, cp]
        for c in range(SSM_GROUP):
            lo = pltpu.roll(jnp.broadcast_to(kf_rows[c:c + 1], (t, t)), 0, 1, stride=1, stride_axis=0)
            up = pltpu.roll(jnp.broadcast_to(kb_rows[c:c + 1], (t, t)), 0, 1, stride=1, stride_axis=0)
            g_ref[0, pl.ds(r0, t), c * t:(c + 1) * t] = jnp.where(causal, lo, up).astype(BF16)
        return carry

    lax.fori_loop(0, SSM_GROUP, build, 0)


def _toeplitz(kf, kb):
    gt = SSM_GROUP * SSM_CHUNK
    lag_spec = pl.BlockSpec((1, SSM_GROUP, SSM_GROUP, SSM_CHUNK), lambda g: (g, 0, 0, 0))
    return pl.pallas_call(
        _toeplitz_kernel,
        grid=(SSM_GROUPS,),
        in_specs=[lag_spec, lag_spec],
        out_specs=pl.BlockSpec((1, gt, gt), lambda g: (g, 0, 0)),
        out_shape=jax.ShapeDtypeStruct((SSM_GROUPS, gt, gt), BF16),
        compiler_params=_cparams(("parallel",)),
        name="ssm_toeplitz",
    )(kf, kb)


def _ssm_kernel(u_ref, g_ref, p_ref, q_ref, a1_ref, a2_ref, y_ref, s_s, x_s, h_s, *, batch, n_chunks):
    t = SSM_CHUNK
    rows = batch * n_chunks
    half = 2 * SSM_STATE
    u = jnp.concatenate([u_ref[0, c].reshape(rows, t) for c in range(SSM_GROUP)], axis=1).astype(BF16)
    y_intra = jnp.dot(u, g_ref[0], preferred_element_type=F32)
    s_all = jnp.dot(u, p_ref[0], preferred_element_type=F32)
    s_s[...] = s_all
    x_s[...] = jnp.concatenate([pltpu.roll(s_all[:, :half], SSM_STATE, 1),
                                pltpu.roll(s_all[:, half:], SSM_STATE, 1)], axis=1)
    a1f, a2f = a1_ref[0, :, :half], a2_ref[0, :, :half]
    a1b, a2b = a1_ref[0, :, half:], a2_ref[0, :, half:]

    sub = SUBLANES
    n_blocks = n_chunks // sub

    def step(kb, carry):
        new = []
        for b in range(batch):
            hf, gf, hb, gb = carry[4 * b:4 * b + 4]
            base_f = pl.multiple_of(b * n_chunks + kb * sub, sub)
            base_b = pl.multiple_of(b * n_chunks + (n_blocks - 1 - kb) * sub, sub)
            sf, xf = s_s[pl.ds(base_f, sub), 0:half], x_s[pl.ds(base_f, sub), 0:half]
            sb, xb = s_s[pl.ds(base_b, sub), half:2 * half], x_s[pl.ds(base_b, sub), half:2 * half]
            hf_rows, hb_rows = [], [None] * sub
            for i in range(sub):
                hf_rows.append(hf)
                hf, gf = a1f * hf + a2f * gf + sf[i:i + 1], a1f * gf - a2f * hf + xf[i:i + 1]
            for i in range(sub - 1, -1, -1):
                hb_rows[i] = hb
                hb, gb = a1b * hb + a2b * gb + sb[i:i + 1], a1b * gb - a2b * hb + xb[i:i + 1]
            h_s[pl.ds(base_f, sub), 0:half] = jnp.concatenate(hf_rows, axis=0)
            h_s[pl.ds(base_b, sub), half:2 * half] = jnp.concatenate(hb_rows, axis=0)
            new += [hf, gf, hb, gb]
        return tuple(new)

    zero = jnp.zeros((1, half), F32)
    lax.fori_loop(0, n_blocks, step, (zero,) * (4 * batch))
    y = y_intra + jnp.dot(h_s[...].astype(BF16), q_ref[0], preferred_element_type=F32)
    for c in range(SSM_GROUP):
        y_ref[0, c] = y[:, c * t:(c + 1) * t].reshape(batch, n_chunks, t)


def _ssm_scan(u_t, g_mat, p_mat, q_mat, a1, a2, batch, length):
    t = SSM_CHUNK
    nk = length // t
    gt = SSM_GROUP * t
    u5 = u_t.reshape(SSM_GROUPS, SSM_GROUP, batch, nk, t)
    blk = (1, SSM_GROUP, batch, nk, t)
    y5 = pl.pallas_call(
        functools.partial(_ssm_kernel, batch=batch, n_chunks=nk),
        grid=(SSM_GROUPS,),
        in_specs=[pl.BlockSpec(blk, lambda g: (g, 0, 0, 0, 0)),
                  pl.BlockSpec((1, gt, gt), lambda g: (g, 0, 0)),
                  pl.BlockSpec((1, gt, 4 * SSM_STATE), lambda g: (g, 0, 0)),
                  pl.BlockSpec((1, 4 * SSM_STATE, gt), lambda g: (g, 0, 0)),
                  pl.BlockSpec((1, 1, 4 * SSM_STATE), lambda g: (g, 0, 0)),
                  pl.BlockSpec((1, 1, 4 * SSM_STATE), lambda g: (g, 0, 0))],
        out_specs=pl.BlockSpec(blk, lambda g: (g, 0, 0, 0, 0)),
        out_shape=jax.ShapeDtypeStruct(u5.shape, F32),
        scratch_shapes=[pltpu.VMEM((batch * nk, 4 * SSM_STATE), F32)] * 3,
        compiler_params=_cparams(("parallel",)),
        name="ssm_scan",
    )(u5, g_mat, p_mat, q_mat, a1, a2)
    return y5.reshape(SSM_WIDTH, batch * length)


def _ssm_tables(a_re, a_im, log_step, b_re, b_im, c_re, c_im):
    t = SSM_CHUNK
    hi = lax.Precision.HIGHEST
    lam = lax.complex(a_re.astype(F32), a_im.astype(F32))
    step = jnp.exp(log_step.astype(F32))[..., None]
    ls = lam * step
    a_bar = jnp.exp(ls)
    b_bar = ((a_bar - 1.0) / lam)[..., None] * lax.complex(b_re.astype(F32), b_im.astype(F32))
    c = lax.complex(c_re.astype(F32), c_im.astype(F32))
    tau = jnp.arange(t + 1, dtype=F32)
    pw = jnp.exp(ls[..., None] * tau)
    kern = jnp.einsum('dgcp,dgpt,dgpe->dgtce', c, pw[..., :t], b_bar, precision=hi).real
    kf = kern[0].at[:, 0].add(kern[1][:, 0]).transpose(0, 3, 2, 1)
    kb = jnp.roll(kern[1][:, ::-1], 1, axis=1).transpose(0, 3, 2, 1)
    pw_f, pw_b = pw[0], pw[1]
    pf = pw_f[:, :, t - 1 - jnp.arange(t)][..., None] * b_bar[0][:, :, None, :]
    pb = pw_b[:, :, :t][..., None] * b_bar[1][:, :, None, :]
    to_rows = lambda z: z.transpose(0, 3, 2, 1).reshape(SSM_GROUPS, SSM_GROUP * t, SSM_STATE)
    p_mat = jnp.concatenate([to_rows(pf.real), to_rows(pf.imag), to_rows(pb.real), to_rows(pb.imag)], axis=-1)
    qf = c[0].transpose(0, 2, 1)[..., None] * pw_f[:, :, 1:][:, :, None, :]
    qb = c[1].transpose(0, 2, 1)[..., None] * pw_b[:, :, t - jnp.arange(t)][:, :, None, :]
    to_cols = lambda z: z.reshape(SSM_GROUPS, SSM_STATE, SSM_GROUP * t)
    q_mat = jnp.concatenate([to_cols(qf.real), -to_cols(qf.imag), to_cols(qb.real), -to_cols(qb.imag)], axis=1)
    at = pw[..., t]
    a1 = jnp.concatenate([at[0].real, at[0].real, at[1].real, at[1].real], axis=-1)[:, None, :]
    a2 = jnp.concatenate([-at[0].imag, at[0].imag, -at[1].imag, at[1].imag], axis=-1)[:, None, :]
    return _toeplitz(kf, kb), p_mat.astype(BF16), q_mat.astype(BF16), a1, a2


def _dft_tables(length):
    j = DFT_INNER
    n = 2 * length
    rn = n // j
    rh = rn // 2
    odd = 2 * jnp.arange(rh, dtype=jnp.int32) + 1
    r = jnp.arange(rn, dtype=jnp.int32)
    ang1 = (math.pi / rn) * ((odd[:, None] * r[None, :]) % (2 * rn)).astype(F32)
    c1, s1 = jnp.cos(ang1), jnp.sin(ang1)
    f1_full = jnp.concatenate([c1, -s1], axis=0)
    f1_top = f1_full[:, :rh]
    f1_inv = jnp.concatenate([c1[:, :rh].T, -s1[:, :rh].T], axis=1)
    jj = jnp.arange(j, dtype=jnp.int32)
    angt = (math.pi / n) * (odd[:, None] * jj[None, :]).astype(F32)
    tw_re, tw_im = jnp.cos(angt), -jnp.sin(angt)
    ang2 = (2.0 * math.pi / j) * ((jj[:, None] * jj[None, :]) % j).astype(F32)
    c2, s2 = jnp.cos(ang2), jnp.sin(ang2)
    f2_fwd = jnp.block([[c2, -s2], [s2, c2]])
    f2_inv = jnp.block([[c2, s2], [-s2, c2]])
    return dict(f1_full=f1_full.astype(BF16), f1_top=f1_top.astype(BF16), f1_inv=f1_inv.astype(BF16),
                tw_re=tw_re, tw_im=tw_im, f2_fwd=f2_fwd.astype(BF16), f2_inv=f2_inv.astype(BF16), rn=rn, rh=rh)


def _twiddle_rows(a, tw_re, tw_im, rk):
    a_re, a_im = a[:rk], a[rk:]
    return jnp.concatenate([a_re * tw_re - a_im * tw_im, a_re * tw_im + a_im * tw_re], axis=1)


def _filter_hidden_kernel(bands_ref, w1t_ref, w1c_ref, w1s_ref, b1_ref, w2t_ref, b2_ref, hid_ref, *, length, tl):
    hi = lax.Precision.HIGHEST
    base = pl.program_id(0) * tl
    idx = (lax.broadcasted_iota(jnp.int32, (1, tl), 1) + base).astype(F32)
    for d in range(2):
        pos = idx if d == 0 else float(length) - idx
        tt = pos / float(length)
        wpos = (2.0 * math.pi / length) * pos
        arg = bands_ref[...] * wpos
        h1 = (w1t_ref[...] * tt
              + jnp.dot(w1c_ref[...], jnp.cos(arg), preferred_element_type=F32, precision=hi)
              - jnp.dot(w1s_ref[...], jnp.sin(arg), preferred_element_type=F32, precision=hi))
        h1 = jnp.sin(h1 + b1_ref[...])
        h2 = jnp.dot(w2t_ref[...], h1, preferred_element_type=F32, precision=hi)
        hid_ref[d] = jnp.sin(h2 + b2_ref[...]).astype(hid_ref.dtype)


def _filter_taps_kernel(hid_ref, w3t_ref, decay_ref, filt_ref, *, length, tl):
    d = pl.program_id(0) % 2
    base = pl.program_id(1) * tl
    idx = (lax.broadcasted_iota(jnp.int32, (1, tl), 1) + base).astype(F32)
    pos = jnp.where(d == 0, idx, float(length) - idx)
    tt = pos / float(length)
    f = jnp.dot(w3t_ref[...], hid_ref[0], preferred_element_type=F32)
    f = f * jnp.exp(-tt * decay_ref[...])
    f = jnp.where(jnp.logical_and(d == 1, idx == 0.0), 0.0, f)
    filt_ref[...] = f.astype(filt_ref.dtype)


def _filter_dft_kernel(filt_ref, f1_ref, twre_ref, twim_ref, f2_ref, kf_ref, *, rk, cb):
    rows = []
    for c in range(cb):
        r = jnp.concatenate([filt_ref[0, 0, c], -filt_ref[0, 1, c]], axis=0)
        a = jnp.dot(f1_ref[...], r, preferred_element_type=F32)
        rows.append(_twiddle_rows(a, twre_ref[...], twim_ref[...], rk))
    x = jnp.dot(jnp.concatenate(rows, axis=0).astype(BF16), f2_ref[...], preferred_element_type=F32)
    for c in range(cb):
        kf_ref[0, c, 0] = x[rk * c:rk * (c + 1), :DFT_INNER].astype(kf_ref.dtype)
        kf_ref[0, c, 1] = x[rk * c:rk * (c + 1), DFT_INNER:].astype(kf_ref.dtype)


def _hyena_filters(length, tabs, w1, b1, w2, b2, w3, log_decay):
    tl = min(FILTER_LANE_TILE, length)
    nl = length // tl
    rn, rh, j = tabs["rn"], tabs["rh"], DFT_INNER
    bands = jnp.linspace(1e-4, FILTER_BANDS - 1, FILTER_BANDS, dtype=F32)[:, None]
    w1f = w1.astype(F32)
    const = lambda shape: pl.BlockSpec(shape, lambda *_: (0,) * len(shape))
    hid = pl.pallas_call(
        functools.partial(_filter_hidden_kernel, length=length, tl=tl),
        grid=(nl,),
        in_specs=[const((FILTER_BANDS, 1)), const((FILTER_HIDDEN, 1)), const((FILTER_HIDDEN, FILTER_BANDS)),
                  const((FILTER_HIDDEN, FILTER_BANDS)), const((FILTER_HIDDEN, 1)),
                  const((FILTER_HIDDEN, FILTER_HIDDEN)), const((FILTER_HIDDEN, 1))],
        out_specs=pl.BlockSpec((2, FILTER_HIDDEN, tl), lambda i: (0, 0, i)),
        out_shape=jax.ShapeDtypeStruct((2, FILTER_HIDDEN, length), BF16),
        compiler_params=_cparams(("parallel",)),
        name="hyena_filter_hidden",
    )(bands, w1f[0:1].T, w1f[1:1 + FILTER_BANDS].T, w1f[1 + FILTER_BANDS:].T, b1.astype(F32)[:, None],
      w2.astype(F32).T, b2.astype(F32)[:, None])
    n_od = HYENA_ORDER * 2
    w3t = w3.T.reshape(n_od, HYENA_WIDTH, FILTER_HIDDEN).astype(BF16)
    decay = jnp.exp(log_decay.astype(F32)).reshape(n_od, HYENA_WIDTH, 1)
    filt = pl.pallas_call(
        functools.partial(_filter_taps_kernel, length=length, tl=tl),
        grid=(n_od, nl),
        in_specs=[pl.BlockSpec((1, FILTER_HIDDEN, tl), lambda od, i: (od % 2, 0, i)),
                  pl.BlockSpec((None, HYENA_WIDTH, FILTER_HIDDEN), lambda od, i: (od, 0, 0)),
                  pl.BlockSpec((None, HYENA_WIDTH, 1), lambda od, i: (od, 0, 0))],
        out_specs=pl.BlockSpec((None, HYENA_WIDTH, tl), lambda od, i: (od, 0, i)),
        out_shape=jax.ShapeDtypeStruct((n_od, HYENA_WIDTH, length), BF16),
        compiler_params=_cparams(("parallel", "parallel")),
        name="hyena_filter_taps",
    )(hid, w3t, decay)
    filt6 = filt.reshape(HYENA_ORDER, 2, HYENA_WIDTH, rh, j)
    cb = HYENA_CH_BLOCK
    return pl.pallas_call(
        functools.partial(_filter_dft_kernel, rk=rh, cb=cb),
        grid=(HYENA_ORDER, HYENA_WIDTH // cb),
        in_specs=[pl.BlockSpec((1, 2, cb, rh, j), lambda o, c: (o, 0, c, 0, 0)),
                  const((rn, rn)), const((rh, j)), const((rh, j)), const((2 * j, 2 * j))],
        out_specs=pl.BlockSpec((1, cb, 2, rh, j), lambda o, c: (o, c, 0, 0, 0)),
        out_shape=jax.ShapeDtypeStruct((HYENA_ORDER, HYENA_WIDTH, 2, rh, j), BF16),
        compiler_params=_cparams(("parallel", "parallel")),
        name="hyena_filter_dft",
    )(filt6, tabs["f1_full"], tabs["tw_re"], tabs["tw_im"], tabs["f2_fwd"])


def _hyena_kernel(scw_ref, scb_ref, hb_ref, x_ref, hg_ref, kf_ref, f1_ref, f1i_ref, twre_ref, twim_ref, f2_ref,
                  f2i_ref, o_ref, *, batch, rn, rh, cb, cs):
    j = DFT_INNER
    n_inv = 2.0 / (rn * j)
    rk = rh
    lane = lax.broadcasted_iota(jnp.int32, (rh, j), 1)
    row = lax.broadcasted_iota(jnp.int32, (rh, j), 0)
    first = jnp.logical_and(lane == 0, row == 0)
    last = jnp.logical_and(lane == j - 1, row == rh - 1)
    tw_re, tw_im = twre_ref[...], twim_ref[...]

    def prev_t(x):
        p = pltpu.roll(x, 1, 1)
        return jnp.where(first, 0.0, jnp.where(lane == 0, pltpu.roll(p, 1, 0), p))

    def next_t(x):
        p = pltpu.roll(x, j - 1, 1)
        return jnp.where(last, 0.0, jnp.where(lane == j - 1, pltpu.roll(p, rh - 1, 0), p))

    def split(m, i):
        return m[rk * i:rk * (i + 1), :j], m[rk * i:rk * (i + 1), j:]

    def long_conv(xs, ks):
        rows = []
        for x in xs:
            a = jnp.dot(f1_ref[...], x.astype(BF16), preferred_element_type=F32)
            rows.append(_twiddle_rows(a, tw_re, tw_im, rk))
        m = jnp.dot(jnp.concatenate(rows, axis=0).astype(BF16), f2_ref[...], preferred_element_type=F32)
        rows = []
        for i, (k_re, k_im) in enumerate(ks):
            x_re, x_im = split(m, i)
            rows.append(jnp.concatenate([x_re * k_re - x_im * k_im, x_re * k_im + x_im * k_re], axis=1))
        m = jnp.dot(jnp.concatenate(rows, axis=0).astype(BF16), f2i_ref[...], preferred_element_type=F32)
        outs = []
        for i in range(len(xs)):
            b_re, b_im = split(m, i)
            bp = jnp.concatenate([b_re * tw_re + b_im * tw_im, b_im * tw_re - b_re * tw_im], axis=0).astype(BF16)
            outs.append(jnp.dot(f1i_ref[...], bp, preferred_element_type=F32) * n_inv)
        return outs

    cbase = pl.program_id(0) * cb
    for c0 in range(0, cb, cs):
        chans = [(ci, b) for ci in range(c0, c0 + cs) for b in range(batch)]
        segs = []
        for sgm in range(3):
            seg = []
            for ci, b in chans:
                chs = sgm * HYENA_WIDTH + cbase + ci
                x = x_ref[sgm, ci, b]
                seg.append(prev_t(x) * scw_ref[chs] + x * scw_ref[3 * HYENA_WIDTH + chs]
                           + next_t(x) * scw_ref[6 * HYENA_WIDTH + chs] + scb_ref[chs])
            segs.append(seg)
        z = segs[0]
        for o in range(HYENA_ORDER):
            conv = long_conv(z, [(kf_ref[o, ci, 0].astype(F32), kf_ref[o, ci, 1].astype(F32)) for ci, _ in chans])
            z = [segs[o + 1][i] * (conv[i] + z[i] * hb_ref[o * HYENA_WIDTH + cbase + ci])
                 for i, (ci, _) in enumerate(chans)]
        for i, (ci, b) in enumerate(chans):
            o_ref[ci, b] = z[i] * hg_ref[ci, b]


def _hyena(hi_t, hg_t, kf, tabs, short_w, short_b, hy_bias, batch, length):
    rn, rh, j = tabs["rn"], tabs["rh"], DFT_INNER
    cb = HYENA_CH_BLOCK
    x5 = hi_t.reshape(3, HYENA_WIDTH, batch, rh, j)
    g4 = hg_t.reshape(HYENA_WIDTH, batch, rh, j)
    smem = pl.BlockSpec(memory_space=pltpu.SMEM)
    const = lambda shape: pl.BlockSpec(shape, lambda c: (0,) * len(shape))
    out = pl.pallas_call(
        functools.partial(_hyena_kernel, batch=batch, rn=rn, rh=rh, cb=cb, cs=HYENA_CH_SUB),
        grid=(HYENA_WIDTH // cb,),
        in_specs=[smem, smem, smem,
                  pl.BlockSpec((3, cb, batch, rh, j), lambda c: (0, c, 0, 0, 0)),
                  pl.BlockSpec((cb, batch, rh, j), lambda c: (c, 0, 0, 0)),
                  pl.BlockSpec((HYENA_ORDER, cb, 2, rh, j), lambda c: (0, c, 0, 0, 0)),
                  const((rn, rh)), const((rh, rn)), const((rh, j)), const((rh, j)),
                  const((2 * j, 2 * j)), const((2 * j, 2 * j))],
        out_specs=pl.BlockSpec((cb, batch, rh, j), lambda c: (c, 0, 0, 0)),
        out_shape=jax.ShapeDtypeStruct(g4.shape, F32),
        compiler_params=_cparams(("parallel",)),
        name="hyena_conv",
    )(short_w.astype(F32).reshape(-1), short_b.astype(F32), hy_bias.astype(F32).reshape(-1),
      x5, g4, kf, tabs["f1_top"], tabs["f1_inv"], tabs["tw_re"], tabs["tw_im"], tabs["f2_fwd"], tabs["f2_inv"])
    return out.reshape(HYENA_WIDTH, batch * length)


def _rms_rows(x, g):
    return x * lax.rsqrt(jnp.mean(x * x, axis=-1, keepdims=True) + EPS) * g


def _rms_cols(x, g):
    return x * lax.rsqrt(jnp.mean(x * x, axis=0, keepdims=True) + EPS) * g


def _post_kernel(ao_ref, ga_ref, mq_ref, mk_ref, mv_ref, y_ref, u_ref, sg_ref, hy_ref, x_ref,
                 d_ref, wglu_ref, g_attn_ref, g_ssm_ref, g_hy_ref, g_mem_ref, post_g_ref, wo_ref, out_ref):
    attn_n = _rms_cols(ao_ref[...] * ga_ref[...], g_attn_ref[...])
    s = jnp.dot(mq_ref[...], mk_ref[0], preferred_element_type=F32)
    ps = []
    for h in range(MEM_HEADS):
        sh = s[:, MEM_TOKENS * h:MEM_TOKENS * (h + 1)]
        e = jnp.exp(sh - jnp.max(sh, axis=-1, keepdims=True))
        ps.append(e * (1.0 / jnp.sum(e, axis=-1, keepdims=True)))
    p = jnp.concatenate(ps, axis=1).astype(BF16)
    cross_n = _rms_rows(jnp.dot(p, mv_ref[0], preferred_element_type=F32), g_mem_ref[...])
    y = y_ref[...] + d_ref[...] * u_ref[...]
    g = y * (0.5 * (1.0 + jnp.tanh(math.sqrt(2.0 / math.pi) * (y + 0.044715 * (y * y * y)))))
    gz = jnp.dot(wglu_ref[...], g.astype(BF16), preferred_element_type=F32)
    ssm_n = _rms_cols(g * _sigmoid(gz) * sg_ref[...], g_ssm_ref[...])
    hy_n = _rms_cols(hy_ref[...], g_hy_ref[...])
    o1, o2, o3 = ATTN_WIDTH, ATTN_WIDTH + SSM_WIDTH, ATTN_WIDTH + SSM_WIDTH + HYENA_WIDTH
    mixed = (jnp.dot(attn_n.T.astype(BF16), wo_ref[0:o1], preferred_element_type=F32)
             + jnp.dot(ssm_n.T.astype(BF16), wo_ref[o1:o2], preferred_element_type=F32)
             + jnp.dot(hy_n.T.astype(BF16), wo_ref[o2:o3], preferred_element_type=F32)
             + jnp.dot(cross_n.astype(BF16), wo_ref[o3:], preferred_element_type=F32))
    out_ref[...] = x_ref[...] + _rms_rows(mixed, post_g_ref[...])


def _post(ao, ga, mq, mk_bd, mv_bd, y_t, u_t, sg_t, hy_t, x2, d, wglu_t, g_attn, g_ssm, g_hy, g_mem, post_g, wo,
          length):
    n = x2.shape[0]
    tm = TOKEN_TILE
    tiles_per_seq = length // tm
    tok_spec = lambda w: pl.BlockSpec((tm, w), lambda i: (i, 0))
    ch_spec = lambda w: pl.BlockSpec((w, tm), lambda i: (0, i))
    const = lambda shape: pl.BlockSpec(shape, lambda i: (0,) * len(shape))
    hm = MEM_HEADS * MEM_TOKENS
    return pl.pallas_call(
        _post_kernel,
        grid=(n // tm,),
        in_specs=[ch_spec(ATTN_WIDTH), ch_spec(ATTN_WIDTH), tok_spec(MEM_WIDTH),
                  pl.BlockSpec((1, MEM_WIDTH, hm), lambda i: (i // tiles_per_seq, 0, 0)),
                  pl.BlockSpec((1, hm, MEM_WIDTH), lambda i: (i // tiles_per_seq, 0, 0)),
                  ch_spec(SSM_WIDTH), ch_spec(SSM_WIDTH), ch_spec(SSM_WIDTH), ch_spec(HYENA_WIDTH),
                  tok_spec(D_MODEL),
                  const((SSM_WIDTH, 1)), const((SSM_WIDTH, SSM_WIDTH)),
                  const((ATTN_WIDTH, 1)), const((SSM_WIDTH, 1)), const((HYENA_WIDTH, 1)), const((1, MEM_WIDTH)),
                  const((1, D_MODEL)), const((MIX_WIDTH, D_MODEL))],
        out_specs=tok_spec(D_MODEL),
        out_shape=jax.ShapeDtypeStruct((n, D_MODEL), F32),
        compiler_params=_cparams(("parallel",)),
        name="post",
    )(ao, ga, mq, mk_bd, mv_bd, y_t, u_t, sg_t, hy_t, x2, d, wglu_t, g_attn, g_ssm, g_hy, g_mem, post_g, wo)


def _rope_tables(length):
    rows = length // GRID_W
    row = jnp.broadcast_to(jnp.arange(rows, dtype=F32)[:, None], (rows, GRID_W)).reshape(length)
    col = jnp.broadcast_to(jnp.arange(GRID_W, dtype=F32)[None, :], (rows, GRID_W)).reshape(length)
    inv_freq = ROPE_THETA ** (-jnp.arange(ROPE_FREQS, dtype=F32) / ROPE_FREQS)
    ang = jnp.stack([row[:, None] * inv_freq, col[:, None] * inv_freq], axis=1)
    ang = jnp.broadcast_to(ang[:, :, None, :], (length, 2, 2, ROPE_FREQS)).reshape(length, HEAD_DIM)
    cos1, sin1 = jnp.cos(ang), jnp.sin(ang)
    low1 = (jnp.arange(HEAD_DIM) % (2 * ROPE_FREQS)) < ROPE_FREQS
    cos_t, sin_t = cos1.T, jnp.where(low1, -sin1, sin1).T
    cos = jnp.concatenate([cos1, cos1], axis=1)
    sin = jnp.concatenate([sin1, sin1], axis=1)
    low = jnp.concatenate([low1, low1])
    return cos, jnp.where(low, -sin, 0.0), jnp.where(low, 0.0, sin), cos_t, sin_t


def _block_diag_heads(mk, mv, batch):
    mk4 = mk.reshape(batch, MEM_TOKENS, MEM_HEADS, HEAD_DIM)
    mv4 = mv.reshape(batch, MEM_TOKENS, MEM_HEADS, HEAD_DIM)
    eye = jnp.eye(MEM_HEADS, dtype=mk.dtype)
    k_bd = jnp.einsum('bmhd,hg->bhdgm', mk4, eye).reshape(batch, MEM_WIDTH, MEM_HEADS * MEM_TOKENS)
    v_bd = jnp.einsum('bmhd,hg->bhmgd', mv4, eye).reshape(batch, MEM_HEADS * MEM_TOKENS, MEM_WIDTH)
    return k_bd.astype(BF16), v_bd.astype(BF16)


def _layer_weights(layer, p):
    w_in = p["w_in"][layer]
    a, kv = ATTN_WIDTH, KV_WIDTH
    k0, v0, g0, mq0 = a, a + kv, a + 2 * kv, w_in.shape[1] - MEM_WIDTH
    w_tok = jnp.concatenate([w_in[:, k0:v0], w_in[:, mq0:]], axis=1).astype(BF16)
    w_ch_t = jnp.concatenate([w_in[:, :k0], w_in[:, v0:mq0]], axis=1).T.astype(BF16)
    bg = p["branch_norm"][layer].astype(F32)
    o1, o2, o3 = a, a + SSM_WIDTH, a + SSM_WIDTH + HYENA_WIDTH
    head_id = jnp.arange(kv) // HEAD_DIM
    return dict(
        w_tok=w_tok, w_ch_t=w_ch_t,
        pre_g=p["pre_norm"][layer].astype(F32)[None, :], post_g=p["post_norm"][layer].astype(F32)[None, :],
        qg=p["q_norm"][layer].astype(F32)[:, None],
        kg=jnp.tile(p["k_norm"][layer].astype(F32), ATTN_KV_HEADS)[None, :],
        ones=(head_id[:, None] == head_id[None, :]).astype(BF16),
        mem_g=p["mem_norm"][layer].astype(F32)[None, :], w_mem_kv=p["w_mem_kv"][layer].astype(BF16),
        ssm=_ssm_tables(p["ssm_a_re"][layer], p["ssm_a_im"][layer], p["ssm_log_step"][layer], p["ssm_b_re"][layer],
                        p["ssm_b_im"][layer], p["ssm_c_re"][layer], p["ssm_c_im"][layer]),
        d=p["ssm_d"][layer].astype(F32)[:, None], wglu_t=p["ssm_w_glu"][layer].T.astype(BF16),
        g_attn=bg[:o1, None], g_ssm=bg[o1:o2, None], g_hy=bg[o2:o3, None], g_mem=bg[None, o3:],
        wo=p["w_out"][layer].astype(BF16),
    )


def _mixer_layer(x2, mem2, lw, kf, tabs, rope, p, layer, batch, length):
    q_t, k, v_t, ga, mq, su_t, sg_t, hi_t, hg_t = _inproj(x2, lw["pre_g"], lw["w_tok"], lw["w_ch_t"], rope,
                                                          lw["qg"], lw["kg"], lw["ones"], length)
    ao = _flash_attention(q_t, k, v_t, batch, length)
    mem_kv = _memkv(mem2, lw["mem_g"], lw["w_mem_kv"])
    mk_bd, mv_bd = _block_diag_heads(mem_kv[:, :MEM_WIDTH], mem_kv[:, MEM_WIDTH:], batch)
    y_t = _ssm_scan(su_t, *lw["ssm"], batch, length)
    hy_t = _hyena(hi_t, hg_t, kf, tabs, p["hyena_short_w"][layer], p["hyena_short_b"][layer],
                  p["hyena_bias"][layer], batch, length)
    return _post(ao, ga, mq, mk_bd, mv_bd, y_t, su_t, sg_t, hy_t, x2, lw["d"], lw["wglu_t"],
                 lw["g_attn"], lw["g_ssm"], lw["g_hy"], lw["g_mem"], lw["post_g"], lw["wo"], length)


def _run_group(x, mem, weights, p):
    batch, length, _ = x.shape
    rope = _rope_tables(length)
    tabs = _dft_tables(length)
    x2 = x.reshape(batch * length, D_MODEL)
    mem2 = mem.reshape(batch * MEM_TOKENS, D_MODEL)
    for layer in range(DEPTH):
        kf = _hyena_filters(length, tabs, p["hyena_ffn_w1"][layer], p["hyena_ffn_b1"][layer],
                            p["hyena_ffn_w2"][layer], p["hyena_ffn_b2"][layer], p["hyena_ffn_w3"][layer],
                            p["hyena_log_decay"][layer])
        x2 = _mixer_layer(x2, mem2, weights[layer], kf, tabs, rope, p, layer, batch, length)
    return x2.reshape(batch, length, D_MODEL)


def kernel(x_prompt, x_sample, mem_prompt, mem_sample, pre_norm, post_norm, w_in, q_norm, k_norm, mem_norm, w_mem_kv, ssm_a_re, ssm_a_im, ssm_log_step, ssm_b_re, ssm_b_im, ssm_c_re, ssm_c_im, ssm_d, ssm_w_glu, hyena_short_w, hyena_short_b, hyena_ffn_w1, hyena_ffn_b1, hyena_ffn_w2, hyena_ffn_b2, hyena_ffn_w3, hyena_log_decay, hyena_bias, branch_norm, w_out):
    p = dict(pre_norm=pre_norm, post_norm=post_norm, w_in=w_in, q_norm=q_norm, k_norm=k_norm, mem_norm=mem_norm,
             w_mem_kv=w_mem_kv, ssm_a_re=ssm_a_re, ssm_a_im=ssm_a_im, ssm_log_step=ssm_log_step, ssm_b_re=ssm_b_re,
             ssm_b_im=ssm_b_im, ssm_c_re=ssm_c_re, ssm_c_im=ssm_c_im, ssm_d=ssm_d, ssm_w_glu=ssm_w_glu,
             hyena_short_w=hyena_short_w, hyena_short_b=hyena_short_b, hyena_ffn_w1=hyena_ffn_w1,
             hyena_ffn_b1=hyena_ffn_b1, hyena_ffn_w2=hyena_ffn_w2, hyena_ffn_b2=hyena_ffn_b2,
             hyena_ffn_w3=hyena_ffn_w3, hyena_log_decay=hyena_log_decay, hyena_bias=hyena_bias,
             branch_norm=branch_norm, w_out=w_out)
    weights = [_layer_weights(layer, p) for layer in range(DEPTH)]
    return (_run_group(x_prompt, mem_prompt, weights, p), _run_group(x_sample, mem_sample, weights, p))
```

```python
import functools
import math

import jax
import jax.numpy as jnp
import numpy as np
from jax import lax
from jax.experimental import pallas as pl
from jax.experimental.pallas import tpu as pltpu

F32 = jnp.float32
BF16 = jnp.bfloat16

D_MODEL = 1024
DEPTH = 2
GRID_W = 64
HEAD_DIM = 64
ATTN_HEADS = 8
ATTN_KV_HEADS = 2
ATTN_GROUP = ATTN_HEADS // ATTN_KV_HEADS
ATTN_WIDTH = ATTN_HEADS * HEAD_DIM
KV_WIDTH = ATTN_KV_HEADS * HEAD_DIM
ROPE_THETA = 10000.0
ROPE_FREQS = HEAD_DIM // 4
SSM_GROUP = 16
SSM_GROUPS = 24
SSM_WIDTH = SSM_GROUP * SSM_GROUPS
SSM_STATE = 64
HYENA_WIDTH = 384
HYENA_ORDER = 2
FILTER_BANDS = 16
FILTER_HIDDEN = 64
MEM_TOKENS = 256
MEM_HEADS = 4
MEM_WIDTH = MEM_HEADS * HEAD_DIM
MIX_WIDTH = ATTN_WIDTH + SSM_WIDTH + HYENA_WIDTH + MEM_WIDTH
EPS = 1e-6

TOK_WIDTH = KV_WIDTH + MEM_WIDTH
CH_WIDTH = 2 * ATTN_WIDTH + KV_WIDTH + 2 * SSM_WIDTH + (HYENA_ORDER + 2) * HYENA_WIDTH
V_ROWS = HEAD_DIM + 16
Q_SCALE = HEAD_DIM ** -0.5 * math.log2(math.e)
SOFTMAX_STATIC_BOUND = 60.0

LANES = 128
SUBLANES = 8
VMEM_LIMIT = 56 * 1024 * 1024
TOKEN_TILE = 512
ATTN_Q_TILE = 256
ATTN_K_TILE = 512
SSM_CHUNK = LANES
DFT_INNER = 256
HYENA_CH_BLOCK = 8
FILTER_CH_BLOCK = 16
HYENA_CH_SUB = 8
FILTER_LANE_TILE = 2048


def _cparams(sem):
    return pltpu.CompilerParams(dimension_semantics=sem, vmem_limit_bytes=VMEM_LIMIT)


def _silu(x):
    return x * (1.0 / (1.0 + jnp.exp(-x)))


def _sigmoid(x):
    return 1.0 / (1.0 + jnp.exp(-x))


def _nt_dot(a, b):
    return lax.dot_general(a, b, (((1,), (1,)), ((), ())), preferred_element_type=F32)


def _rope_128(xn, cos, s_lo, s_hi):
    outs = []
    for c in range(xn.shape[1] // LANES):
        xc = xn[:, LANES * c:LANES * (c + 1)]
        outs.append(xc * cos + pltpu.roll(xc, LANES - ROPE_FREQS, 1) * s_lo + pltpu.roll(xc, ROPE_FREQS, 1) * s_hi)
    return outs[0] if len(outs) == 1 else jnp.concatenate(outs, axis=1)


def _inproj_kernel(x_ref, pre_g_ref, wtok_ref, wch_ref, cos_ref, slo_ref, shi_ref, cost_ref, sint_ref,
                   qg_ref, kg_ref, ones_ref,
                   qt_ref, k_ref, vt_ref, ga_ref, mq_ref, su_ref, sg_ref, hi_ref, hg_ref):
    tm = x_ref.shape[0]
    x = x_ref[...]
    h = x * lax.rsqrt(jnp.mean(x * x, axis=-1, keepdims=True) + EPS) * pre_g_ref[...]
    hb = h.astype(BF16)
    tok = jnp.dot(hb, wtok_ref[...], preferred_element_type=F32)
    k = tok[:, 0:KV_WIDTH]
    mq = tok[:, KV_WIDTH:]
    k_ms = jnp.dot((k * k).astype(BF16), ones_ref[...], preferred_element_type=F32) * (1.0 / HEAD_DIM)
    kn = k * lax.rsqrt(k_ms + EPS) * kg_ref[...]
    kr = _rope_128(kn, cos_ref[...], slo_ref[...], shi_ref[...])
    for j in range(ATTN_KV_HEADS):
        k_ref[j] = kr[:, HEAD_DIM * j:HEAD_DIM * (j + 1)].astype(BF16)
    mq_ref[...] = (mq * (HEAD_DIM ** -0.5)).astype(BF16)
    ch = _nt_dot(wch_ref[...], hb)
    q3 = ch[0:ATTN_WIDTH].reshape(ATTN_HEADS, HEAD_DIM, tm)
    qn = q3 * lax.rsqrt(jnp.mean(q3 * q3, axis=1, keepdims=True) + EPS) * qg_ref[...][None]
    f = ROPE_FREQS
    rot = jnp.concatenate([qn[:, f:2 * f], qn[:, 0:f], qn[:, 3 * f:4 * f], qn[:, 2 * f:3 * f]], axis=1)
    qr = (qn * cost_ref[...][None] + rot * sint_ref[...][None]) * Q_SCALE
    qt_ref[...] = qr.reshape(ATTN_WIDTH, tm).astype(BF16)
    o = ATTN_WIDTH
    ones_row = (lax.broadcasted_iota(jnp.int32, (V_ROWS - HEAD_DIM, tm), 0) == 0).astype(BF16)
    for j in range(ATTN_KV_HEADS):
        vt_ref[j, 0:HEAD_DIM] = ch[o + HEAD_DIM * j:o + HEAD_DIM * (j + 1)].astype(BF16)
        vt_ref[j, HEAD_DIM:V_ROWS] = ones_row
    o += KV_WIDTH
    ga_ref[...] = _silu(ch[o:o + ATTN_WIDTH])
    o += ATTN_WIDTH
    su_ref[...] = ch[o:o + SSM_WIDTH]
    sg_ref[...] = _silu(ch[o + SSM_WIDTH:o + 2 * SSM_WIDTH])
    o += 2 * SSM_WIDTH
    hi_ref[...] = ch[o:o + 3 * HYENA_WIDTH]
    hg_ref[...] = _silu(ch[o + 3 * HYENA_WIDTH:])


def _inproj(x2, pre_g, w_tok, w_ch_t, rope, qg, kg, ones, length):
    cos, s_lo, s_hi, cos_t, sin_t = rope
    n = x2.shape[0]
    tm = TOKEN_TILE
    nt = n // tm
    tiles_per_seq = length // tm
    tok_spec = lambda w: pl.BlockSpec((tm, w), lambda i: (i, 0))
    ch_spec = lambda w: pl.BlockSpec((w, tm), lambda i: (0, i))
    const = lambda shape: pl.BlockSpec(shape, lambda i: (0,) * len(shape))
    pos_spec = pl.BlockSpec((tm, LANES), lambda i: (i % tiles_per_seq, 0))
    pos_t_spec = pl.BlockSpec((HEAD_DIM, tm), lambda i: (0, i % tiles_per_seq))
    return pl.pallas_call(
        _inproj_kernel,
        grid=(nt,),
        in_specs=[tok_spec(D_MODEL), const((1, D_MODEL)), const((D_MODEL, TOK_WIDTH)), const((CH_WIDTH, D_MODEL)),
                  pos_spec, pos_spec, pos_spec, pos_t_spec, pos_t_spec,
                  const((HEAD_DIM, 1)), const((1, KV_WIDTH)), const((KV_WIDTH, KV_WIDTH))],
        out_specs=[ch_spec(ATTN_WIDTH),
                   pl.BlockSpec((ATTN_KV_HEADS, tm, HEAD_DIM), lambda i: (0, i, 0)),
                   pl.BlockSpec((ATTN_KV_HEADS, V_ROWS, tm), lambda i: (0, 0, i)),
                   ch_spec(ATTN_WIDTH), tok_spec(MEM_WIDTH),
                   ch_spec(SSM_WIDTH), ch_spec(SSM_WIDTH), ch_spec(3 * HYENA_WIDTH), ch_spec(HYENA_WIDTH)],
        out_shape=[jax.ShapeDtypeStruct((ATTN_WIDTH, n), BF16),
                   jax.ShapeDtypeStruct((ATTN_KV_HEADS, n, HEAD_DIM), BF16),
                   jax.ShapeDtypeStruct((ATTN_KV_HEADS, V_ROWS, n), BF16),
                   jax.ShapeDtypeStruct((ATTN_WIDTH, n), F32),
                   jax.ShapeDtypeStruct((n, MEM_WIDTH), BF16),
                   jax.ShapeDtypeStruct((SSM_WIDTH, n), F32),
                   jax.ShapeDtypeStruct((SSM_WIDTH, n), F32),
                   jax.ShapeDtypeStruct((3 * HYENA_WIDTH, n), F32),
                   jax.ShapeDtypeStruct((HYENA_WIDTH, n), F32)],
        compiler_params=_cparams(("parallel",)),
        name="inproj",
    )(x2, pre_g, w_tok, w_ch_t, cos, s_lo, s_hi, cos_t, sin_t, qg, kg, ones)


def _flash_kernel(qt_ref, k_ref, vt_ref, o_ref, acc_s, s_s, kmax_s, *, tq, tk, n_kv):
    w = ATTN_GROUP * tq
    q4t = jnp.concatenate([qt_ref[HEAD_DIM * i:HEAD_DIM * (i + 1), :] for i in range(ATTN_GROUP)], axis=1)

    def chunk(ref_slice, c):
        return ref_slice(pl.ds(pl.multiple_of(c * tk, tk), tk))

    k_chunk = lambda c: chunk(lambda d: k_ref[0, d, :], c)
    v_chunk = lambda c: chunk(lambda d: vt_ref[0, :, d], c)

    @pl.when(pl.program_id(2) == 0)
    def _():
        def key_norm(c, mx):
            kc = k_chunk(c).astype(F32)
            return jnp.maximum(mx, jnp.max(jnp.sum(kc * kc, axis=1, keepdims=True), axis=0, keepdims=True))

        mx = lax.fori_loop(0, n_kv, key_norm, jnp.zeros((1, 1), F32))
        kmax_s[...] = jnp.broadcast_to(jnp.sqrt(mx), kmax_s.shape)

    qf = q4t.astype(F32)
    bound = jnp.sqrt(jnp.sum(qf * qf, axis=0, keepdims=True)) * kmax_s[:, 0:1] * (1.0 + 2.0 ** -10)
    bound_max = jnp.max(bound)

    def finish():
        o = acc_s[0:HEAD_DIM] * (1.0 / acc_s[HEAD_DIM:HEAD_DIM + 1])
        for i in range(ATTN_GROUP):
            o_ref[HEAD_DIM * i:HEAD_DIM * (i + 1), :] = o[:, i * tq:(i + 1) * tq]

    @pl.when(bound_max <= SOFTMAX_STATIC_BOUND)
    def _():
        acc_s[...] = jnp.zeros(acc_s.shape, F32)

        def body(c, carry):
            s = jnp.dot(k_chunk(c), q4t, preferred_element_type=F32)
            p = jnp.exp2(s - bound).astype(BF16)
            acc_s[...] += jnp.dot(v_chunk(c), p, preferred_element_type=F32)
            return carry

        lax.fori_loop(0, n_kv, body, 0, unroll=8)
        finish()

    @pl.when(jnp.logical_not(bound_max <= SOFTMAX_STATIC_BOUND))
    def _():
        acc_s[...] = jnp.zeros(acc_s.shape, F32)

        def scores(c, slot):
            s = jnp.dot(k_chunk(c), q4t, preferred_element_type=F32)
            s_s[slot] = s
            return jnp.max(s, axis=0, keepdims=True)

        def consume(c, slot, m, m_chunk):
            m_new = jnp.maximum(m, m_chunk)
            p = jnp.exp2(s_s[slot] - m_new).astype(BF16)
            pv = jnp.dot(v_chunk(c), p, preferred_element_type=F32)
            acc_s[...] = jnp.exp2(m - m_new) * acc_s[...] + pv
            return m_new

        def body(cc, carry):
            m, mc0 = carry
            c = 2 * cc
            mc1 = scores(c + 1, 1)
            m = consume(c, 0, m, mc0)
            mc0 = scores(c + 2, 0)
            return consume(c + 1, 1, m, mc1), mc0

        m, mc0 = lax.fori_loop(0, n_kv // 2 - 1, body, (jnp.full((1, w), -jnp.inf, F32), scores(0, 0)), unroll=2)
        mc1 = scores(n_kv - 1, 1)
        consume(n_kv - 1, 1, consume(n_kv - 2, 0, m, mc0), mc1)
        finish()


def _flash_attention(q_t, k, v_t, batch, length):
    n = q_t.shape[1]
    tq, tk = ATTN_Q_TILE, ATTN_K_TILE
    nq, nk = length // tq, length // tk
    gw = ATTN_GROUP * HEAD_DIM
    return pl.pallas_call(
        functools.partial(_flash_kernel, tq=tq, tk=tk, n_kv=nk),
        grid=(batch, ATTN_KV_HEADS, nq),
        in_specs=[pl.BlockSpec((gw, tq), lambda b, h, i: (h, b * nq + i)),
                  pl.BlockSpec((1, length, HEAD_DIM), lambda b, h, i: (h, b, 0)),
                  pl.BlockSpec((1, V_ROWS, length), lambda b, h, i: (h, 0, b))],
        out_specs=pl.BlockSpec((gw, tq), lambda b, h, i: (h, b * nq + i)),
        out_shape=jax.ShapeDtypeStruct((ATTN_WIDTH, n), F32),
        scratch_shapes=[pltpu.VMEM((V_ROWS, ATTN_GROUP * tq), F32),
                        pltpu.VMEM((2, tk, ATTN_GROUP * tq), F32),
                        pltpu.VMEM((1, LANES), F32)],
        compiler_params=_cparams(("parallel", "parallel", "arbitrary")),
        name="flash_attn",
    )(q_t, k, v_t)


def _memkv_kernel(mem_ref, g_ref, w_ref, kv_ref):
    m = mem_ref[...]
    mn = m * lax.rsqrt(jnp.mean(m * m, axis=-1, keepdims=True) + EPS) * g_ref[...]
    kv_ref[...] = jnp.dot(mn.astype(BF16), w_ref[...], preferred_element_type=F32)


def _memkv(mem2, mem_g, w_kv):
    rows = mem2.shape[0]
    return pl.pallas_call(
        _memkv_kernel,
        grid=(rows // MEM_TOKENS,),
        in_specs=[pl.BlockSpec((MEM_TOKENS, D_MODEL), lambda i: (i, 0)),
                  pl.BlockSpec((1, D_MODEL), lambda i: (0, 0)),
                  pl.BlockSpec((D_MODEL, 2 * MEM_WIDTH), lambda i: (0, 0))],
        out_specs=pl.BlockSpec((MEM_TOKENS, 2 * MEM_WIDTH), lambda i: (i, 0)),
        out_shape=jax.ShapeDtypeStruct((rows, 2 * MEM_WIDTH), F32),
        compiler_params=_cparams(("parallel",)),
        name="mem_kv",
    )(mem2, mem_g, w_kv)


def _toeplitz_kernel(kf_ref, kb_ref, g_ref):
    t = SSM_CHUNK
    causal = lax.broadcasted_iota(jnp.int32, (t, t), 1) >= lax.broadcasted_iota(jnp.int32, (t, t), 0)

    def build(cp, carry):
        r0 = pl.multiple_of(cp * t, t)
        kf_rows, kb_rows = kf_ref[0, cp], kb_ref[0, cp]
        for c in range(SSM_GROUP):
            lo = pltpu.roll(jnp.broadcast_to(kf_rows[c:c + 1], (t, t)), 0, 1, stride=1, stride_axis=0)
            up = pltpu.roll(jnp.broadcast_to(kb_rows[c:c + 1], (t, t)), 0, 1, stride=1, stride_axis=0)
            g_ref[0, pl.ds(r0, t), c * t:(c + 1) * t] = jnp.where(causal, lo, up).astype(BF16)
        return carry

    lax.fori_loop(0, SSM_GROUP, build, 0)


def _toeplitz(kf, kb):
    gt = SSM_GROUP * SSM_CHUNK
    lag_spec = pl.BlockSpec((1, SSM_GROUP, SSM_GROUP, SSM_CHUNK), lambda g: (g, 0, 0, 0))
    return pl.pallas_call(
        _toeplitz_kernel,
        grid=(SSM_GROUPS,),
        in_specs=[lag_spec, lag_spec],
        out_specs=pl.BlockSpec((1, gt, gt), lambda g: (g, 0, 0)),
        out_shape=jax.ShapeDtypeStruct((SSM_GROUPS, gt, gt), BF16),
        compiler_params=_cparams(("parallel",)),
        name="ssm_toeplitz",
    )(kf, kb)


def _ssm_kernel(u_ref, g_ref, p_ref, q_ref, a1_ref, a2_ref, y_ref, s_s, x_s, h_s, *, batch, n_chunks):
    t = SSM_CHUNK
    rows = batch * n_chunks
    half = 2 * SSM_STATE
    u = jnp.concatenate([u_ref[0, c].reshape(rows, t) for c in range(SSM_GROUP)], axis=1).astype(BF16)
    y_intra = jnp.dot(u, g_ref[0], preferred_element_type=F32)
    s_all = jnp.dot(u, p_ref[0], preferred_element_type=F32)
    s_s[...] = s_all
    x_s[...] = jnp.concatenate([pltpu.roll(s_all[:, :half], SSM_STATE, 1),
                                pltpu.roll(s_all[:, half:], SSM_STATE, 1)], axis=1)
    a1f, a2f = a1_ref[0, :, :half], a2_ref[0, :, :half]
    a1b, a2b = a1_ref[0, :, half:], a2_ref[0, :, half:]

    sub = SUBLANES
    n_blocks = n_chunks // sub

    def step(kb, carry):
        new = []
        for b in range(batch):
            hf, gf, hb, gb = carry[4 * b:4 * b + 4]
            base_f = pl.multiple_of(b * n_chunks + kb * sub, sub)
            base_b = pl.multiple_of(b * n_chunks + (n_blocks - 1 - kb) * sub, sub)
            sf, xf = s_s[pl.ds(base_f, sub), 0:half], x_s[pl.ds(base_f, sub), 0:half]
            sb, xb = s_s[pl.ds(base_b, sub), half:2 * half], x_s[pl.ds(base_b, sub), half:2 * half]
            hf_rows, hb_rows = [], [None] * sub
            for i in range(sub):
                hf_rows.append(hf)
                hf, gf = a1f * hf + a2f * gf + sf[i:i + 1], a1f * gf - a2f * hf + xf[i:i + 1]
            for i in range(sub - 1, -1, -1):
                hb_rows[i] = hb
                hb, gb = a1b * hb + a2b * gb + sb[i:i + 1], a1b * gb - a2b * hb + xb[i:i + 1]
            h_s[pl.ds(base_f, sub), 0:half] = jnp.concatenate(hf_rows, axis=0)
            h_s[pl.ds(base_b, sub), half:2 * half] = jnp.concatenate(hb_rows, axis=0)
            new += [hf, gf, hb, gb]
        return tuple(new)

    zero = jnp.zeros((1, half), F32)
    lax.fori_loop(0, n_blocks, step, (zero,) * (4 * batch))
    y = y_intra + jnp.dot(h_s[...].astype(BF16), q_ref[0], preferred_element_type=F32)
    for c in range(SSM_GROUP):
        y_ref[0, c] = y[:, c * t:(c + 1) * t].reshape(batch, n_chunks, t)


def _ssm_scan(u_t, g_mat, p_mat, q_mat, a1, a2, batch, length):
    t = SSM_CHUNK
    nk = length // t
    gt = SSM_GROUP * t
    u5 = u_t.reshape(SSM_GROUPS, SSM_GROUP, batch, nk, t)
    blk = (1, SSM_GROUP, batch, nk, t)
    y5 = pl.pallas_call(
        functools.partial(_ssm_kernel, batch=batch, n_chunks=nk),
        grid=(SSM_GROUPS,),
        in_specs=[pl.BlockSpec(blk, lambda g: (g, 0, 0, 0, 0)),
                  pl.BlockSpec((1, gt, gt), lambda g: (g, 0, 0)),
                  pl.BlockSpec((1, gt, 4 * SSM_STATE), lambda g: (g, 0, 0)),
                  pl.BlockSpec((1, 4 * SSM_STATE, gt), lambda g: (g, 0, 0)),
                  pl.BlockSpec((1, 1, 4 * SSM_STATE), lambda g: (g, 0, 0)),
                  pl.BlockSpec((1, 1, 4 * SSM_STATE), lambda g: (g, 0, 0))],
        out_specs=pl.BlockSpec(blk, lambda g: (g, 0, 0, 0, 0)),
        out_shape=jax.ShapeDtypeStruct(u5.shape, F32),
        scratch_shapes=[pltpu.VMEM((batch * nk, 4 * SSM_STATE), F32)] * 3,
        compiler_params=_cparams(("parallel",)),
        name="ssm_scan",
    )(u5, g_mat, p_mat, q_mat, a1, a2)
    return y5.reshape(SSM_WIDTH, batch * length)


def _ssm_tables(a_re, a_im, log_step, b_re, b_im, c_re, c_im):
    t = SSM_CHUNK
    hi = lax.Precision.HIGHEST
    lam = lax.complex(a_re.astype(F32), a_im.astype(F32))
    step = jnp.exp(log_step.astype(F32))[..., None]
    ls = lam * step
    a_bar = jnp.exp(ls)
    b_bar = ((a_bar - 1.0) / lam)[..., None] * lax.complex(b_re.astype(F32), b_im.astype(F32))
    c = lax.complex(c_re.astype(F32), c_im.astype(F32))
    tau = jnp.arange(t + 1, dtype=F32)
    pw = jnp.exp(ls[..., None] * tau)
    kern = jnp.einsum('dgcp,dgpt,dgpe->dgtce', c, pw[..., :t], b_bar, precision=hi).real
    kf = kern[0].at[:, 0].add(kern[1][:, 0]).transpose(0, 3, 2, 1)
    kb = jnp.roll(kern[1][:, ::-1], 1, axis=1).transpose(0, 3, 2, 1)
    pw_f, pw_b = pw[0], pw[1]
    pf = pw_f[:, :, t - 1 - jnp.arange(t)][..., None] * b_bar[0][:, :, None, :]
    pb = pw_b[:, :, :t][..., None] * b_bar[1][:, :, None, :]
    to_rows = lambda z: z.transpose(0, 3, 2, 1).reshape(SSM_GROUPS, SSM_GROUP * t, SSM_STATE)
    p_mat = jnp.concatenate([to_rows(pf.real), to_rows(pf.imag), to_rows(pb.real), to_rows(pb.imag)], axis=-1)
    qf = c[0].transpose(0, 2, 1)[..., None] * pw_f[:, :, 1:][:, :, None, :]
    qb = c[1].transpose(0, 2, 1)[..., None] * pw_b[:, :, t - jnp.arange(t)][:, :, None, :]
    to_cols = lambda z: z.reshape(SSM_GROUPS, SSM_STATE, SSM_GROUP * t)
    q_mat = jnp.concatenate([to_cols(qf.real), -to_cols(qf.imag), to_cols(qb.real), -to_cols(qb.imag)], axis=1)
    at = pw[..., t]
    a1 = jnp.concatenate([at[0].real, at[0].real, at[1].real, at[1].real], axis=-1)[:, None, :]
    a2 = jnp.concatenate([-at[0].imag, at[0].imag, -at[1].imag, at[1].imag], axis=-1)[:, None, :]
    return _toeplitz(kf, kb), p_mat.astype(BF16), q_mat.astype(BF16), a1, a2


def _dft_tables(length):
    j = DFT_INNER
    n = 2 * length
    rn = n // j
    rh = rn // 2
    odd = 2 * jnp.arange(rh, dtype=jnp.int32) + 1
    r = jnp.arange(rn, dtype=jnp.int32)
    ang1 = (math.pi / rn) * ((odd[:, None] * r[None, :]) % (2 * rn)).astype(F32)
    c1, s1 = jnp.cos(ang1), jnp.sin(ang1)
    f1_full = jnp.concatenate([c1, -s1], axis=0)
    f1_top = f1_full[:, :rh]
    f1_inv = jnp.concatenate([c1[:, :rh].T, -s1[:, :rh].T], axis=1)
    jj = jnp.arange(j, dtype=jnp.int32)
    angt = (math.pi / n) * (odd[:, None] * jj[None, :]).astype(F32)
    tw_re, tw_im = jnp.cos(angt), -jnp.sin(angt)
    ang2 = (2.0 * math.pi / j) * ((jj[:, None] * jj[None, :]) % j).astype(F32)
    c2, s2 = jnp.cos(ang2), jnp.sin(ang2)
    f2_fwd = jnp.block([[c2, -s2], [s2, c2]])
    f2_inv = jnp.block([[c2, s2], [-s2, c2]])
    return dict(f1_full=f1_full.astype(BF16), f1_top=f1_top.astype(BF16), f1_inv=f1_inv.astype(BF16),
                tw_re=tw_re, tw_im=tw_im, f2_fwd=f2_fwd.astype(BF16), f2_inv=f2_inv.astype(BF16), rn=rn, rh=rh)


def _twiddle_rows(a, tw_re, tw_im, rk):
    a_re, a_im = a[:rk], a[rk:]
    return jnp.concatenate([a_re * tw_re - a_im * tw_im, a_re * tw_im + a_im * tw_re], axis=1)


def _filter_hidden_kernel(bands_ref, w1t_ref, w1c_ref, w1s_ref, b1_ref, w2t_ref, b2_ref, hid_ref, *, length, tl):
    hi = lax.Precision.HIGHEST
    base = pl.program_id(0) * tl
    idx = (lax.broadcasted_iota(jnp.int32, (1, tl), 1) + base).astype(F32)
    for d in range(2):
        pos = idx if d == 0 else float(length) - idx
        tt = pos / float(length)
        wpos = (2.0 * math.pi / length) * pos
        arg = bands_ref[...] * wpos
        h1 = (w1t_ref[...] * tt
              + jnp.dot(w1c_ref[...], jnp.cos(arg), preferred_element_type=F32, precision=hi)
              - jnp.dot(w1s_ref[...], jnp.sin(arg), preferred_element_type=F32, precision=hi))
        h1 = jnp.sin(h1 + b1_ref[...])
        h2 = jnp.dot(w2t_ref[...], h1, preferred_element_type=F32, precision=hi)
        hid_ref[d] = jnp.sin(h2 + b2_ref[...]).astype(hid_ref.dtype)


def _filter_taps_kernel(hid_ref, w3t_ref, decay_ref, filt_ref, *, length, tl):
    d = pl.program_id(0) % 2
    base = pl.program_id(1) * tl
    idx = (lax.broadcasted_iota(jnp.int32, (1, tl), 1) + base).astype(F32)
    pos = jnp.where(d == 0, idx, float(length) - idx)
    tt = pos / float(length)
    f = jnp.dot(w3t_ref[...], hid_ref[0], preferred_element_type=F32)
    f = f * jnp.exp(-tt * decay_ref[...])
    f = jnp.where(jnp.logical_and(d == 1, idx == 0.0), 0.0, f)
    filt_ref[...] = f.astype(filt_ref.dtype)


def _filter_dft_kernel(filt_ref, f1_ref, twre_ref, twim_ref, f2_ref, kf_ref, *, rk, cb):
    rows = []
    for c in range(cb):
        r = jnp.concatenate([filt_ref[0, 0, c], -filt_ref[0, 1, c]], axis=0)
        a = jnp.dot(f1_ref[...], r, preferred_element_type=F32)
        rows.append(_twiddle_rows(a, twre_ref[...], twim_ref[...], rk))
    x = jnp.dot(jnp.concatenate(rows, axis=0).astype(BF16), f2_ref[...], preferred_element_type=F32)
    for c in range(cb):
        kf_ref[0, c, 0] = x[rk * c:rk * (c + 1), :DFT_INNER].astype(kf_ref.dtype)
        kf_ref[0, c, 1] = x[rk * c:rk * (c + 1), DFT_INNER:].astype(kf_ref.dtype)


def _hyena_filters(length, tabs, w1, b1, w2, b2, w3, log_decay):
    tl = min(FILTER_LANE_TILE, length)
    nl = length // tl
    rn, rh, j = tabs["rn"], tabs["rh"], DFT_INNER
    bands = jnp.linspace(1e-4, FILTER_BANDS - 1, FILTER_BANDS, dtype=F32)[:, None]
    w1f = w1.astype(F32)
    const = lambda shape: pl.BlockSpec(shape, lambda *_: (0,) * len(shape))
    hid = pl.pallas_call(
        functools.partial(_filter_hidden_kernel, length=length, tl=tl),
        grid=(nl,),
        in_specs=[const((FILTER_BANDS, 1)), const((FILTER_HIDDEN, 1)), const((FILTER_HIDDEN, FILTER_BANDS)),
                  const((FILTER_HIDDEN, FILTER_BANDS)), const((FILTER_HIDDEN, 1)),
                  const((FILTER_HIDDEN, FILTER_HIDDEN)), const((FILTER_HIDDEN, 1))],
        out_specs=pl.BlockSpec((2, FILTER_HIDDEN, tl), lambda i: (0, 0, i)),
        out_shape=jax.ShapeDtypeStruct((2, FILTER_HIDDEN, length), BF16),
        compiler_params=_cparams(("parallel",)),
        name="hyena_filter_hidden",
    )(bands, w1f[0:1].T, w1f[1:1 + FILTER_BANDS].T, w1f[1 + FILTER_BANDS:].T, b1.astype(F32)[:, None],
      w2.astype(F32).T, b2.astype(F32)[:, None])
    n_od = HYENA_ORDER * 2
    w3t = w3.T.reshape(n_od, HYENA_WIDTH, FILTER_HIDDEN).astype(BF16)
    decay = jnp.exp(log_decay.astype(F32)).reshape(n_od, HYENA_WIDTH, 1)
    filt = pl.pallas_call(
        functools.partial(_filter_taps_kernel, length=length, tl=tl),
        grid=(n_od, nl),
        in_specs=[pl.BlockSpec((1, FILTER_HIDDEN, tl), lambda od, i: (od % 2, 0, i)),
                  pl.BlockSpec((None, HYENA_WIDTH, FILTER_HIDDEN), lambda od, i: (od, 0, 0)),
                  pl.BlockSpec((None, HYENA_WIDTH, 1), lambda od, i: (od, 0, 0))],
        out_specs=pl.BlockSpec((None, HYENA_WIDTH, tl), lambda od, i: (od, 0, i)),
        out_shape=jax.ShapeDtypeStruct((n_od, HYENA_WIDTH, length), BF16),
        compiler_params=_cparams(("parallel", "parallel")),
        name="hyena_filter_taps",
    )(hid, w3t, decay)
    filt6 = filt.reshape(HYENA_ORDER, 2, HYENA_WIDTH, rh, j)
    cb = FILTER_CH_BLOCK
    return pl.pallas_call(
        functools.partial(_filter_dft_kernel, rk=rh, cb=cb),
        grid=(HYENA_ORDER, HYENA_WIDTH // cb),
        in_specs=[pl.BlockSpec((1, 2, cb, rh, j), lambda o, c: (o, 0, c, 0, 0)),
                  const((rn, rn)), const((rh, j)), const((rh, j)), const((2 * j, 2 * j))],
        out_specs=pl.BlockSpec((1, cb, 2, rh, j), lambda o, c: (o, c, 0, 0, 0)),
        out_shape=jax.ShapeDtypeStruct((HYENA_ORDER, HYENA_WIDTH, 2, rh, j), BF16),
        compiler_params=_cparams(("parallel", "parallel")),
        name="hyena_filter_dft",
    )(filt6, tabs["f1_full"], tabs["tw_re"], tabs["tw_im"], tabs["f2_fwd"])


def _hyena_kernel(scw_ref, scb_ref, hb_ref, x_ref, hg_ref, kf_ref, f1_ref, f1i_ref, twre_ref, twim_ref, f2_ref,
                  f2i_ref, o_ref, *, batch, rn, rh, cb, cs):
    j = DFT_INNER
    n_inv = 2.0 / (rn * j)
    rk = rh
    lane = lax.broadcasted_iota(jnp.int32, (rh, j), 1)
    row = lax.broadcasted_iota(jnp.int32, (rh, j), 0)
    first = jnp.logical_and(lane == 0, row == 0)
    last = jnp.logical_and(lane == j - 1, row == rh - 1)
    tw_re, tw_im = twre_ref[...], twim_ref[...]

    def prev_t(x):
        p = pltpu.roll(x, 1, 1)
        return jnp.where(first, 0.0, jnp.where(lane == 0, pltpu.roll(p, 1, 0), p))

    def next_t(x):
        p = pltpu.roll(x, j - 1, 1)
        return jnp.where(last, 0.0, jnp.where(lane == j - 1, pltpu.roll(p, rh - 1, 0), p))

    def split(m, i):
        return m[rk * i:rk * (i + 1), :j], m[rk * i:rk * (i + 1), j:]

    def long_conv(xs, ks):
        rows = []
        for x in xs:
            a = jnp.dot(f1_ref[...], x.astype(BF16), preferred_element_type=F32)
            rows.append(_twiddle_rows(a, tw_re, tw_im, rk))
        m = jnp.dot(jnp.concatenate(rows, axis=0).astype(BF16), f2_ref[...], preferred_element_type=F32)
        rows = []
        for i, (k_re, k_im) in enumerate(ks):
            x_re, x_im = split(m, i)
            rows.append(jnp.concatenate([x_re * k_re - x_im * k_im, x_re * k_im + x_im * k_re], axis=1))
        m = jnp.dot(jnp.concatenate(rows, axis=0).astype(BF16), f2i_ref[...], preferred_element_type=F32)
        outs = []
        for i in range(len(xs)):
            b_re, b_im = split(m, i)
            bp = jnp.concatenate([b_re * tw_re + b_im * tw_im, b_im * tw_re - b_re * tw_im], axis=0).astype(BF16)
            outs.append(jnp.dot(f1i_ref[...], bp, preferred_element_type=F32) * n_inv)
        return outs

    cbase = pl.program_id(0) * cb
    for c0 in range(0, cb, cs):
        chans = [(ci, b) for ci in range(c0, c0 + cs) for b in range(batch)]
        segs = []
        for sgm in range(3):
            seg = []
            for ci, b in chans:
                chs = sgm * HYENA_WIDTH + cbase + ci
                x = x_ref[sgm, ci, b]
                seg.append(prev_t(x) * scw_ref[chs] + x * scw_ref[3 * HYENA_WIDTH + chs]
                           + next_t(x) * scw_ref[6 * HYENA_WIDTH + chs] + scb_ref[chs])
            segs.append(seg)
        z = segs[0]
        for o in range(HYENA_ORDER):
            conv = long_conv(z, [(kf_ref[o, ci, 0].astype(F32), kf_ref[o, ci, 1].astype(F32)) for ci, _ in chans])
            z = [segs[o + 1][i] * (conv[i] + z[i] * hb_ref[o * HYENA_WIDTH + cbase + ci])
                 for i, (ci, _) in enumerate(chans)]
        for i, (ci, b) in enumerate(chans):
            o_ref[ci, b] = z[i] * hg_ref[ci, b]


def _hyena(hi_t, hg_t, kf, tabs, short_w, short_b, hy_bias, batch, length):
    rn, rh, j = tabs["rn"], tabs["rh"], DFT_INNER
    cb = HYENA_CH_BLOCK
    x5 = hi_t.reshape(3, HYENA_WIDTH, batch, rh, j)
    g4 = hg_t.reshape(HYENA_WIDTH, batch, rh, j)
    smem = pl.BlockSpec(memory_space=pltpu.SMEM)
    const = lambda shape: pl.BlockSpec(shape, lambda c: (0,) * len(shape))
    out = pl.pallas_call(
        functools.partial(_hyena_kernel, batch=batch, rn=rn, rh=rh, cb=cb, cs=HYENA_CH_SUB),
        grid=(HYENA_WIDTH // cb,),
        in_specs=[smem, smem, smem,
                  pl.BlockSpec((3, cb, batch, rh, j), lambda c: (0, c, 0, 0, 0)),
                  pl.BlockSpec((cb, batch, rh, j), lambda c: (c, 0, 0, 0)),
                  pl.BlockSpec((HYENA_ORDER, cb, 2, rh, j), lambda c: (0, c, 0, 0, 0)),
                  const((rn, rh)), const((rh, rn)), const((rh, j)), const((rh, j)),
                  const((2 * j, 2 * j)), const((2 * j, 2 * j))],
        out_specs=pl.BlockSpec((cb, batch, rh, j), lambda c: (c, 0, 0, 0)),
        out_shape=jax.ShapeDtypeStruct(g4.shape, F32),
        compiler_params=_cparams(("parallel",)),
        name="hyena_conv",
    )(short_w.astype(F32).reshape(-1), short_b.astype(F32), hy_bias.astype(F32).reshape(-1),
      x5, g4, kf, tabs["f1_top"], tabs["f1_inv"], tabs["tw_re"], tabs["tw_im"], tabs["f2_fwd"], tabs["f2_inv"])
    return out.reshape(HYENA_WIDTH, batch * length)


def _rms_rows(x, g):
    return x * lax.rsqrt(jnp.mean(x * x, axis=-1, keepdims=True) + EPS) * g


def _rms_cols(x, g):
    return x * lax.rsqrt(jnp.mean(x * x, axis=0, keepdims=True) + EPS) * g


def _post_kernel(ao_ref, ga_ref, mq_ref, mk_ref, mv_ref, y_ref, u_ref, sg_ref, hy_ref, x_ref,
                 d_ref, wglu_ref, g_attn_ref, g_ssm_ref, g_hy_ref, g_mem_ref, post_g_ref, wo_ref, out_ref):
    attn_n = _rms_cols(ao_ref[...] * ga_ref[...], g_attn_ref[...])
    s = jnp.dot(mq_ref[...], mk_ref[0], preferred_element_type=F32)
    ps = []
    for h in range(MEM_HEADS):
        sh = s[:, MEM_TOKENS * h:MEM_TOKENS * (h + 1)]
        e = jnp.exp(sh - jnp.max(sh, axis=-1, keepdims=True))
        ps.append(e * (1.0 / jnp.sum(e, axis=-1, keepdims=True)))
    p = jnp.concatenate(ps, axis=1).astype(BF16)
    cross_n = _rms_rows(jnp.dot(p, mv_ref[0], preferred_element_type=F32), g_mem_ref[...])
    y = y_ref[...] + d_ref[...] * u_ref[...]
    g = y * (0.5 * (1.0 + jnp.tanh(math.sqrt(2.0 / math.pi) * (y + 0.044715 * (y * y * y)))))
    gz = jnp.dot(wglu_ref[...], g.astype(BF16), preferred_element_type=F32)
    ssm_n = _rms_cols(g * _sigmoid(gz) * sg_ref[...], g_ssm_ref[...])
    hy_n = _rms_cols(hy_ref[...], g_hy_ref[...])
    o1, o2, o3 = ATTN_WIDTH, ATTN_WIDTH + SSM_WIDTH, ATTN_WIDTH + SSM_WIDTH + HYENA_WIDTH
    mixed = (jnp.dot(attn_n.T.astype(BF16), wo_ref[0:o1], preferred_element_type=F32)
             + jnp.dot(ssm_n.T.astype(BF16), wo_ref[o1:o2], preferred_element_type=F32)
             + jnp.dot(hy_n.T.astype(BF16), wo_ref[o2:o3], preferred_element_type=F32)
             + jnp.dot(cross_n.astype(BF16), wo_ref[o3:], preferred_element_type=F32))
    out_ref[...] = x_ref[...] + _rms_rows(mixed, post_g_ref[...])


def _post(ao, ga, mq, mk_bd, mv_bd, y_t, u_t, sg_t, hy_t, x2, d, wglu_t, g_attn, g_ssm, g_hy, g_mem, post_g, wo,
          length):
    n = x2.shape[0]
    tm = TOKEN_TILE
    tiles_per_seq = length // tm
    tok_spec = lambda w: pl.BlockSpec((tm, w), lambda i: (i, 0))
    ch_spec = lambda w: pl.BlockSpec((w, tm), lambda i: (0, i))
    const = lambda shape: pl.BlockSpec(shape, lambda i: (0,) * len(shape))
    hm = MEM_HEADS * MEM_TOKENS
    return pl.pallas_call(
        _post_kernel,
        grid=(n // tm,),
        in_specs=[ch_spec(ATTN_WIDTH), ch_spec(ATTN_WIDTH), tok_spec(MEM_WIDTH),
                  pl.BlockSpec((1, MEM_WIDTH, hm), lambda i: (i // tiles_per_seq, 0, 0)),
                  pl.BlockSpec((1, hm, MEM_WIDTH), lambda i: (i // tiles_per_seq, 0, 0)),
                  ch_spec(SSM_WIDTH), ch_spec(SSM_WIDTH), ch_spec(SSM_WIDTH), ch_spec(HYENA_WIDTH),
                  tok_spec(D_MODEL),
                  const((SSM_WIDTH, 1)), const((SSM_WIDTH, SSM_WIDTH)),
                  const((ATTN_WIDTH, 1)), const((SSM_WIDTH, 1)), const((HYENA_WIDTH, 1)), const((1, MEM_WIDTH)),
                  const((1, D_MODEL)), const((MIX_WIDTH, D_MODEL))],
        out_specs=tok_spec(D_MODEL),
        out_shape=jax.ShapeDtypeStruct((n, D_MODEL), F32),
        compiler_params=_cparams(("parallel",)),
        name="post",
    )(ao, ga, mq, mk_bd, mv_bd, y_t, u_t, sg_t, hy_t, x2, d, wglu_t, g_attn, g_ssm, g_hy, g_mem, post_g, wo)


def _rope_tables(length):
    rows = length // GRID_W
    row = jnp.broadcast_to(jnp.arange(rows, dtype=F32)[:, None], (rows, GRID_W)).reshape(length)
    col = jnp.broadcast_to(jnp.arange(GRID_W, dtype=F32)[None, :], (rows, GRID_W)).reshape(length)
    inv_freq = ROPE_THETA ** (-jnp.arange(ROPE_FREQS, dtype=F32) / ROPE_FREQS)
    ang = jnp.stack([row[:, None] * inv_freq, col[:, None] * inv_freq], axis=1)
    ang = jnp.broadcast_to(ang[:, :, None, :], (length, 2, 2, ROPE_FREQS)).reshape(length, HEAD_DIM)
    cos1, sin1 = jnp.cos(ang), jnp.sin(ang)
    low1 = (jnp.arange(HEAD_DIM) % (2 * ROPE_FREQS)) < ROPE_FREQS
    cos_t, sin_t = cos1.T, jnp.where(low1, -sin1, sin1).T
    cos = jnp.concatenate([cos1, cos1], axis=1)
    sin = jnp.concatenate([sin1, sin1], axis=1)
    low = jnp.concatenate([low1, low1])
    return cos, jnp.where(low, -sin, 0.0), jnp.where(low, 0.0, sin), cos_t, sin_t


def _block_diag_heads(mk, mv, batch):
    mk4 = mk.reshape(batch, MEM_TOKENS, MEM_HEADS, HEAD_DIM)
    mv4 = mv.reshape(batch, MEM_TOKENS, MEM_HEADS, HEAD_DIM)
    eye = jnp.eye(MEM_HEADS, dtype=mk.dtype)
    k_bd = jnp.einsum('bmhd,hg->bhdgm', mk4, eye).reshape(batch, MEM_WIDTH, MEM_HEADS * MEM_TOKENS)
    v_bd = jnp.einsum('bmhd,hg->bhmgd', mv4, eye).reshape(batch, MEM_HEADS * MEM_TOKENS, MEM_WIDTH)
    return k_bd.astype(BF16), v_bd.astype(BF16)


def _layer_weights(layer, p):
    w_in = p["w_in"][layer]
    a, kv = ATTN_WIDTH, KV_WIDTH
    k0, v0, g0, mq0 = a, a + kv, a + 2 * kv, w_in.shape[1] - MEM_WIDTH
    w_tok = jnp.concatenate([w_in[:, k0:v0], w_in[:, mq0:]], axis=1).astype(BF16)
    w_ch_t = jnp.concatenate([w_in[:, :k0], w_in[:, v0:mq0]], axis=1).T.astype(BF16)
    bg = p["branch_norm"][layer].astype(F32)
    o1, o2, o3 = a, a + SSM_WIDTH, a + SSM_WIDTH + HYENA_WIDTH
    head_id = jnp.arange(kv) // HEAD_DIM
    return dict(
        w_tok=w_tok, w_ch_t=w_ch_t,
        pre_g=p["pre_norm"][layer].astype(F32)[None, :], post_g=p["post_norm"][layer].astype(F32)[None, :],
        qg=p["q_norm"][layer].astype(F32)[:, None],
        kg=jnp.tile(p["k_norm"][layer].astype(F32), ATTN_KV_HEADS)[None, :],
        ones=(head_id[:, None] == head_id[None, :]).astype(BF16),
        mem_g=p["mem_norm"][layer].astype(F32)[None, :], w_mem_kv=p["w_mem_kv"][layer].astype(BF16),
        ssm=_ssm_tables(p["ssm_a_re"][layer], p["ssm_a_im"][layer], p["ssm_log_step"][layer], p["ssm_b_re"][layer],
                        p["ssm_b_im"][layer], p["ssm_c_re"][layer], p["ssm_c_im"][layer]),
        d=p["ssm_d"][layer].astype(F32)[:, None], wglu_t=p["ssm_w_glu"][layer].T.astype(BF16),
        g_attn=bg[:o1, None], g_ssm=bg[o1:o2, None], g_hy=bg[o2:o3, None], g_mem=bg[None, o3:],
        wo=p["w_out"][layer].astype(BF16),
    )


def _mixer_layer(x2, mem2, lw, kf, tabs, rope, p, layer, batch, length):
    q_t, k, v_t, ga, mq, su_t, sg_t, hi_t, hg_t = _inproj(x2, lw["pre_g"], lw["w_tok"], lw["w_ch_t"], rope,
                                                          lw["qg"], lw["kg"], lw["ones"], length)
    ao = _flash_attention(q_t, k, v_t, batch, length)
    mem_kv = _memkv(mem2, lw["mem_g"], lw["w_mem_kv"])
    mk_bd, mv_bd = _block_diag_heads(mem_kv[:, :MEM_WIDTH], mem_kv[:, MEM_WIDTH:], batch)
    y_t = _ssm_scan(su_t, *lw["ssm"], batch, length)
    hy_t = _hyena(hi_t, hg_t, kf, tabs, p["hyena_short_w"][layer], p["hyena_short_b"][layer],
                  p["hyena_bias"][layer], batch, length)
    return _post(ao, ga, mq, mk_bd, mv_bd, y_t, su_t, sg_t, hy_t, x2, lw["d"], lw["wglu_t"],
                 lw["g_attn"], lw["g_ssm"], lw["g_hy"], lw["g_mem"], lw["post_g"], lw["wo"], length)


def _run_group(x, mem, weights, p):
    batch, length, _ = x.shape
    rope = _rope_tables(length)
    tabs = _dft_tables(length)
    x2 = x.reshape(batch * length, D_MODEL)
    mem2 = mem.reshape(batch * MEM_TOKENS, D_MODEL)
    for layer in range(DEPTH):
        kf = _hyena_filters(length, tabs, p["hyena_ffn_w1"][layer], p["hyena_ffn_b1"][layer],
                            p["hyena_ffn_w2"][layer], p["hyena_ffn_b2"][layer], p["hyena_ffn_w3"][layer],
                            p["hyena_log_decay"][layer])
        x2 = _mixer_layer(x2, mem2, weights[layer], kf, tabs, rope, p, layer, batch, length)
    return x2.reshape(batch, length, D_MODEL)


def kernel(x_prompt, x_sample, mem_prompt, mem_sample, pre_norm, post_norm, w_in, q_norm, k_norm, mem_norm, w_mem_kv, ssm_a_re, ssm_a_im, ssm_log_step, ssm_b_re, ssm_b_im, ssm_c_re, ssm_c_im, ssm_d, ssm_w_glu, hyena_short_w, hyena_short_b, hyena_ffn_w1, hyena_ffn_b1, hyena_ffn_w2, hyena_ffn_b2, hyena_ffn_w3, hyena_log_decay, hyena_bias, branch_norm, w_out):
    p = dict(pre_norm=pre_norm, post_norm=post_norm, w_in=w_in, q_norm=q_norm, k_norm=k_norm, mem_norm=mem_norm,
             w_mem_kv=w_mem_kv, ssm_a_re=ssm_a_re, ssm_a_im=ssm_a_im, ssm_log_step=ssm_log_step, ssm_b_re=ssm_b_re,
             ssm_b_im=ssm_b_im, ssm_c_re=ssm_c_re, ssm_c_im=ssm_c_im, ssm_d=ssm_d, ssm_w_glu=ssm_w_glu,
             hyena_short_w=hyena_short_w, hyena_short_b=hyena_short_b, hyena_ffn_w1=hyena_ffn_w1,
             hyena_ffn_b1=hyena_ffn_b1, hyena_ffn_w2=hyena_ffn_w2, hyena_ffn_b2=hyena_ffn_b2,
             hyena_ffn_w3=hyena_ffn_w3, hyena_log_decay=hyena_log_decay, hyena_bias=hyena_bias,
             branch_norm=branch_norm, w_out=w_out)
    weights = [_layer_weights(layer, p) for layer in range(DEPTH)]
    return (_run_group(x_prompt, mem_prompt, weights, p), _run_group(x_sample, mem_sample, weights, p))
```

```python
import functools
import math

import jax
import jax.numpy as jnp
import numpy as np
from jax import lax
from jax.experimental import pallas as pl
from jax.experimental.pallas import tpu as pltpu

F32 = jnp.float32
BF16 = jnp.bfloat16

D_MODEL = 1024
DEPTH = 2
GRID_W = 64
HEAD_DIM = 64
ATTN_HEADS = 8
ATTN_KV_HEADS = 2
ATTN_GROUP = ATTN_HEADS // ATTN_KV_HEADS
ATTN_WIDTH = ATTN_HEADS * HEAD_DIM
KV_WIDTH = ATTN_KV_HEADS * HEAD_DIM
ROPE_THETA = 10000.0
ROPE_FREQS = HEAD_DIM // 4
SSM_GROUP = 16
SSM_GROUPS = 24
SSM_WIDTH = SSM_GROUP * SSM_GROUPS
SSM_STATE = 64
HYENA_WIDTH = 384
HYENA_ORDER = 2
FILTER_BANDS = 16
FILTER_HIDDEN = 64
MEM_TOKENS = 256
MEM_HEADS = 4
MEM_WIDTH = MEM_HEADS * HEAD_DIM
MIX_WIDTH = ATTN_WIDTH + SSM_WIDTH + HYENA_WIDTH + MEM_WIDTH
EPS = 1e-6

TOK_WIDTH = KV_WIDTH + MEM_WIDTH
CH_WIDTH = 2 * ATTN_WIDTH + KV_WIDTH + 2 * SSM_WIDTH + (HYENA_ORDER + 2) * HYENA_WIDTH
V_ROWS = HEAD_DIM + 16
Q_SCALE = HEAD_DIM ** -0.5 * math.log2(math.e)
SOFTMAX_STATIC_BOUND = 60.0

LANES = 128
SUBLANES = 8
VMEM_LIMIT = 56 * 1024 * 1024
TOKEN_TILE = 512
ATTN_Q_TILE = 256
ATTN_K_TILE = 1024
SSM_CHUNK = LANES
DFT_INNER = 256
HYENA_CH_BLOCK = 8
FILTER_CH_BLOCK = 16
HYENA_CH_SUB = 8
FILTER_LANE_TILE = 2048


def _cparams(sem):
    return pltpu.CompilerParams(dimension_semantics=sem, vmem_limit_bytes=VMEM_LIMIT)


def _silu(x):
    return x * (1.0 / (1.0 + jnp.exp(-x)))


def _sigmoid(x):
    return 1.0 / (1.0 + jnp.exp(-x))


def _nt_dot(a, b):
    return lax.dot_general(a, b, (((1,), (1,)), ((), ())), preferred_element_type=F32)


def _rope_128(xn, cos, s_lo, s_hi):
    outs = []
    for c in range(xn.shape[1] // LANES):
        xc = xn[:, LANES * c:LANES * (c + 1)]
        outs.append(xc * cos + pltpu.roll(xc, LANES - ROPE_FREQS, 1) * s_lo + pltpu.roll(xc, ROPE_FREQS, 1) * s_hi)
    return outs[0] if len(outs) == 1 else jnp.concatenate(outs, axis=1)


def _inproj_kernel(x_ref, pre_g_ref, wtok_ref, wch_ref, cos_ref, slo_ref, shi_ref, cost_ref, sint_ref,
                   qg_ref, kg_ref, ones_ref,
                   qt_ref, k_ref, vt_ref, ga_ref, mq_ref, su_ref, sg_ref, hi_ref, hg_ref):
    tm = x_ref.shape[0]
    x = x_ref[...]
    h = x * lax.rsqrt(jnp.mean(x * x, axis=-1, keepdims=True) + EPS) * pre_g_ref[...]
    hb = h.astype(BF16)
    tok = jnp.dot(hb, wtok_ref[...], preferred_element_type=F32)
    k = tok[:, 0:KV_WIDTH]
    mq = tok[:, KV_WIDTH:]
    k_ms = jnp.dot((k * k).astype(BF16), ones_ref[...], preferred_element_type=F32) * (1.0 / HEAD_DIM)
    kn = k * lax.rsqrt(k_ms + EPS) * kg_ref[...]
    kr = _rope_128(kn, cos_ref[...], slo_ref[...], shi_ref[...])
    for j in range(ATTN_KV_HEADS):
        k_ref[j] = kr[:, HEAD_DIM * j:HEAD_DIM * (j + 1)].astype(BF16)
    mq_ref[...] = (mq * (HEAD_DIM ** -0.5)).astype(BF16)
    ch = _nt_dot(wch_ref[...], hb)
    q3 = ch[0:ATTN_WIDTH].reshape(ATTN_HEADS, HEAD_DIM, tm)
    qn = q3 * lax.rsqrt(jnp.mean(q3 * q3, axis=1, keepdims=True) + EPS) * qg_ref[...][None]
    f = ROPE_FREQS
    rot = jnp.concatenate([qn[:, f:2 * f], qn[:, 0:f], qn[:, 3 * f:4 * f], qn[:, 2 * f:3 * f]], axis=1)
    qr = (qn * cost_ref[...][None] + rot * sint_ref[...][None]) * Q_SCALE
    qt_ref[...] = qr.reshape(ATTN_WIDTH, tm).astype(BF16)
    o = ATTN_WIDTH
    ones_row = (lax.broadcasted_iota(jnp.int32, (V_ROWS - HEAD_DIM, tm), 0) == 0).astype(BF16)
    for j in range(ATTN_KV_HEADS):
        vt_ref[j, 0:HEAD_DIM] = ch[o + HEAD_DIM * j:o + HEAD_DIM * (j + 1)].astype(BF16)
        vt_ref[j, HEAD_DIM:V_ROWS] = ones_row
    o += KV_WIDTH
    ga_ref[...] = _silu(ch[o:o + ATTN_WIDTH])
    o += ATTN_WIDTH
    su_ref[...] = ch[o:o + SSM_WIDTH]
    sg_ref[...] = _silu(ch[o + SSM_WIDTH:o + 2 * SSM_WIDTH])
    o += 2 * SSM_WIDTH
    hi_ref[...] = ch[o:o + 3 * HYENA_WIDTH]
    hg_ref[...] = _silu(ch[o + 3 * HYENA_WIDTH:])


def _inproj(x2, pre_g, w_tok, w_ch_t, rope, qg, kg, ones, length):
    cos, s_lo, s_hi, cos_t, sin_t = rope
    n = x2.shape[0]
    tm = TOKEN_TILE
    nt = n // tm
    tiles_per_seq = length // tm
    tok_spec = lambda w: pl.BlockSpec((tm, w), lambda i: (i, 0))
    ch_spec = lambda w: pl.BlockSpec((w, tm), lambda i: (0, i))
    const = lambda shape: pl.BlockSpec(shape, lambda i: (0,) * len(shape))
    pos_spec = pl.BlockSpec((tm, LANES), lambda i: (i % tiles_per_seq, 0))
    pos_t_spec = pl.BlockSpec((HEAD_DIM, tm), lambda i: (0, i % tiles_per_seq))
    return pl.pallas_call(
        _inproj_kernel,
        grid=(nt,),
        in_specs=[tok_spec(D_MODEL), const((1, D_MODEL)), const((D_MODEL, TOK_WIDTH)), const((CH_WIDTH, D_MODEL)),
                  pos_spec, pos_spec, pos_spec, pos_t_spec, pos_t_spec,
                  const((HEAD_DIM, 1)), const((1, KV_WIDTH)), const((KV_WIDTH, KV_WIDTH))],
        out_specs=[ch_spec(ATTN_WIDTH),
                   pl.BlockSpec((ATTN_KV_HEADS, tm, HEAD_DIM), lambda i: (0, i, 0)),
                   pl.BlockSpec((ATTN_KV_HEADS, V_ROWS, tm), lambda i: (0, 0, i)),
                   ch_spec(ATTN_WIDTH), tok_spec(MEM_WIDTH),
                   ch_spec(SSM_WIDTH), ch_spec(SSM_WIDTH), ch_spec(3 * HYENA_WIDTH), ch_spec(HYENA_WIDTH)],
        out_shape=[jax.ShapeDtypeStruct((ATTN_WIDTH, n), BF16),
                   jax.ShapeDtypeStruct((ATTN_KV_HEADS, n, HEAD_DIM), BF16),
                   jax.ShapeDtypeStruct((ATTN_KV_HEADS, V_ROWS, n), BF16),
                   jax.ShapeDtypeStruct((ATTN_WIDTH, n), F32),
                   jax.ShapeDtypeStruct((n, MEM_WIDTH), BF16),
                   jax.ShapeDtypeStruct((SSM_WIDTH, n), F32),
                   jax.ShapeDtypeStruct((SSM_WIDTH, n), F32),
                   jax.ShapeDtypeStruct((3 * HYENA_WIDTH, n), F32),
                   jax.ShapeDtypeStruct((HYENA_WIDTH, n), F32)],
        compiler_params=_cparams(("parallel",)),
        name="inproj",
    )(x2, pre_g, w_tok, w_ch_t, cos, s_lo, s_hi, cos_t, sin_t, qg, kg, ones)


def _flash_kernel(qt_ref, k_ref, vt_ref, o_ref, acc_s, s_s, kmax_s, *, tq, tk, n_kv):
    w = ATTN_GROUP * tq
    q4t = jnp.concatenate([qt_ref[HEAD_DIM * i:HEAD_DIM * (i + 1), :] for i in range(ATTN_GROUP)], axis=1)

    def chunk(ref_slice, c):
        return ref_slice(pl.ds(pl.multiple_of(c * tk, tk), tk))

    k_chunk = lambda c: chunk(lambda d: k_ref[0, d, :], c)
    v_chunk = lambda c: chunk(lambda d: vt_ref[0, :, d], c)

    @pl.when(pl.program_id(2) == 0)
    def _():
        def key_norm(c, mx):
            kc = k_chunk(c).astype(F32)
            return jnp.maximum(mx, jnp.max(jnp.sum(kc * kc, axis=1, keepdims=True), axis=0, keepdims=True))

        mx = lax.fori_loop(0, n_kv, key_norm, jnp.zeros((1, 1), F32))
        kmax_s[...] = jnp.broadcast_to(jnp.sqrt(mx), kmax_s.shape)

    qf = q4t.astype(F32)
    bound = jnp.sqrt(jnp.sum(qf * qf, axis=0, keepdims=True)) * kmax_s[:, 0:1] * (1.0 + 2.0 ** -10)
    bound_max = jnp.max(bound)

    def finish():
        o = acc_s[0:HEAD_DIM] * (1.0 / acc_s[HEAD_DIM:HEAD_DIM + 1])
        for i in range(ATTN_GROUP):
            o_ref[HEAD_DIM * i:HEAD_DIM * (i + 1), :] = o[:, i * tq:(i + 1) * tq]

    @pl.when(bound_max <= SOFTMAX_STATIC_BOUND)
    def _():
        acc_s[...] = jnp.zeros(acc_s.shape, F32)

        def body(c, carry):
            s = jnp.dot(k_chunk(c), q4t, preferred_element_type=F32)
            p = jnp.exp2(s - bound).astype(BF16)
            acc_s[...] += jnp.dot(v_chunk(c), p, preferred_element_type=F32)
            return carry

        lax.fori_loop(0, n_kv, body, 0, unroll=8)
        finish()

    @pl.when(jnp.logical_not(bound_max <= SOFTMAX_STATIC_BOUND))
    def _():
        acc_s[...] = jnp.zeros(acc_s.shape, F32)

        def scores(c, slot):
            s = jnp.dot(k_chunk(c), q4t, preferred_element_type=F32)
            s_s[slot] = s
            return jnp.max(s, axis=0, keepdims=True)

        def consume(c, slot, m, m_chunk):
            m_new = jnp.maximum(m, m_chunk)
            p = jnp.exp2(s_s[slot] - m_new).astype(BF16)
            pv = jnp.dot(v_chunk(c), p, preferred_element_type=F32)
            acc_s[...] = jnp.exp2(m - m_new) * acc_s[...] + pv
            return m_new

        def body(cc, carry):
            m, mc0 = carry
            c = 2 * cc
            mc1 = scores(c + 1, 1)
            m = consume(c, 0, m, mc0)
            mc0 = scores(c + 2, 0)
            return consume(c + 1, 1, m, mc1), mc0

        m, mc0 = lax.fori_loop(0, n_kv // 2 - 1, body, (jnp.full((1, w), -jnp.inf, F32), scores(0, 0)), unroll=2)
        mc1 = scores(n_kv - 1, 1)
        consume(n_kv - 1, 1, consume(n_kv - 2, 0, m, mc0), mc1)
        finish()


def _flash_attention(q_t, k, v_t, batch, length):
    n = q_t.shape[1]
    tq, tk = ATTN_Q_TILE, ATTN_K_TILE
    nq, nk = length // tq, length // tk
    gw = ATTN_GROUP * HEAD_DIM
    return pl.pallas_call(
        functools.partial(_flash_kernel, tq=tq, tk=tk, n_kv=nk),
        grid=(batch, ATTN_KV_HEADS, nq),
        in_specs=[pl.BlockSpec((gw, tq), lambda b, h, i: (h, b * nq + i)),
                  pl.BlockSpec((1, length, HEAD_DIM), lambda b, h, i: (h, b, 0)),
                  pl.BlockSpec((1, V_ROWS, length), lambda b, h, i: (h, 0, b))],
        out_specs=pl.BlockSpec((gw, tq), lambda b, h, i: (h, b * nq + i)),
        out_shape=jax.ShapeDtypeStruct((ATTN_WIDTH, n), F32),
        scratch_shapes=[pltpu.VMEM((V_ROWS, ATTN_GROUP * tq), F32),
                        pltpu.VMEM((2, tk, ATTN_GROUP * tq), F32),
                        pltpu.VMEM((1, LANES), F32)],
        compiler_params=_cparams(("parallel", "parallel", "arbitrary")),
        name="flash_attn",
    )(q_t, k, v_t)


def _memkv_kernel(mem_ref, g_ref, w_ref, kv_ref):
    m = mem_ref[...]
    mn = m * lax.rsqrt(jnp.mean(m * m, axis=-1, keepdims=True) + EPS) * g_ref[...]
    kv_ref[...] = jnp.dot(mn.astype(BF16), w_ref[...], preferred_element_type=F32)


def _memkv(mem2, mem_g, w_kv):
    rows = mem2.shape[0]
    return pl.pallas_call(
        _memkv_kernel,
        grid=(rows // MEM_TOKENS,),
        in_specs=[pl.BlockSpec((MEM_TOKENS, D_MODEL), lambda i: (i, 0)),
                  pl.BlockSpec((1, D_MODEL), lambda i: (0, 0)),
                  pl.BlockSpec((D_MODEL, 2 * MEM_WIDTH), lambda i: (0, 0))],
        out_specs=pl.BlockSpec((MEM_TOKENS, 2 * MEM_WIDTH), lambda i: (i, 0)),
        out_shape=jax.ShapeDtypeStruct((rows, 2 * MEM_WIDTH), F32),
        compiler_params=_cparams(("parallel",)),
        name="mem_kv",
    )(mem2, mem_g, w_kv)


def _toeplitz_kernel(kf_ref, kb_ref, g_ref):
    t = SSM_CHUNK
    causal = lax.broadcasted_iota(jnp.int32, (t, t), 1) >= lax.broadcasted_iota(jnp.int32, (t, t), 0)

    def build(cp, carry):
        r0 = pl.multiple_of(cp * t, t)
        kf_rows, kb_rows = kf_ref[0, cp], kb_ref[0, cp]
        for c in range(SSM_GROUP):
            lo = pltpu.roll(jnp.broadcast_to(kf_rows[c:c + 1], (t, t)), 0, 1, stride=1, stride_axis=0)
            up = pltpu.roll(jnp.broadcast_to(kb_rows[c:c + 1], (t, t)), 0, 1, stride=1, stride_axis=0)
            g_ref[0, pl.ds(r0, t), c * t:(c + 1) * t] = jnp.where(causal, lo, up).astype(BF16)
        return carry

    lax.fori_loop(0, SSM_GROUP, build, 0)


def _toeplitz(kf, kb):
    gt = SSM_GROUP * SSM_CHUNK
    lag_spec = pl.BlockSpec((1, SSM_GROUP, SSM_GROUP, SSM_CHUNK), lambda g: (g, 0, 0, 0))
    return pl.pallas_call(
        _toeplitz_kernel,
        grid=(SSM_GROUPS,),
        in_specs=[lag_spec, lag_spec],
        out_specs=pl.BlockSpec((1, gt, gt), lambda g: (g, 0, 0)),
        out_shape=jax.ShapeDtypeStruct((SSM_GROUPS, gt, gt), BF16),
        compiler_params=_cparams(("parallel",)),
        name="ssm_toeplitz",
    )(kf, kb)


def _ssm_kernel(u_ref, g_ref, p_ref, q_ref, a1_ref, a2_ref, y_ref, s_s, x_s, h_s, *, batch, n_chunks):
    t = SSM_CHUNK
    rows = batch * n_chunks
    half = 2 * SSM_STATE
    u = jnp.concatenate([u_ref[0, c].reshape(rows, t) for c in range(SSM_GROUP)], axis=1).astype(BF16)
    y_intra = jnp.dot(u, g_ref[0], preferred_element_type=F32)
    s_all = jnp.dot(u, p_ref[0], preferred_element_type=F32)
    s_s[...] = s_all
    x_s[...] = jnp.concatenate([pltpu.roll(s_all[:, :half], SSM_STATE, 1),
                                pltpu.roll(s_all[:, half:], SSM_STATE, 1)], axis=1)
    a1f, a2f = a1_ref[0, :, :half], a2_ref[0, :, :half]
    a1b, a2b = a1_ref[0, :, half:], a2_ref[0, :, half:]

    sub = SUBLANES
    n_blocks = n_chunks // sub

    def step(kb, carry):
        new = []
        for b in range(batch):
            hf, gf, hb, gb = carry[4 * b:4 * b + 4]
            base_f = pl.multiple_of(b * n_chunks + kb * sub, sub)
            base_b = pl.multiple_of(b * n_chunks + (n_blocks - 1 - kb) * sub, sub)
            sf, xf = s_s[pl.ds(base_f, sub), 0:half], x_s[pl.ds(base_f, sub), 0:half]
            sb, xb = s_s[pl.ds(base_b, sub), half:2 * half], x_s[pl.ds(base_b, sub), half:2 * half]
            hf_rows, hb_rows = [], [None] * sub
            for i in range(sub):
                hf_rows.append(hf)
                hf, gf = a1f * hf + a2f * gf + sf[i:i + 1], a1f * gf - a2f * hf + xf[i:i + 1]
            for i in range(sub - 1, -1, -1):
                hb_rows[i] = hb
                hb, gb = a1b * hb + a2b * gb + sb[i:i + 1], a1b * gb - a2b * hb + xb[i:i + 1]
            h_s[pl.ds(base_f, sub), 0:half] = jnp.concatenate(hf_rows, axis=0)
            h_s[pl.ds(base_b, sub), half:2 * half] = jnp.concatenate(hb_rows, axis=0)
            new += [hf, gf, hb, gb]
        return tuple(new)

    zero = jnp.zeros((1, half), F32)
    lax.fori_loop(0, n_blocks, step, (zero,) * (4 * batch))
    y = y_intra + jnp.dot(h_s[...].astype(BF16), q_ref[0], preferred_element_type=F32)
    for c in range(SSM_GROUP):
        y_ref[0, c] = y[:, c * t:(c + 1) * t].reshape(batch, n_chunks, t)


def _ssm_scan(u_t, g_mat, p_mat, q_mat, a1, a2, batch, length):
    t = SSM_CHUNK
    nk = length // t
    gt = SSM_GROUP * t
    u5 = u_t.reshape(SSM_GROUPS, SSM_GROUP, batch, nk, t)
    blk = (1, SSM_GROUP, batch, nk, t)
    y5 = pl.pallas_call(
        functools.partial(_ssm_kernel, batch=batch, n_chunks=nk),
        grid=(SSM_GROUPS,),
        in_specs=[pl.BlockSpec(blk, lambda g: (g, 0, 0, 0, 0)),
                  pl.BlockSpec((1, gt, gt), lambda g: (g, 0, 0)),
                  pl.BlockSpec((1, gt, 4 * SSM_STATE), lambda g: (g, 0, 0)),
                  pl.BlockSpec((1, 4 * SSM_STATE, gt), lambda g: (g, 0, 0)),
                  pl.BlockSpec((1, 1, 4 * SSM_STATE), lambda g: (g, 0, 0)),
                  pl.BlockSpec((1, 1, 4 * SSM_STATE), lambda g: (g, 0, 0))],
        out_specs=pl.BlockSpec(blk, lambda g: (g, 0, 0, 0, 0)),
        out_shape=jax.ShapeDtypeStruct(u5.shape, F32),
        scratch_shapes=[pltpu.VMEM((batch * nk, 4 * SSM_STATE), F32)] * 3,
        compiler_params=_cparams(("parallel",)),
        name="ssm_scan",
    )(u5, g_mat, p_mat, q_mat, a1, a2)
    return y5.reshape(SSM_WIDTH, batch * length)


def _ssm_tables(a_re, a_im, log_step, b_re, b_im, c_re, c_im):
    t = SSM_CHUNK
    hi = lax.Precision.HIGHEST
    lam = lax.complex(a_re.astype(F32), a_im.astype(F32))
    step = jnp.exp(log_step.astype(F32))[..., None]
    ls = lam * step
    a_bar = jnp.exp(ls)
    b_bar = ((a_bar - 1.0) / lam)[..., None] * lax.complex(b_re.astype(F32), b_im.astype(F32))
    c = lax.complex(c_re.astype(F32), c_im.astype(F32))
    tau = jnp.arange(t + 1, dtype=F32)
    pw = jnp.exp(ls[..., None] * tau)
    kern = jnp.einsum('dgcp,dgpt,dgpe->dgtce', c, pw[..., :t], b_bar, precision=hi).real
    kf = kern[0].at[:, 0].add(kern[1][:, 0]).transpose(0, 3, 2, 1)
    kb = jnp.roll(kern[1][:, ::-1], 1, axis=1).transpose(0, 3, 2, 1)
    pw_f, pw_b = pw[0], pw[1]
    pf = pw_f[:, :, t - 1 - jnp.arange(t)][..., None] * b_bar[0][:, :, None, :]
    pb = pw_b[:, :, :t][..., None] * b_bar[1][:, :, None, :]
    to_rows = lambda z: z.transpose(0, 3, 2, 1).reshape(SSM_GROUPS, SSM_GROUP * t, SSM_STATE)
    p_mat = jnp.concatenate([to_rows(pf.real), to_rows(pf.imag), to_rows(pb.real), to_rows(pb.imag)], axis=-1)
    qf = c[0].transpose(0, 2, 1)[..., None] * pw_f[:, :, 1:][:, :, None, :]
    qb = c[1].transpose(0, 2, 1)[..., None] * pw_b[:, :, t - jnp.arange(t)][:, :, None, :]
    to_cols = lambda z: z.reshape(SSM_GROUPS, SSM_STATE, SSM_GROUP * t)
    q_mat = jnp.concatenate([to_cols(qf.real), -to_cols(qf.imag), to_cols(qb.real), -to_cols(qb.imag)], axis=1)
    at = pw[..., t]
    a1 = jnp.concatenate([at[0].real, at[0].real, at[1].real, at[1].real], axis=-1)[:, None, :]
    a2 = jnp.concatenate([-at[0].imag, at[0].imag, -at[1].imag, at[1].imag], axis=-1)[:, None, :]
    return _toeplitz(kf, kb), p_mat.astype(BF16), q_mat.astype(BF16), a1, a2


def _dft_tables(length):
    j = DFT_INNER
    n = 2 * length
    rn = n // j
    rh = rn // 2
    odd = 2 * jnp.arange(rh, dtype=jnp.int32) + 1
    r = jnp.arange(rn, dtype=jnp.int32)
    ang1 = (math.pi / rn) * ((odd[:, None] * r[None, :]) % (2 * rn)).astype(F32)
    c1, s1 = jnp.cos(ang1), jnp.sin(ang1)
    f1_full = jnp.concatenate([c1, -s1], axis=0)
    f1_top = f1_full[:, :rh]
    f1_inv = jnp.concatenate([c1[:, :rh].T, -s1[:, :rh].T], axis=1)
    jj = jnp.arange(j, dtype=jnp.int32)
    angt = (math.pi / n) * (odd[:, None] * jj[None, :]).astype(F32)
    tw_re, tw_im = jnp.cos(angt), -jnp.sin(angt)
    ang2 = (2.0 * math.pi / j) * ((jj[:, None] * jj[None, :]) % j).astype(F32)
    c2, s2 = jnp.cos(ang2), jnp.sin(ang2)
    f2_fwd = jnp.block([[c2, -s2], [s2, c2]])
    f2_inv = jnp.block([[c2, s2], [-s2, c2]])
    return dict(f1_full=f1_full.astype(BF16), f1_top=f1_top.astype(BF16), f1_inv=f1_inv.astype(BF16),
                tw_re=tw_re, tw_im=tw_im, f2_fwd=f2_fwd.astype(BF16), f2_inv=f2_inv.astype(BF16), rn=rn, rh=rh)


def _twiddle_rows(a, tw_re, tw_im, rk):
    a_re, a_im = a[:rk], a[rk:]
    return jnp.concatenate([a_re * tw_re - a_im * tw_im, a_re * tw_im + a_im * tw_re], axis=1)


def _filter_hidden_kernel(bands_ref, w1t_ref, w1c_ref, w1s_ref, b1_ref, w2t_ref, b2_ref, hid_ref, *, length, tl):
    hi = lax.Precision.HIGHEST
    base = pl.program_id(0) * tl
    idx = (lax.broadcasted_iota(jnp.int32, (1, tl), 1) + base).astype(F32)
    for d in range(2):
        pos = idx if d == 0 else float(length) - idx
        tt = pos / float(length)
        wpos = (2.0 * math.pi / length) * pos
        arg = bands_ref[...] * wpos
        h1 = (w1t_ref[...] * tt
              + jnp.dot(w1c_ref[...], jnp.cos(arg), preferred_element_type=F32, precision=hi)
              - jnp.dot(w1s_ref[...], jnp.sin(arg), preferred_element_type=F32, precision=hi))
        h1 = jnp.sin(h1 + b1_ref[...])
        h2 = jnp.dot(w2t_ref[...], h1, preferred_element_type=F32, precision=hi)
        hid_ref[d] = jnp.sin(h2 + b2_ref[...]).astype(hid_ref.dtype)


def _filter_taps_kernel(hid_ref, w3t_ref, decay_ref, filt_ref, *, length, tl):
    d = pl.program_id(0) % 2
    base = pl.program_id(1) * tl
    idx = (lax.broadcasted_iota(jnp.int32, (1, tl), 1) + base).astype(F32)
    pos = jnp.where(d == 0, idx, float(length) - idx)
    tt = pos / float(length)
    f = jnp.dot(w3t_ref[...], hid_ref[0], preferred_element_type=F32)
    f = f * jnp.exp(-tt * decay_ref[...])
    f = jnp.where(jnp.logical_and(d == 1, idx == 0.0), 0.0, f)
    filt_ref[...] = f.astype(filt_ref.dtype)


def _filter_dft_kernel(filt_ref, f1_ref, twre_ref, twim_ref, f2_ref, kf_ref, *, rk, cb):
    rows = []
    for c in range(cb):
        r = jnp.concatenate([filt_ref[0, 0, c], -filt_ref[0, 1, c]], axis=0)
        a = jnp.dot(f1_ref[...], r, preferred_element_type=F32)
        rows.append(_twiddle_rows(a, twre_ref[...], twim_ref[...], rk))
    x = jnp.dot(jnp.concatenate(rows, axis=0).astype(BF16), f2_ref[...], preferred_element_type=F32)
    for c in range(cb):
        kf_ref[0, c, 0] = x[rk * c:rk * (c + 1), :DFT_INNER].astype(kf_ref.dtype)
        kf_ref[0, c, 1] = x[rk * c:rk * (c + 1), DFT_INNER:].astype(kf_ref.dtype)


def _hyena_filters(length, tabs, w1, b1, w2, b2, w3, log_decay):
    tl = min(FILTER_LANE_TILE, length)
    nl = length // tl
    rn, rh, j = tabs["rn"], tabs["rh"], DFT_INNER
    bands = jnp.linspace(1e-4, FILTER_BANDS - 1, FILTER_BANDS, dtype=F32)[:, None]
    w1f = w1.astype(F32)
    const = lambda shape: pl.BlockSpec(shape, lambda *_: (0,) * len(shape))
    hid = pl.pallas_call(
        functools.partial(_filter_hidden_kernel, length=length, tl=tl),
        grid=(nl,),
        in_specs=[const((FILTER_BANDS, 1)), const((FILTER_HIDDEN, 1)), const((FILTER_HIDDEN, FILTER_BANDS)),
                  const((FILTER_HIDDEN, FILTER_BANDS)), const((FILTER_HIDDEN, 1)),
                  const((FILTER_HIDDEN, FILTER_HIDDEN)), const((FILTER_HIDDEN, 1))],
        out_specs=pl.BlockSpec((2, FILTER_HIDDEN, tl), lambda i: (0, 0, i)),
        out_shape=jax.ShapeDtypeStruct((2, FILTER_HIDDEN, length), BF16),
        compiler_params=_cparams(("parallel",)),
        name="hyena_filter_hidden",
    )(bands, w1f[0:1].T, w1f[1:1 + FILTER_BANDS].T, w1f[1 + FILTER_BANDS:].T, b1.astype(F32)[:, None],
      w2.astype(F32).T, b2.astype(F32)[:, None])
    n_od = HYENA_ORDER * 2
    w3t = w3.T.reshape(n_od, HYENA_WIDTH, FILTER_HIDDEN).astype(BF16)
    decay = jnp.exp(log_decay.astype(F32)).reshape(n_od, HYENA_WIDTH, 1)
    filt = pl.pallas_call(
        functools.partial(_filter_taps_kernel, length=length, tl=tl),
        grid=(n_od, nl),
        in_specs=[pl.BlockSpec((1, FILTER_HIDDEN, tl), lambda od, i: (od % 2, 0, i)),
                  pl.BlockSpec((None, HYENA_WIDTH, FILTER_HIDDEN), lambda od, i: (od, 0, 0)),
                  pl.BlockSpec((None, HYENA_WIDTH, 1), lambda od, i: (od, 0, 0))],
        out_specs=pl.BlockSpec((None, HYENA_WIDTH, tl), lambda od, i: (od, 0, i)),
        out_shape=jax.ShapeDtypeStruct((n_od, HYENA_WIDTH, length), BF16),
        compiler_params=_cparams(("parallel", "parallel")),
        name="hyena_filter_taps",
    )(hid, w3t, decay)
    filt6 = filt.reshape(HYENA_ORDER, 2, HYENA_WIDTH, rh, j)
    cb = FILTER_CH_BLOCK
    return pl.pallas_call(
        functools.partial(_filter_dft_kernel, rk=rh, cb=cb),
        grid=(HYENA_ORDER, HYENA_WIDTH // cb),
        in_specs=[pl.BlockSpec((1, 2, cb, rh, j), lambda o, c: (o, 0, c, 0, 0)),
                  const((rn, rn)), const((rh, j)), const((rh, j)), const((2 * j, 2 * j))],
        out_specs=pl.BlockSpec((1, cb, 2, rh, j), lambda o, c: (o, c, 0, 0, 0)),
        out_shape=jax.ShapeDtypeStruct((HYENA_ORDER, HYENA_WIDTH, 2, rh, j), BF16),
        compiler_params=_cparams(("parallel", "parallel")),
        name="hyena_filter_dft",
    )(filt6, tabs["f1_full"], tabs["tw_re"], tabs["tw_im"], tabs["f2_fwd"])


def _hyena_kernel(scw_ref, scb_ref, hb_ref, x_ref, hg_ref, kf_ref, f1_ref, f1i_ref, twre_ref, twim_ref, f2_ref,
                  f2i_ref, o_ref, *, batch, rn, rh, cb, cs):
    j = DFT_INNER
    n_inv = 2.0 / (rn * j)
    rk = rh
    lane = lax.broadcasted_iota(jnp.int32, (rh, j), 1)
    row = lax.broadcasted_iota(jnp.int32, (rh, j), 0)
    first = jnp.logical_and(lane == 0, row == 0)
    last = jnp.logical_and(lane == j - 1, row == rh - 1)
    tw_re, tw_im = twre_ref[...], twim_ref[...]

    def prev_t(x):
        p = pltpu.roll(x, 1, 1)
        return jnp.where(first, 0.0, jnp.where(lane == 0, pltpu.roll(p, 1, 0), p))

    def next_t(x):
        p = pltpu.roll(x, j - 1, 1)
        return jnp.where(last, 0.0, jnp.where(lane == j - 1, pltpu.roll(p, rh - 1, 0), p))

    def split(m, i):
        return m[rk * i:rk * (i + 1), :j], m[rk * i:rk * (i + 1), j:]

    def long_conv(xs, ks):
        rows = []
        for x in xs:
            a = jnp.dot(f1_ref[...], x.astype(BF16), preferred_element_type=F32)
            rows.append(_twiddle_rows(a, tw_re, tw_im, rk))
        m = jnp.dot(jnp.concatenate(rows, axis=0).astype(BF16), f2_ref[...], preferred_element_type=F32)
        rows = []
        for i, (k_re, k_im) in enumerate(ks):
            x_re, x_im = split(m, i)
            rows.append(jnp.concatenate([x_re * k_re - x_im * k_im, x_re * k_im + x_im * k_re], axis=1))
        m = jnp.dot(jnp.concatenate(rows, axis=0).astype(BF16), f2i_ref[...], preferred_element_type=F32)
        outs = []
        for i in range(len(xs)):
            b_re, b_im = split(m, i)
            bp = jnp.concatenate([b_re * tw_re + b_im * tw_im, b_im * tw_re - b_re * tw_im], axis=0).astype(BF16)
            outs.append(jnp.dot(f1i_ref[...], bp, preferred_element_type=F32) * n_inv)
        return outs

    cbase = pl.program_id(0) * cb
    for c0 in range(0, cb, cs):
        chans = [(ci, b) for ci in range(c0, c0 + cs) for b in range(batch)]
        segs = []
        for sgm in range(3):
            seg = []
            for ci, b in chans:
                chs = sgm * HYENA_WIDTH + cbase + ci
                x = x_ref[sgm, ci, b]
                seg.append(prev_t(x) * scw_ref[chs] + x * scw_ref[3 * HYENA_WIDTH + chs]
                           + next_t(x) * scw_ref[6 * HYENA_WIDTH + chs] + scb_ref[chs])
            segs.append(seg)
        z = segs[0]
        for o in range(HYENA_ORDER):
            conv = long_conv(z, [(kf_ref[o, ci, 0].astype(F32), kf_ref[o, ci, 1].astype(F32)) for ci, _ in chans])
            z = [segs[o + 1][i] * (conv[i] + z[i] * hb_ref[o * HYENA_WIDTH + cbase + ci])
                 for i, (ci, _) in enumerate(chans)]
        for i, (ci, b) in enumerate(chans):
            o_ref[ci, b] = z[i] * hg_ref[ci, b]


def _hyena(hi_t, hg_t, kf, tabs, short_w, short_b, hy_bias, batch, length):
    rn, rh, j = tabs["rn"], tabs["rh"], DFT_INNER
    cb = HYENA_CH_BLOCK
    x5 = hi_t.reshape(3, HYENA_WIDTH, batch, rh, j)
    g4 = hg_t.reshape(HYENA_WIDTH, batch, rh, j)
    smem = pl.BlockSpec(memory_space=pltpu.SMEM)
    const = lambda shape: pl.BlockSpec(shape, lambda c: (0,) * len(shape))
    out = pl.pallas_call(
        functools.partial(_hyena_kernel, batch=batch, rn=rn, rh=rh, cb=cb, cs=HYENA_CH_SUB),
        grid=(HYENA_WIDTH // cb,),
        in_specs=[smem, smem, smem,
                  pl.BlockSpec((3, cb, batch, rh, j), lambda c: (0, c, 0, 0, 0)),
                  pl.BlockSpec((cb, batch, rh, j), lambda c: (c, 0, 0, 0)),
                  pl.BlockSpec((HYENA_ORDER, cb, 2, rh, j), lambda c: (0, c, 0, 0, 0)),
                  const((rn, rh)), const((rh, rn)), const((rh, j)), const((rh, j)),
                  const((2 * j, 2 * j)), const((2 * j, 2 * j))],
        out_specs=pl.BlockSpec((cb, batch, rh, j), lambda c: (c, 0, 0, 0)),
        out_shape=jax.ShapeDtypeStruct(g4.shape, F32),
        compiler_params=_cparams(("parallel",)),
        name="hyena_conv",
    )(short_w.astype(F32).reshape(-1), short_b.astype(F32), hy_bias.astype(F32).reshape(-1),
      x5, g4, kf, tabs["f1_top"], tabs["f1_inv"], tabs["tw_re"], tabs["tw_im"], tabs["f2_fwd"], tabs["f2_inv"])
    return out.reshape(HYENA_WIDTH, batch * length)


def _rms_rows(x, g):
    return x * lax.rsqrt(jnp.mean(x * x, axis=-1, keepdims=True) + EPS) * g


def _rms_cols(x, g):
    return x * lax.rsqrt(jnp.mean(x * x, axis=0, keepdims=True) + EPS) * g


def _post_kernel(ao_ref, ga_ref, mq_ref, mk_ref, mv_ref, y_ref, u_ref, sg_ref, hy_ref, x_ref,
                 d_ref, wglu_ref, g_attn_ref, g_ssm_ref, g_hy_ref, g_mem_ref, post_g_ref, wo_ref, out_ref):
    attn_n = _rms_cols(ao_ref[...] * ga_ref[...], g_attn_ref[...])
    s = jnp.dot(mq_ref[...], mk_ref[0], preferred_element_type=F32)
    ps = []
    for h in range(MEM_HEADS):
        sh = s[:, MEM_TOKENS * h:MEM_TOKENS * (h + 1)]
        e = jnp.exp(sh - jnp.max(sh, axis=-1, keepdims=True))
        ps.append(e * (1.0 / jnp.sum(e, axis=-1, keepdims=True)))
    p = jnp.concatenate(ps, axis=1).astype(BF16)
    cross_n = _rms_rows(jnp.dot(p, mv_ref[0], preferred_element_type=F32), g_mem_ref[...])
    y = y_ref[...] + d_ref[...] * u_ref[...]
    g = y * (0.5 * (1.0 + jnp.tanh(math.sqrt(2.0 / math.pi) * (y + 0.044715 * (y * y * y)))))
    gz = jnp.dot(wglu_ref[...], g.astype(BF16), preferred_element_type=F32)
    ssm_n = _rms_cols(g * _sigmoid(gz) * sg_ref[...], g_ssm_ref[...])
    hy_n = _rms_cols(hy_ref[...], g_hy_ref[...])
    o1, o2, o3 = ATTN_WIDTH, ATTN_WIDTH + SSM_WIDTH, ATTN_WIDTH + SSM_WIDTH + HYENA_WIDTH
    mixed = (jnp.dot(attn_n.T.astype(BF16), wo_ref[0:o1], preferred_element_type=F32)
             + jnp.dot(ssm_n.T.astype(BF16), wo_ref[o1:o2], preferred_element_type=F32)
             + jnp.dot(hy_n.T.astype(BF16), wo_ref[o2:o3], preferred_element_type=F32)
             + jnp.dot(cross_n.astype(BF16), wo_ref[o3:], preferred_element_type=F32))
    out_ref[...] = x_ref[...] + _rms_rows(mixed, post_g_ref[...])


def _post(ao, ga, mq, mk_bd, mv_bd, y_t, u_t, sg_t, hy_t, x2, d, wglu_t, g_attn, g_ssm, g_hy, g_mem, post_g, wo,
          length):
    n = x2.shape[0]
    tm = TOKEN_TILE
    tiles_per_seq = length // tm
    tok_spec = lambda w: pl.BlockSpec((tm, w), lambda i: (i, 0))
    ch_spec = lambda w: pl.BlockSpec((w, tm), lambda i: (0, i))
    const = lambda shape: pl.BlockSpec(shape, lambda i: (0,) * len(shape))
    hm = MEM_HEADS * MEM_TOKENS
    return pl.pallas_call(
        _post_kernel,
        grid=(n // tm,),
        in_specs=[ch_spec(ATTN_WIDTH), ch_spec(ATTN_WIDTH), tok_spec(MEM_WIDTH),
                  pl.BlockSpec((1, MEM_WIDTH, hm), lambda i: (i // tiles_per_seq, 0, 0)),
                  pl.BlockSpec((1, hm, MEM_WIDTH), lambda i: (i // tiles_per_seq, 0, 0)),
                  ch_spec(SSM_WIDTH), ch_spec(SSM_WIDTH), ch_spec(SSM_WIDTH), ch_spec(HYENA_WIDTH),
                  tok_spec(D_MODEL),
                  const((SSM_WIDTH, 1)), const((SSM_WIDTH, SSM_WIDTH)),
                  const((ATTN_WIDTH, 1)), const((SSM_WIDTH, 1)), const((HYENA_WIDTH, 1)), const((1, MEM_WIDTH)),
                  const((1, D_MODEL)), const((MIX_WIDTH, D_MODEL))],
        out_specs=tok_spec(D_MODEL),
        out_shape=jax.ShapeDtypeStruct((n, D_MODEL), F32),
        compiler_params=_cparams(("parallel",)),
        name="post",
    )(ao, ga, mq, mk_bd, mv_bd, y_t, u_t, sg_t, hy_t, x2, d, wglu_t, g_attn, g_ssm, g_hy, g_mem, post_g, wo)


def _rope_tables(length):
    rows = length // GRID_W
    row = jnp.broadcast_to(jnp.arange(rows, dtype=F32)[:, None], (rows, GRID_W)).reshape(length)
    col = jnp.broadcast_to(jnp.arange(GRID_W, dtype=F32)[None, :], (rows, GRID_W)).reshape(length)
    inv_freq = ROPE_THETA ** (-jnp.arange(ROPE_FREQS, dtype=F32) / ROPE_FREQS)
    ang = jnp.stack([row[:, None] * inv_freq, col[:, None] * inv_freq], axis=1)
    ang = jnp.broadcast_to(ang[:, :, None, :], (length, 2, 2, ROPE_FREQS)).reshape(length, HEAD_DIM)
    cos1, sin1 = jnp.cos(ang), jnp.sin(ang)
    low1 = (jnp.arange(HEAD_DIM) % (2 * ROPE_FREQS)) < ROPE_FREQS
    cos_t, sin_t = cos1.T, jnp.where(low1, -sin1, sin1).T
    cos = jnp.concatenate([cos1, cos1], axis=1)
    sin = jnp.concatenate([sin1, sin1], axis=1)
    low = jnp.concatenate([low1, low1])
    return cos, jnp.where(low, -sin, 0.0), jnp.where(low, 0.0, sin), cos_t, sin_t


def _block_diag_heads(mk, mv, batch):
    mk4 = mk.reshape(batch, MEM_TOKENS, MEM_HEADS, HEAD_DIM)
    mv4 = mv.reshape(batch, MEM_TOKENS, MEM_HEADS, HEAD_DIM)
    eye = jnp.eye(MEM_HEADS, dtype=mk.dtype)
    k_bd = jnp.einsum('bmhd,hg->bhdgm', mk4, eye).reshape(batch, MEM_WIDTH, MEM_HEADS * MEM_TOKENS)
    v_bd = jnp.einsum('bmhd,hg->bhmgd', mv4, eye).reshape(batch, MEM_HEADS * MEM_TOKENS, MEM_WIDTH)
    return k_bd.astype(BF16), v_bd.astype(BF16)


def _layer_weights(layer, p):
    w_in = p["w_in"][layer]
    a, kv = ATTN_WIDTH, KV_WIDTH
    k0, v0, g0, mq0 = a, a + kv, a + 2 * kv, w_in.shape[1] - MEM_WIDTH
    w_tok = jnp.concatenate([w_in[:, k0:v0], w_in[:, mq0:]], axis=1).astype(BF16)
    w_ch_t = jnp.concatenate([w_in[:, :k0], w_in[:, v0:mq0]], axis=1).T.astype(BF16)
    bg = p["branch_norm"][layer].astype(F32)
    o1, o2, o3 = a, a + SSM_WIDTH, a + SSM_WIDTH + HYENA_WIDTH
    head_id = jnp.arange(kv) // HEAD_DIM
    return dict(
        w_tok=w_tok, w_ch_t=w_ch_t,
        pre_g=p["pre_norm"][layer].astype(F32)[None, :], post_g=p["post_norm"][layer].astype(F32)[None, :],
        qg=p["q_norm"][layer].astype(F32)[:, None],
        kg=jnp.tile(p["k_norm"][layer].astype(F32), ATTN_KV_HEADS)[None, :],
        ones=(head_id[:, None] == head_id[None, :]).astype(BF16),
        mem_g=p["mem_norm"][layer].astype(F32)[None, :], w_mem_kv=p["w_mem_kv"][layer].astype(BF16),
        ssm=_ssm_tables(p["ssm_a_re"][layer], p["ssm_a_im"][layer], p["ssm_log_step"][layer], p["ssm_b_re"][layer],
                        p["ssm_b_im"][layer], p["ssm_c_re"][layer], p["ssm_c_im"][layer]),
        d=p["ssm_d"][layer].astype(F32)[:, None], wglu_t=p["ssm_w_glu"][layer].T.astype(BF16),
        g_attn=bg[:o1, None], g_ssm=bg[o1:o2, None], g_hy=bg[o2:o3, None], g_mem=bg[None, o3:],
        wo=p["w_out"][layer].astype(BF16),
    )


def _mixer_layer(x2, mem2, lw, kf, tabs, rope, p, layer, batch, length):
    q_t, k, v_t, ga, mq, su_t, sg_t, hi_t, hg_t = _inproj(x2, lw["pre_g"], lw["w_tok"], lw["w_ch_t"], rope,
                                                          lw["qg"], lw["kg"], lw["ones"], length)
    ao = _flash_attention(q_t, k, v_t, batch, length)
    mem_kv = _memkv(mem2, lw["mem_g"], lw["w_mem_kv"])
    mk_bd, mv_bd = _block_diag_heads(mem_kv[:, :MEM_WIDTH], mem_kv[:, MEM_WIDTH:], batch)
    y_t = _ssm_scan(su_t, *lw["ssm"], batch, length)
    hy_t = _hyena(hi_t, hg_t, kf, tabs, p["hyena_short_w"][layer], p["hyena_short_b"][layer],
                  p["hyena_bias"][layer], batch, length)
    return _post(ao, ga, mq, mk_bd, mv_bd, y_t, su_t, sg_t, hy_t, x2, lw["d"], lw["wglu_t"],
                 lw["g_attn"], lw["g_ssm"], lw["g_hy"], lw["g_mem"], lw["post_g"], lw["wo"], length)


def _run_group(x, mem, weights, p):
    batch, length, _ = x.shape
    rope = _rope_tables(length)
    tabs = _dft_tables(length)
    x2 = x.reshape(batch * length, D_MODEL)
    mem2 = mem.reshape(batch * MEM_TOKENS, D_MODEL)
    for layer in range(DEPTH):
        kf = _hyena_filters(length, tabs, p["hyena_ffn_w1"][layer], p["hyena_ffn_b1"][layer],
                            p["hyena_ffn_w2"][layer], p["hyena_ffn_b2"][layer], p["hyena_ffn_w3"][layer],
                            p["hyena_log_decay"][layer])
        x2 = _mixer_layer(x2, mem2, weights[layer], kf, tabs, rope, p, layer, batch, length)
    return x2.reshape(batch, length, D_MODEL)


def kernel(x_prompt, x_sample, mem_prompt, mem_sample, pre_norm, post_norm, w_in, q_norm, k_norm, mem_norm, w_mem_kv, ssm_a_re, ssm_a_im, ssm_log_step, ssm_b_re, ssm_b_im, ssm_c_re, ssm_c_im, ssm_d, ssm_w_glu, hyena_short_w, hyena_short_b, hyena_ffn_w1, hyena_ffn_b1, hyena_ffn_w2, hyena_ffn_b2, hyena_ffn_w3, hyena_log_decay, hyena_bias, branch_norm, w_out):
    p = dict(pre_norm=pre_norm, post_norm=post_norm, w_in=w_in, q_norm=q_norm, k_norm=k_norm, mem_norm=mem_norm,
             w_mem_kv=w_mem_kv, ssm_a_re=ssm_a_re, ssm_a_im=ssm_a_im, ssm_log_step=ssm_log_step, ssm_b_re=ssm_b_re,
             ssm_b_im=ssm_b_im, ssm_c_re=ssm_c_re, ssm_c_im=ssm_c_im, ssm_d=ssm_d, ssm_w_glu=ssm_w_glu,
             hyena_short_w=hyena_short_w, hyena_short_b=hyena_short_b, hyena_ffn_w1=hyena_ffn_w1,
             hyena_ffn_b1=hyena_ffn_b1, hyena_ffn_w2=hyena_ffn_w2, hyena_ffn_b2=hyena_ffn_b2,
             hyena_ffn_w3=hyena_ffn_w3, hyena_log_decay=hyena_log_decay, hyena_bias=hyena_bias,
             branch_norm=branch_norm, w_out=w_out)
    weights = [_layer_weights(layer, p) for layer in range(DEPTH)]
    return (_run_group(x_prompt, mem_prompt, weights, p), _run_group(x_sample, mem_sample, weights, p))
```
